```python
import math
import jax, jax.numpy as jnp
from jax import lax
import numpy as np

D_MODEL = 2048
BATCH = 1
SEQ = 8192
DEPTH = 4

MIX_WIDTH = D_MODEL
GROUP_WIDTH = MIX_WIDTH // 4

GM_CHUNK = 128
GM_HEADS = 4
GM_HEAD_CH = GROUP_WIDTH // GM_HEADS

DA_HEADS = 4
DA_QK_DIM = 64
DA_V_DIM = 2 * DA_QK_DIM

CV_CH = GROUP_WIDTH
CV_WIDTH = 31

NSA_HEADS = 8
NSA_KV_HEADS = 2
NSA_HEAD_DIM = 64
NSA_CMP_LEN = 32
NSA_CMP_STRIDE = 16
NSA_CMP_HIDDEN = 256
NSA_SLC_LEN = 64
NSA_SLC_TOPK = 16
NSA_WINDOW = 512

Q_BLOCK = 128
ROPE_THETA = 10000.0
NORM_EPS = 1e-6
NEG = -1e30
FORCE = 1e9

D_FF = 5632
N_EXPERTS = 8
TOP_K = 2
D_FF_EXPERT = 5632
N_DENSE = DEPTH - DEPTH // 2
N_MOE = DEPTH // 2

IN_SPLITS = [
    2 * GROUP_WIDTH,
    DA_HEADS * 2 * DA_QK_DIM,
    DA_HEADS * 2 * DA_QK_DIM,
    DA_HEADS * DA_V_DIM,
    2 * CV_CH,
    NSA_HEADS * NSA_HEAD_DIM,
    6 * NSA_KV_HEADS * NSA_HEAD_DIM,
    3 * NSA_HEADS,
]
IN_WIDTH = sum(IN_SPLITS)
IN_OFFSETS = [int(o) for o in np.cumsum(IN_SPLITS)[:-1]]

kernel_name = 'hybrid_parallel_head_group_moe_trunk'


def rms_norm(x, g):
    xf = x.astype(jnp.float32)
    y = xf * lax.rsqrt(jnp.mean(xf * xf, axis=-1, keepdims=True) + NORM_EPS)
    return (y * g.astype(jnp.float32)).astype(x.dtype)


def layer_norm(x, g, b):
    xf = x.astype(jnp.float32)
    mu = jnp.mean(xf, axis=-1, keepdims=True)
    var = jnp.mean(jnp.square(xf - mu), axis=-1, keepdims=True)
    y = (xf - mu) * lax.rsqrt(var + NORM_EPS)
    return (y * g.astype(jnp.float32) + b.astype(jnp.float32)).astype(x.dtype)


def rope_tables(T, dim):
    inv = ROPE_THETA ** (-jnp.arange(0, dim, 2, dtype=jnp.float32) / dim)
    ang = jnp.arange(T, dtype=jnp.float32)[:, None] * inv[None, :]
    return jnp.cos(ang), jnp.sin(ang)


def rope(x, cos, sin):
    half = x.shape[-1] // 2
    x1, x2 = x[..., :half], x[..., half:]
    c, s = cos.astype(x.dtype), sin.astype(x.dtype)
    return jnp.concatenate([x1 * c - x2 * s, x1 * s + x2 * c], axis=-1)


def spatial_gating(z, ln_g, ln_b, w_s, b_s):
    B, T, _ = z.shape
    z = jax.nn.gelu(z)
    u, v = jnp.split(z, 2, axis=-1)
    v = layer_norm(v, ln_g, ln_b)
    v = v.reshape(B, T // GM_CHUNK, GM_CHUNK, GM_HEADS, GM_HEAD_CH)
    causal = jnp.tril(jnp.ones((GM_CHUNK, GM_CHUNK), dtype=bool))
    w = jnp.where(causal[None], w_s, jnp.zeros_like(w_s))
    s = jnp.einsum('hts,bcshe->bcthe', w, v) + b_s.T[None, None, :, :, None]
    return u * s.reshape(B, T, GROUP_WIDTH)


def diff_attention(q, k, v, lam, subln_g, lambda_init, cos, sin):
    B, T, _ = q.shape
    H, dk, dv = DA_HEADS, DA_QK_DIM, DA_V_DIM
    q = rope(q.reshape(B, T, H, 2, dk).transpose(0, 2, 3, 1, 4), cos, sin) * (dk ** -0.5)
    k = rope(k.reshape(B, T, H, 2, dk).transpose(0, 2, 3, 1, 4), cos, sin)
    v = v.reshape(B, T, H, dv).transpose(0, 2, 1, 3)
    lf = lam.astype(jnp.float32)
    lam_full = jnp.exp(jnp.sum(lf[0] * lf[1])) - jnp.exp(jnp.sum(lf[2] * lf[3])) + lambda_init
    nb = T // Q_BLOCK
    qb = jnp.moveaxis(q.reshape(B, H, 2, nb, Q_BLOCK, dk), 3, 0)
    kpos = jnp.arange(T)

    def block(args):
        qi, n = args
        s = jnp.einsum('bhiqd,bhikd->bhiqk', qi, k).astype(jnp.float32)
        qpos = n * Q_BLOCK + jnp.arange(Q_BLOCK)
        mask = kpos[None, :] <= qpos[:, None]
        p = jax.nn.softmax(jnp.where(mask, s, NEG), axis=-1)
        a = p[:, :, 0] - lam_full * p[:, :, 1]
        return jnp.einsum('bhqk,bhkd->bhqd', a.astype(v.dtype), v)

    o = lax.map(block, (qb, jnp.arange(nb)))
    o = jnp.moveaxis(o, 0, 2).reshape(B, H, T, dv)
    o = rms_norm(o, subln_g) * (1.0 - lambda_init)
    return o.transpose(0, 2, 1, 3).reshape(B, T, H * dv)


def conformer_conv(z, dw_w, dw_b, ln_g, ln_b):
    a, g = jnp.split(z, 2, axis=-1)
    h = a * jax.nn.sigmoid(g)
    h = lax.conv_general_dilated(h, dw_w[:, None, :].astype(h.dtype), window_strides=(1,),
                                 padding=[(CV_WIDTH - 1, 0)],
                                 dimension_numbers=('NWC', 'WIO', 'NWC'),
                                 feature_group_count=CV_CH) + dw_b
    h = layer_norm(h, ln_g, ln_b)
    return jax.nn.silu(h)


def nsa_attention(q, kv, gates, cmp_w1, cmp_w2, cmp_pe, cos, sin):
    B, T, _ = q.shape
    H, G, dk = NSA_HEADS, NSA_KV_HEADS, NSA_HEAD_DIM
    hpg = H // G
    L, S, LS, W = NSA_CMP_LEN, NSA_CMP_STRIDE, NSA_SLC_LEN, NSA_WINDOW
    n_slc = T // LS
    NS = min(NSA_SLC_TOPK, n_slc)
    q = rope(q.reshape(B, T, H, dk).transpose(0, 2, 1, 3), cos, sin) * (dk ** -0.5)
    q = q.reshape(B, G, hpg, T, dk)
    kv = kv.reshape(B, T, 6, G, dk).transpose(2, 0, 3, 1, 4)
    k_c, v_c = rope(kv[0], cos, sin), kv[1]
    k_s, v_s = rope(kv[2], cos, sin), kv[3]
    k_w, v_w = rope(kv[4], cos, sin), kv[5]
    t = jnp.arange(T)

    n_cmp = (T - L) // S + 1
    blk_idx = jnp.arange(n_cmp)[:, None] * S + jnp.arange(L)[None, :]

    def compress(x, w1, w2, pe):
        blk = x[:, :, blk_idx] + pe
        return jax.nn.gelu(blk.reshape(B, G, n_cmp, L * dk) @ w1) @ w2

    kc = compress(k_c, cmp_w1[0], cmp_w2[0], cmp_pe[0])
    vc = compress(v_c, cmp_w1[1], cmp_w2[1], cmp_pe[1])
    s = jnp.einsum('bghtd,bgnd->bghtn', q, kc).astype(jnp.float32)
    cmask = (jnp.arange(n_cmp) * S + L - 1)[None, :] <= t[:, None]
    p_cmp = jax.nn.softmax(jnp.where(cmask, s, NEG), axis=-1) * cmask
    o_cmp = jnp.einsum('bghtn,bgnd->bghtd', p_cmp.astype(vc.dtype), vc)

    cstart = jnp.arange(n_cmp) * S
    sstart = jnp.arange(n_slc) * LS
    overlap = ((cstart[:, None] < sstart[None, :] + LS) &
               (cstart[:, None] + L > sstart[None, :])).astype(jnp.float32)
    imp = jnp.einsum('bghtn,nj->bgtj', p_cmp, overlap)
    cur = t // LS
    j = jnp.arange(n_slc)
    forced = (j[None, :] == 0) | (j[None, :] == cur[:, None]) | (j[None, :] == cur[:, None] - 1)
    valid = j[None, :] <= cur[:, None]
    score = jnp.where(forced, FORCE, jnp.where(valid, imp, -1.0))
    top_score, top_idx = lax.top_k(score, NS)
    sel_ok = top_score >= 0.0

    nb = T // Q_BLOCK
    ks_blk = k_s.reshape(B, G, n_slc, LS, dk)
    vs_blk = v_s.reshape(B, G, n_slc, LS, dk)
    kw_pad = jnp.pad(k_w, ((0, 0), (0, 0), (W, 0), (0, 0)))
    vw_pad = jnp.pad(v_w, ((0, 0), (0, 0), (W, 0), (0, 0)))
    qb = jnp.moveaxis(q.reshape(B, G, hpg, nb, Q_BLOCK, dk), 3, 0)
    idxb = jnp.moveaxis(top_idx.reshape(B, G, nb, Q_BLOCK, NS), 2, 0)
    okb = jnp.moveaxis(sel_ok.reshape(B, G, nb, Q_BLOCK, NS), 2, 0)
    bi = jnp.arange(B)[:, None, None, None]
    gi = jnp.arange(G)[None, :, None, None]

    def block(args):
        qi, ii, ok, n = args
        qpos = n * Q_BLOCK + jnp.arange(Q_BLOCK)
        kb = ks_blk[bi, gi, ii]
        vb = vs_blk[bi, gi, ii]
        kpos = ii[..., None] * LS + jnp.arange(LS)
        m = (kpos <= qpos[None, None, :, None, None]) & ok[..., None]
        s = jnp.einsum('bghqd,bgqnsd->bghqns', qi, kb).astype(jnp.float32)
        s = jnp.where(m[:, :, None], s, NEG).reshape(B, G, hpg, Q_BLOCK, NS * LS)
        p = jax.nn.softmax(s, axis=-1).reshape(B, G, hpg, Q_BLOCK, NS, LS)
        o_s = jnp.einsum('bghqns,bgqnsd->bghqd', p.astype(vb.dtype), vb)
        start = n * Q_BLOCK
        kw = lax.dynamic_slice_in_dim(kw_pad, start, W + Q_BLOCK, axis=2)
        vw = lax.dynamic_slice_in_dim(vw_pad, start, W + Q_BLOCK, axis=2)
        wpos = start - W + jnp.arange(W + Q_BLOCK)
        wm = ((wpos[None, :] <= qpos[:, None]) & (wpos[None, :] > qpos[:, None] - W)
              & (wpos[None, :] >= 0))
        s = jnp.einsum('bghqd,bgkd->bghqk', qi, kw).astype(jnp.float32)
        p = jax.nn.softmax(jnp.where(wm, s, NEG), axis=-1)
        o_w = jnp.einsum('bghqk,bgkd->bghqd', p.astype(vw.dtype), vw)
        return o_s, o_w

    o_sel, o_win = lax.map(block, (qb, idxb, okb, jnp.arange(nb)))
    o_sel = jnp.moveaxis(o_sel, 0, 3).reshape(B, G, hpg, T, dk)
    o_win = jnp.moveaxis(o_win, 0, 3).reshape(B, G, hpg, T, dk)
    g = jax.nn.sigmoid(gates).reshape(B, T, 3, H).transpose(2, 0, 3, 1).reshape(3, B, G, hpg, T, 1)
    o = g[0] * o_cmp + g[1] * o_sel + g[2] * o_win
    return o.reshape(B, H, T, dk).transpose(0, 2, 1, 3).reshape(B, T, H * dk)


def swiglu(x, wg, wu, wd):
    return (jax.nn.silu(x @ wg) * (x @ wu)) @ wd


def moe_swiglu(x, w_router, wg, wu, wd):
    logits = (x @ w_router).astype(jnp.float32)
    top_v, top_i = lax.top_k(logits, TOP_K)
    w = jax.nn.softmax(top_v, axis=-1)
    gate = jnp.sum(jax.nn.one_hot(top_i, N_EXPERTS, dtype=jnp.float32) * w[..., None], axis=-2)
    out = jnp.zeros_like(x)
    for e in range(N_EXPERTS):
        out = out + gate[..., e:e + 1].astype(x.dtype) * swiglu(x, wg[e], wu[e], wd[e])
    return out


def setup_inputs(seed: int = 0) -> dict:
    key = jax.random.key(seed)
    ks = jax.random.split(key, 32)
    f32 = jnp.float32

    def nrm(k, shape, scale):
        return jax.random.normal(k, shape, f32) * scale

    D = D_MODEL
    return {
        'x': nrm(ks[0], (BATCH, SEQ, D), 1.0),
        'attn_norm': 1.0 + nrm(ks[1], (DEPTH, D), 0.02),
        'w_in': nrm(ks[2], (DEPTH, D, IN_WIDTH), D ** -0.5),
        'w_out': nrm(ks[3], (DEPTH, MIX_WIDTH, D), MIX_WIDTH ** -0.5),
        'gm_ln_g': 1.0 + nrm(ks[4], (DEPTH, GROUP_WIDTH), 0.02),
        'gm_ln_b': nrm(ks[5], (DEPTH, GROUP_WIDTH), 0.02),
        'gm_ws': nrm(ks[6], (DEPTH, GM_HEADS, GM_CHUNK, GM_CHUNK), 0.5 * GM_CHUNK ** -0.5),
        'gm_bs': 1.0 + nrm(ks[7], (DEPTH, GM_HEADS, GM_CHUNK), 0.02),
        'da_lambda': nrm(ks[8], (DEPTH, 4, DA_QK_DIM), 0.1),
        'da_subln': 1.0 + nrm(ks[9], (DEPTH, DA_V_DIM), 0.02),
        'cv_dw_w': nrm(ks[10], (DEPTH, CV_WIDTH, CV_CH), CV_WIDTH ** -0.5),
        'cv_dw_b': nrm(ks[11], (DEPTH, CV_CH), 0.02),
        'cv_ln_g': 1.0 + nrm(ks[12], (DEPTH, CV_CH), 0.02),
        'cv_ln_b': nrm(ks[13], (DEPTH, CV_CH), 0.02),
        'nsa_cmp_w1': nrm(ks[14], (DEPTH, 2, NSA_CMP_LEN * NSA_HEAD_DIM, NSA_CMP_HIDDEN),
                          (NSA_CMP_LEN * NSA_HEAD_DIM) ** -0.5),
        'nsa_cmp_w2': nrm(ks[15], (DEPTH, 2, NSA_CMP_HIDDEN, NSA_HEAD_DIM), NSA_CMP_HIDDEN ** -0.5),
        'nsa_cmp_pe': nrm(ks[16], (DEPTH, 2, NSA_CMP_LEN, NSA_HEAD_DIM), 0.1),
        'ffn_norm': 1.0 + nrm(ks[17], (DEPTH, D), 0.02),
        'ffn_wg': nrm(ks[18], (N_DENSE, D, D_FF), D ** -0.5),
        'ffn_wu': nrm(ks[19], (N_DENSE, D, D_FF), D ** -0.5),
        'ffn_wd': nrm(ks[20], (N_DENSE, D_FF, D), D_FF ** -0.5),
        'router_w': nrm(ks[21], (N_MOE, D, N_EXPERTS), D ** -0.5),
        'exp_wg': nrm(ks[22], (N_MOE, N_EXPERTS, D, D_FF_EXPERT), D ** -0.5),
        'exp_wu': nrm(ks[23], (N_MOE, N_EXPERTS, D, D_FF_EXPERT), D ** -0.5),
        'exp_wd': nrm(ks[24], (N_MOE, N_EXPERTS, D_FF_EXPERT, D), D_FF_EXPERT ** -0.5),
        'final_norm': 1.0 + nrm(ks[25], (D,), 0.02),
    }


def reference(x, attn_norm, w_in, w_out, gm_ln_g, gm_ln_b, gm_ws, gm_bs, da_lambda, da_subln,
              cv_dw_w, cv_dw_b, cv_ln_g, cv_ln_b, nsa_cmp_w1, nsa_cmp_w2, nsa_cmp_pe,
              ffn_norm, ffn_wg, ffn_wu, ffn_wd, router_w, exp_wg, exp_wu, exp_wd, final_norm):
    T = x.shape[1]
    cos, sin = rope_tables(T, DA_QK_DIM)
    for l in range(DEPTH):
        h = rms_norm(x, attn_norm[l])
        z = h @ w_in[l]
        z_gm, q_da, k_da, v_da, z_cv, q_ns, kv_ns, g_ns = jnp.split(z, IN_OFFSETS, axis=-1)
        lambda_init = 0.8 - 0.6 * math.exp(-0.3 * l)
        o_a = spatial_gating(z_gm, gm_ln_g[l], gm_ln_b[l], gm_ws[l], gm_bs[l])
        o_b = diff_attention(q_da, k_da, v_da, da_lambda[l], da_subln[l], lambda_init, cos, sin)
        o_c = conformer_conv(z_cv, cv_dw_w[l], cv_dw_b[l], cv_ln_g[l], cv_ln_b[l])
        o_d = nsa_attention(q_ns, kv_ns, g_ns, nsa_cmp_w1[l], nsa_cmp_w2[l], nsa_cmp_pe[l], cos, sin)
        x = x + jnp.concatenate([o_a, o_b, o_c, o_d], axis=-1) @ w_out[l]
        h = rms_norm(x, ffn_norm[l])
        if l % 2 == 0:
            x = x + swiglu(h, ffn_wg[l // 2], ffn_wu[l // 2], ffn_wd[l // 2])
        else:
            x = x + moe_swiglu(h, router_w[l // 2], exp_wg[l // 2], exp_wu[l // 2], exp_wd[l // 2])
    return rms_norm(x, final_norm)
```

```python
import functools
import math

import jax
import jax.numpy as jnp
from jax import lax
from jax.experimental import pallas as pl
from jax.experimental.pallas import tpu as pltpu

F32 = jnp.float32
BF16 = jnp.bfloat16

D_MODEL = 2048
GROUP_WIDTH = 512
GM_CHUNK = 128
GM_HEADS = 4
DA_HEADS = 4
DA_QK_DIM = 64
CV_WIDTH = 31
NSA_HEADS = 8
NSA_KV_HEADS = 2
NSA_HEAD_DIM = 64
NSA_CMP_LEN = 32
NSA_CMP_STRIDE = 16
NSA_SLC_LEN = 64
NSA_SLC_TOPK = 16
NSA_WINDOW = 512
ROPE_THETA = 10000.0
NORM_EPS = 1e-6
NEG = -1e30
FORCE = 1e9
N_EXPERTS = 8
LANES = 128

IN_WIDTH = 4888
IN_WIDTH_PAD = 4992
COL_GM = 0
COL_QDA = 1024
COL_KDA = 1536
COL_VDA = 2048
COL_CV = 2560
COL_QNS = 3584
COL_KVNS = 4096
COL_GNS = 4864

VMEM_LIMIT = 56 * 1024 * 1024


def _cparams(sem):
    return pltpu.CompilerParams(dimension_semantics=sem, vmem_limit_bytes=VMEM_LIMIT)


def _rms(x, g):
    ms = jnp.mean(x * x, axis=-1, keepdims=True)
    return x * lax.rsqrt(ms + NORM_EPS) * g


def _layer_norm(x, g, b):
    mu = jnp.mean(x, axis=-1, keepdims=True)
    xc = x - mu
    var = jnp.mean(xc * xc, axis=-1, keepdims=True)
    return xc * lax.rsqrt(var + NORM_EPS) * g + b


def _dot(a, b):
    return jnp.dot(a, b, preferred_element_type=F32)


def _dot_nt(a, b):
    return lax.dot_general(a, b, (((1,), (1,)), ((), ())), preferred_element_type=F32)


def _lane(shape):
    return lax.broadcasted_iota(jnp.int32, shape, len(shape) - 1)


def _row(shape):
    return lax.broadcasted_iota(jnp.int32, shape, len(shape) - 2)


def _norm_mm_body(x_ref, g_ref, w_ref, o_ref, hn_ref):
    @pl.when(pl.program_id(1) == 0)
    def _():
        hn_ref[...] = _rms(x_ref[...], g_ref[...]).astype(BF16)

    o_ref[...] = _dot(hn_ref[...], w_ref[...])


def norm_mm(x, g, w, tm, tn):
    T, K = x.shape
    N = w.shape[1]
    return pl.pallas_call(
        _norm_mm_body,
        grid=(T // tm, N // tn),
        in_specs=[pl.BlockSpec((tm, K), lambda i, j: (i, 0)),
                  pl.BlockSpec((1, K), lambda i, j: (0, 0)),
                  pl.BlockSpec((K, tn), lambda i, j: (0, j))],
        out_specs=pl.BlockSpec((tm, tn), lambda i, j: (i, j)),
        out_shape=jax.ShapeDtypeStruct((T, N), F32),
        scratch_shapes=[pltpu.VMEM((tm, K), BF16)],
        compiler_params=_cparams(("parallel", "arbitrary")),
        name="norm_in_proj",
    )(x, g, w)


def _gmlp_body(z_ref, g_ref, b_ref, ws_ref, bs_ref, o_ref):
    tr = z_ref.shape[0]
    z = jax.nn.gelu(z_ref[...])
    u = z[:, :GROUP_WIDTH]
    v = _layer_norm(z[:, GROUP_WIDTH:], g_ref[...], b_ref[...]).astype(BF16)
    causal = _row((GM_CHUNK, GM_CHUNK)) >= _lane((GM_CHUNK, GM_CHUNK))
    bias = bs_ref[...]
    for h in range(GM_HEADS):
        w = jnp.where(causal, ws_ref[h], 0.0).astype(BF16)
        cols = slice(h * LANES, (h + 1) * LANES)
        for c in range(tr // GM_CHUNK):
            rows = slice(c * GM_CHUNK, (c + 1) * GM_CHUNK)
            s = _dot(w, v[rows, cols]) + bias[:, cols]
            o_ref[rows, cols] = (u[rows, cols] * s).astype(BF16)


def gmlp(z, ln_g, ln_b, ws, bs_rows, tr):
    T = z.shape[0]
    return pl.pallas_call(
        _gmlp_body,
        grid=(T // tr,),
        in_specs=[pl.BlockSpec((tr, 2 * GROUP_WIDTH), lambda i: (i, COL_GM // (2 * GROUP_WIDTH))),
                  pl.BlockSpec((1, GROUP_WIDTH), lambda i: (0, 0)),
                  pl.BlockSpec((1, GROUP_WIDTH), lambda i: (0, 0)),
                  pl.BlockSpec((GM_HEADS, GM_CHUNK, GM_CHUNK), lambda i: (0, 0, 0)),
                  pl.BlockSpec((GM_CHUNK, GROUP_WIDTH), lambda i: (0, 0))],
        out_specs=pl.BlockSpec((tr, GROUP_WIDTH), lambda i: (i, 0)),
        out_shape=jax.ShapeDtypeStruct((T, GROUP_WIDTH), BF16),
        compiler_params=_cparams(("parallel",)),
        name="gmlp",
    )(z, ln_g, ln_b, ws, bs_rows)


CV_HALO = 32
CV_SUB = 64


def _conv_body(a_ref, g_ref, ap_ref, gp_ref, w_ref, b_ref, lg_ref, lb_ref, o_ref, hs_ref):
    tr = a_ref.shape[0]
    first = pl.program_id(0) == 0
    prev = ap_ref[...] * jax.nn.sigmoid(gp_ref[...])
    hs_ref[0:CV_HALO, :] = jnp.where(first, 0.0, prev)
    hs_ref[CV_HALO:, :] = a_ref[...] * jax.nn.sigmoid(g_ref[...])
    w = w_ref[...]
    for r0 in range(0, tr, CV_SUB):
        acc = jnp.zeros((CV_SUB, GROUP_WIDTH), F32)
        for k in range(CV_WIDTH):
            start = r0 + CV_HALO - (CV_WIDTH - 1) + k
            acc = acc + hs_ref[start:start + CV_SUB, :] * w[k:k + 1, :]
        y = _layer_norm(acc + b_ref[...], lg_ref[...], lb_ref[...])
        o_ref[r0:r0 + CV_SUB, :] = (y * jax.nn.sigmoid(y)).astype(BF16)


def conformer_conv(z, dw_w, dw_b, ln_g, ln_b, tr):
    T = z.shape[0]
    ca = COL_CV // GROUP_WIDTH
    per = tr // CV_HALO

    def halo(col):
        return lambda i: (jnp.maximum(i * per - 1, 0), col)

    return pl.pallas_call(
        _conv_body,
        grid=(T // tr,),
        in_specs=[pl.BlockSpec((tr, GROUP_WIDTH), lambda i: (i, ca)),
                  pl.BlockSpec((tr, GROUP_WIDTH), lambda i: (i, ca + 1)),
                  pl.BlockSpec((CV_HALO, GROUP_WIDTH), halo(ca)),
                  pl.BlockSpec((CV_HALO, GROUP_WIDTH), halo(ca + 1)),
                  pl.BlockSpec((CV_HALO, GROUP_WIDTH), lambda i: (0, 0)),
                  pl.BlockSpec((1, GROUP_WIDTH), lambda i: (0, 0)),
                  pl.BlockSpec((1, GROUP_WIDTH), lambda i: (0, 0)),
                  pl.BlockSpec((1, GROUP_WIDTH), lambda i: (0, 0))],
        out_specs=pl.BlockSpec((tr, GROUP_WIDTH), lambda i: (i, 0)),
        out_shape=jax.ShapeDtypeStruct((T, GROUP_WIDTH), BF16),
        scratch_shapes=[pltpu.VMEM((tr + CV_HALO, GROUP_WIDTH), F32)],
        compiler_params=_cparams(("parallel",)),
        name="conformer_conv",
    )(z, z, z, z, dw_w, dw_b, ln_g, ln_b)


def _rope128(x, cos, sin_signed):
    lo = (_lane(x.shape) & 63) < 32
    rot = jnp.where(lo, pltpu.roll(x, 96, 1), pltpu.roll(x, 32, 1))
    return x * cos + rot * sin_signed


def _low_half(x):
    return jnp.where(_lane(x.shape) < 64, x, 0.0)


def _high_half_to_low(x):
    return jnp.where(_lane(x.shape) < 64, pltpu.roll(x, 64, 1), 0.0)


def _prep_body(qd_ref, kd_ref, vd_ref, qn_ref, kvc_ref, kvs_ref, kvw_ref, gt_ref, cos_ref, sin_ref,
               qd_o, kd_o, vd_o, qn_o, kc_o, vc_o, ks_o, vs_o, kw_o, vw_o, gx_o):
    cos = cos_ref[...]
    sin = sin_ref[...]
    scale = DA_QK_DIM ** -0.5
    for h in range(DA_HEADS):
        cols = slice(h * LANES, (h + 1) * LANES)
        qd_o[h] = (_rope128(qd_ref[:, cols], cos, sin) * scale).astype(BF16)
        kd_o[h] = _rope128(kd_ref[:, cols], cos, sin).astype(BF16)
    vd_o[...] = vd_ref[...].astype(BF16)
    for c in range(NSA_HEADS // 2):
        q = _rope128(qn_ref[:, c * LANES:(c + 1) * LANES], cos, sin) * (NSA_HEAD_DIM ** -0.5)
        qn_o[2 * c] = _low_half(q).astype(BF16)
        qn_o[2 * c + 1] = _high_half_to_low(q).astype(BF16)
    kc_o[...] = _rope128(kvc_ref[:, :LANES], cos, sin)
    vc_o[...] = kvc_ref[:, LANES:]
    for k_o, v_o, ref in ((ks_o, vs_o, kvs_ref), (kw_o, vw_o, kvw_ref)):
        k = _rope128(ref[:, :LANES], cos, sin)
        v = ref[:, LANES:]
        k_o[0] = _low_half(k).astype(BF16)
        k_o[1] = _high_half_to_low(k).astype(BF16)
        v_o[0] = _low_half(v).astype(BF16)
        v_o[1] = _high_half_to_low(v).astype(BF16)
    gate = jax.nn.sigmoid(gt_ref[...])
    low = _lane((gate.shape[0], LANES)) < 64
    for c in range(3):
        for hp in range(NSA_HEADS // 2):
            j = c * NSA_HEADS + 2 * hp
            gx_o[c, :, hp * LANES:(hp + 1) * LANES] = jnp.where(low, gate[:, j:j + 1], gate[:, j + 1:j + 2])


def prep(z, cos128, sin128, tr):
    T = z.shape[0]
    G = NSA_KV_HEADS

    def zspec(width, col):
        return pl.BlockSpec((tr, width), lambda i: (i, col // width))

    def heads(n):
        return pl.BlockSpec((n, tr, LANES), lambda i: (0, i, 0))

    def hshape(n):
        return jax.ShapeDtypeStruct((n, T, LANES), BF16)

    row128 = pl.BlockSpec((tr, LANES), lambda i: (i, 0))
    return pl.pallas_call(
        _prep_body,
        grid=(T // tr,),
        in_specs=[zspec(512, COL_QDA), zspec(512, COL_KDA), zspec(512, COL_VDA), zspec(512, COL_QNS),
                  zspec(256, COL_KVNS), zspec(256, COL_KVNS + 256), zspec(256, COL_KVNS + 512),
                  zspec(128, COL_GNS), row128, row128],
        out_specs=[heads(DA_HEADS), heads(DA_HEADS), pl.BlockSpec((tr, GROUP_WIDTH), lambda i: (i, 0)),
                   heads(NSA_HEADS), row128, row128, heads(G), heads(G), heads(G), heads(G),
                   pl.BlockSpec((3, tr, GROUP_WIDTH), lambda i: (0, i, 0))],
        out_shape=[hshape(DA_HEADS), hshape(DA_HEADS), jax.ShapeDtypeStruct((T, GROUP_WIDTH), BF16),
                   hshape(NSA_HEADS), jax.ShapeDtypeStruct((T, LANES), F32),
                   jax.ShapeDtypeStruct((T, LANES), F32), hshape(G), hshape(G), hshape(G), hshape(G),
                   jax.ShapeDtypeStruct((3, T, GROUP_WIDTH), F32)],
        compiler_params=_cparams(("parallel",)),
        name="attention_prep",
    )(z, z, z, z, z, z, z, z, cos128, sin128)


def _flash_init(m_ref, l_ref, acc_ref):
    m_ref[...] = jnp.full(m_ref.shape, -jnp.inf, F32)
    l_ref[...] = jnp.zeros(l_ref.shape, F32)
    acc_ref[...] = jnp.zeros(acc_ref.shape, F32)


def _flash_step(s, v, m_ref, l_ref, acc_ref):
    m_prev = m_ref[...]
    m_new = jnp.maximum(m_prev, jnp.max(s, axis=-1, keepdims=True))
    alpha = jnp.exp(m_prev - m_new)
    p = jnp.exp(s - m_new)
    l_ref[...] = alpha * l_ref[...] + jnp.sum(p, axis=-1, keepdims=True)
    acc_ref[...] = alpha * acc_ref[...] + _dot(p.astype(BF16), v)
    m_ref[...] = m_new


def _diff_attn_body(q_ref, k_ref, v_ref, lam_ref, sg_ref, o_ref, qs_ref, m_ref, l_ref, acc_ref, *,
                    tq, tk, lambda_init):
    qi = pl.program_id(1)
    ki = pl.program_id(2)

    @pl.when(ki == 0)
    def _():
        q = q_ref[0]
        first = _lane(q.shape) < DA_QK_DIM
        qs_ref[0:tq, :] = jnp.where(first, q, jnp.zeros_like(q))
        qs_ref[tq:, :] = jnp.where(first, jnp.zeros_like(q), q)
        _flash_init(m_ref, l_ref, acc_ref)

    @pl.when(ki * tk <= qi * tq + tq - 1)
    def _():
        s = _dot_nt(qs_ref[...], k_ref[0])
        qpos = qi * tq + (_row(s.shape) & (tq - 1))
        kpos = ki * tk + _lane(s.shape)
        s = jnp.where(kpos <= qpos, s, NEG)
        _flash_step(s, v_ref[...], m_ref, l_ref, acc_ref)

    @pl.when(ki == pl.num_programs(2) - 1)
    def _():
        lam = lam_ref[...]
        lam_full = (jnp.exp(jnp.sum(lam[0:1] * lam[1:2], axis=-1, keepdims=True))
                    - jnp.exp(jnp.sum(lam[2:3] * lam[3:4], axis=-1, keepdims=True)) + lambda_init)
        o = acc_ref[...] / l_ref[...]
        a = o[0:tq] - lam_full * o[tq:]
        o_ref[...] = (_rms(a, sg_ref[...]) * (1.0 - lambda_init)).astype(BF16)


def diff_attention(qd, kd, vd, lam, subln, lambda_init, tq, tk):
    H, T, _ = qd.shape

    def last(qi):
        return (qi * tq + tq - 1) // tk

    return pl.pallas_call(
        functools.partial(_diff_attn_body, tq=tq, tk=tk, lambda_init=lambda_init),
        grid=(H, T // tq, T // tk),
        in_specs=[pl.BlockSpec((1, tq, LANES), lambda h, qi, ki: (h, qi, 0)),
                  pl.BlockSpec((1, tk, LANES), lambda h, qi, ki: (h, jnp.minimum(ki, last(qi)), 0)),
                  pl.BlockSpec((tk, LANES), lambda h, qi, ki: (jnp.minimum(ki, last(qi)), h)),
                  pl.BlockSpec((4, DA_QK_DIM), lambda h, qi, ki: (0, 0)),
                  pl.BlockSpec((1, LANES), lambda h, qi, ki: (0, 0))],
        out_specs=pl.BlockSpec((tq, LANES), lambda h, qi, ki: (qi, h)),
        out_shape=jax.ShapeDtypeStruct((T, GROUP_WIDTH), BF16),
        scratch_shapes=[pltpu.VMEM((2 * tq, LANES), BF16), pltpu.VMEM((2 * tq, 1), F32),
                        pltpu.VMEM((2 * tq, 1), F32), pltpu.VMEM((2 * tq, LANES), F32)],
        compiler_params=_cparams(("parallel", "parallel", "arbitrary")),
        name="diff_attention",
    )(qd, kd, vd, lam, subln)


def _compress_body(x_ref, pe_ref, w1_ref, w2_ref, o_ref):
    x = x_ref[0, 0]
    half = x.shape[1]
    a = _dot((x + pe_ref[0, 0:1, :]).astype(BF16), w1_ref[0, 0:half, :])
    b = _dot((x + pe_ref[0, 1:2, :]).astype(BF16), w1_ref[0, half:, :])
    hid = jax.nn.gelu(a + pltpu.roll(b, b.shape[0] - 1, 0))
    o_ref[0, 0] = _dot(hid.astype(BF16), w2_ref[0]).astype(BF16)


def compress(x16, pe2, w1, w2p):
    _, G, n, half = x16.shape
    hid = w1.shape[2]
    return pl.pallas_call(
        _compress_body,
        grid=(2, G),
        in_specs=[pl.BlockSpec((1, 1, n, half), lambda c, g: (c, g, 0, 0)),
                  pl.BlockSpec((1, 2, half), lambda c, g: (c, 0, 0)),
                  pl.BlockSpec((1, 2 * half, hid), lambda c, g: (c, 0, 0)),
                  pl.BlockSpec((1, hid, LANES), lambda c, g: (c, 0, 0))],
        out_specs=pl.BlockSpec((1, 1, n, LANES), lambda c, g: (c, g, 0, 0)),
        out_shape=jax.ShapeDtypeStruct((2, G, n, LANES), BF16),
        compiler_params=_cparams(("parallel", "parallel")),
        name="nsa_compress",
    )(x16, pe2, w1, w2p)


def _split3(x):
    hi = x.astype(BF16)
    r1 = x - hi.astype(F32)
    mid = r1.astype(BF16)
    lo = (r1 - mid.astype(F32)).astype(BF16)
    return hi, mid, lo


def _pack_heads(o, tq):
    pair = lambda a, b: a + pltpu.roll(b, 64, 1)
    return jnp.concatenate([pair(o[0:tq], o[tq:2 * tq]), pair(o[2 * tq:3 * tq], o[3 * tq:])], axis=-1)


def _cmp_select_body(q_ref, kc_ref, vc_ref, ov_ref, gx_ref, o_ref, sb_ref, *, tq):
    qi = pl.program_id(1)
    hpg = q_ref.shape[0]
    q = q_ref[...].reshape(hpg * tq, LANES)
    s = _dot_nt(q, kc_ref[0, 0])
    t = qi * tq + (_row(s.shape) & (tq - 1))
    cmask = _lane(s.shape) * NSA_CMP_STRIDE + (NSA_CMP_LEN - 1) <= t
    s = jnp.where(cmask, s, NEG)
    e = jnp.exp(s - jnp.max(s, axis=-1, keepdims=True))
    p = jnp.where(cmask, e / jnp.sum(e, axis=-1, keepdims=True), 0.0)
    o = _dot(p.astype(BF16), vc_ref[0, 0])
    o_ref[...] = (gx_ref[0] * _pack_heads(o, tq)).astype(BF16)

    psum = p[0:tq]
    for hh in range(1, hpg):
        psum = psum + p[hh * tq:(hh + 1) * tq]
    ov = ov_ref[...]
    imp = sum(_dot(part, ov) for part in _split3(psum))
    j = _lane(imp.shape)
    cur = (qi * tq + _row(imp.shape)) >> 6
    forced = (j == 0) | (j == cur) | (j == cur - 1)
    score = jnp.where(forced, FORCE, jnp.where(j <= cur, imp, -1.0))
    jf = j.astype(F32)
    sel = jnp.zeros(imp.shape, jnp.bool_)
    for _ in range(NSA_SLC_TOPK):
        m = jnp.max(score, axis=-1, keepdims=True)
        first = jnp.min(jnp.where(score == m, jf, float(LANES)), axis=-1, keepdims=True)
        hit = jf == first
        sel = sel | (hit & (m >= 0.0))
        score = jnp.where(hit, -2.0, score)
    sb_ref[0] = jnp.where(sel, 0.0, NEG).astype(BF16)


def cmp_select(qn, kvc, overlap, gx, tq):
    H, T, _ = qn.shape
    G = NSA_KV_HEADS
    hpg = H // G
    n = kvc.shape[2]
    return pl.pallas_call(
        functools.partial(_cmp_select_body, tq=tq),
        grid=(G, T // tq),
        in_specs=[pl.BlockSpec((hpg, tq, LANES), lambda g, qi: (g, qi, 0)),
                  pl.BlockSpec((1, 1, n, LANES), lambda g, qi: (0, g, 0, 0)),
                  pl.BlockSpec((1, 1, n, LANES), lambda g, qi: (1, g, 0, 0)),
                  pl.BlockSpec((n, LANES), lambda g, qi: (0, 0)),
                  pl.BlockSpec((1, tq, hpg * NSA_HEAD_DIM), lambda g, qi: (0, qi, g))],
        out_specs=[pl.BlockSpec((tq, hpg * NSA_HEAD_DIM), lambda g, qi: (qi, g)),
                   pl.BlockSpec((1, tq, LANES), lambda g, qi: (g, qi, 0))],
        out_shape=[jax.ShapeDtypeStruct((T, GROUP_WIDTH), BF16),
                   jax.ShapeDtypeStruct((G, T, LANES), BF16)],
        compiler_params=_cparams(("parallel", "parallel")),
        name="nsa_compressed_select",
    )(qn, kvc, kvc, overlap, gx)


def _sel_attn_body(q_ref, sb_ref, k_ref, v_ref, gx_ref, o_ref, qs_ref, ks_ref, m_ref, l_ref, acc_ref, *,
                   tq, tk):
    qi = pl.program_id(1)
    ki = pl.program_id(2)
    hpg = q_ref.shape[0]

    @pl.when(ki == 0)
    def _():
        for hh in range(hpg):
            qs_ref[hh * tq:(hh + 1) * tq, 0:LANES] = sb_ref[0]
            qs_ref[hh * tq:(hh + 1) * tq, LANES:] = q_ref[hh]
        _flash_init(m_ref, l_ref, acc_ref)

    @pl.when(ki * tk <= qi * tq + tq - 1)
    def _():
        blk = (ki * tk + _row((tk, LANES))) >> 6
        ks_ref[:, 0:LANES] = jnp.where(blk == _lane((tk, LANES)), 1.0, 0.0).astype(BF16)
        ks_ref[:, LANES:] = k_ref[0]
        s = _dot_nt(qs_ref[...], ks_ref[...])
        qpos = qi * tq + (_row(s.shape) & (tq - 1))
        kpos = ki * tk + _lane(s.shape)
        s = jnp.where(kpos <= qpos, s, NEG)
        _flash_step(s, v_ref[0], m_ref, l_ref, acc_ref)

    @pl.when(ki == pl.num_programs(2) - 1)
    def _():
        o = acc_ref[...] / l_ref[...]
        o_ref[...] = (gx_ref[0] * _pack_heads(o, tq)).astype(BF16)


def sel_attention(qn, selbias, ks, vs, gx, tq, tk):
    H, T, _ = qn.shape
    G = NSA_KV_HEADS
    hpg = H // G

    def last(qi):
        return (qi * tq + tq - 1) // tk

    return pl.pallas_call(
        functools.partial(_sel_attn_body, tq=tq, tk=tk),
        grid=(G, T // tq, T // tk),
        in_specs=[pl.BlockSpec((hpg, tq, LANES), lambda g, qi, ki: (g, qi, 0)),
                  pl.BlockSpec((1, tq, LANES), lambda g, qi, ki: (g, qi, 0)),
                  pl.BlockSpec((1, tk, LANES), lambda g, qi, ki: (g, jnp.minimum(ki, last(qi)), 0)),
                  pl.BlockSpec((1, tk, LANES), lambda g, qi, ki: (g, jnp.minimum(ki, last(qi)), 0)),
                  pl.BlockSpec((1, tq, hpg * NSA_HEAD_DIM), lambda g, qi, ki: (1, qi, g))],
        out_specs=pl.BlockSpec((tq, hpg * NSA_HEAD_DIM), lambda g, qi, ki: (qi, g)),
        out_shape=jax.ShapeDtypeStruct((T, GROUP_WIDTH), BF16),
        scratch_shapes=[pltpu.VMEM((hpg * tq, 2 * LANES), BF16), pltpu.VMEM((tk, 2 * LANES), BF16),
                        pltpu.VMEM((hpg * tq, 1), F32), pltpu.VMEM((hpg * tq, 1), F32),
                        pltpu.VMEM((hpg * tq, LANES), F32)],
        compiler_params=_cparams(("parallel", "parallel", "arbitrary")),
        name="nsa_selected_attention",
    )(qn, selbias, ks, vs, gx)


def _win_attn_body(q_ref, k_ref, v_ref, gx_ref, o_ref, m_ref, l_ref, acc_ref, *, tq, back):
    qi = pl.program_id(1)
    j = pl.program_id(2)
    hpg = q_ref.shape[0]
    kb = qi - back + j

    @pl.when(j == 0)
    def _():
        _flash_init(m_ref, l_ref, acc_ref)

    @pl.when(kb >= 0)
    def _():
        q = q_ref[...].reshape(hpg * tq, LANES)
        s = _dot_nt(q, k_ref[0])
        qpos = qi * tq + (_row(s.shape) & (tq - 1))
        kpos = kb * tq + _lane(s.shape)
        s = jnp.where((kpos <= qpos) & (kpos > qpos - NSA_WINDOW), s, NEG)
        _flash_step(s, v_ref[0], m_ref, l_ref, acc_ref)

    @pl.when(j == pl.num_programs(2) - 1)
    def _():
        o = acc_ref[...] / l_ref[...]
        o_ref[...] = (gx_ref[0] * _pack_heads(o, tq)).astype(BF16)


def win_attention(qn, kw, vw, gx, tq):
    H, T, _ = qn.shape
    G = NSA_KV_HEADS
    hpg = H // G
    back = NSA_WINDOW // tq

    def kblock(qi, j):
        return jnp.maximum(qi - back + j, 0)

    return pl.pallas_call(
        functools.partial(_win_attn_body, tq=tq, back=back),
        grid=(G, T // tq, back + 1),
        in_specs=[pl.BlockSpec((hpg, tq, LANES), lambda g, qi, j: (g, qi, 0)),
                  pl.BlockSpec((1, tq, LANES), lambda g, qi, j: (g, kblock(qi, j), 0)),
                  pl.BlockSpec((1, tq, LANES), lambda g, qi, j: (g, kblock(qi, j), 0)),
                  pl.BlockSpec((1, tq, hpg * NSA_HEAD_DIM), lambda g, qi, j: (2, qi, g))],
        out_specs=pl.BlockSpec((tq, hpg * NSA_HEAD_DIM), lambda g, qi, j: (qi, g)),
        out_shape=jax.ShapeDtypeStruct((T, GROUP_WIDTH), BF16),
        scratch_shapes=[pltpu.VMEM((hpg * tq, 1), F32), pltpu.VMEM((hpg * tq, 1), F32),
                        pltpu.VMEM((hpg * tq, LANES), F32)],
        compiler_params=_cparams(("parallel", "parallel", "arbitrary")),
        name="nsa_window_attention",
    )(qn, kw, vw, gx)


def _out_proj_body(a_ref, b_ref, c_ref, d1_ref, d2_ref, d3_ref, w_ref, x_ref, o_ref, lhs_ref):
    @pl.when(pl.program_id(1) == 0)
    def _():
        lhs_ref[:, 0:GROUP_WIDTH] = a_ref[...]
        lhs_ref[:, GROUP_WIDTH:2 * GROUP_WIDTH] = b_ref[...]
        lhs_ref[:, 2 * GROUP_WIDTH:3 * GROUP_WIDTH] = c_ref[...]
        d = d1_ref[...].astype(F32) + d2_ref[...].astype(F32) + d3_ref[...].astype(F32)
        lhs_ref[:, 3 * GROUP_WIDTH:] = d.astype(BF16)

    o_ref[...] = x_ref[...] + _dot(lhs_ref[...], w_ref[...])


def out_proj(parts, w, x, tm, tn):
    T, N = x.shape
    part = pl.BlockSpec((tm, GROUP_WIDTH), lambda i, j: (i, 0))
    return pl.pallas_call(
        _out_proj_body,
        grid=(T // tm, N // tn),
        in_specs=[part] * 6 + [pl.BlockSpec((4 * GROUP_WIDTH, tn), lambda i, j: (0, j)),
                               pl.BlockSpec((tm, tn), lambda i, j: (i, j))],
        out_specs=pl.BlockSpec((tm, tn), lambda i, j: (i, j)),
        out_shape=jax.ShapeDtypeStruct((T, N), F32),
        scratch_shapes=[pltpu.VMEM((tm, 4 * GROUP_WIDTH), BF16)],
        compiler_params=_cparams(("parallel", "arbitrary")),
        name="out_proj",
    )(*parts, w, x)


def _up_body(te_ref, na_ref, x_ref, g_ref, wg_ref, wu_ref, o_ref, hn_ref):
    i = pl.program_id(0)

    @pl.when(i < na_ref[0])
    def _():
        @pl.when(pl.program_id(1) == 0)
        def _():
            hn_ref[...] = _rms(x_ref[...], g_ref[...]).astype(BF16)

        h = hn_ref[...]
        a = _dot(h, wg_ref[0])
        o_ref[...] = (a * jax.nn.sigmoid(a) * _dot(h, wu_ref[0])).astype(BF16)

    @pl.when(i >= na_ref[0])
    def _():
        o_ref[...] = jnp.zeros(o_ref.shape, BF16)


def swiglu_up(tile_expert, n_active, x, g, wg, wu, tm, tn):
    P, K = x.shape
    F = wg.shape[2]
    nj = F // tn

    def row(i, j, te, na):
        return (jnp.minimum(i, na[0] - 1), 0)

    def wmap(i, j, te, na):
        return (te[i], 0, jnp.where(i < na[0], j, nj - 1))

    return pl.pallas_call(
        _up_body,
        grid_spec=pltpu.PrefetchScalarGridSpec(
            num_scalar_prefetch=2,
            grid=(P // tm, nj),
            in_specs=[pl.BlockSpec((tm, K), row),
                      pl.BlockSpec((1, K), lambda i, j, te, na: (0, 0)),
                      pl.BlockSpec((1, K, tn), wmap),
                      pl.BlockSpec((1, K, tn), wmap)],
            out_specs=pl.BlockSpec((tm, tn), lambda i, j, te, na: (i, j)),
            scratch_shapes=[pltpu.VMEM((tm, K), BF16)]),
        out_shape=jax.ShapeDtypeStruct((P, F), BF16),
        compiler_params=_cparams(("arbitrary", "arbitrary")),
        name="swiglu_up",
    )(tile_expert, n_active, x, g, wg, wu)


def _down_res_body(te_ref, na_ref, a_ref, w_ref, r_ref, o_ref):
    o_ref[...] = r_ref[...] + _dot(a_ref[...], w_ref[0])


def _down_scale_body(te_ref, na_ref, a_ref, w_ref, s_ref, o_ref):
    o_ref[...] = s_ref[...] * _dot(a_ref[...], w_ref[0])


def swiglu_down(tile_expert, n_active, act, wd, extra, tm, tn, scaled):
    P, F = act.shape
    N = wd.shape[2]
    nj = N // tn

    def row(i, j, te, na):
        return (jnp.minimum(i, na[0] - 1), 0)

    def wmap(i, j, te, na):
        return (te[i], 0, jnp.where(i < na[0], j, nj - 1))

    if scaled:
        body = _down_scale_body
        extra_spec = pl.BlockSpec((tm, 1), lambda i, j, te, na: (i, 0))
    else:
        body = _down_res_body
        extra_spec = pl.BlockSpec((tm, tn), lambda i, j, te, na: (i, j))
    return pl.pallas_call(
        body,
        grid_spec=pltpu.PrefetchScalarGridSpec(
            num_scalar_prefetch=2,
            grid=(P // tm, nj),
            in_specs=[pl.BlockSpec((tm, F), row), pl.BlockSpec((1, F, tn), wmap), extra_spec],
            out_specs=pl.BlockSpec((tm, tn), lambda i, j, te, na: (i, j))),
        out_shape=jax.ShapeDtypeStruct((P, N), F32),
        compiler_params=_cparams(("arbitrary", "arbitrary")),
        name="swiglu_down",
    )(tile_expert, n_active, act, wd, extra)


def _router_body(x_ref, g_ref, w_ref, o_ref):
    h = _rms(x_ref[...], g_ref[...])
    logits = jnp.dot(h, w_ref[...], preferred_element_type=F32, precision=lax.Precision.HIGHEST)
    lane = _lane(logits.shape)
    lf = lane.astype(F32)
    logits = jnp.where(lane < N_EXPERTS, logits, -jnp.inf)
    v0 = jnp.max(logits, axis=-1, keepdims=True)
    i0 = jnp.min(jnp.where(logits == v0, lf, float(LANES)), axis=-1, keepdims=True)
    rest = jnp.where(lf == i0, -jnp.inf, logits)
    v1 = jnp.max(rest, axis=-1, keepdims=True)
    i1 = jnp.min(jnp.where(rest == v1, lf, float(LANES)), axis=-1, keepdims=True)
    e1 = jnp.exp(v1 - v0)
    w0 = 1.0 / (1.0 + e1)
    w1 = e1 / (1.0 + e1)
    o_ref[...] = jnp.where(lane == 0, i0, jnp.where(lane == 1, i1, jnp.where(lane == 2, w0, w1)))


def router(x, g, w_pad, tm):
    T, K = x.shape
    return pl.pallas_call(
        _router_body,
        grid=(T // tm,),
        in_specs=[pl.BlockSpec((tm, K), lambda i: (i, 0)),
                  pl.BlockSpec((1, K), lambda i: (0, 0)),
                  pl.BlockSpec((K, LANES), lambda i: (0, 0))],
        out_specs=pl.BlockSpec((tm, LANES), lambda i: (i, 0)),
        out_shape=jax.ShapeDtypeStruct((T, LANES), F32),
        compiler_params=_cparams(("parallel",)),
        name="moe_router",
    )(x, g, w_pad)


def _row_copy(src_hbm, row, dst_ref, r, sem):
    return pltpu.make_async_copy(src_hbm.at[pl.ds(row, 1)], dst_ref.at[pl.ds(r, 1)], sem)


def _gather_rows(idx_ref, base, src_hbm, dst_ref, sem):
    n = dst_ref.shape[0]

    def start(r, c):
        _row_copy(src_hbm, idx_ref[base + r], dst_ref, r, sem).start()
        return c

    def wait(r, c):
        _row_copy(src_hbm, 0, dst_ref, r, sem).wait()
        return c

    lax.fori_loop(0, n, start, 0)
    lax.fori_loop(0, n, wait, 0)


def _gather_body(tok_ref, na_ref, x_hbm, o_ref, sem):
    i = pl.program_id(0)

    @pl.when(i < na_ref[0])
    def _():
        _gather_rows(tok_ref, i * o_ref.shape[0], x_hbm, o_ref, sem)

    @pl.when(i >= na_ref[0])
    def _():
        o_ref[...] = jnp.zeros(o_ref.shape, o_ref.dtype)


def gather_tokens(tok_of_slot, n_active, x, tm):
    P = tok_of_slot.shape[0]
    K = x.shape[1]
    return pl.pallas_call(
        _gather_body,
        grid_spec=pltpu.PrefetchScalarGridSpec(
            num_scalar_prefetch=2,
            grid=(P // tm,),
            in_specs=[pl.BlockSpec(memory_space=pl.ANY)],
            out_specs=pl.BlockSpec((tm, K), lambda i, tok, na: (i, 0)),
            scratch_shapes=[pltpu.SemaphoreType.DMA(())]),
        out_shape=jax.ShapeDtypeStruct((P, K), x.dtype),
        compiler_params=_cparams(("arbitrary",)),
        name="moe_gather",
    )(tok_of_slot, n_active, x)


def _combine_body(s0_ref, s1_ref, x_ref, y_hbm, g_ref, o_ref, b0_ref, b1_ref, sem0, sem1, *, final):
    tm = x_ref.shape[0]
    base = pl.program_id(0) * tm

    def start(r, c):
        _row_copy(y_hbm, s0_ref[base + r], b0_ref, r, sem0).start()
        _row_copy(y_hbm, s1_ref[base + r], b1_ref, r, sem1).start()
        return c

    def wait(r, c):
        _row_copy(y_hbm, 0, b0_ref, r, sem0).wait()
        _row_copy(y_hbm, 0, b1_ref, r, sem1).wait()
        return c

    lax.fori_loop(0, tm, start, 0)
    lax.fori_loop(0, tm, wait, 0)
    y = x_ref[...] + b0_ref[...] + b1_ref[...]
    o_ref[...] = _rms(y, g_ref[...]) if final else y


def moe_combine(slot0, slot1, x, ys, g, tm, final):
    T, K = x.shape
    return pl.pallas_call(
        functools.partial(_combine_body, final=final),
        grid_spec=pltpu.PrefetchScalarGridSpec(
            num_scalar_prefetch=2,
            grid=(T // tm,),
            in_specs=[pl.BlockSpec((tm, K), lambda i, s0, s1: (i, 0)),
                      pl.BlockSpec(memory_space=pl.ANY),
                      pl.BlockSpec((1, K), lambda i, s0, s1: (0, 0))],
            out_specs=pl.BlockSpec((tm, K), lambda i, s0, s1: (i, 0)),
            scratch_shapes=[pltpu.VMEM((tm, K), F32), pltpu.VMEM((tm, K), F32),
                            pltpu.SemaphoreType.DMA(()), pltpu.SemaphoreType.DMA(())]),
        out_shape=jax.ShapeDtypeStruct((T, K), F32),
        compiler_params=_cparams(("arbitrary",)),
        name="moe_combine",
    )(slot0, slot1, x, ys, g)


def _final_norm_body(x_ref, g_ref, o_ref):
    o_ref[...] = _rms(x_ref[...], g_ref[...])


def final_norm(x, g, tm):
    T, K = x.shape
    return pl.pallas_call(
        _final_norm_body,
        grid=(T // tm,),
        in_specs=[pl.BlockSpec((tm, K), lambda i: (i, 0)), pl.BlockSpec((1, K), lambda i: (0, 0))],
        out_specs=pl.BlockSpec((tm, K), lambda i: (i, 0)),
        out_shape=jax.ShapeDtypeStruct((T, K), F32),
        compiler_params=_cparams(("parallel",)),
        name="final_norm",
    )(x, g)


def _routing_tables(route, tm):
    T = route.shape[0]
    top_i = route[:, 0:2].astype(jnp.int32)
    gate = route[:, 2:4]
    e_flat = top_i.reshape(-1)
    onehot = (e_flat[:, None] == jnp.arange(N_EXPERTS)[None, :]).astype(jnp.int32)
    rank = jnp.take_along_axis(jnp.cumsum(onehot, axis=0) - onehot, e_flat[:, None], axis=1)[:, 0]
    count = jnp.sum(onehot, axis=0)
    padded = ((count + tm - 1) // tm) * tm
    end = jnp.cumsum(padded)
    start = end - padded
    slot = start[e_flat] + rank
    n_slots = 2 * T + N_EXPERTS * tm
    n_tiles = n_slots // tm
    n_active = (end[-1] // tm).astype(jnp.int32)
    tile_start = jnp.minimum(jnp.arange(n_tiles, dtype=jnp.int32), n_active - 1) * tm
    tile_expert = jnp.minimum(jnp.sum(tile_start[:, None] >= end[None, :], axis=1), N_EXPERTS - 1)
    tok_of_slot = jnp.zeros((n_slots,), jnp.int32).at[slot].set(jnp.arange(2 * T, dtype=jnp.int32) // 2)
    gate_of_slot = jnp.zeros((n_slots,), F32).at[slot].set(gate.reshape(-1))
    slot2 = slot.reshape(T, 2).astype(jnp.int32)
    return (tile_expert.astype(jnp.int32), n_active.reshape(1), tok_of_slot, gate_of_slot[:, None],
            slot2[:, 0], slot2[:, 1])


def _rope_tables128(T):
    half = NSA_HEAD_DIM // 2
    inv = ROPE_THETA ** (-jnp.arange(0, NSA_HEAD_DIM, 2, dtype=F32) / NSA_HEAD_DIM)
    ang = jnp.arange(T, dtype=F32)[:, None] * inv[None, :]
    cos, sin = jnp.cos(ang), jnp.sin(ang)
    cos128 = jnp.tile(cos, (1, LANES // half))
    sin128 = jnp.tile(jnp.concatenate([-sin, sin], axis=1), (1, LANES // NSA_HEAD_DIM))
    return cos128, sin128


def _overlap_matrix(n_cmp_pad):
    cstart = jnp.arange(n_cmp_pad) * NSA_CMP_STRIDE
    sstart = jnp.arange(LANES) * NSA_SLC_LEN
    ov = (cstart[:, None] < sstart[None, :] + NSA_SLC_LEN) & (cstart[:, None] + NSA_CMP_LEN > sstart[None, :])
    return ov.astype(BF16)


def kernel(x, attn_norm, w_in, w_out, gm_ln_g, gm_ln_b, gm_ws, gm_bs, da_lambda, da_subln, cv_dw_w, cv_dw_b,
           cv_ln_g, cv_ln_b, nsa_cmp_w1, nsa_cmp_w2, nsa_cmp_pe, ffn_norm, ffn_wg, ffn_wu, ffn_wd, router_w,
           exp_wg, exp_wu, exp_wd, final_norm_g):
    B, T, D = x.shape
    assert B == 1 and D == D_MODEL and T % 512 == 0 and T // NSA_SLC_LEN <= LANES
    depth = w_in.shape[0]
    G = NSA_KV_HEADS
    n16 = T // NSA_CMP_STRIDE
    tm = 512

    cos128, sin128 = _rope_tables128(T)
    overlap = _overlap_matrix(n16)
    dense_te = jnp.zeros((T // tm,), jnp.int32)
    dense_na = jnp.full((1,), T // tm, jnp.int32)
    row = lambda v: v.reshape(1, -1)

    xs = x[0]
    for l in range(depth):
        lambda_init = 0.8 - 0.6 * math.exp(-0.3 * l)
        w_in_l = jnp.pad(w_in[l], ((0, 0), (0, IN_WIDTH_PAD - IN_WIDTH))).astype(BF16)
        z = norm_mm(xs, row(attn_norm[l]), w_in_l, tm, IN_WIDTH_PAD // 3)

        bs_rows = jnp.repeat(gm_bs[l].T, GM_CHUNK, axis=1)
        o_a = gmlp(z, row(gm_ln_g[l]), row(gm_ln_b[l]), gm_ws[l], bs_rows, 512)
        dw_w = jnp.pad(cv_dw_w[l], ((0, CV_HALO - CV_WIDTH), (0, 0)))
        o_c = conformer_conv(z, dw_w, row(cv_dw_b[l]), row(cv_ln_g[l]), row(cv_ln_b[l]), 256)

        qd, kd, vd, qn, kc, vc, ks, vs, kw, vw, gx = prep(z, cos128, sin128, 256)
        o_b = diff_attention(qd, kd, vd, da_lambda[l], row(da_subln[l]), lambda_init, 256, 512)

        x16 = jnp.stack([kc, vc]).reshape(2, T, G, NSA_HEAD_DIM).transpose(0, 2, 1, 3)
        x16 = x16.reshape(2, G, n16, NSA_CMP_STRIDE * NSA_HEAD_DIM)
        pe2 = nsa_cmp_pe[l].reshape(2, 2, NSA_CMP_STRIDE * NSA_HEAD_DIM)
        w2p = jnp.pad(nsa_cmp_w2[l], ((0, 0), (0, 0), (0, LANES - NSA_HEAD_DIM))).astype(BF16)
        kvc = compress(x16, pe2, nsa_cmp_w1[l].astype(BF16), w2p)
        o_cmp, selbias = cmp_select(qn, kvc, overlap, gx, 128)
        o_sel = sel_attention(qn, selbias, ks, vs, gx, 128, 512)
        o_win = win_attention(qn, kw, vw, gx, 256)

        xs = out_proj((o_a, o_b, o_c, o_cmp, o_sel, o_win), w_out[l].astype(BF16), xs, tm, 512)

        g_ffn = row(ffn_norm[l])
        if l % 2 == 0:
            e = l // 2
            act = swiglu_up(dense_te, dense_na, xs, g_ffn, ffn_wg[e][None].astype(BF16),
                            ffn_wu[e][None].astype(BF16), tm, 512)
            xs = swiglu_down(dense_te, dense_na, act, ffn_wd[e][None].astype(BF16), xs, tm, 512, scaled=False)
            if l == depth - 1:
                xs = final_norm(xs, row(final_norm_g), tm)
        else:
            e = l // 2
            w_r = jnp.pad(router_w[e], ((0, 0), (0, LANES - N_EXPERTS)))
            route = router(xs, g_ffn, w_r, tm)
            tile_expert, n_active, tok_of_slot, gate_of_slot, slot0, slot1 = _routing_tables(route, tm)
            xg = gather_tokens(tok_of_slot, n_active, xs, tm)
            act = swiglu_up(tile_expert, n_active, xg, g_ffn, exp_wg[e].astype(BF16), exp_wu[e].astype(BF16),
                            tm, 512)
            ys = swiglu_down(tile_expert, n_active, act, exp_wd[e].astype(BF16), gate_of_slot, tm, 512,
                             scaled=True)
            xs = moe_combine(slot0, slot1, xs, ys, row(final_norm_g), 256, final=(l == depth - 1))
    return xs[None]
```

```python
import functools
import math

import jax
import jax.numpy as jnp
from jax import lax
from jax.experimental import pallas as pl
from jax.experimental.pallas import tpu as pltpu

F32 = jnp.float32
BF16 = jnp.bfloat16

D_MODEL = 2048
GROUP_WIDTH = 512
GM_CHUNK = 128
GM_HEADS = 4
DA_HEADS = 4
DA_QK_DIM = 64
CV_WIDTH = 31
NSA_HEADS = 8
NSA_KV_HEADS = 2
NSA_HEAD_DIM = 64
NSA_CMP_LEN = 32
NSA_CMP_STRIDE = 16
NSA_SLC_LEN = 64
NSA_SLC_TOPK = 16
NSA_WINDOW = 512
ROPE_THETA = 10000.0
NORM_EPS = 1e-6
NEG = -1e30
FORCE = 1e9
N_EXPERTS = 8
LANES = 128
LOG2E = math.log2(math.e)

IN_WIDTH = 4888
IN_WIDTH_PAD = 4992
COL_GM = 0
COL_QDA = 1024
COL_KDA = 1536
COL_VDA = 2048
COL_CV = 2560
COL_QNS = 3584
COL_KVNS = 4096
COL_GNS = 4864

VMEM_LIMIT = 56 * 1024 * 1024


def _cparams(sem):
    return pltpu.CompilerParams(dimension_semantics=sem, vmem_limit_bytes=VMEM_LIMIT)


def _rms(x, g):
    ms = jnp.mean(x * x, axis=-1, keepdims=True)
    return x * lax.rsqrt(ms + NORM_EPS) * g


def _layer_norm(x, g, b):
    mu = jnp.mean(x, axis=-1, keepdims=True)
    xc = x - mu
    var = jnp.mean(xc * xc, axis=-1, keepdims=True)
    return xc * lax.rsqrt(var + NORM_EPS) * g + b


def _dot(a, b):
    return jnp.dot(a, b, preferred_element_type=F32)


def _dot_nt(a, b):
    return lax.dot_general(a, b, (((1,), (1,)), ((), ())), preferred_element_type=F32)


def _lane(shape):
    return lax.broadcasted_iota(jnp.int32, shape, len(shape) - 1)


def _row(shape):
    return lax.broadcasted_iota(jnp.int32, shape, len(shape) - 2)


def _norm_mm_body(x_ref, g_ref, w_ref, o_ref, hn_ref):
    @pl.when(pl.program_id(1) == 0)
    def _():
        hn_ref[...] = _rms(x_ref[...], g_ref[...]).astype(BF16)

    o_ref[...] = _dot(hn_ref[...], w_ref[...])


def norm_mm(x, g, w, tm, tn):
    T, K = x.shape
    N = w.shape[1]
    return pl.pallas_call(
        _norm_mm_body,
        grid=(T // tm, N // tn),
        in_specs=[pl.BlockSpec((tm, K), lambda i, j: (i, 0)),
                  pl.BlockSpec((1, K), lambda i, j: (0, 0)),
                  pl.BlockSpec((K, tn), lambda i, j: (0, j))],
        out_specs=pl.BlockSpec((tm, tn), lambda i, j: (i, j)),
        out_shape=jax.ShapeDtypeStruct((T, N), F32),
        scratch_shapes=[pltpu.VMEM((tm, K), BF16)],
        compiler_params=_cparams(("parallel", "arbitrary")),
        name="norm_in_proj",
    )(x, g, w)


def _gmlp_body(z_ref, g_ref, b_ref, ws_ref, bs_ref, o_ref):
    tr = z_ref.shape[0]
    z = jax.nn.gelu(z_ref[...])
    u = z[:, :GROUP_WIDTH]
    v = _layer_norm(z[:, GROUP_WIDTH:], g_ref[...], b_ref[...]).astype(BF16)
    causal = _row((GM_CHUNK, GM_CHUNK)) >= _lane((GM_CHUNK, GM_CHUNK))
    bias = bs_ref[...]
    for h in range(GM_HEADS):
        w = jnp.where(causal, ws_ref[h], 0.0).astype(BF16)
        cols = slice(h * LANES, (h + 1) * LANES)
        for c in range(tr // GM_CHUNK):
            rows = slice(c * GM_CHUNK, (c + 1) * GM_CHUNK)
            s = _dot(w, v[rows, cols]) + bias[:, cols]
            o_ref[rows, cols] = (u[rows, cols] * s).astype(BF16)


def gmlp(z, ln_g, ln_b, ws, bs_rows, tr):
    T = z.shape[0]
    return pl.pallas_call(
        _gmlp_body,
        grid=(T // tr,),
        in_specs=[pl.BlockSpec((tr, 2 * GROUP_WIDTH), lambda i: (i, COL_GM // (2 * GROUP_WIDTH))),
                  pl.BlockSpec((1, GROUP_WIDTH), lambda i: (0, 0)),
                  pl.BlockSpec((1, GROUP_WIDTH), lambda i: (0, 0)),
                  pl.BlockSpec((GM_HEADS, GM_CHUNK, GM_CHUNK), lambda i: (0, 0, 0)),
                  pl.BlockSpec((GM_CHUNK, GROUP_WIDTH), lambda i: (0, 0))],
        out_specs=pl.BlockSpec((tr, GROUP_WIDTH), lambda i: (i, 0)),
        out_shape=jax.ShapeDtypeStruct((T, GROUP_WIDTH), BF16),
        compiler_params=_cparams(("parallel",)),
        name="gmlp",
    )(z, ln_g, ln_b, ws, bs_rows)


CV_HALO = 32
CV_SUB = 64


def _conv_body(a_ref, g_ref, ap_ref, gp_ref, w_ref, b_ref, lg_ref, lb_ref, o_ref, hs_ref):
    tr = a_ref.shape[0]
    first = pl.program_id(0) == 0
    prev = ap_ref[...] * jax.nn.sigmoid(gp_ref[...])
    hs_ref[0:CV_HALO, :] = jnp.where(first, 0.0, prev)
    hs_ref[CV_HALO:, :] = a_ref[...] * jax.nn.sigmoid(g_ref[...])
    w = w_ref[...]
    for r0 in range(0, tr, CV_SUB):
        acc = jnp.zeros((CV_SUB, GROUP_WIDTH), F32)
        for k in range(CV_WIDTH):
            start = r0 + CV_HALO - (CV_WIDTH - 1) + k
            acc = acc + hs_ref[start:start + CV_SUB, :] * w[k:k + 1, :]
        y = _layer_norm(acc + b_ref[...], lg_ref[...], lb_ref[...])
        o_ref[r0:r0 + CV_SUB, :] = (y * jax.nn.sigmoid(y)).astype(BF16)


def conformer_conv(z, dw_w, dw_b, ln_g, ln_b, tr):
    T = z.shape[0]
    ca = COL_CV // GROUP_WIDTH
    per = tr // CV_HALO

    def halo(col):
        return lambda i: (jnp.maximum(i * per - 1, 0), col)

    return pl.pallas_call(
        _conv_body,
        grid=(T // tr,),
        in_specs=[pl.BlockSpec((tr, GROUP_WIDTH), lambda i: (i, ca)),
                  pl.BlockSpec((tr, GROUP_WIDTH), lambda i: (i, ca + 1)),
                  pl.BlockSpec((CV_HALO, GROUP_WIDTH), halo(ca)),
                  pl.BlockSpec((CV_HALO, GROUP_WIDTH), halo(ca + 1)),
                  pl.BlockSpec((CV_HALO, GROUP_WIDTH), lambda i: (0, 0)),
                  pl.BlockSpec((1, GROUP_WIDTH), lambda i: (0, 0)),
                  pl.BlockSpec((1, GROUP_WIDTH), lambda i: (0, 0)),
                  pl.BlockSpec((1, GROUP_WIDTH), lambda i: (0, 0))],
        out_specs=pl.BlockSpec((tr, GROUP_WIDTH), lambda i: (i, 0)),
        out_shape=jax.ShapeDtypeStruct((T, GROUP_WIDTH), BF16),
        scratch_shapes=[pltpu.VMEM((tr + CV_HALO, GROUP_WIDTH), F32)],
        compiler_params=_cparams(("parallel",)),
        name="conformer_conv",
    )(z, z, z, z, dw_w, dw_b, ln_g, ln_b)


def _rope128(x, cos, sin_signed):
    lo = (_lane(x.shape) & 63) < 32
    rot = jnp.where(lo, pltpu.roll(x, 96, 1), pltpu.roll(x, 32, 1))
    return x * cos + rot * sin_signed


def _low_half(x, fill=0.0):
    return jnp.where(_lane(x.shape) < 64, x, fill)


def _high_half_to_low(x, fill=0.0):
    return jnp.where(_lane(x.shape) < 64, pltpu.roll(x, 64, 1), fill)


def _prep_body(qd_ref, kd_ref, vd_ref, qn_ref, kvc_ref, kvs_ref, kvw_ref, gt_ref, cos_ref, sin_ref,
               qd_o, kd_o, vd_o, qn_o, kc_o, vc_o, ks_o, vs_o, kw_o, vw_o, gx_o):
    tr = cos_ref.shape[0]
    cos = cos_ref[...]
    sin = sin_ref[...]
    q_scale = DA_QK_DIM ** -0.5 * LOG2E
    ones = jnp.ones((tr, LANES), BF16)
    for h in range(DA_HEADS):
        cols = slice(h * LANES, (h + 1) * LANES)
        qd_o[h] = (_rope128(qd_ref[:, cols], cos, sin) * q_scale).astype(BF16)
        kd_o[h] = _rope128(kd_ref[:, cols], cos, sin).astype(BF16)
        vd_o[h, :, 0:LANES] = vd_ref[:, cols].astype(BF16)
        vd_o[h, :, LANES:] = ones
    for c in range(NSA_HEADS // 2):
        q = _rope128(qn_ref[:, c * LANES:(c + 1) * LANES], cos, sin) * (NSA_HEAD_DIM ** -0.5 * LOG2E)
        qn_o[2 * c] = _low_half(q).astype(BF16)
        qn_o[2 * c + 1] = _high_half_to_low(q).astype(BF16)
    kc_o[...] = _rope128(kvc_ref[:, :LANES], cos, sin)
    vc_o[...] = kvc_ref[:, LANES:]
    blk = (pl.program_id(0) * tr + _row((tr, LANES))) >> 6
    onehot = jnp.where(blk == _lane((tr, LANES)), 1.0, 0.0).astype(BF16)
    k = _rope128(kvs_ref[:, :LANES], cos, sin)
    for g, half in enumerate((_low_half, _high_half_to_low)):
        ks_o[g, :, 0:LANES] = onehot
        ks_o[g, :, LANES:] = half(k).astype(BF16)
        vs_o[g] = half(kvs_ref[:, LANES:], 1.0).astype(BF16)
    k = _rope128(kvw_ref[:, :LANES], cos, sin)
    for g, half in enumerate((_low_half, _high_half_to_low)):
        kw_o[g] = half(k).astype(BF16)
        vw_o[g] = half(kvw_ref[:, LANES:], 1.0).astype(BF16)
    gate = jax.nn.sigmoid(gt_ref[...])
    low = _lane((tr, LANES)) < 64
    for c in range(3):
        for hp in range(NSA_HEADS // 2):
            j = c * NSA_HEADS + 2 * hp
            gx_o[c, :, hp * LANES:(hp + 1) * LANES] = jnp.where(low, gate[:, j:j + 1], gate[:, j + 1:j + 2])


def prep(z, cos128, sin128, tr):
    T = z.shape[0]
    G = NSA_KV_HEADS

    def zspec(width, col):
        return pl.BlockSpec((tr, width), lambda i: (i, col // width))

    def heads(n, width=LANES):
        return pl.BlockSpec((n, tr, width), lambda i: (0, i, 0))

    def hshape(n, width=LANES):
        return jax.ShapeDtypeStruct((n, T, width), BF16)

    row128 = pl.BlockSpec((tr, LANES), lambda i: (i, 0))
    return pl.pallas_call(
        _prep_body,
        grid=(T // tr,),
        in_specs=[zspec(512, COL_QDA), zspec(512, COL_KDA), zspec(512, COL_VDA), zspec(512, COL_QNS),
                  zspec(256, COL_KVNS), zspec(256, COL_KVNS + 256), zspec(256, COL_KVNS + 512),
                  zspec(128, COL_GNS), row128, row128],
        out_specs=[heads(DA_HEADS), heads(DA_HEADS), heads(DA_HEADS, 2 * LANES),
                   heads(NSA_HEADS), row128, row128, heads(G, 2 * LANES), heads(G), heads(G), heads(G),
                   pl.BlockSpec((3, tr, GROUP_WIDTH), lambda i: (0, i, 0))],
        out_shape=[hshape(DA_HEADS), hshape(DA_HEADS), hshape(DA_HEADS, 2 * LANES),
                   hshape(NSA_HEADS), jax.ShapeDtypeStruct((T, LANES), F32),
                   jax.ShapeDtypeStruct((T, LANES), F32), hshape(G, 2 * LANES), hshape(G), hshape(G), hshape(G),
                   jax.ShapeDtypeStruct((3, T, GROUP_WIDTH), F32)],
        compiler_params=_cparams(("parallel",)),
        name="attention_prep",
    )(z, z, z, z, z, z, z, z, cos128, sin128)


def _lane_tile(x, n):
    return x if n == 1 else jnp.concatenate([x] * n, axis=-1)


def _softmax_tile(qs_ref, k, v, m_ref, acc_ref, mask):
    s = _dot_nt(qs_ref[...], k)
    if mask is not None:
        s = jnp.where(mask(s.shape), s, NEG)
    m_prev = m_ref[...]
    m_new = jnp.maximum(m_prev, jnp.max(s, axis=-1, keepdims=True))
    alpha = jnp.exp2(m_prev - m_new)
    p = jnp.exp2(s - _lane_tile(m_new, s.shape[1] // LANES))
    acc_ref[...] = _lane_tile(alpha, acc_ref.shape[1] // LANES) * acc_ref[...] + _dot(p.astype(BF16), v)
    m_ref[...] = m_new


def _causal_sweep(qs_ref, k_ref, v_ref, m_ref, acc_ref, q0, tq, tk):
    m_ref[...] = jnp.full(m_ref.shape, -jnp.inf, F32)
    acc_ref[...] = jnp.zeros(acc_ref.shape, F32)

    def tile(start, mask):
        start = pl.multiple_of(start, tk)
        _softmax_tile(qs_ref, k_ref[0, pl.ds(start, tk), :], v_ref[0, pl.ds(start, tk), :], m_ref, acc_ref, mask)

    n_full = q0 // tk

    def pair(i, c):
        tile(2 * i * tk, None)
        tile((2 * i + 1) * tk, None)
        return c

    lax.fori_loop(0, n_full // 2, pair, 0)

    @pl.when(n_full % 2 == 1)
    def _():
        tile((n_full - 1) * tk, None)

    def diagonal(shape):
        return n_full * tk + _lane(shape) <= q0 + (_row(shape) & (tq - 1))

    tile(n_full * tk, diagonal)


def _diff_attn_body(q_ref, k_ref, v_ref, lam_ref, sg_ref, o_ref, qs_ref, m_ref, acc_ref, *,
                    tq, tk, lambda_init):
    q = q_ref[0]
    first = _lane(q.shape) < DA_QK_DIM
    qs_ref[0:tq, :] = jnp.where(first, q, jnp.zeros_like(q))
    qs_ref[tq:, :] = jnp.where(first, jnp.zeros_like(q), q)
    _causal_sweep(qs_ref, k_ref, v_ref, m_ref, acc_ref, pl.program_id(1) * tq, tq, tk)
    lam = lam_ref[...]
    lam_full = (jnp.exp(jnp.sum(lam[0:1] * lam[1:2], axis=-1, keepdims=True))
                - jnp.exp(jnp.sum(lam[2:3] * lam[3:4], axis=-1, keepdims=True)) + lambda_init)
    o = acc_ref[:, 0:LANES] / acc_ref[:, LANES:]
    a = o[0:tq] - lam_full * o[tq:]
    o_ref[...] = (_rms(a, sg_ref[...]) * (1.0 - lambda_init)).astype(BF16)


def diff_attention(qd, kd, vd, lam, subln, lambda_init, tq, tk):
    H, T, _ = qd.shape
    return pl.pallas_call(
        functools.partial(_diff_attn_body, tq=tq, tk=tk, lambda_init=lambda_init),
        grid=(H, T // tq),
        in_specs=[pl.BlockSpec((1, tq, LANES), lambda h, qi: (h, qi, 0)),
                  pl.BlockSpec((1, T, LANES), lambda h, qi: (h, 0, 0)),
                  pl.BlockSpec((1, T, 2 * LANES), lambda h, qi: (h, 0, 0)),
                  pl.BlockSpec((4, DA_QK_DIM), lambda h, qi: (0, 0)),
                  pl.BlockSpec((1, LANES), lambda h, qi: (0, 0))],
        out_specs=pl.BlockSpec((tq, LANES), lambda h, qi: (qi, h)),
        out_shape=jax.ShapeDtypeStruct((T, GROUP_WIDTH), BF16),
        scratch_shapes=[pltpu.VMEM((2 * tq, LANES), BF16), pltpu.VMEM((2 * tq, LANES), F32),
                        pltpu.VMEM((2 * tq, 2 * LANES), F32)],
        compiler_params=_cparams(("parallel", "arbitrary")),
        name="diff_attention",
    )(qd, kd, vd, lam, subln)


def _compress_body(x_ref, pe_ref, w1_ref, w2_ref, o_ref):
    x = x_ref[0, 0]
    half = x.shape[1]
    a = _dot((x + pe_ref[0, 0:1, :]).astype(BF16), w1_ref[0, 0:half, :])
    b = _dot((x + pe_ref[0, 1:2, :]).astype(BF16), w1_ref[0, half:, :])
    hid = jax.nn.gelu(a + pltpu.roll(b, b.shape[0] - 1, 0))
    o_ref[0, 0] = _dot(hid.astype(BF16), w2_ref[0]).astype(BF16)


def compress(x16, pe2, w1, w2p):
    _, G, n, half = x16.shape
    hid = w1.shape[2]
    return pl.pallas_call(
        _compress_body,
        grid=(2, G),
        in_specs=[pl.BlockSpec((1, 1, n, half), lambda c, g: (c, g, 0, 0)),
                  pl.BlockSpec((1, 2, half), lambda c, g: (c, 0, 0)),
                  pl.BlockSpec((1, 2 * half, hid), lambda c, g: (c, 0, 0)),
                  pl.BlockSpec((1, hid, LANES), lambda c, g: (c, 0, 0))],
        out_specs=pl.BlockSpec((1, 1, n, LANES), lambda c, g: (c, g, 0, 0)),
        out_shape=jax.ShapeDtypeStruct((2, G, n, LANES), BF16),
        compiler_params=_cparams(("parallel", "parallel")),
        name="nsa_compress",
    )(x16, pe2, w1, w2p)


def _split3(x):
    hi = x.astype(BF16)
    r1 = x - hi.astype(F32)
    mid = r1.astype(BF16)
    lo = (r1 - mid.astype(F32)).astype(BF16)
    return hi, mid, lo


def _pack_heads(o, tq):
    pair = lambda a, b: a + pltpu.roll(b, 64, 1)
    return jnp.concatenate([pair(o[0:tq], o[tq:2 * tq]), pair(o[2 * tq:3 * tq], o[3 * tq:])], axis=-1)


def _normalize_low_half(acc):
    return jnp.where(_lane(acc.shape) < 64, acc / pltpu.roll(acc, 64, 1), 0.0)


def _cmp_select_body(q_ref, kc_ref, vc_ref, ovt_ref, gx_ref, o_ref, sb_ref, *, tq):
    qi = pl.program_id(1)
    hpg = q_ref.shape[0]
    q = q_ref[...].reshape(hpg * tq, LANES)
    s = _dot_nt(q, kc_ref[0, 0])
    t = qi * tq + (_row(s.shape) & (tq - 1))
    cmask = _lane(s.shape) * NSA_CMP_STRIDE + (NSA_CMP_LEN - 1) <= t
    s = jnp.where(cmask, s, NEG)
    e = jnp.exp2(s - jnp.max(s, axis=-1, keepdims=True))
    p = jnp.where(cmask, e / jnp.sum(e, axis=-1, keepdims=True), 0.0)
    o = _dot(p.astype(BF16), vc_ref[0, 0])
    o_ref[...] = (gx_ref[0] * _pack_heads(o, tq)).astype(BF16)

    psum = p[0:tq]
    for hh in range(1, hpg):
        psum = psum + p[hh * tq:(hh + 1) * tq]
    ovt = ovt_ref[...]
    imp = sum(_dot_nt(ovt, part) for part in _split3(psum))
    j = _row(imp.shape)
    cur = (qi * tq + _lane(imp.shape)) >> 6
    forced = (j == 0) | (j == cur) | (j == cur - 1)
    score = jnp.where(forced, FORCE, jnp.where(j <= cur, imp, -1.0))
    jf = j.astype(F32)
    sel = jnp.zeros(imp.shape, jnp.bool_)
    for _ in range(NSA_SLC_TOPK):
        m = jnp.max(score, axis=0, keepdims=True)
        first = jnp.min(jnp.where(score == m, jf, float(LANES)), axis=0, keepdims=True)
        hit = jf == first
        sel = sel | (hit & (m >= 0.0))
        score = jnp.where(hit, -2.0, score)
    sb_ref[0] = jnp.where(sel, 0.0, NEG).T.astype(BF16)


def cmp_select(qn, kvc, overlap_t, gx, tq):
    H, T, _ = qn.shape
    G = NSA_KV_HEADS
    hpg = H // G
    n = kvc.shape[2]
    return pl.pallas_call(
        functools.partial(_cmp_select_body, tq=tq),
        grid=(G, T // tq),
        in_specs=[pl.BlockSpec((hpg, tq, LANES), lambda g, qi: (g, qi, 0)),
                  pl.BlockSpec((1, 1, n, LANES), lambda g, qi: (0, g, 0, 0)),
                  pl.BlockSpec((1, 1, n, LANES), lambda g, qi: (1, g, 0, 0)),
                  pl.BlockSpec((LANES, n), lambda g, qi: (0, 0)),
                  pl.BlockSpec((1, tq, hpg * NSA_HEAD_DIM), lambda g, qi: (0, qi, g))],
        out_specs=[pl.BlockSpec((tq, hpg * NSA_HEAD_DIM), lambda g, qi: (qi, g)),
                   pl.BlockSpec((1, tq, LANES), lambda g, qi: (g, qi, 0))],
        out_shape=[jax.ShapeDtypeStruct((T, GROUP_WIDTH), BF16),
                   jax.ShapeDtypeStruct((G, T, LANES), BF16)],
        compiler_params=_cparams(("parallel", "parallel")),
        name="nsa_compressed_select",
    )(qn, kvc, kvc, overlap_t, gx)


def _sel_attn_body(q_ref, sb_ref, k_ref, v_ref, gx_ref, o_ref, qs_ref, m_ref, acc_ref, *, tq, tk):
    hpg = q_ref.shape[0]
    for hh in range(hpg):
        qs_ref[hh * tq:(hh + 1) * tq, 0:LANES] = sb_ref[0]
        qs_ref[hh * tq:(hh + 1) * tq, LANES:] = q_ref[hh]
    _causal_sweep(qs_ref, k_ref, v_ref, m_ref, acc_ref, pl.program_id(1) * tq, tq, tk)
    o = _normalize_low_half(acc_ref[...])
    o_ref[...] = (gx_ref[0] * _pack_heads(o, tq)).astype(BF16)


def sel_attention(qn, selbias, ks, vs, gx, tq, tk):
    H, T, _ = qn.shape
    G = NSA_KV_HEADS
    hpg = H // G
    return pl.pallas_call(
        functools.partial(_sel_attn_body, tq=tq, tk=tk),
        grid=(G, T // tq),
        in_specs=[pl.BlockSpec((hpg, tq, LANES), lambda g, qi: (g, qi, 0)),
                  pl.BlockSpec((1, tq, LANES), lambda g, qi: (g, qi, 0)),
                  pl.BlockSpec((1, T, 2 * LANES), lambda g, qi: (g, 0, 0)),
                  pl.BlockSpec((1, T, LANES), lambda g, qi: (g, 0, 0)),
                  pl.BlockSpec((1, tq, hpg * NSA_HEAD_DIM), lambda g, qi: (1, qi, g))],
        out_specs=pl.BlockSpec((tq, hpg * NSA_HEAD_DIM), lambda g, qi: (qi, g)),
        out_shape=jax.ShapeDtypeStruct((T, GROUP_WIDTH), BF16),
        scratch_shapes=[pltpu.VMEM((hpg * tq, 2 * LANES), BF16), pltpu.VMEM((hpg * tq, LANES), F32),
                        pltpu.VMEM((hpg * tq, LANES), F32)],
        compiler_params=_cparams(("parallel", "arbitrary")),
        name="nsa_selected_attention",
    )(qn, selbias, ks, vs, gx)


def _win_attn_body(q_ref, k_ref, v_ref, gx_ref, o_ref, *, tq, span):
    hpg = q_ref.shape[0]
    q0 = pl.program_id(1) * tq
    lo = pl.multiple_of(jnp.maximum(q0 + tq - span, 0), tq)
    q = q_ref[...].reshape(hpg * tq, LANES)
    s = _dot_nt(q, k_ref[0, pl.ds(lo, span), :])
    qpos = q0 + (_row(s.shape) & (tq - 1))
    kpos = lo + _lane(s.shape)
    s = jnp.where((kpos <= qpos) & (kpos > qpos - NSA_WINDOW), s, NEG)
    p = jnp.exp2(s - jnp.max(s, axis=-1, keepdims=True))
    acc = _dot(p.astype(BF16), v_ref[0, pl.ds(lo, span), :])
    o_ref[...] = (gx_ref[0] * _pack_heads(_normalize_low_half(acc), tq)).astype(BF16)


def win_attention(qn, kw, vw, gx, tq):
    H, T, _ = qn.shape
    G = NSA_KV_HEADS
    hpg = H // G
    return pl.pallas_call(
        functools.partial(_win_attn_body, tq=tq, span=NSA_WINDOW + tq),
        grid=(G, T // tq),
        in_specs=[pl.BlockSpec((hpg, tq, LANES), lambda g, qi: (g, qi, 0)),
                  pl.BlockSpec((1, T, LANES), lambda g, qi: (g, 0, 0)),
                  pl.BlockSpec((1, T, LANES), lambda g, qi: (g, 0, 0)),
                  pl.BlockSpec((1, tq, hpg * NSA_HEAD_DIM), lambda g, qi: (2, qi, g))],
        out_specs=pl.BlockSpec((tq, hpg * NSA_HEAD_DIM), lambda g, qi: (qi, g)),
        out_shape=jax.ShapeDtypeStruct((T, GROUP_WIDTH), BF16),
        compiler_params=_cparams(("parallel", "arbitrary")),
        name="nsa_window_attention",
    )(qn, kw, vw, gx)


def _out_proj_body(a_ref, b_ref, c_ref, d1_ref, d2_ref, d3_ref, w_ref, x_ref, g_ref, o_ref, hn_ref, lhs_ref):
    lhs_ref[:, 0:GROUP_WIDTH] = a_ref[...]
    lhs_ref[:, GROUP_WIDTH:2 * GROUP_WIDTH] = b_ref[...]
    lhs_ref[:, 2 * GROUP_WIDTH:3 * GROUP_WIDTH] = c_ref[...]
    d = d1_ref[...].astype(F32) + d2_ref[...].astype(F32) + d3_ref[...].astype(F32)
    lhs_ref[:, 3 * GROUP_WIDTH:] = d.astype(BF16)
    y = x_ref[...] + _dot(lhs_ref[...], w_ref[...])
    o_ref[...] = y
    hn_ref[...] = _rms(y, g_ref[...]).astype(BF16)


def out_proj(parts, w, x, g_next, tm):
    T, N = x.shape
    part = pl.BlockSpec((tm, GROUP_WIDTH), lambda i: (i, 0))
    rows = pl.BlockSpec((tm, N), lambda i: (i, 0))
    return pl.pallas_call(
        _out_proj_body,
        grid=(T // tm,),
        in_specs=[part] * 6 + [pl.BlockSpec((4 * GROUP_WIDTH, N), lambda i: (0, 0)), rows,
                               pl.BlockSpec((1, N), lambda i: (0, 0))],
        out_specs=[rows, rows],
        out_shape=[jax.ShapeDtypeStruct((T, N), F32), jax.ShapeDtypeStruct((T, N), BF16)],
        scratch_shapes=[pltpu.VMEM((tm, 4 * GROUP_WIDTH), BF16)],
        compiler_params=_cparams(("parallel",)),
        name="out_proj",
    )(*parts, w, x, g_next)


def _group_starts(te_ref, i):
    return (i == 0) | (te_ref[i] != te_ref[jnp.maximum(i - 1, 0)])


def _up_body(te_ref, na_ref, h_ref, wg_ref, wu_ref, o_ref, wgb_ref, wub_ref):
    i = pl.program_id(1)

    @pl.when(i < na_ref[0])
    def _():
        @pl.when(_group_starts(te_ref, i))
        def _():
            wgb_ref[...] = wg_ref[0].astype(BF16)
            wub_ref[...] = wu_ref[0].astype(BF16)

        h = h_ref[...]
        a = _dot(h, wgb_ref[...])
        o_ref[...] = (a * jax.nn.sigmoid(a) * _dot(h, wub_ref[...])).astype(BF16)

    @pl.when(i >= na_ref[0])
    def _():
        o_ref[...] = jnp.zeros(o_ref.shape, BF16)


def swiglu_up(tile_expert, n_active, hn, wg, wu, tm, tn):
    P, K = hn.shape
    F = wg.shape[2]

    def row(j, i, te, na):
        return (jnp.minimum(i, na[0] - 1), 0)

    def wmap(j, i, te, na):
        return (te[i], 0, j)

    return pl.pallas_call(
        _up_body,
        grid_spec=pltpu.PrefetchScalarGridSpec(
            num_scalar_prefetch=2,
            grid=(F // tn, P // tm),
            in_specs=[pl.BlockSpec((tm, K), row), pl.BlockSpec((1, K, tn), wmap), pl.BlockSpec((1, K, tn), wmap)],
            out_specs=pl.BlockSpec((tm, tn), lambda j, i, te, na: (i, j)),
            scratch_shapes=[pltpu.VMEM((K, tn), BF16), pltpu.VMEM((K, tn), BF16)]),
        out_shape=jax.ShapeDtypeStruct((P, F), BF16),
        compiler_params=_cparams(("arbitrary", "arbitrary")),
        name="swiglu_up",
    )(tile_expert, n_active, hn, wg, wu)


def _down_body(te_ref, na_ref, a_ref, w_ref, e_ref, o_ref, wb_ref, *, scaled):
    i = pl.program_id(1)

    @pl.when(_group_starts(te_ref, i))
    def _():
        wb_ref[...] = w_ref[0].astype(BF16)

    y = _dot(a_ref[...], wb_ref[...])
    o_ref[...] = e_ref[...] * y if scaled else e_ref[...] + y


def swiglu_down(tile_expert, n_active, act, wd, extra, tm, tn, scaled):
    P, F = act.shape
    N = wd.shape[2]

    def row(j, i, te, na):
        return (jnp.minimum(i, na[0] - 1), 0)

    if scaled:
        extra_spec = pl.BlockSpec((tm, 1), lambda j, i, te, na: (i, 0))
    else:
        extra_spec = pl.BlockSpec((tm, tn), lambda j, i, te, na: (i, j))
    return pl.pallas_call(
        functools.partial(_down_body, scaled=scaled),
        grid_spec=pltpu.PrefetchScalarGridSpec(
            num_scalar_prefetch=2,
            grid=(N // tn, P // tm),
            in_specs=[pl.BlockSpec((tm, F), row),
                      pl.BlockSpec((1, F, tn), lambda j, i, te, na: (te[i], 0, j)),
                      extra_spec],
            out_specs=pl.BlockSpec((tm, tn), lambda j, i, te, na: (i, j)),
            scratch_shapes=[pltpu.VMEM((F, tn), BF16)]),
        out_shape=jax.ShapeDtypeStruct((P, N), F32),
        compiler_params=_cparams(("arbitrary", "arbitrary")),
        name="swiglu_down",
    )(tile_expert, n_active, act, wd, extra)


def _router_body(x_ref, g_ref, w_ref, o_ref):
    h = _rms(x_ref[...], g_ref[...])
    logits = jnp.dot(h, w_ref[...], preferred_element_type=F32, precision=lax.Precision.HIGHEST)
    lane = _lane(logits.shape)
    lf = lane.astype(F32)
    logits = jnp.where(lane < N_EXPERTS, logits, -jnp.inf)
    v0 = jnp.max(logits, axis=-1, keepdims=True)
    i0 = jnp.min(jnp.where(logits == v0, lf, float(LANES)), axis=-1, keepdims=True)
    rest = jnp.where(lf == i0, -jnp.inf, logits)
    v1 = jnp.max(rest, axis=-1, keepdims=True)
    i1 = jnp.min(jnp.where(rest == v1, lf, float(LANES)), axis=-1, keepdims=True)
    e1 = jnp.exp(v1 - v0)
    w0 = 1.0 / (1.0 + e1)
    w1 = e1 / (1.0 + e1)
    o_ref[...] = jnp.where(lane == 0, i0, jnp.where(lane == 1, i1, jnp.where(lane == 2, w0, w1)))


def router(x, g, w_pad, tm):
    T, K = x.shape
    return pl.pallas_call(
        _router_body,
        grid=(T // tm,),
        in_specs=[pl.BlockSpec((tm, K), lambda i: (i, 0)),
                  pl.BlockSpec((1, K), lambda i: (0, 0)),
                  pl.BlockSpec((K, LANES), lambda i: (0, 0))],
        out_specs=pl.BlockSpec((tm, LANES), lambda i: (i, 0)),
        out_shape=jax.ShapeDtypeStruct((T, LANES), F32),
        compiler_params=_cparams(("parallel",)),
        name="moe_router",
    )(x, g, w_pad)


def _row_copy(src_hbm, row, dst_ref, r, sem):
    return pltpu.make_async_copy(src_hbm.at[pl.ds(row, 1)], dst_ref.at[pl.ds(r, 1)], sem)


def _gather_body(tok_ref, na_ref, x_hbm, g_ref, o_ref, buf_ref, sem):
    i = pl.program_id(0)
    tm = buf_ref.shape[0]

    @pl.when(i < na_ref[0])
    def _():
        def start(r, c):
            _row_copy(x_hbm, tok_ref[i * tm + r], buf_ref, r, sem).start()
            return c

        def wait(r, c):
            _row_copy(x_hbm, 0, buf_ref, r, sem).wait()
            return c

        lax.fori_loop(0, tm, start, 0)
        lax.fori_loop(0, tm, wait, 0)
        o_ref[...] = _rms(buf_ref[...], g_ref[...]).astype(BF16)

    @pl.when(i >= na_ref[0])
    def _():
        o_ref[...] = jnp.zeros(o_ref.shape, BF16)


def gather_norm_tokens(tok_of_slot, n_active, x, g, tm):
    P = tok_of_slot.shape[0]
    K = x.shape[1]
    return pl.pallas_call(
        _gather_body,
        grid_spec=pltpu.PrefetchScalarGridSpec(
            num_scalar_prefetch=2,
            grid=(P // tm,),
            in_specs=[pl.BlockSpec(memory_space=pl.ANY), pl.BlockSpec((1, K), lambda i, tok, na: (0, 0))],
            out_specs=pl.BlockSpec((tm, K), lambda i, tok, na: (i, 0)),
            scratch_shapes=[pltpu.VMEM((tm, K), F32), pltpu.SemaphoreType.DMA(())]),
        out_shape=jax.ShapeDtypeStruct((P, K), BF16),
        compiler_params=_cparams(("arbitrary",)),
        name="moe_gather",
    )(tok_of_slot, n_active, x, g)


def _combine_body(s0_ref, s1_ref, x_ref, y_hbm, g_ref, o_ref, b0_ref, b1_ref, sem0, sem1, *, final):
    tm = x_ref.shape[0]
    base = pl.program_id(0) * tm

    def start(r, c):
        _row_copy(y_hbm, s0_ref[base + r], b0_ref, r, sem0).start()
        _row_copy(y_hbm, s1_ref[base + r], b1_ref, r, sem1).start()
        return c

    def wait(r, c):
        _row_copy(y_hbm, 0, b0_ref, r, sem0).wait()
        _row_copy(y_hbm, 0, b1_ref, r, sem1).wait()
        return c

    lax.fori_loop(0, tm, start, 0)
    lax.fori_loop(0, tm, wait, 0)
    y = x_ref[...] + b0_ref[...] + b1_ref[...]
    o_ref[...] = _rms(y, g_ref[...]) if final else y


def moe_combine(slot0, slot1, x, ys, g, tm, final):
    T, K = x.shape
    return pl.pallas_call(
        functools.partial(_combine_body, final=final),
        grid_spec=pltpu.PrefetchScalarGridSpec(
            num_scalar_prefetch=2,
            grid=(T // tm,),
            in_specs=[pl.BlockSpec((tm, K), lambda i, s0, s1: (i, 0)),
                      pl.BlockSpec(memory_space=pl.ANY),
                      pl.BlockSpec((1, K), lambda i, s0, s1: (0, 0))],
            out_specs=pl.BlockSpec((tm, K), lambda i, s0, s1: (i, 0)),
            scratch_shapes=[pltpu.VMEM((tm, K), F32), pltpu.VMEM((tm, K), F32),
                            pltpu.SemaphoreType.DMA(()), pltpu.SemaphoreType.DMA(())]),
        out_shape=jax.ShapeDtypeStruct((T, K), F32),
        compiler_params=_cparams(("arbitrary",)),
        name="moe_combine",
    )(slot0, slot1, x, ys, g)


def _final_norm_body(x_ref, g_ref, o_ref):
    o_ref[...] = _rms(x_ref[...], g_ref[...])


def final_norm(x, g, tm):
    T, K = x.shape
    return pl.pallas_call(
        _final_norm_body,
        grid=(T // tm,),
        in_specs=[pl.BlockSpec((tm, K), lambda i: (i, 0)), pl.BlockSpec((1, K), lambda i: (0, 0))],
        out_specs=pl.BlockSpec((tm, K), lambda i: (i, 0)),
        out_shape=jax.ShapeDtypeStruct((T, K), F32),
        compiler_params=_cparams(("parallel",)),
        name="final_norm",
    )(x, g)


def _routing_tables(route, tm, expert_base):
    T = route.shape[0]
    top_i = route[:, 0:2].astype(jnp.int32)
    gate = route[:, 2:4]
    e_flat = top_i.reshape(-1)
    onehot = (e_flat[:, None] == jnp.arange(N_EXPERTS)[None, :]).astype(jnp.int32)
    rank = jnp.take_along_axis(jnp.cumsum(onehot, axis=0) - onehot, e_flat[:, None], axis=1)[:, 0]
    count = jnp.sum(onehot, axis=0)
    padded = ((count + tm - 1) // tm) * tm
    end = jnp.cumsum(padded)
    start = end - padded
    slot = start[e_flat] + rank
    n_slots = 2 * T + N_EXPERTS * tm
    n_tiles = n_slots // tm
    n_active = (end[-1] // tm).astype(jnp.int32)
    tile_start = jnp.minimum(jnp.arange(n_tiles, dtype=jnp.int32), n_active - 1) * tm
    tile_expert = jnp.minimum(jnp.sum(tile_start[:, None] >= end[None, :], axis=1), N_EXPERTS - 1)
    tok_of_slot = jnp.zeros((n_slots,), jnp.int32).at[slot].set(jnp.arange(2 * T, dtype=jnp.int32) // 2)
    gate_of_slot = jnp.zeros((n_slots,), F32).at[slot].set(gate.reshape(-1))
    slot2 = slot.reshape(T, 2).astype(jnp.int32)
    return ((tile_expert + expert_base).astype(jnp.int32), n_active.reshape(1), tok_of_slot,
            gate_of_slot[:, None], slot2[:, 0], slot2[:, 1])


def _rope_tables128(T):
    half = NSA_HEAD_DIM // 2
    inv = ROPE_THETA ** (-jnp.arange(0, NSA_HEAD_DIM, 2, dtype=F32) / NSA_HEAD_DIM)
    ang = jnp.arange(T, dtype=F32)[:, None] * inv[None, :]
    cos, sin = jnp.cos(ang), jnp.sin(ang)
    cos128 = jnp.tile(cos, (1, LANES // half))
    sin128 = jnp.tile(jnp.concatenate([-sin, sin], axis=1), (1, LANES // NSA_HEAD_DIM))
    return cos128, sin128


def _overlap_matrix_t(n_cmp_pad):
    sstart = jnp.arange(LANES) * NSA_SLC_LEN
    cstart = jnp.arange(n_cmp_pad) * NSA_CMP_STRIDE
    ov = (cstart[None, :] < sstart[:, None] + NSA_SLC_LEN) & (cstart[None, :] + NSA_CMP_LEN > sstart[:, None])
    return ov.astype(BF16)


def kernel(x, attn_norm, w_in, w_out, gm_ln_g, gm_ln_b, gm_ws, gm_bs, da_lambda, da_subln, cv_dw_w, cv_dw_b,
           cv_ln_g, cv_ln_b, nsa_cmp_w1, nsa_cmp_w2, nsa_cmp_pe, ffn_norm, ffn_wg, ffn_wu, ffn_wd, router_w,
           exp_wg, exp_wu, exp_wd, final_norm_g):
    B, T, D = x.shape
    assert B == 1 and D == D_MODEL and T % 512 == 0 and T // NSA_SLC_LEN <= LANES
    depth = w_in.shape[0]
    G = NSA_KV_HEADS
    n16 = T // NSA_CMP_STRIDE
    tm = 512

    cos128, sin128 = _rope_tables128(T)
    overlap_t = _overlap_matrix_t(n16)
    dense_na = jnp.full((1,), T // tm, jnp.int32)
    row = lambda v: v.reshape(1, -1)
    merge = lambda w: w.reshape((-1,) + w.shape[2:])
    exp_wg, exp_wu, exp_wd = merge(exp_wg), merge(exp_wu), merge(exp_wd)

    xs = x[0]
    for l in range(depth):
        lambda_init = 0.8 - 0.6 * math.exp(-0.3 * l)
        w_in_l = jnp.pad(w_in[l], ((0, 0), (0, IN_WIDTH_PAD - IN_WIDTH))).astype(BF16)
        z = norm_mm(xs, row(attn_norm[l]), w_in_l, tm, IN_WIDTH_PAD // 3)

        bs_rows = jnp.repeat(gm_bs[l].T, GM_CHUNK, axis=1)
        o_a = gmlp(z, row(gm_ln_g[l]), row(gm_ln_b[l]), gm_ws[l], bs_rows, 512)
        dw_w = jnp.pad(cv_dw_w[l], ((0, CV_HALO - CV_WIDTH), (0, 0)))
        o_c = conformer_conv(z, dw_w, row(cv_dw_b[l]), row(cv_ln_g[l]), row(cv_ln_b[l]), 256)

        qd, kd, vd, qn, kc, vc, ks, vs, kw, vw, gx = prep(z, cos128, sin128, 256)
        o_b = diff_attention(qd, kd, vd, da_lambda[l], row(da_subln[l]), lambda_init, 256, 512)

        x16 = jnp.stack([kc, vc]).reshape(2, T, G, NSA_HEAD_DIM).transpose(0, 2, 1, 3)
        x16 = x16.reshape(2, G, n16, NSA_CMP_STRIDE * NSA_HEAD_DIM)
        pe2 = nsa_cmp_pe[l].reshape(2, 2, NSA_CMP_STRIDE * NSA_HEAD_DIM)
        w2p = jnp.pad(nsa_cmp_w2[l], ((0, 0), (0, 0), (0, LANES - NSA_HEAD_DIM))).astype(BF16)
        kvc = compress(x16, pe2, nsa_cmp_w1[l].astype(BF16), w2p)
        o_cmp, selbias = cmp_select(qn, kvc, overlap_t, gx, 128)
        o_sel = sel_attention(qn, selbias, ks, vs, gx, 128, 512)
        o_win = win_attention(qn, kw, vw, gx, 256)

        g_ffn = row(ffn_norm[l])
        xs, hn = out_proj((o_a, o_b, o_c, o_cmp, o_sel, o_win), w_out[l].astype(BF16), xs, g_ffn, 256)

        e = l // 2
        if l % 2 == 0:
            dense_te = jnp.full((T // tm,), e, jnp.int32)
            act = swiglu_up(dense_te, dense_na, hn, ffn_wg, ffn_wu, tm, 512)
            xs = swiglu_down(dense_te, dense_na, act, ffn_wd, xs, tm, 512, scaled=False)
            if l == depth - 1:
                xs = final_norm(xs, row(final_norm_g), tm)
        else:
            w_r = jnp.pad(router_w[e], ((0, 0), (0, LANES - N_EXPERTS)))
            route = router(xs, g_ffn, w_r, tm)
            tile_expert, n_active, tok_of_slot, gate_of_slot, slot0, slot1 = _routing_tables(
                route, tm, e * N_EXPERTS)
            hg = gather_norm_tokens(tok_of_slot, n_active, xs, g_ffn, tm)
            act = swiglu_up(tile_expert, n_active, hg, exp_wg, exp_wu, tm, 512)
            ys = swiglu_down(tile_expert, n_active, act, exp_wd, gate_of_slot, tm, 512, scaled=True)
            xs = moe_combine(slot0, slot1, xs, ys, row(final_norm_g), 256, final=(l == depth - 1))
    return xs[None]
```

```python
import functools
import math

import jax
import jax.numpy as jnp
from jax import lax
from jax.experimental import pallas as pl
from jax.experimental.pallas import tpu as pltpu

F32 = jnp.float32
BF16 = jnp.bfloat16

D_MODEL = 2048
GROUP_WIDTH = 512
GM_CHUNK = 128
GM_HEADS = 4
DA_HEADS = 4
DA_QK_DIM = 64
CV_WIDTH = 31
NSA_HEADS = 8
NSA_KV_HEADS = 2
NSA_HEAD_DIM = 64
NSA_CMP_LEN = 32
NSA_CMP_STRIDE = 16
NSA_SLC_LEN = 64
NSA_SLC_TOPK = 16
NSA_WINDOW = 512
ROPE_THETA = 10000.0
NORM_EPS = 1e-6
NEG = -1e30
FORCE = 1e9
N_EXPERTS = 8
LANES = 128
LOG2E = math.log2(math.e)

IN_WIDTH = 4888
IN_WIDTH_PAD = 4992
COL_GM = 0
COL_QDA = 1024
COL_KDA = 1536
COL_VDA = 2048
COL_CV = 2560
COL_QNS = 3584
COL_KVNS = 4096
COL_GNS = 4864

VMEM_LIMIT = 56 * 1024 * 1024


def _cparams(sem, **kw):
    return pltpu.CompilerParams(dimension_semantics=sem, vmem_limit_bytes=VMEM_LIMIT, **kw)


def _rms(x, g):
    ms = jnp.mean(x * x, axis=-1, keepdims=True)
    return x * lax.rsqrt(ms + NORM_EPS) * g


def _layer_norm(x, g, b):
    mu = jnp.mean(x, axis=-1, keepdims=True)
    xc = x - mu
    var = jnp.mean(xc * xc, axis=-1, keepdims=True)
    return xc * lax.rsqrt(var + NORM_EPS) * g + b


def _dot(a, b):
    return jnp.dot(a, b, preferred_element_type=F32)


def _dot_nt(a, b):
    return lax.dot_general(a, b, (((1,), (1,)), ((), ())), preferred_element_type=F32)


def _lane(shape):
    return lax.broadcasted_iota(jnp.int32, shape, len(shape) - 1)


def _row(shape):
    return lax.broadcasted_iota(jnp.int32, shape, len(shape) - 2)


def _norm_mm_body(x_ref, g_ref, w_ref, o_ref, hn_ref):
    @pl.when(pl.program_id(1) == 0)
    def _():
        hn_ref[...] = _rms(x_ref[...], g_ref[...]).astype(BF16)

    o_ref[...] = _dot(hn_ref[...], w_ref[0])


def norm_mm(x, g, w, layer, tm, tn):
    T, K = x.shape
    N = w.shape[2]
    return pl.pallas_call(
        _norm_mm_body,
        grid=(T // tm, N // tn),
        in_specs=[pl.BlockSpec((tm, K), lambda i, j: (i, 0)),
                  pl.BlockSpec((1, K), lambda i, j: (0, 0)),
                  pl.BlockSpec((1, K, tn), lambda i, j: (layer, 0, j))],
        out_specs=pl.BlockSpec((tm, tn), lambda i, j: (i, j)),
        out_shape=jax.ShapeDtypeStruct((T, N), F32),
        scratch_shapes=[pltpu.VMEM((tm, K), BF16)],
        compiler_params=_cparams(("parallel", "arbitrary")),
        name="norm_in_proj",
    )(x, g, w)


def _gmlp_body(z_ref, g_ref, b_ref, ws_ref, bs_ref, o_ref):
    tr = z_ref.shape[0]
    z = jax.nn.gelu(z_ref[...])
    u = z[:, :GROUP_WIDTH]
    v = _layer_norm(z[:, GROUP_WIDTH:], g_ref[...], b_ref[...]).astype(BF16)
    causal = _row((GM_CHUNK, GM_CHUNK)) >= _lane((GM_CHUNK, GM_CHUNK))
    bias = bs_ref[...]
    for h in range(GM_HEADS):
        w = jnp.where(causal, ws_ref[h], 0.0).astype(BF16)
        cols = slice(h * LANES, (h + 1) * LANES)
        for c in range(tr // GM_CHUNK):
            rows = slice(c * GM_CHUNK, (c + 1) * GM_CHUNK)
            s = _dot(w, v[rows, cols]) + bias[:, cols]
            o_ref[rows, cols] = (u[rows, cols] * s).astype(BF16)


def gmlp(z, ln_g, ln_b, ws, bs_rows, tr):
    T = z.shape[0]
    return pl.pallas_call(
        _gmlp_body,
        grid=(T // tr,),
        in_specs=[pl.BlockSpec((tr, 2 * GROUP_WIDTH), lambda i: (i, COL_GM // (2 * GROUP_WIDTH))),
                  pl.BlockSpec((1, GROUP_WIDTH), lambda i: (0, 0)),
                  pl.BlockSpec((1, GROUP_WIDTH), lambda i: (0, 0)),
                  pl.BlockSpec((GM_HEADS, GM_CHUNK, GM_CHUNK), lambda i: (0, 0, 0)),
                  pl.BlockSpec((GM_CHUNK, GROUP_WIDTH), lambda i: (0, 0))],
        out_specs=pl.BlockSpec((tr, GROUP_WIDTH), lambda i: (i, 0)),
        out_shape=jax.ShapeDtypeStruct((T, GROUP_WIDTH), BF16),
        compiler_params=_cparams(("parallel",)),
        name="gmlp",
    )(z, ln_g, ln_b, ws, bs_rows)


CV_HALO = 32
CV_SUB = 64


def _conv_body(a_ref, g_ref, ap_ref, gp_ref, w_ref, b_ref, lg_ref, lb_ref, o_ref, hs_ref):
    tr = a_ref.shape[0]
    first = pl.program_id(0) == 0
    prev = ap_ref[...] * jax.nn.sigmoid(gp_ref[...])
    hs_ref[0:CV_HALO, :] = jnp.where(first, 0.0, prev)
    hs_ref[CV_HALO:, :] = a_ref[...] * jax.nn.sigmoid(g_ref[...])
    w = w_ref[...]
    for r0 in range(0, tr, CV_SUB):
        acc = jnp.zeros((CV_SUB, GROUP_WIDTH), F32)
        for k in range(CV_WIDTH):
            start = r0 + CV_HALO - (CV_WIDTH - 1) + k
            acc = acc + hs_ref[start:start + CV_SUB, :] * w[k:k + 1, :]
        y = _layer_norm(acc + b_ref[...], lg_ref[...], lb_ref[...])
        o_ref[r0:r0 + CV_SUB, :] = (y * jax.nn.sigmoid(y)).astype(BF16)


def conformer_conv(z, dw_w, dw_b, ln_g, ln_b, tr):
    T = z.shape[0]
    ca = COL_CV // GROUP_WIDTH
    per = tr // CV_HALO

    def halo(col):
        return lambda i: (jnp.maximum(i * per - 1, 0), col)

    return pl.pallas_call(
        _conv_body,
        grid=(T // tr,),
        in_specs=[pl.BlockSpec((tr, GROUP_WIDTH), lambda i: (i, ca)),
                  pl.BlockSpec((tr, GROUP_WIDTH), lambda i: (i, ca + 1)),
                  pl.BlockSpec((CV_HALO, GROUP_WIDTH), halo(ca)),
                  pl.BlockSpec((CV_HALO, GROUP_WIDTH), halo(ca + 1)),
                  pl.BlockSpec((CV_HALO, GROUP_WIDTH), lambda i: (0, 0)),
                  pl.BlockSpec((1, GROUP_WIDTH), lambda i: (0, 0)),
                  pl.BlockSpec((1, GROUP_WIDTH), lambda i: (0, 0)),
                  pl.BlockSpec((1, GROUP_WIDTH), lambda i: (0, 0))],
        out_specs=pl.BlockSpec((tr, GROUP_WIDTH), lambda i: (i, 0)),
        out_shape=jax.ShapeDtypeStruct((T, GROUP_WIDTH), BF16),
        scratch_shapes=[pltpu.VMEM((tr + CV_HALO, GROUP_WIDTH), F32)],
        compiler_params=_cparams(("parallel",)),
        name="conformer_conv",
    )(z, z, z, z, dw_w, dw_b, ln_g, ln_b)


def _rope128(x, cos, sin_signed):
    lo = (_lane(x.shape) & 63) < 32
    rot = jnp.where(lo, pltpu.roll(x, 96, 1), pltpu.roll(x, 32, 1))
    return x * cos + rot * sin_signed


def _low_half(x, fill=0.0):
    return jnp.where(_lane(x.shape) < 64, x, fill)


def _high_half_to_low(x, fill=0.0):
    return jnp.where(_lane(x.shape) < 64, pltpu.roll(x, 64, 1), fill)


def _prep_body(qd_ref, kd_ref, vd_ref, qn_ref, kvc_ref, kvs_ref, kvw_ref, gt_ref, cos_ref, sin_ref,
               qd_o, kd_o, vd_o, qn_o, kc_o, vc_o, ks_o, vs_o, kw_o, vw_o, gx_o):
    tr = cos_ref.shape[0]
    cos = cos_ref[...]
    sin = sin_ref[...]
    q_scale = DA_QK_DIM ** -0.5 * LOG2E
    ones = jnp.ones((tr, LANES), BF16)
    for h in range(DA_HEADS):
        cols = slice(h * LANES, (h + 1) * LANES)
        qd_o[h] = (_rope128(qd_ref[:, cols], cos, sin) * q_scale).astype(BF16)
        kd_o[h] = _rope128(kd_ref[:, cols], cos, sin).astype(BF16)
        vd_o[h, :, 0:LANES] = vd_ref[:, cols].astype(BF16)
        vd_o[h, :, LANES:] = ones
    for c in range(NSA_HEADS // 2):
        q = _rope128(qn_ref[:, c * LANES:(c + 1) * LANES], cos, sin) * (NSA_HEAD_DIM ** -0.5 * LOG2E)
        qn_o[2 * c] = _low_half(q).astype(BF16)
        qn_o[2 * c + 1] = _high_half_to_low(q).astype(BF16)
    kc_o[...] = _rope128(kvc_ref[:, :LANES], cos, sin)
    vc_o[...] = kvc_ref[:, LANES:]
    blk = (pl.program_id(0) * tr + _row((tr, LANES))) >> 6
    onehot = jnp.where(blk == _lane((tr, LANES)), 1.0, 0.0).astype(BF16)
    k = _rope128(kvs_ref[:, :LANES], cos, sin)
    for g, half in enumerate((_low_half, _high_half_to_low)):
        ks_o[g, :, 0:LANES] = onehot
        ks_o[g, :, LANES:] = half(k).astype(BF16)
        vs_o[g] = half(kvs_ref[:, LANES:], 1.0).astype(BF16)
    k = _rope128(kvw_ref[:, :LANES], cos, sin)
    for g, half in enumerate((_low_half, _high_half_to_low)):
        kw_o[g] = half(k).astype(BF16)
        vw_o[g] = half(kvw_ref[:, LANES:], 1.0).astype(BF16)
    gate = jax.nn.sigmoid(gt_ref[...])
    low = _lane((tr, LANES)) < 64
    for c in range(3):
        for hp in range(NSA_HEADS // 2):
            j = c * NSA_HEADS + 2 * hp
            gx_o[c, :, hp * LANES:(hp + 1) * LANES] = jnp.where(low, gate[:, j:j + 1], gate[:, j + 1:j + 2])


def prep(z, cos128, sin128, tr):
    T = z.shape[0]
    G = NSA_KV_HEADS

    def zspec(width, col):
        return pl.BlockSpec((tr, width), lambda i: (i, col // width))

    def heads(n, width=LANES):
        return pl.BlockSpec((n, tr, width), lambda i: (0, i, 0))

    def hshape(n, width=LANES):
        return jax.ShapeDtypeStruct((n, T, width), BF16)

    row128 = pl.BlockSpec((tr, LANES), lambda i: (i, 0))
    return pl.pallas_call(
        _prep_body,
        grid=(T // tr,),
        in_specs=[zspec(512, COL_QDA), zspec(512, COL_KDA), zspec(512, COL_VDA), zspec(512, COL_QNS),
                  zspec(256, COL_KVNS), zspec(256, COL_KVNS + 256), zspec(256, COL_KVNS + 512),
                  zspec(128, COL_GNS), row128, row128],
        out_specs=[heads(DA_HEADS), heads(DA_HEADS), heads(DA_HEADS, 2 * LANES),
                   heads(NSA_HEADS), row128, row128, heads(G, 2 * LANES), heads(G), heads(G), heads(G),
                   pl.BlockSpec((3, tr, GROUP_WIDTH), lambda i: (0, i, 0))],
        out_shape=[hshape(DA_HEADS), hshape(DA_HEADS), hshape(DA_HEADS, 2 * LANES),
                   hshape(NSA_HEADS), jax.ShapeDtypeStruct((T, LANES), F32),
                   jax.ShapeDtypeStruct((T, LANES), F32), hshape(G, 2 * LANES), hshape(G), hshape(G), hshape(G),
                   jax.ShapeDtypeStruct((3, T, GROUP_WIDTH), F32)],
        compiler_params=_cparams(("parallel",)),
        name="attention_prep",
    )(z, z, z, z, z, z, z, z, cos128, sin128)


def _lane_tile(x, n):
    return x if n == 1 else jnp.concatenate([x] * n, axis=-1)


def _softmax_update(s, v, m_ref, acc_ref):
    m_prev = m_ref[...]
    m_new = jnp.maximum(m_prev, jnp.max(s, axis=-1, keepdims=True))
    alpha = jnp.exp2(m_prev - m_new)
    p = jnp.exp2(s - _lane_tile(m_new, s.shape[1] // LANES))
    acc_ref[...] = _lane_tile(alpha, acc_ref.shape[1] // LANES) * acc_ref[...] + _dot(p.astype(BF16), v)
    m_ref[...] = m_new


def _causal_sweep(qs_ref, k_ref, v_ref, m_ref, acc_ref, s_ref, q0, tq, tk):
    m_ref[...] = jnp.full(m_ref.shape, -jnp.inf, F32)
    acc_ref[...] = jnp.zeros(acc_ref.shape, F32)

    def keys(ref, t):
        return ref[0, pl.ds(pl.multiple_of(t * tk, tk), tk), :]

    def scores(t):
        return _dot_nt(qs_ref[...], keys(k_ref, t))

    n_full = q0 // tk
    s_ref[...] = scores(0)

    def pair(i, c):
        t = 2 * i
        s_a = s_ref[...]
        s_b = scores(t + 1)
        _softmax_update(s_a, keys(v_ref, t), m_ref, acc_ref)
        s_c = scores(t + 2)
        _softmax_update(s_b, keys(v_ref, t + 1), m_ref, acc_ref)
        s_ref[...] = s_c
        return c

    lax.fori_loop(0, n_full // 2, pair, 0)

    @pl.when(n_full % 2 == 1)
    def _():
        s_next = scores(n_full)
        _softmax_update(s_ref[...], keys(v_ref, n_full - 1), m_ref, acc_ref)
        s_ref[...] = s_next
    s = s_ref[...]
    visible = n_full * tk + _lane(s.shape) <= q0 + (_row(s.shape) & (tq - 1))
    _softmax_update(jnp.where(visible, s, NEG), keys(v_ref, n_full), m_ref, acc_ref)


def _diff_attn_body(q_ref, k_ref, v_ref, lam_ref, sg_ref, o_ref, qs_ref, m_ref, acc_ref, s_ref, *,
                    tq, tk, lambda_init):
    q = q_ref[0]
    first = _lane(q.shape) < DA_QK_DIM
    qs_ref[0:tq, :] = jnp.where(first, q, jnp.zeros_like(q))
    qs_ref[tq:, :] = jnp.where(first, jnp.zeros_like(q), q)
    _causal_sweep(qs_ref, k_ref, v_ref, m_ref, acc_ref, s_ref, pl.program_id(1) * tq, tq, tk)
    lam = lam_ref[...]
    lam_full = (jnp.exp(jnp.sum(lam[0:1] * lam[1:2], axis=-1, keepdims=True))
                - jnp.exp(jnp.sum(lam[2:3] * lam[3:4], axis=-1, keepdims=True)) + lambda_init)
    o = acc_ref[:, 0:LANES] / acc_ref[:, LANES:]
    a = o[0:tq] - lam_full * o[tq:]
    o_ref[...] = (_rms(a, sg_ref[...]) * (1.0 - lambda_init)).astype(BF16)


def diff_attention(qd, kd, vd, lam, subln, lambda_init, tq, tk):
    H, T, _ = qd.shape
    return pl.pallas_call(
        functools.partial(_diff_attn_body, tq=tq, tk=tk, lambda_init=lambda_init),
        grid=(H, T // tq),
        in_specs=[pl.BlockSpec((1, tq, LANES), lambda h, qi: (h, qi, 0)),
                  pl.BlockSpec((1, T, LANES), lambda h, qi: (h, 0, 0)),
                  pl.BlockSpec((1, T, 2 * LANES), lambda h, qi: (h, 0, 0)),
                  pl.BlockSpec((4, DA_QK_DIM), lambda h, qi: (0, 0)),
                  pl.BlockSpec((1, LANES), lambda h, qi: (0, 0))],
        out_specs=pl.BlockSpec((tq, LANES), lambda h, qi: (qi, h)),
        out_shape=jax.ShapeDtypeStruct((T, GROUP_WIDTH), BF16),
        scratch_shapes=[pltpu.VMEM((2 * tq, LANES), BF16), pltpu.VMEM((2 * tq, LANES), F32),
                        pltpu.VMEM((2 * tq, 2 * LANES), F32), pltpu.VMEM((2 * tq, tk), F32)],
        compiler_params=_cparams(("parallel", "arbitrary")),
        name="diff_attention",
    )(qd, kd, vd, lam, subln)


def _compress_body(x_ref, pe_ref, w1_ref, w2_ref, o_ref):
    x = x_ref[0, 0]
    half = x.shape[1]
    a = _dot((x + pe_ref[0, 0:1, :]).astype(BF16), w1_ref[0, 0:half, :])
    b = _dot((x + pe_ref[0, 1:2, :]).astype(BF16), w1_ref[0, half:, :])
    hid = jax.nn.gelu(a + pltpu.roll(b, b.shape[0] - 1, 0))
    o_ref[0, 0] = _dot(hid.astype(BF16), w2_ref[0]).astype(BF16)


def compress(x16, pe2, w1, w2p):
    _, G, n, half = x16.shape
    hid = w1.shape[2]
    return pl.pallas_call(
        _compress_body,
        grid=(2, G),
        in_specs=[pl.BlockSpec((1, 1, n, half), lambda c, g: (c, g, 0, 0)),
                  pl.BlockSpec((1, 2, half), lambda c, g: (c, 0, 0)),
                  pl.BlockSpec((1, 2 * half, hid), lambda c, g: (c, 0, 0)),
                  pl.BlockSpec((1, hid, LANES), lambda c, g: (c, 0, 0))],
        out_specs=pl.BlockSpec((1, 1, n, LANES), lambda c, g: (c, g, 0, 0)),
        out_shape=jax.ShapeDtypeStruct((2, G, n, LANES), BF16),
        compiler_params=_cparams(("parallel", "parallel")),
        name="nsa_compress",
    )(x16, pe2, w1, w2p)


def _split3(x):
    hi = x.astype(BF16)
    r1 = x - hi.astype(F32)
    mid = r1.astype(BF16)
    lo = (r1 - mid.astype(F32)).astype(BF16)
    return hi, mid, lo


def _pack_heads(o, tq):
    pair = lambda a, b: a + pltpu.roll(b, 64, 1)
    return jnp.concatenate([pair(o[0:tq], o[tq:2 * tq]), pair(o[2 * tq:3 * tq], o[3 * tq:])], axis=-1)


def _normalize_low_half(acc):
    return jnp.where(_lane(acc.shape) < 64, acc / pltpu.roll(acc, 64, 1), 0.0)


def _cmp_select_body(q_ref, kc_ref, vc_ref, ovt_ref, gx_ref, o_ref, sb_ref, *, tq):
    qi = pl.program_id(1)
    hpg = q_ref.shape[0]
    q = q_ref[...].reshape(hpg * tq, LANES)
    s = _dot_nt(q, kc_ref[0, 0])
    t = qi * tq + (_row(s.shape) & (tq - 1))
    cmask = _lane(s.shape) * NSA_CMP_STRIDE + (NSA_CMP_LEN - 1) <= t
    s = jnp.where(cmask, s, NEG)
    e = jnp.exp2(s - jnp.max(s, axis=-1, keepdims=True))
    p = jnp.where(cmask, e / jnp.sum(e, axis=-1, keepdims=True), 0.0)
    o = _dot(p.astype(BF16), vc_ref[0, 0])
    o_ref[...] = (gx_ref[0] * _pack_heads(o, tq)).astype(BF16)

    psum = p[0:tq]
    for hh in range(1, hpg):
        psum = psum + p[hh * tq:(hh + 1) * tq]
    ovt = ovt_ref[...]
    imp = sum(_dot_nt(ovt, part) for part in _split3(psum))
    j = _row(imp.shape)
    cur = (qi * tq + _lane(imp.shape)) >> 6
    forced = (j == 0) | (j == cur) | (j == cur - 1)
    score = jnp.where(forced, FORCE, jnp.where(j <= cur, imp, -1.0))
    jf = j.astype(F32)
    sel = jnp.zeros(imp.shape, jnp.bool_)
    for _ in range(NSA_SLC_TOPK):
        m = jnp.max(score, axis=0, keepdims=True)
        first = jnp.min(jnp.where(score == m, jf, float(LANES)), axis=0, keepdims=True)
        hit = jf == first
        sel = sel | (hit & (m >= 0.0))
        score = jnp.where(hit, -2.0, score)
    sb_ref[0] = jnp.where(sel, 0.0, NEG).T.astype(BF16)


def cmp_select(qn, kvc, overlap_t, gx, tq):
    H, T, _ = qn.shape
    G = NSA_KV_HEADS
    hpg = H // G
    n = kvc.shape[2]
    return pl.pallas_call(
        functools.partial(_cmp_select_body, tq=tq),
        grid=(G, T // tq),
        in_specs=[pl.BlockSpec((hpg, tq, LANES), lambda g, qi: (g, qi, 0)),
                  pl.BlockSpec((1, 1, n, LANES), lambda g, qi: (0, g, 0, 0)),
                  pl.BlockSpec((1, 1, n, LANES), lambda g, qi: (1, g, 0, 0)),
                  pl.BlockSpec((LANES, n), lambda g, qi: (0, 0)),
                  pl.BlockSpec((1, tq, hpg * NSA_HEAD_DIM), lambda g, qi: (0, qi, g))],
        out_specs=[pl.BlockSpec((tq, hpg * NSA_HEAD_DIM), lambda g, qi: (qi, g)),
                   pl.BlockSpec((1, tq, LANES), lambda g, qi: (g, qi, 0))],
        out_shape=[jax.ShapeDtypeStruct((T, GROUP_WIDTH), BF16),
                   jax.ShapeDtypeStruct((G, T, LANES), BF16)],
        compiler_params=_cparams(("parallel", "parallel")),
        name="nsa_compressed_select",
    )(qn, kvc, kvc, overlap_t, gx)


def _sel_attn_body(q_ref, sb_ref, k_ref, v_ref, gx_ref, o_ref, qs_ref, m_ref, acc_ref, s_ref, *, tq, tk):
    hpg = q_ref.shape[0]
    for hh in range(hpg):
        qs_ref[hh * tq:(hh + 1) * tq, 0:LANES] = sb_ref[0]
        qs_ref[hh * tq:(hh + 1) * tq, LANES:] = q_ref[hh]
    _causal_sweep(qs_ref, k_ref, v_ref, m_ref, acc_ref, s_ref, pl.program_id(1) * tq, tq, tk)
    o = _normalize_low_half(acc_ref[...])
    o_ref[...] = (gx_ref[0] * _pack_heads(o, tq)).astype(BF16)


def sel_attention(qn, selbias, ks, vs, gx, tq, tk):
    H, T, _ = qn.shape
    G = NSA_KV_HEADS
    hpg = H // G
    return pl.pallas_call(
        functools.partial(_sel_attn_body, tq=tq, tk=tk),
        grid=(G, T // tq),
        in_specs=[pl.BlockSpec((hpg, tq, LANES), lambda g, qi: (g, qi, 0)),
                  pl.BlockSpec((1, tq, LANES), lambda g, qi: (g, qi, 0)),
                  pl.BlockSpec((1, T, 2 * LANES), lambda g, qi: (g, 0, 0)),
                  pl.BlockSpec((1, T, LANES), lambda g, qi: (g, 0, 0)),
                  pl.BlockSpec((1, tq, hpg * NSA_HEAD_DIM), lambda g, qi: (1, qi, g))],
        out_specs=pl.BlockSpec((tq, hpg * NSA_HEAD_DIM), lambda g, qi: (qi, g)),
        out_shape=jax.ShapeDtypeStruct((T, GROUP_WIDTH), BF16),
        scratch_shapes=[pltpu.VMEM((hpg * tq, 2 * LANES), BF16), pltpu.VMEM((hpg * tq, LANES), F32),
                        pltpu.VMEM((hpg * tq, LANES), F32), pltpu.VMEM((hpg * tq, tk), F32)],
        compiler_params=_cparams(("parallel", "arbitrary")),
        name="nsa_selected_attention",
    )(qn, selbias, ks, vs, gx)


def _win_attn_body(q_ref, k_ref, v_ref, gx_ref, o_ref, *, tq, span):
    hpg = q_ref.shape[0]
    q0 = pl.program_id(1) * tq
    lo = pl.multiple_of(jnp.maximum(q0 + tq - span, 0), tq)
    q = q_ref[...].reshape(hpg * tq, LANES)
    s = _dot_nt(q, k_ref[0, pl.ds(lo, span), :])
    qpos = q0 + (_row(s.shape) & (tq - 1))
    kpos = lo + _lane(s.shape)
    s = jnp.where((kpos <= qpos) & (kpos > qpos - NSA_WINDOW), s, NEG)
    p = jnp.exp2(s - jnp.max(s, axis=-1, keepdims=True))
    acc = _dot(p.astype(BF16), v_ref[0, pl.ds(lo, span), :])
    o_ref[...] = (gx_ref[0] * _pack_heads(_normalize_low_half(acc), tq)).astype(BF16)


def win_attention(qn, kw, vw, gx, tq):
    H, T, _ = qn.shape
    G = NSA_KV_HEADS
    hpg = H // G
    return pl.pallas_call(
        functools.partial(_win_attn_body, tq=tq, span=NSA_WINDOW + tq),
        grid=(G, T // tq),
        in_specs=[pl.BlockSpec((hpg, tq, LANES), lambda g, qi: (g, qi, 0)),
                  pl.BlockSpec((1, T, LANES), lambda g, qi: (g, 0, 0)),
                  pl.BlockSpec((1, T, LANES), lambda g, qi: (g, 0, 0)),
                  pl.BlockSpec((1, tq, hpg * NSA_HEAD_DIM), lambda g, qi: (2, qi, g))],
        out_specs=pl.BlockSpec((tq, hpg * NSA_HEAD_DIM), lambda g, qi: (qi, g)),
        out_shape=jax.ShapeDtypeStruct((T, GROUP_WIDTH), BF16),
        compiler_params=_cparams(("parallel", "arbitrary")),
        name="nsa_window_attention",
    )(qn, kw, vw, gx)


def _out_proj_body(a_ref, b_ref, c_ref, d1_ref, d2_ref, d3_ref, w_ref, x_ref, g_ref, o_ref, hn_ref, lhs_ref):
    lhs_ref[:, 0:GROUP_WIDTH] = a_ref[...]
    lhs_ref[:, GROUP_WIDTH:2 * GROUP_WIDTH] = b_ref[...]
    lhs_ref[:, 2 * GROUP_WIDTH:3 * GROUP_WIDTH] = c_ref[...]
    d = d1_ref[...].astype(F32) + d2_ref[...].astype(F32) + d3_ref[...].astype(F32)
    lhs_ref[:, 3 * GROUP_WIDTH:] = d.astype(BF16)
    y = x_ref[...] + _dot(lhs_ref[...], w_ref[0])
    o_ref[...] = y
    hn_ref[...] = _rms(y, g_ref[...]).astype(BF16)


def out_proj(parts, w, layer, x, g_next, tm):
    T, N = x.shape
    part = pl.BlockSpec((tm, GROUP_WIDTH), lambda i: (i, 0))
    rows = pl.BlockSpec((tm, N), lambda i: (i, 0))
    return pl.pallas_call(
        _out_proj_body,
        grid=(T // tm,),
        in_specs=[part] * 6 + [pl.BlockSpec((1, 4 * GROUP_WIDTH, N), lambda i: (layer, 0, 0)), rows,
                               pl.BlockSpec((1, N), lambda i: (0, 0))],
        out_specs=[rows, rows],
        out_shape=[jax.ShapeDtypeStruct((T, N), F32), jax.ShapeDtypeStruct((T, N), BF16)],
        scratch_shapes=[pltpu.VMEM((tm, 4 * GROUP_WIDTH), BF16)],
        compiler_params=_cparams(("parallel",)),
        name="out_proj",
    )(*parts, w, x, g_next)


def _group_starts(te_ref, i):
    return (i == 0) | (te_ref[i] != te_ref[jnp.maximum(i - 1, 0)])


def _up_body(te_ref, na_ref, h_ref, wg_ref, wu_ref, o_ref, wgb_ref, wub_ref):
    i = pl.program_id(1)

    @pl.when(i < na_ref[0])
    def _():
        @pl.when(_group_starts(te_ref, i))
        def _():
            wgb_ref[...] = wg_ref[0].astype(BF16)
            wub_ref[...] = wu_ref[0].astype(BF16)

        h = h_ref[...]
        a = _dot(h, wgb_ref[...])
        o_ref[...] = (a * jax.nn.sigmoid(a) * _dot(h, wub_ref[...])).astype(BF16)

    @pl.when(i >= na_ref[0])
    def _():
        o_ref[...] = jnp.zeros(o_ref.shape, BF16)


def swiglu_up(tile_expert, n_active, hn, wg, wu, tm, tn):
    P, K = hn.shape
    F = wg.shape[2]

    def row(j, i, te, na):
        return (jnp.minimum(i, na[0] - 1), 0)

    def wmap(j, i, te, na):
        return (te[i], 0, j)

    return pl.pallas_call(
        _up_body,
        grid_spec=pltpu.PrefetchScalarGridSpec(
            num_scalar_prefetch=2,
            grid=(F // tn, P // tm),
            in_specs=[pl.BlockSpec((tm, K), row), pl.BlockSpec((1, K, tn), wmap), pl.BlockSpec((1, K, tn), wmap)],
            out_specs=pl.BlockSpec((tm, tn), lambda j, i, te, na: (i, j)),
            scratch_shapes=[pltpu.VMEM((K, tn), BF16), pltpu.VMEM((K, tn), BF16)]),
        out_shape=jax.ShapeDtypeStruct((P, F), BF16),
        compiler_params=_cparams(("arbitrary", "arbitrary")),
        name="swiglu_up",
    )(tile_expert, n_active, hn, wg, wu)


def _down_body(te_ref, na_ref, a_ref, w_ref, e_ref, o_ref, wb_ref, *, scaled):
    i = pl.program_id(1)

    @pl.when(_group_starts(te_ref, i))
    def _():
        wb_ref[...] = w_ref[0].astype(BF16)

    y = _dot(a_ref[...], wb_ref[...])
    o_ref[...] = e_ref[...] * y if scaled else e_ref[...] + y


def swiglu_down(tile_expert, n_active, act, wd, extra, tm, tn, scaled):
    P, F = act.shape
    N = wd.shape[2]

    def row(j, i, te, na):
        return (jnp.minimum(i, na[0] - 1), 0)

    if scaled:
        extra_spec = pl.BlockSpec((tm, 1), lambda j, i, te, na: (i, 0))
    else:
        extra_spec = pl.BlockSpec((tm, tn), lambda j, i, te, na: (i, j))
    return pl.pallas_call(
        functools.partial(_down_body, scaled=scaled),
        grid_spec=pltpu.PrefetchScalarGridSpec(
            num_scalar_prefetch=2,
            grid=(N // tn, P // tm),
            in_specs=[pl.BlockSpec((tm, F), row),
                      pl.BlockSpec((1, F, tn), lambda j, i, te, na: (te[i], 0, j)),
                      extra_spec],
            out_specs=pl.BlockSpec((tm, tn), lambda j, i, te, na: (i, j)),
            scratch_shapes=[pltpu.VMEM((F, tn), BF16)]),
        out_shape=jax.ShapeDtypeStruct((P, N), F32),
        compiler_params=_cparams(("arbitrary", "arbitrary")),
        name="swiglu_down",
    )(tile_expert, n_active, act, wd, extra)


def _router_body(x_ref, g_ref, w_ref, o_ref):
    h = _rms(x_ref[...], g_ref[...])
    logits = jnp.dot(h, w_ref[...], preferred_element_type=F32, precision=lax.Precision.HIGHEST)
    lane = _lane(logits.shape)
    lf = lane.astype(F32)
    logits = jnp.where(lane < N_EXPERTS, logits, -jnp.inf)
    v0 = jnp.max(logits, axis=-1, keepdims=True)
    i0 = jnp.min(jnp.where(logits == v0, lf, float(LANES)), axis=-1, keepdims=True)
    rest = jnp.where(lf == i0, -jnp.inf, logits)
    v1 = jnp.max(rest, axis=-1, keepdims=True)
    i1 = jnp.min(jnp.where(rest == v1, lf, float(LANES)), axis=-1, keepdims=True)
    e1 = jnp.exp(v1 - v0)
    w0 = 1.0 / (1.0 + e1)
    w1 = e1 / (1.0 + e1)
    o_ref[...] = jnp.where(lane == 0, i0, jnp.where(lane == 1, i1, jnp.where(lane == 2, w0, w1)))


def router(x, g, w_pad, tm):
    T, K = x.shape
    return pl.pallas_call(
        _router_body,
        grid=(T // tm,),
        in_specs=[pl.BlockSpec((tm, K), lambda i: (i, 0)),
                  pl.BlockSpec((1, K), lambda i: (0, 0)),
                  pl.BlockSpec((K, LANES), lambda i: (0, 0))],
        out_specs=pl.BlockSpec((tm, LANES), lambda i: (i, 0)),
        out_shape=jax.ShapeDtypeStruct((T, LANES), F32),
        compiler_params=_cparams(("parallel",)),
        name="moe_router",
    )(x, g, w_pad)


ROW_DMA_UNROLL = 8


def _row_copy(src_hbm, row, dst_ref, r, sem):
    return pltpu.make_async_copy(src_hbm.at[pl.ds(row, 1)], dst_ref.at[pl.ds(r, 1)], sem)


def _gather_body(tok_ref, na_ref, x_hbm, g_ref, o_ref, buf_ref, sem):
    i = pl.program_id(0)
    tm = buf_ref.shape[0]

    @pl.when(i < na_ref[0])
    def _():
        def start(r, c):
            _row_copy(x_hbm, tok_ref[i * tm + r], buf_ref, r, sem).start()
            return c

        def wait(r, c):
            _row_copy(x_hbm, 0, buf_ref, r, sem).wait()
            return c

        lax.fori_loop(0, tm, start, 0, unroll=ROW_DMA_UNROLL)
        lax.fori_loop(0, tm, wait, 0, unroll=ROW_DMA_UNROLL)
        o_ref[...] = _rms(buf_ref[...], g_ref[...]).astype(BF16)

    @pl.when(i >= na_ref[0])
    def _():
        o_ref[...] = jnp.zeros(o_ref.shape, BF16)


def gather_norm_tokens(tok_of_slot, n_active, x, g, tm):
    P = tok_of_slot.shape[0]
    K = x.shape[1]
    return pl.pallas_call(
        _gather_body,
        grid_spec=pltpu.PrefetchScalarGridSpec(
            num_scalar_prefetch=2,
            grid=(P // tm,),
            in_specs=[pl.BlockSpec(memory_space=pl.ANY), pl.BlockSpec((1, K), lambda i, tok, na: (0, 0))],
            out_specs=pl.BlockSpec((tm, K), lambda i, tok, na: (i, 0)),
            scratch_shapes=[pltpu.VMEM((tm, K), F32), pltpu.SemaphoreType.DMA(())]),
        out_shape=jax.ShapeDtypeStruct((P, K), BF16),
        compiler_params=_cparams(("arbitrary",)),
        name="moe_gather",
    )(tok_of_slot, n_active, x, g)


def _combine_body(s0_ref, s1_ref, x_ref, y_hbm, g_ref, o_ref, b0_ref, b1_ref, sem0, sem1, *, final):
    tm = x_ref.shape[0]
    base = pl.program_id(0) * tm

    def start(r, c):
        _row_copy(y_hbm, s0_ref[base + r], b0_ref, r, sem0).start()
        _row_copy(y_hbm, s1_ref[base + r], b1_ref, r, sem1).start()
        return c

    def wait(r, c):
        _row_copy(y_hbm, 0, b0_ref, r, sem0).wait()
        _row_copy(y_hbm, 0, b1_ref, r, sem1).wait()
        return c

    lax.fori_loop(0, tm, start, 0, unroll=ROW_DMA_UNROLL)
    lax.fori_loop(0, tm, wait, 0, unroll=ROW_DMA_UNROLL)
    y = x_ref[...] + b0_ref[...] + b1_ref[...]
    o_ref[...] = _rms(y, g_ref[...]) if final else y


def moe_combine(slot0, slot1, x, ys, g, tm, final):
    T, K = x.shape
    return pl.pallas_call(
        functools.partial(_combine_body, final=final),
        grid_spec=pltpu.PrefetchScalarGridSpec(
            num_scalar_prefetch=2,
            grid=(T // tm,),
            in_specs=[pl.BlockSpec((tm, K), lambda i, s0, s1: (i, 0)),
                      pl.BlockSpec(memory_space=pl.ANY),
                      pl.BlockSpec((1, K), lambda i, s0, s1: (0, 0))],
            out_specs=pl.BlockSpec((tm, K), lambda i, s0, s1: (i, 0)),
            scratch_shapes=[pltpu.VMEM((tm, K), F32), pltpu.VMEM((tm, K), F32),
                            pltpu.SemaphoreType.DMA(()), pltpu.SemaphoreType.DMA(())]),
        out_shape=jax.ShapeDtypeStruct((T, K), F32),
        compiler_params=_cparams(("arbitrary",)),
        name="moe_combine",
    )(slot0, slot1, x, ys, g)


def _final_norm_body(x_ref, g_ref, o_ref):
    o_ref[...] = _rms(x_ref[...], g_ref[...])


def final_norm(x, g, tm):
    T, K = x.shape
    return pl.pallas_call(
        _final_norm_body,
        grid=(T // tm,),
        in_specs=[pl.BlockSpec((tm, K), lambda i: (i, 0)), pl.BlockSpec((1, K), lambda i: (0, 0))],
        out_specs=pl.BlockSpec((tm, K), lambda i: (i, 0)),
        out_shape=jax.ShapeDtypeStruct((T, K), F32),
        compiler_params=_cparams(("parallel",)),
        name="final_norm",
    )(x, g)


def _routing_tables(route, tm, expert_base):
    T = route.shape[0]
    top_i = route[:, 0:2].astype(jnp.int32)
    gate = route[:, 2:4]
    e_flat = top_i.reshape(-1)
    onehot = (e_flat[:, None] == jnp.arange(N_EXPERTS)[None, :]).astype(jnp.int32)
    rank = jnp.take_along_axis(jnp.cumsum(onehot, axis=0) - onehot, e_flat[:, None], axis=1)[:, 0]
    count = jnp.sum(onehot, axis=0)
    padded = ((count + tm - 1) // tm) * tm
    end = jnp.cumsum(padded)
    start = end - padded
    slot = start[e_flat] + rank
    n_slots = 2 * T + N_EXPERTS * tm
    n_tiles = n_slots // tm
    n_active = (end[-1] // tm).astype(jnp.int32)
    tile_start = jnp.minimum(jnp.arange(n_tiles, dtype=jnp.int32), n_active - 1) * tm
    tile_expert = jnp.minimum(jnp.sum(tile_start[:, None] >= end[None, :], axis=1), N_EXPERTS - 1)
    tok_of_slot = jnp.zeros((n_slots,), jnp.int32).at[slot].set(jnp.arange(2 * T, dtype=jnp.int32) // 2)
    gate_of_slot = jnp.zeros((n_slots,), F32).at[slot].set(gate.reshape(-1))
    slot2 = slot.reshape(T, 2).astype(jnp.int32)
    return ((tile_expert + expert_base).astype(jnp.int32), n_active.reshape(1), tok_of_slot,
            gate_of_slot[:, None], slot2[:, 0], slot2[:, 1])


def _rope_tables128(T):
    half = NSA_HEAD_DIM // 2
    inv = ROPE_THETA ** (-jnp.arange(0, NSA_HEAD_DIM, 2, dtype=F32) / NSA_HEAD_DIM)
    ang = jnp.arange(T, dtype=F32)[:, None] * inv[None, :]
    cos, sin = jnp.cos(ang), jnp.sin(ang)
    cos128 = jnp.tile(cos, (1, LANES // half))
    sin128 = jnp.tile(jnp.concatenate([-sin, sin], axis=1), (1, LANES // NSA_HEAD_DIM))
    return cos128, sin128


def _overlap_matrix_t(n_cmp_pad):
    sstart = jnp.arange(LANES) * NSA_SLC_LEN
    cstart = jnp.arange(n_cmp_pad) * NSA_CMP_STRIDE
    ov = (cstart[None, :] < sstart[:, None] + NSA_SLC_LEN) & (cstart[None, :] + NSA_CMP_LEN > sstart[:, None])
    return ov.astype(BF16)


def kernel(x, attn_norm, w_in, w_out, gm_ln_g, gm_ln_b, gm_ws, gm_bs, da_lambda, da_subln, cv_dw_w, cv_dw_b,
           cv_ln_g, cv_ln_b, nsa_cmp_w1, nsa_cmp_w2, nsa_cmp_pe, ffn_norm, ffn_wg, ffn_wu, ffn_wd, router_w,
           exp_wg, exp_wu, exp_wd, final_norm_g):
    B, T, D = x.shape
    assert B == 1 and D == D_MODEL and T % 1024 == 0 and T // NSA_SLC_LEN <= LANES
    depth = w_in.shape[0]
    G = NSA_KV_HEADS
    n16 = T // NSA_CMP_STRIDE
    tm = 512

    cos128, sin128 = _rope_tables128(T)
    overlap_t = _overlap_matrix_t(n16)
    dense_na = jnp.full((1,), T // tm, jnp.int32)
    row = lambda v: v.reshape(1, -1)
    merge = lambda w: w.reshape((-1,) + w.shape[2:])
    exp_wg, exp_wu, exp_wd = merge(exp_wg), merge(exp_wu), merge(exp_wd)
    w_in_b = jnp.pad(w_in, ((0, 0), (0, 0), (0, IN_WIDTH_PAD - IN_WIDTH))).astype(BF16)
    w_out_b = w_out.astype(BF16)

    xs = x[0]
    for l in range(depth):
        lambda_init = 0.8 - 0.6 * math.exp(-0.3 * l)
        z = norm_mm(xs, row(attn_norm[l]), w_in_b, l, 1024, IN_WIDTH_PAD // 3)

        bs_rows = jnp.repeat(gm_bs[l].T, GM_CHUNK, axis=1)
        o_a = gmlp(z, row(gm_ln_g[l]), row(gm_ln_b[l]), gm_ws[l], bs_rows, 512)
        dw_w = jnp.pad(cv_dw_w[l], ((0, CV_HALO - CV_WIDTH), (0, 0)))
        o_c = conformer_conv(z, dw_w, row(cv_dw_b[l]), row(cv_ln_g[l]), row(cv_ln_b[l]), 256)

        qd, kd, vd, qn, kc, vc, ks, vs, kw, vw, gx = prep(z, cos128, sin128, 256)
        o_b = diff_attention(qd, kd, vd, da_lambda[l], row(da_subln[l]), lambda_init, 512, 512)

        x16 = jnp.stack([kc, vc]).reshape(2, T, G, NSA_HEAD_DIM).transpose(0, 2, 1, 3)
        x16 = x16.reshape(2, G, n16, NSA_CMP_STRIDE * NSA_HEAD_DIM)
        pe2 = nsa_cmp_pe[l].reshape(2, 2, NSA_CMP_STRIDE * NSA_HEAD_DIM)
        w2p = jnp.pad(nsa_cmp_w2[l], ((0, 0), (0, 0), (0, LANES - NSA_HEAD_DIM))).astype(BF16)
        kvc = compress(x16, pe2, nsa_cmp_w1[l].astype(BF16), w2p)
        o_cmp, selbias = cmp_select(qn, kvc, overlap_t, gx, 128)
        o_sel = sel_attention(qn, selbias, ks, vs, gx, 256, 512)
        o_win = win_attention(qn, kw, vw, gx, 256)

        g_ffn = row(ffn_norm[l])
        xs, hn = out_proj((o_a, o_b, o_c, o_cmp, o_sel, o_win), w_out_b, l, xs, g_ffn, 256)

        e = l // 2
        if l % 2 == 0:
            dense_te = jnp.full((T // tm,), e, jnp.int32)
            act = swiglu_up(dense_te[:T // 1024], dense_na // 2, hn, ffn_wg, ffn_wu, 1024, 512)
            xs = swiglu_down(dense_te, dense_na, act, ffn_wd, xs, tm, 512, scaled=False)
            if l == depth - 1:
                xs = final_norm(xs, row(final_norm_g), tm)
        else:
            w_r = jnp.pad(router_w[e], ((0, 0), (0, LANES - N_EXPERTS)))
            route = router(xs, g_ffn, w_r, tm)
            tile_expert, n_active, tok_of_slot, gate_of_slot, slot0, slot1 = _routing_tables(
                route, tm, e * N_EXPERTS)
            hg = gather_norm_tokens(tok_of_slot, n_active, xs, g_ffn, tm)
            act = swiglu_up(tile_expert, n_active, hg, exp_wg, exp_wu, tm, 512)
            ys = swiglu_down(tile_expert, n_active, act, exp_wd, gate_of_slot, tm, 512, scaled=True)
            xs = moe_combine(slot0, slot1, xs, ys, row(final_norm_g), 256, final=(l == depth - 1))
    return xs[None]
```

```python
import functools
import math

import jax
import jax.numpy as jnp
from jax import lax
from jax.experimental import pallas as pl
from jax.experimental.pallas import tpu as pltpu

F32 = jnp.float32
BF16 = jnp.bfloat16

D_MODEL = 2048
GROUP_WIDTH = 512
GM_CHUNK = 128
GM_HEADS = 4
DA_HEADS = 4
DA_QK_DIM = 64
CV_WIDTH = 31
NSA_HEADS = 8
NSA_KV_HEADS = 2
NSA_HEAD_DIM = 64
NSA_CMP_LEN = 32
NSA_CMP_STRIDE = 16
NSA_SLC_LEN = 64
NSA_SLC_TOPK = 16
NSA_WINDOW = 512
ROPE_THETA = 10000.0
NORM_EPS = 1e-6
NEG = -1e30
FORCE = 1e9
N_EXPERTS = 8
LANES = 128
LOG2E = math.log2(math.e)

IN_WIDTH = 4888
IN_WIDTH_PAD = 4992
COL_GM = 0
COL_QDA = 1024
COL_KDA = 1536
COL_VDA = 2048
COL_CV = 2560
COL_QNS = 3584
COL_KVNS = 4096
COL_GNS = 4864

VMEM_LIMIT = 56 * 1024 * 1024


def _cparams(sem, **kw):
    return pltpu.CompilerParams(dimension_semantics=sem, vmem_limit_bytes=VMEM_LIMIT, **kw)


def _rms(x, g):
    ms = jnp.mean(x * x, axis=-1, keepdims=True)
    return x * lax.rsqrt(ms + NORM_EPS) * g


def _layer_norm(x, g, b):
    mu = jnp.mean(x, axis=-1, keepdims=True)
    xc = x - mu
    var = jnp.mean(xc * xc, axis=-1, keepdims=True)
    return xc * lax.rsqrt(var + NORM_EPS) * g + b


def _dot(a, b):
    return jnp.dot(a, b, preferred_element_type=F32)


def _dot_nt(a, b):
    return lax.dot_general(a, b, (((1,), (1,)), ((), ())), preferred_element_type=F32)


def _lane(shape):
    return lax.broadcasted_iota(jnp.int32, shape, len(shape) - 1)


def _row(shape):
    return lax.broadcasted_iota(jnp.int32, shape, len(shape) - 2)


def _norm_mm_body(x_ref, g_ref, w_ref, o_ref, hn_ref):
    @pl.when(pl.program_id(1) == 0)
    def _():
        hn_ref[...] = _rms(x_ref[...], g_ref[...]).astype(BF16)

    o_ref[...] = _dot(hn_ref[...], w_ref[0])


def norm_mm(x, g, w, layer, tm, tn):
    T, K = x.shape
    N = w.shape[2]
    return pl.pallas_call(
        _norm_mm_body,
        grid=(T // tm, N // tn),
        in_specs=[pl.BlockSpec((tm, K), lambda i, j: (i, 0)),
                  pl.BlockSpec((1, K), lambda i, j: (0, 0)),
                  pl.BlockSpec((1, K, tn), lambda i, j: (layer, 0, j))],
        out_specs=pl.BlockSpec((tm, tn), lambda i, j: (i, j)),
        out_shape=jax.ShapeDtypeStruct((T, N), F32),
        scratch_shapes=[pltpu.VMEM((tm, K), BF16)],
        compiler_params=_cparams(("parallel", "arbitrary")),
        name="norm_in_proj",
    )(x, g, w)


def _gmlp_body(z_ref, g_ref, b_ref, ws_ref, bs_ref, o_ref):
    tr = z_ref.shape[0]
    z = jax.nn.gelu(z_ref[...])
    u = z[:, :GROUP_WIDTH]
    v = _layer_norm(z[:, GROUP_WIDTH:], g_ref[...], b_ref[...]).astype(BF16)
    causal = _row((GM_CHUNK, GM_CHUNK)) >= _lane((GM_CHUNK, GM_CHUNK))
    bias = bs_ref[...]
    for h in range(GM_HEADS):
        w = jnp.where(causal, ws_ref[h], 0.0).astype(BF16)
        cols = slice(h * LANES, (h + 1) * LANES)
        for c in range(tr // GM_CHUNK):
            rows = slice(c * GM_CHUNK, (c + 1) * GM_CHUNK)
            s = _dot(w, v[rows, cols]) + bias[:, cols]
            o_ref[rows, cols] = (u[rows, cols] * s).astype(BF16)


def gmlp(z, ln_g, ln_b, ws, bs_rows, tr):
    T = z.shape[0]
    return pl.pallas_call(
        _gmlp_body,
        grid=(T // tr,),
        in_specs=[pl.BlockSpec((tr, 2 * GROUP_WIDTH), lambda i: (i, COL_GM // (2 * GROUP_WIDTH))),
                  pl.BlockSpec((1, GROUP_WIDTH), lambda i: (0, 0)),
                  pl.BlockSpec((1, GROUP_WIDTH), lambda i: (0, 0)),
                  pl.BlockSpec((GM_HEADS, GM_CHUNK, GM_CHUNK), lambda i: (0, 0, 0)),
                  pl.BlockSpec((GM_CHUNK, GROUP_WIDTH), lambda i: (0, 0))],
        out_specs=pl.BlockSpec((tr, GROUP_WIDTH), lambda i: (i, 0)),
        out_shape=jax.ShapeDtypeStruct((T, GROUP_WIDTH), BF16),
        compiler_params=_cparams(("parallel",)),
        name="gmlp",
    )(z, ln_g, ln_b, ws, bs_rows)


CV_HALO = 32
CV_SUB = 64


def _conv_body(a_ref, g_ref, ap_ref, gp_ref, w_ref, b_ref, lg_ref, lb_ref, o_ref, hs_ref):
    tr = a_ref.shape[0]
    first = pl.program_id(0) == 0
    prev = ap_ref[...] * jax.nn.sigmoid(gp_ref[...])
    hs_ref[0:CV_HALO, :] = jnp.where(first, 0.0, prev)
    hs_ref[CV_HALO:, :] = a_ref[...] * jax.nn.sigmoid(g_ref[...])
    w = w_ref[...]
    for r0 in range(0, tr, CV_SUB):
        acc = jnp.zeros((CV_SUB, GROUP_WIDTH), F32)
        for k in range(CV_WIDTH):
            start = r0 + CV_HALO - (CV_WIDTH - 1) + k
            acc = acc + hs_ref[start:start + CV_SUB, :] * w[k:k + 1, :]
        y = _layer_norm(acc + b_ref[...], lg_ref[...], lb_ref[...])
        o_ref[r0:r0 + CV_SUB, :] = (y * jax.nn.sigmoid(y)).astype(BF16)


def conformer_conv(z, dw_w, dw_b, ln_g, ln_b, tr):
    T = z.shape[0]
    ca = COL_CV // GROUP_WIDTH
    per = tr // CV_HALO

    def halo(col):
        return lambda i: (jnp.maximum(i * per - 1, 0), col)

    return pl.pallas_call(
        _conv_body,
        grid=(T // tr,),
        in_specs=[pl.BlockSpec((tr, GROUP_WIDTH), lambda i: (i, ca)),
                  pl.BlockSpec((tr, GROUP_WIDTH), lambda i: (i, ca + 1)),
                  pl.BlockSpec((CV_HALO, GROUP_WIDTH), halo(ca)),
                  pl.BlockSpec((CV_HALO, GROUP_WIDTH), halo(ca + 1)),
                  pl.BlockSpec((CV_HALO, GROUP_WIDTH), lambda i: (0, 0)),
                  pl.BlockSpec((1, GROUP_WIDTH), lambda i: (0, 0)),
                  pl.BlockSpec((1, GROUP_WIDTH), lambda i: (0, 0)),
                  pl.BlockSpec((1, GROUP_WIDTH), lambda i: (0, 0))],
        out_specs=pl.BlockSpec((tr, GROUP_WIDTH), lambda i: (i, 0)),
        out_shape=jax.ShapeDtypeStruct((T, GROUP_WIDTH), BF16),
        scratch_shapes=[pltpu.VMEM((tr + CV_HALO, GROUP_WIDTH), F32)],
        compiler_params=_cparams(("parallel",)),
        name="conformer_conv",
    )(z, z, z, z, dw_w, dw_b, ln_g, ln_b)


def _rope128(x, cos, sin_signed):
    lo = (_lane(x.shape) & 63) < 32
    rot = jnp.where(lo, pltpu.roll(x, 96, 1), pltpu.roll(x, 32, 1))
    return x * cos + rot * sin_signed


def _low_half(x, fill=0.0):
    return jnp.where(_lane(x.shape) < 64, x, fill)


def _high_half_to_low(x, fill=0.0):
    return jnp.where(_lane(x.shape) < 64, pltpu.roll(x, 64, 1), fill)


def _prep_body(qd_ref, kd_ref, vd_ref, qn_ref, kvc_ref, kvs_ref, kvw_ref, gt_ref, cos_ref, sin_ref,
               qd_o, kd_o, vd_o, qn_o, kc_o, vc_o, ks_o, vs_o, kw_o, vw_o, gx_o):
    tr = cos_ref.shape[0]
    cos = cos_ref[...]
    sin = sin_ref[...]
    q_scale = DA_QK_DIM ** -0.5 * LOG2E
    ones = jnp.ones((tr, LANES), BF16)
    for h in range(DA_HEADS):
        cols = slice(h * LANES, (h + 1) * LANES)
        qd_o[h] = (_rope128(qd_ref[:, cols], cos, sin) * q_scale).astype(BF16)
        kd_o[h] = _rope128(kd_ref[:, cols], cos, sin).astype(BF16)
        vd_o[h, :, 0:LANES] = vd_ref[:, cols].astype(BF16)
        vd_o[h, :, LANES:] = ones
    for c in range(NSA_HEADS // 2):
        q = _rope128(qn_ref[:, c * LANES:(c + 1) * LANES], cos, sin) * (NSA_HEAD_DIM ** -0.5 * LOG2E)
        qn_o[2 * c] = _low_half(q).astype(BF16)
        qn_o[2 * c + 1] = _high_half_to_low(q).astype(BF16)
    kc_o[...] = _rope128(kvc_ref[:, :LANES], cos, sin)
    vc_o[...] = kvc_ref[:, LANES:]
    blk = (pl.program_id(0) * tr + _row((tr, LANES))) >> 6
    onehot = jnp.where(blk == _lane((tr, LANES)), 1.0, 0.0).astype(BF16)
    k = _rope128(kvs_ref[:, :LANES], cos, sin)
    for g, half in enumerate((_low_half, _high_half_to_low)):
        ks_o[g, :, 0:LANES] = onehot
        ks_o[g, :, LANES:] = half(k).astype(BF16)
        vs_o[g] = half(kvs_ref[:, LANES:], 1.0).astype(BF16)
    k = _rope128(kvw_ref[:, :LANES], cos, sin)
    for g, half in enumerate((_low_half, _high_half_to_low)):
        kw_o[g] = half(k).astype(BF16)
        vw_o[g] = half(kvw_ref[:, LANES:], 1.0).astype(BF16)
    gate = jax.nn.sigmoid(gt_ref[...])
    low = _lane((tr, LANES)) < 64
    for c in range(3):
        for hp in range(NSA_HEADS // 2):
            j = c * NSA_HEADS + 2 * hp
            gx_o[c, :, hp * LANES:(hp + 1) * LANES] = jnp.where(low, gate[:, j:j + 1], gate[:, j + 1:j + 2])


def prep(z, cos128, sin128, tr):
    T = z.shape[0]
    G = NSA_KV_HEADS

    def zspec(width, col):
        return pl.BlockSpec((tr, width), lambda i: (i, col // width))

    def heads(n, width=LANES):
        return pl.BlockSpec((n, tr, width), lambda i: (0, i, 0))

    def hshape(n, width=LANES):
        return jax.ShapeDtypeStruct((n, T, width), BF16)

    row128 = pl.BlockSpec((tr, LANES), lambda i: (i, 0))
    return pl.pallas_call(
        _prep_body,
        grid=(T // tr,),
        in_specs=[zspec(512, COL_QDA), zspec(512, COL_KDA), zspec(512, COL_VDA), zspec(512, COL_QNS),
                  zspec(256, COL_KVNS), zspec(256, COL_KVNS + 256), zspec(256, COL_KVNS + 512),
                  zspec(128, COL_GNS), row128, row128],
        out_specs=[heads(DA_HEADS), heads(DA_HEADS), heads(DA_HEADS, 2 * LANES),
                   heads(NSA_HEADS), row128, row128, heads(G, 2 * LANES), heads(G), heads(G), heads(G),
                   pl.BlockSpec((3, tr, GROUP_WIDTH), lambda i: (0, i, 0))],
        out_shape=[hshape(DA_HEADS), hshape(DA_HEADS), hshape(DA_HEADS, 2 * LANES),
                   hshape(NSA_HEADS), jax.ShapeDtypeStruct((T, LANES), F32),
                   jax.ShapeDtypeStruct((T, LANES), F32), hshape(G, 2 * LANES), hshape(G), hshape(G), hshape(G),
                   jax.ShapeDtypeStruct((3, T, GROUP_WIDTH), F32)],
        compiler_params=_cparams(("parallel",)),
        name="attention_prep",
    )(z, z, z, z, z, z, z, z, cos128, sin128)


def _lane_tile(x, n):
    return x if n == 1 else jnp.concatenate([x] * n, axis=-1)


def _softmax_update(s, v, m_ref, acc_ref):
    m_prev = m_ref[...]
    m_new = jnp.maximum(m_prev, jnp.max(s, axis=-1, keepdims=True))
    alpha = jnp.exp2(m_prev - m_new)
    p = jnp.exp2(s - _lane_tile(m_new, s.shape[1] // LANES))
    acc_ref[...] = _lane_tile(alpha, acc_ref.shape[1] // LANES) * acc_ref[...] + _dot(p.astype(BF16), v)
    m_ref[...] = m_new


def _causal_sweep(qs_ref, k_ref, v_ref, m_ref, acc_ref, s_ref, q0, tq, tk):
    m_ref[...] = jnp.full(m_ref.shape, -jnp.inf, F32)
    acc_ref[...] = jnp.zeros(acc_ref.shape, F32)

    def keys(ref, t):
        return ref[0, pl.ds(pl.multiple_of(t * tk, tk), tk), :]

    def scores(t):
        return _dot_nt(qs_ref[...], keys(k_ref, t))

    n_full = q0 // tk
    s_ref[...] = scores(0)

    def pair(i, c):
        t = 2 * i
        s_a = s_ref[...]
        s_b = scores(t + 1)
        _softmax_update(s_a, keys(v_ref, t), m_ref, acc_ref)
        s_c = scores(t + 2)
        _softmax_update(s_b, keys(v_ref, t + 1), m_ref, acc_ref)
        s_ref[...] = s_c
        return c

    lax.fori_loop(0, n_full // 2, pair, 0)

    @pl.when(n_full % 2 == 1)
    def _():
        s_next = scores(n_full)
        _softmax_update(s_ref[...], keys(v_ref, n_full - 1), m_ref, acc_ref)
        s_ref[...] = s_next
    s = s_ref[...]
    visible = n_full * tk + _lane(s.shape) <= q0 + (_row(s.shape) & (tq - 1))
    _softmax_update(jnp.where(visible, s, NEG), keys(v_ref, n_full), m_ref, acc_ref)


def _diff_attn_body(q_ref, k_ref, v_ref, lam_ref, sg_ref, o_ref, qs_ref, m_ref, acc_ref, s_ref, *,
                    tq, tk, lambda_init):
    q = q_ref[0]
    first = _lane(q.shape) < DA_QK_DIM
    qs_ref[0:tq, :] = jnp.where(first, q, jnp.zeros_like(q))
    qs_ref[tq:, :] = jnp.where(first, jnp.zeros_like(q), q)
    _causal_sweep(qs_ref, k_ref, v_ref, m_ref, acc_ref, s_ref, pl.program_id(1) * tq, tq, tk)
    lam = lam_ref[...]
    lam_full = (jnp.exp(jnp.sum(lam[0:1] * lam[1:2], axis=-1, keepdims=True))
                - jnp.exp(jnp.sum(lam[2:3] * lam[3:4], axis=-1, keepdims=True)) + lambda_init)
    o = acc_ref[:, 0:LANES] / acc_ref[:, LANES:]
    a = o[0:tq] - lam_full * o[tq:]
    o_ref[...] = (_rms(a, sg_ref[...]) * (1.0 - lambda_init)).astype(BF16)


def diff_attention(qd, kd, vd, lam, subln, lambda_init, tq, tk):
    H, T, _ = qd.shape
    return pl.pallas_call(
        functools.partial(_diff_attn_body, tq=tq, tk=tk, lambda_init=lambda_init),
        grid=(H, T // tq),
        in_specs=[pl.BlockSpec((1, tq, LANES), lambda h, qi: (h, qi, 0)),
                  pl.BlockSpec((1, T, LANES), lambda h, qi: (h, 0, 0)),
                  pl.BlockSpec((1, T, 2 * LANES), lambda h, qi: (h, 0, 0)),
                  pl.BlockSpec((4, DA_QK_DIM), lambda h, qi: (0, 0)),
                  pl.BlockSpec((1, LANES), lambda h, qi: (0, 0))],
        out_specs=pl.BlockSpec((tq, LANES), lambda h, qi: (qi, h)),
        out_shape=jax.ShapeDtypeStruct((T, GROUP_WIDTH), BF16),
        scratch_shapes=[pltpu.VMEM((2 * tq, LANES), BF16), pltpu.VMEM((2 * tq, LANES), F32),
                        pltpu.VMEM((2 * tq, 2 * LANES), F32), pltpu.VMEM((2 * tq, tk), F32)],
        compiler_params=_cparams(("parallel", "arbitrary")),
        name="diff_attention",
    )(qd, kd, vd, lam, subln)


def _compress_body(x_ref, pe_ref, w1_ref, w2_ref, o_ref):
    x = x_ref[0, 0]
    half = x.shape[1]
    a = _dot((x + pe_ref[0, 0:1, :]).astype(BF16), w1_ref[0, 0:half, :])
    b = _dot((x + pe_ref[0, 1:2, :]).astype(BF16), w1_ref[0, half:, :])
    hid = jax.nn.gelu(a + pltpu.roll(b, b.shape[0] - 1, 0))
    o_ref[0, 0] = _dot(hid.astype(BF16), w2_ref[0]).astype(BF16)


def compress(x16, pe2, w1, w2p):
    _, G, n, half = x16.shape
    hid = w1.shape[2]
    return pl.pallas_call(
        _compress_body,
        grid=(2, G),
        in_specs=[pl.BlockSpec((1, 1, n, half), lambda c, g: (c, g, 0, 0)),
                  pl.BlockSpec((1, 2, half), lambda c, g: (c, 0, 0)),
                  pl.BlockSpec((1, 2 * half, hid), lambda c, g: (c, 0, 0)),
                  pl.BlockSpec((1, hid, LANES), lambda c, g: (c, 0, 0))],
        out_specs=pl.BlockSpec((1, 1, n, LANES), lambda c, g: (c, g, 0, 0)),
        out_shape=jax.ShapeDtypeStruct((2, G, n, LANES), BF16),
        compiler_params=_cparams(("parallel", "parallel")),
        name="nsa_compress",
    )(x16, pe2, w1, w2p)


def _split3(x):
    hi = x.astype(BF16)
    r1 = x - hi.astype(F32)
    mid = r1.astype(BF16)
    lo = (r1 - mid.astype(F32)).astype(BF16)
    return hi, mid, lo


def _pack_heads(o, tq):
    pair = lambda a, b: a + pltpu.roll(b, 64, 1)
    return jnp.concatenate([pair(o[0:tq], o[tq:2 * tq]), pair(o[2 * tq:3 * tq], o[3 * tq:])], axis=-1)


def _normalize_low_half(acc):
    return jnp.where(_lane(acc.shape) < 64, acc / pltpu.roll(acc, 64, 1), 0.0)


def _cmp_select_body(q_ref, kc_ref, vc_ref, ovt_ref, gx_ref, o_ref, sb_ref, *, tq):
    qi = pl.program_id(1)
    hpg = q_ref.shape[0]
    q = q_ref[...].reshape(hpg * tq, LANES)
    s = _dot_nt(q, kc_ref[0, 0])
    t = qi * tq + (_row(s.shape) & (tq - 1))
    cmask = _lane(s.shape) * NSA_CMP_STRIDE + (NSA_CMP_LEN - 1) <= t
    s = jnp.where(cmask, s, NEG)
    e = jnp.exp2(s - jnp.max(s, axis=-1, keepdims=True))
    p = jnp.where(cmask, e / jnp.sum(e, axis=-1, keepdims=True), 0.0)
    o = _dot(p.astype(BF16), vc_ref[0, 0])
    o_ref[...] = (gx_ref[0] * _pack_heads(o, tq)).astype(BF16)

    psum = p[0:tq]
    for hh in range(1, hpg):
        psum = psum + p[hh * tq:(hh + 1) * tq]
    ovt = ovt_ref[...]
    imp = sum(_dot_nt(ovt, part) for part in _split3(psum))
    j = _row(imp.shape)
    cur = (qi * tq + _lane(imp.shape)) >> 6
    forced = (j == 0) | (j == cur) | (j == cur - 1)
    score = jnp.where(forced, FORCE, jnp.where(j <= cur, imp, -1.0))
    jf = j.astype(F32)
    bias = jnp.full(imp.shape, NEG, F32)
    for _ in range(NSA_SLC_TOPK):
        m = jnp.max(score, axis=0, keepdims=True)
        first = jnp.min(jnp.where(score == m, jf, float(LANES)), axis=0, keepdims=True)
        hit = jf == first
        bias = jnp.where(hit, jnp.where(m >= 0.0, 0.0, NEG), bias)
        score = jnp.where(hit, -2.0, score)
    sb_ref[0] = bias.T.astype(BF16)


def cmp_select(qn, kvc, overlap_t, gx, tq):
    H, T, _ = qn.shape
    G = NSA_KV_HEADS
    hpg = H // G
    n = kvc.shape[2]
    return pl.pallas_call(
        functools.partial(_cmp_select_body, tq=tq),
        grid=(G, T // tq),
        in_specs=[pl.BlockSpec((hpg, tq, LANES), lambda g, qi: (g, qi, 0)),
                  pl.BlockSpec((1, 1, n, LANES), lambda g, qi: (0, g, 0, 0)),
                  pl.BlockSpec((1, 1, n, LANES), lambda g, qi: (1, g, 0, 0)),
                  pl.BlockSpec((LANES, n), lambda g, qi: (0, 0)),
                  pl.BlockSpec((1, tq, hpg * NSA_HEAD_DIM), lambda g, qi: (0, qi, g))],
        out_specs=[pl.BlockSpec((tq, hpg * NSA_HEAD_DIM), lambda g, qi: (qi, g)),
                   pl.BlockSpec((1, tq, LANES), lambda g, qi: (g, qi, 0))],
        out_shape=[jax.ShapeDtypeStruct((T, GROUP_WIDTH), BF16),
                   jax.ShapeDtypeStruct((G, T, LANES), BF16)],
        compiler_params=_cparams(("parallel", "parallel")),
        name="nsa_compressed_select",
    )(qn, kvc, kvc, overlap_t, gx)


def _sel_attn_body(q_ref, sb_ref, k_ref, v_ref, gx_ref, o_ref, qs_ref, m_ref, acc_ref, s_ref, *, tq, tk):
    hpg = q_ref.shape[0]
    for hh in range(hpg):
        qs_ref[hh * tq:(hh + 1) * tq, 0:LANES] = sb_ref[0]
        qs_ref[hh * tq:(hh + 1) * tq, LANES:] = q_ref[hh]
    _causal_sweep(qs_ref, k_ref, v_ref, m_ref, acc_ref, s_ref, pl.program_id(1) * tq, tq, tk)
    o = _normalize_low_half(acc_ref[...])
    o_ref[...] = (gx_ref[0] * _pack_heads(o, tq)).astype(BF16)


def sel_attention(qn, selbias, ks, vs, gx, tq, tk):
    H, T, _ = qn.shape
    G = NSA_KV_HEADS
    hpg = H // G
    return pl.pallas_call(
        functools.partial(_sel_attn_body, tq=tq, tk=tk),
        grid=(G, T // tq),
        in_specs=[pl.BlockSpec((hpg, tq, LANES), lambda g, qi: (g, qi, 0)),
                  pl.BlockSpec((1, tq, LANES), lambda g, qi: (g, qi, 0)),
                  pl.BlockSpec((1, T, 2 * LANES), lambda g, qi: (g, 0, 0)),
                  pl.BlockSpec((1, T, LANES), lambda g, qi: (g, 0, 0)),
                  pl.BlockSpec((1, tq, hpg * NSA_HEAD_DIM), lambda g, qi: (1, qi, g))],
        out_specs=pl.BlockSpec((tq, hpg * NSA_HEAD_DIM), lambda g, qi: (qi, g)),
        out_shape=jax.ShapeDtypeStruct((T, GROUP_WIDTH), BF16),
        scratch_shapes=[pltpu.VMEM((hpg * tq, 2 * LANES), BF16), pltpu.VMEM((hpg * tq, LANES), F32),
                        pltpu.VMEM((hpg * tq, LANES), F32), pltpu.VMEM((hpg * tq, tk), F32)],
        compiler_params=_cparams(("parallel", "arbitrary")),
        name="nsa_selected_attention",
    )(qn, selbias, ks, vs, gx)


def _win_attn_body(q_ref, k_ref, v_ref, gx_ref, o_ref, *, tq, span):
    hpg = q_ref.shape[0]
    q0 = pl.program_id(1) * tq
    lo = pl.multiple_of(jnp.maximum(q0 + tq - span, 0), tq)
    q = q_ref[...].reshape(hpg * tq, LANES)
    s = _dot_nt(q, k_ref[0, pl.ds(lo, span), :])
    qpos = q0 + (_row(s.shape) & (tq - 1))
    kpos = lo + _lane(s.shape)
    s = jnp.where((kpos <= qpos) & (kpos > qpos - NSA_WINDOW), s, NEG)
    p = jnp.exp2(s - jnp.max(s, axis=-1, keepdims=True))
    acc = _dot(p.astype(BF16), v_ref[0, pl.ds(lo, span), :])
    o_ref[...] = (gx_ref[0] * _pack_heads(_normalize_low_half(acc), tq)).astype(BF16)


def win_attention(qn, kw, vw, gx, tq):
    H, T, _ = qn.shape
    G = NSA_KV_HEADS
    hpg = H // G
    return pl.pallas_call(
        functools.partial(_win_attn_body, tq=tq, span=NSA_WINDOW + tq),
        grid=(G, T // tq),
        in_specs=[pl.BlockSpec((hpg, tq, LANES), lambda g, qi: (g, qi, 0)),
                  pl.BlockSpec((1, T, LANES), lambda g, qi: (g, 0, 0)),
                  pl.BlockSpec((1, T, LANES), lambda g, qi: (g, 0, 0)),
                  pl.BlockSpec((1, tq, hpg * NSA_HEAD_DIM), lambda g, qi: (2, qi, g))],
        out_specs=pl.BlockSpec((tq, hpg * NSA_HEAD_DIM), lambda g, qi: (qi, g)),
        out_shape=jax.ShapeDtypeStruct((T, GROUP_WIDTH), BF16),
        compiler_params=_cparams(("parallel", "arbitrary")),
        name="nsa_window_attention",
    )(qn, kw, vw, gx)


def _out_proj_body(a_ref, b_ref, c_ref, d1_ref, d2_ref, d3_ref, w_ref, x_ref, g_ref, o_ref, hn_ref, lhs_ref):
    lhs_ref[:, 0:GROUP_WIDTH] = a_ref[...]
    lhs_ref[:, GROUP_WIDTH:2 * GROUP_WIDTH] = b_ref[...]
    lhs_ref[:, 2 * GROUP_WIDTH:3 * GROUP_WIDTH] = c_ref[...]
    d = d1_ref[...].astype(F32) + d2_ref[...].astype(F32) + d3_ref[...].astype(F32)
    lhs_ref[:, 3 * GROUP_WIDTH:] = d.astype(BF16)
    y = x_ref[...] + _dot(lhs_ref[...], w_ref[0])
    o_ref[...] = y
    hn_ref[...] = _rms(y, g_ref[...]).astype(BF16)


def out_proj(parts, w, layer, x, g_next, tm):
    T, N = x.shape
    part = pl.BlockSpec((tm, GROUP_WIDTH), lambda i: (i, 0))
    rows = pl.BlockSpec((tm, N), lambda i: (i, 0))
    return pl.pallas_call(
        _out_proj_body,
        grid=(T // tm,),
        in_specs=[part] * 6 + [pl.BlockSpec((1, 4 * GROUP_WIDTH, N), lambda i: (layer, 0, 0)), rows,
                               pl.BlockSpec((1, N), lambda i: (0, 0))],
        out_specs=[rows, rows],
        out_shape=[jax.ShapeDtypeStruct((T, N), F32), jax.ShapeDtypeStruct((T, N), BF16)],
        scratch_shapes=[pltpu.VMEM((tm, 4 * GROUP_WIDTH), BF16)],
        compiler_params=_cparams(("parallel",)),
        name="out_proj",
    )(*parts, w, x, g_next)


def _group_starts(te_ref, i):
    return (i == 0) | (te_ref[i] != te_ref[jnp.maximum(i - 1, 0)])


def _stream_group_weights(te_ref, nx_ref, w_hbms, wbuf_ref, wb_refs, sem_ref, slot_ref):
    j = pl.program_id(0)
    i = pl.program_id(1)
    tn = wbuf_ref.shape[3]

    def copies(e, jj, slot):
        cols = pl.ds(pl.multiple_of(jj * tn, tn), tn)
        return [pltpu.make_async_copy(w.at[e, :, cols], wbuf_ref.at[slot, n], sem_ref.at[slot])
                for n, w in enumerate(w_hbms)]

    @pl.when((j == 0) & (i == 0))
    def _():
        slot_ref[0] = 0
        for c in copies(te_ref[0], 0, 0):
            c.start()

    @pl.when(_group_starts(te_ref, i))
    def _():
        slot = slot_ref[0]
        for c in copies(te_ref[i], j, slot):
            c.wait()
        for n, wb_ref in enumerate(wb_refs):
            wb_ref[...] = wbuf_ref[slot, n].astype(BF16)
        in_sweep = nx_ref[i] >= 0

        @pl.when(in_sweep | (j + 1 < pl.num_programs(0)))
        def _():
            for c in copies(jnp.where(in_sweep, nx_ref[i], te_ref[0]), jnp.where(in_sweep, j, j + 1), 1 - slot):
                c.start()

        slot_ref[0] = 1 - slot


def _weight_stream_scratch(n_weights, k, tn):
    return [pltpu.VMEM((2, n_weights, k, tn), F32)] + [pltpu.VMEM((k, tn), BF16)] * n_weights + [
        pltpu.SemaphoreType.DMA((2,)), pltpu.SMEM((1,), jnp.int32)]


def _up_body(te_ref, nx_ref, na_ref, h_ref, wg_hbm, wu_hbm, o_ref, wbuf_ref, wgb_ref, wub_ref, sem_ref, slot_ref):
    i = pl.program_id(1)
    _stream_group_weights(te_ref, nx_ref, (wg_hbm, wu_hbm), wbuf_ref, (wgb_ref, wub_ref), sem_ref, slot_ref)

    @pl.when(i < na_ref[0])
    def _():
        h = h_ref[...]
        a = _dot(h, wgb_ref[...])
        o_ref[...] = (a * jax.nn.sigmoid(a) * _dot(h, wub_ref[...])).astype(BF16)

    @pl.when(i >= na_ref[0])
    def _():
        o_ref[...] = jnp.zeros(o_ref.shape, BF16)


def swiglu_up(tile_expert, next_expert, n_active, hn, wg, wu, tm, tn):
    P, K = hn.shape
    F = wg.shape[2]
    return pl.pallas_call(
        _up_body,
        grid_spec=pltpu.PrefetchScalarGridSpec(
            num_scalar_prefetch=3,
            grid=(F // tn, P // tm),
            in_specs=[pl.BlockSpec((tm, K), lambda j, i, te, nx, na: (jnp.minimum(i, na[0] - 1), 0)),
                      pl.BlockSpec(memory_space=pl.ANY), pl.BlockSpec(memory_space=pl.ANY)],
            out_specs=pl.BlockSpec((tm, tn), lambda j, i, te, nx, na: (i, j)),
            scratch_shapes=_weight_stream_scratch(2, K, tn)),
        out_shape=jax.ShapeDtypeStruct((P, F), BF16),
        compiler_params=_cparams(("arbitrary", "arbitrary")),
        name="swiglu_up",
    )(tile_expert, next_expert, n_active, hn, wg, wu)


def _down_body(te_ref, nx_ref, na_ref, a_ref, w_hbm, *rest, residual):
    r_ref = rest[0] if residual else None
    o_ref, wbuf_ref, wb_ref, sem_ref, slot_ref = rest[1:] if residual else rest
    _stream_group_weights(te_ref, nx_ref, (w_hbm,), wbuf_ref, (wb_ref,), sem_ref, slot_ref)
    y = _dot(a_ref[...], wb_ref[...])
    o_ref[...] = r_ref[...] + y if residual else y


def swiglu_down(tile_expert, next_expert, n_active, act, wd, residual, tm, tn):
    P, F = act.shape
    N = wd.shape[2]
    tile = pl.BlockSpec((tm, tn), lambda j, i, te, nx, na: (i, j))
    in_specs = [pl.BlockSpec((tm, F), lambda j, i, te, nx, na: (jnp.minimum(i, na[0] - 1), 0)),
                pl.BlockSpec(memory_space=pl.ANY)]
    operands = [act, wd]
    if residual is not None:
        in_specs.append(tile)
        operands.append(residual)
    return pl.pallas_call(
        functools.partial(_down_body, residual=residual is not None),
        grid_spec=pltpu.PrefetchScalarGridSpec(
            num_scalar_prefetch=3,
            grid=(N // tn, P // tm),
            in_specs=in_specs,
            out_specs=tile,
            scratch_shapes=_weight_stream_scratch(1, F, tn)),
        out_shape=jax.ShapeDtypeStruct((P, N), F32),
        compiler_params=_cparams(("arbitrary", "arbitrary")),
        name="swiglu_down",
    )(tile_expert, next_expert, n_active, *operands)


def _router_body(x_ref, g_ref, w_ref, o_ref):
    h = _rms(x_ref[...], g_ref[...])
    logits = jnp.dot(h, w_ref[...], preferred_element_type=F32, precision=lax.Precision.HIGHEST)
    lane = _lane(logits.shape)
    lf = lane.astype(F32)
    logits = jnp.where(lane < N_EXPERTS, logits, -jnp.inf)
    v0 = jnp.max(logits, axis=-1, keepdims=True)
    i0 = jnp.min(jnp.where(logits == v0, lf, float(LANES)), axis=-1, keepdims=True)
    rest = jnp.where(lf == i0, -jnp.inf, logits)
    v1 = jnp.max(rest, axis=-1, keepdims=True)
    i1 = jnp.min(jnp.where(rest == v1, lf, float(LANES)), axis=-1, keepdims=True)
    e1 = jnp.exp(v1 - v0)
    w0 = 1.0 / (1.0 + e1)
    w1 = e1 / (1.0 + e1)
    o_ref[...] = jnp.where(lane == 0, i0, jnp.where(lane == 1, i1, jnp.where(lane == 2, w0, w1)))


def router(x, g, w_pad, tm):
    T, K = x.shape
    return pl.pallas_call(
        _router_body,
        grid=(T // tm,),
        in_specs=[pl.BlockSpec((tm, K), lambda i: (i, 0)),
                  pl.BlockSpec((1, K), lambda i: (0, 0)),
                  pl.BlockSpec((K, LANES), lambda i: (0, 0))],
        out_specs=pl.BlockSpec((tm, LANES), lambda i: (i, 0)),
        out_shape=jax.ShapeDtypeStruct((T, LANES), F32),
        compiler_params=_cparams(("parallel",)),
        name="moe_router",
    )(x, g, w_pad)


ROW_DMA_UNROLL = 8


def _row_copy(src_hbm, row, dst_ref, r, sem):
    return pltpu.make_async_copy(src_hbm.at[pl.ds(row, 1)], dst_ref.at[pl.ds(r, 1)], sem)


def _gather_body(tok_ref, na_ref, x_hbm, g_ref, o_ref, buf_ref, sem):
    i = pl.program_id(0)
    tm = buf_ref.shape[0]

    @pl.when(i < na_ref[0])
    def _():
        def start(r, c):
            _row_copy(x_hbm, tok_ref[i * tm + r], buf_ref, r, sem).start()
            return c

        def wait(r, c):
            _row_copy(x_hbm, 0, buf_ref, r, sem).wait()
            return c

        lax.fori_loop(0, tm, start, 0, unroll=ROW_DMA_UNROLL)
        lax.fori_loop(0, tm, wait, 0, unroll=ROW_DMA_UNROLL)
        o_ref[...] = _rms(buf_ref[...], g_ref[...]).astype(BF16)

    @pl.when(i >= na_ref[0])
    def _():
        o_ref[...] = jnp.zeros(o_ref.shape, BF16)


def gather_norm_tokens(tok_of_slot, n_active, x, g, tm):
    P = tok_of_slot.shape[0]
    K = x.shape[1]
    return pl.pallas_call(
        _gather_body,
        grid_spec=pltpu.PrefetchScalarGridSpec(
            num_scalar_prefetch=2,
            grid=(P // tm,),
            in_specs=[pl.BlockSpec(memory_space=pl.ANY), pl.BlockSpec((1, K), lambda i, tok, na: (0, 0))],
            out_specs=pl.BlockSpec((tm, K), lambda i, tok, na: (i, 0)),
            scratch_shapes=[pltpu.VMEM((tm, K), F32), pltpu.SemaphoreType.DMA(())]),
        out_shape=jax.ShapeDtypeStruct((P, K), BF16),
        compiler_params=_cparams(("arbitrary",)),
        name="moe_gather",
    )(tok_of_slot, n_active, x, g)


def _combine_body(s0_ref, s1_ref, x_ref, y_hbm, rt_ref, g_ref, o_ref, b0_ref, b1_ref, sem0, sem1, *, final):
    tm = x_ref.shape[0]
    base = pl.program_id(0) * tm

    def start(r, c):
        _row_copy(y_hbm, s0_ref[base + r], b0_ref, r, sem0).start()
        _row_copy(y_hbm, s1_ref[base + r], b1_ref, r, sem1).start()
        return c

    def wait(r, c):
        _row_copy(y_hbm, 0, b0_ref, r, sem0).wait()
        _row_copy(y_hbm, 0, b1_ref, r, sem1).wait()
        return c

    lax.fori_loop(0, tm, start, 0, unroll=ROW_DMA_UNROLL)
    lax.fori_loop(0, tm, wait, 0, unroll=ROW_DMA_UNROLL)
    gates = rt_ref[...]
    y = x_ref[...] + gates[:, 2:3] * b0_ref[...] + gates[:, 3:4] * b1_ref[...]
    o_ref[...] = _rms(y, g_ref[...]) if final else y


def moe_combine(slot0, slot1, x, ys, route, g, tm, final):
    T, K = x.shape
    return pl.pallas_call(
        functools.partial(_combine_body, final=final),
        grid_spec=pltpu.PrefetchScalarGridSpec(
            num_scalar_prefetch=2,
            grid=(T // tm,),
            in_specs=[pl.BlockSpec((tm, K), lambda i, s0, s1: (i, 0)),
                      pl.BlockSpec(memory_space=pl.ANY),
                      pl.BlockSpec((tm, LANES), lambda i, s0, s1: (i, 0)),
                      pl.BlockSpec((1, K), lambda i, s0, s1: (0, 0))],
            out_specs=pl.BlockSpec((tm, K), lambda i, s0, s1: (i, 0)),
            scratch_shapes=[pltpu.VMEM((tm, K), F32), pltpu.VMEM((tm, K), F32),
                            pltpu.SemaphoreType.DMA(()), pltpu.SemaphoreType.DMA(())]),
        out_shape=jax.ShapeDtypeStruct((T, K), F32),
        compiler_params=_cparams(("arbitrary",)),
        name="moe_combine",
    )(slot0, slot1, x, ys, route, g)


def _final_norm_body(x_ref, g_ref, o_ref):
    o_ref[...] = _rms(x_ref[...], g_ref[...])


def final_norm(x, g, tm):
    T, K = x.shape
    return pl.pallas_call(
        _final_norm_body,
        grid=(T // tm,),
        in_specs=[pl.BlockSpec((tm, K), lambda i: (i, 0)), pl.BlockSpec((1, K), lambda i: (0, 0))],
        out_specs=pl.BlockSpec((tm, K), lambda i: (i, 0)),
        out_shape=jax.ShapeDtypeStruct((T, K), F32),
        compiler_params=_cparams(("parallel",)),
        name="final_norm",
    )(x, g)


def _routing_tables(route, tm, expert_base):
    T = route.shape[0]
    top_i = route[:, 0:2].astype(jnp.int32)
    e_flat = top_i.reshape(-1)
    onehot = (e_flat[:, None] == jnp.arange(N_EXPERTS)[None, :]).astype(jnp.int32)
    rank = jnp.take_along_axis(jnp.cumsum(onehot, axis=0) - onehot, e_flat[:, None], axis=1)[:, 0]
    count = jnp.sum(onehot, axis=0)
    padded = ((count + tm - 1) // tm) * tm
    end = jnp.cumsum(padded)
    start = end - padded
    slot = start[e_flat] + rank
    n_slots = 2 * T + N_EXPERTS * tm
    n_tiles = n_slots // tm
    n_active = (end[-1] // tm).astype(jnp.int32)
    tile_start = jnp.minimum(jnp.arange(n_tiles, dtype=jnp.int32), n_active - 1) * tm
    tile_expert = jnp.minimum(jnp.sum(tile_start[:, None] >= end[None, :], axis=1), N_EXPERTS - 1)
    group_end = end[tile_expert] // tm
    next_expert = jnp.where(group_end < n_active, tile_expert[jnp.minimum(group_end, n_tiles - 1)] + expert_base, -1)
    tok_of_slot = jnp.zeros((n_slots,), jnp.int32).at[slot].set(jnp.arange(2 * T, dtype=jnp.int32) // 2)
    slot2 = slot.reshape(T, 2).astype(jnp.int32)
    return ((tile_expert + expert_base).astype(jnp.int32), next_expert.astype(jnp.int32), n_active.reshape(1),
            tok_of_slot, slot2[:, 0], slot2[:, 1])


def _rope_tables128(T):
    half = NSA_HEAD_DIM // 2
    inv = ROPE_THETA ** (-jnp.arange(0, NSA_HEAD_DIM, 2, dtype=F32) / NSA_HEAD_DIM)
    ang = jnp.arange(T, dtype=F32)[:, None] * inv[None, :]
    cos, sin = jnp.cos(ang), jnp.sin(ang)
    cos128 = jnp.tile(cos, (1, LANES // half))
    sin128 = jnp.tile(jnp.concatenate([-sin, sin], axis=1), (1, LANES // NSA_HEAD_DIM))
    return cos128, sin128


def _overlap_matrix_t(n_cmp_pad):
    sstart = jnp.arange(LANES) * NSA_SLC_LEN
    cstart = jnp.arange(n_cmp_pad) * NSA_CMP_STRIDE
    ov = (cstart[None, :] < sstart[:, None] + NSA_SLC_LEN) & (cstart[None, :] + NSA_CMP_LEN > sstart[:, None])
    return ov.astype(BF16)


def kernel(x, attn_norm, w_in, w_out, gm_ln_g, gm_ln_b, gm_ws, gm_bs, da_lambda, da_subln, cv_dw_w, cv_dw_b,
           cv_ln_g, cv_ln_b, nsa_cmp_w1, nsa_cmp_w2, nsa_cmp_pe, ffn_norm, ffn_wg, ffn_wu, ffn_wd, router_w,
           exp_wg, exp_wu, exp_wd, final_norm_g):
    B, T, D = x.shape
    assert B == 1 and D == D_MODEL and T % 1024 == 0 and T // NSA_SLC_LEN <= LANES
    depth = w_in.shape[0]
    G = NSA_KV_HEADS
    n16 = T // NSA_CMP_STRIDE
    tm = 512

    cos128, sin128 = _rope_tables128(T)
    overlap_t = _overlap_matrix_t(n16)
    dense_na = jnp.full((1,), T // tm, jnp.int32)
    row = lambda v: v.reshape(1, -1)
    merge = lambda w: w.reshape((-1,) + w.shape[2:])
    exp_wg, exp_wu, exp_wd = merge(exp_wg), merge(exp_wu), merge(exp_wd)
    w_in_b = jnp.pad(w_in, ((0, 0), (0, 0), (0, IN_WIDTH_PAD - IN_WIDTH))).astype(BF16)
    w_out_b = w_out.astype(BF16)

    xs = x[0]
    for l in range(depth):
        lambda_init = 0.8 - 0.6 * math.exp(-0.3 * l)
        z = norm_mm(xs, row(attn_norm[l]), w_in_b, l, 1024, IN_WIDTH_PAD // 3)

        bs_rows = jnp.repeat(gm_bs[l].T, GM_CHUNK, axis=1)
        o_a = gmlp(z, row(gm_ln_g[l]), row(gm_ln_b[l]), gm_ws[l], bs_rows, 512)
        dw_w = jnp.pad(cv_dw_w[l], ((0, CV_HALO - CV_WIDTH), (0, 0)))
        o_c = conformer_conv(z, dw_w, row(cv_dw_b[l]), row(cv_ln_g[l]), row(cv_ln_b[l]), 256)

        qd, kd, vd, qn, kc, vc, ks, vs, kw, vw, gx = prep(z, cos128, sin128, 256)
        o_b = diff_attention(qd, kd, vd, da_lambda[l], row(da_subln[l]), lambda_init, 512, 512)

        x16 = jnp.stack([kc, vc]).reshape(2, T, G, NSA_HEAD_DIM).transpose(0, 2, 1, 3)
        x16 = x16.reshape(2, G, n16, NSA_CMP_STRIDE * NSA_HEAD_DIM)
        pe2 = nsa_cmp_pe[l].reshape(2, 2, NSA_CMP_STRIDE * NSA_HEAD_DIM)
        w2p = jnp.pad(nsa_cmp_w2[l], ((0, 0), (0, 0), (0, LANES - NSA_HEAD_DIM))).astype(BF16)
        kvc = compress(x16, pe2, nsa_cmp_w1[l].astype(BF16), w2p)
        o_cmp, selbias = cmp_select(qn, kvc, overlap_t, gx, 256)
        o_sel = sel_attention(qn, selbias, ks, vs, gx, 256, 512)
        o_win = win_attention(qn, kw, vw, gx, 256)

        g_ffn = row(ffn_norm[l])
        xs, hn = out_proj((o_a, o_b, o_c, o_cmp, o_sel, o_win), w_out_b, l, xs, g_ffn, 256)

        e = l // 2
        if l % 2 == 0:
            dense_te = jnp.full((T // tm,), e, jnp.int32)
            dense_nx = jnp.full((T // tm,), -1, jnp.int32)
            act = swiglu_up(dense_te[:T // 1024], dense_nx[:T // 1024], dense_na // 2, hn, ffn_wg, ffn_wu, 1024, 512)
            xs = swiglu_down(dense_te, dense_nx, dense_na, act, ffn_wd, xs, tm, 512)
            if l == depth - 1:
                xs = final_norm(xs, row(final_norm_g), tm)
        else:
            w_r = jnp.pad(router_w[e], ((0, 0), (0, LANES - N_EXPERTS)))
            route = router(xs, g_ffn, w_r, tm)
            tile_expert, next_expert, n_active, tok_of_slot, slot0, slot1 = _routing_tables(
                route, tm, e * N_EXPERTS)
            hg = gather_norm_tokens(tok_of_slot, n_active, xs, g_ffn, tm)
            act = swiglu_up(tile_expert, next_expert, n_active, hg, exp_wg, exp_wu, tm, 512)
            ys = swiglu_down(tile_expert, next_expert, n_active, act, exp_wd, None, tm, 512)
            xs = moe_combine(slot0, slot1, xs, ys, route, row(final_norm_g), 256, final=(l == depth - 1))
    return xs[None]
```

```python
import functools
import math

import jax
import jax.numpy as jnp
from jax import lax
from jax.experimental import pallas as pl
from jax.experimental.pallas import tpu as pltpu

F32 = jnp.float32
BF16 = jnp.bfloat16

D_MODEL = 2048
GROUP_WIDTH = 512
GM_CHUNK = 128
GM_HEADS = 4
DA_HEADS = 4
DA_QK_DIM = 64
CV_WIDTH = 31
NSA_HEADS = 8
NSA_KV_HEADS = 2
NSA_HEAD_DIM = 64
NSA_CMP_LEN = 32
NSA_CMP_STRIDE = 16
NSA_SLC_LEN = 64
NSA_SLC_TOPK = 16
NSA_WINDOW = 512
ROPE_THETA = 10000.0
NORM_EPS = 1e-6
NEG = -1e30
FORCE = 1e9
N_EXPERTS = 8
LANES = 128
SUBLANES = 8
LOG2E = math.log2(math.e)

IN_WIDTH = 4888
IN_WIDTH_PAD = 4992
COL_GM = 0
COL_QDA = 1024
COL_KDA = 1536
COL_VDA = 2048
COL_CV = 2560
COL_QNS = 3584
COL_KVNS = 4096
COL_GNS = 4864

VMEM_LIMIT = 56 * 1024 * 1024


def _cparams(sem, **kw):
    return pltpu.CompilerParams(dimension_semantics=sem, vmem_limit_bytes=VMEM_LIMIT, **kw)


def _rms(x, g):
    ms = jnp.mean(x * x, axis=-1, keepdims=True)
    return x * lax.rsqrt(ms + NORM_EPS) * g


def _layer_norm(x, g, b):
    mu = jnp.mean(x, axis=-1, keepdims=True)
    xc = x - mu
    var = jnp.mean(xc * xc, axis=-1, keepdims=True)
    return xc * lax.rsqrt(var + NORM_EPS) * g + b


def _dot(a, b):
    return jnp.dot(a, b, preferred_element_type=F32)


def _dot_nt(a, b):
    return lax.dot_general(a, b, (((1,), (1,)), ((), ())), preferred_element_type=F32)


def _lane(shape):
    return lax.broadcasted_iota(jnp.int32, shape, len(shape) - 1)


def _row(shape):
    return lax.broadcasted_iota(jnp.int32, shape, len(shape) - 2)


def _norm_mm_body(x_ref, g_ref, w_ref, o_ref, hn_ref):
    @pl.when(pl.program_id(1) == 0)
    def _():
        hn_ref[...] = _rms(x_ref[...], g_ref[...]).astype(BF16)

    o_ref[...] = _dot(hn_ref[...], w_ref[0])


def norm_mm(x, g, w, layer, tm, tn):
    T, K = x.shape
    N = w.shape[2]
    return pl.pallas_call(
        _norm_mm_body,
        grid=(T // tm, N // tn),
        in_specs=[pl.BlockSpec((tm, K), lambda i, j: (i, 0)),
                  pl.BlockSpec((1, K), lambda i, j: (0, 0)),
                  pl.BlockSpec((1, K, tn), lambda i, j: (layer, 0, j))],
        out_specs=pl.BlockSpec((tm, tn), lambda i, j: (i, j)),
        out_shape=jax.ShapeDtypeStruct((T, N), F32),
        scratch_shapes=[pltpu.VMEM((tm, K), BF16)],
        compiler_params=_cparams(("parallel", "arbitrary")),
        name="norm_in_proj",
    )(x, g, w)


def _gmlp_body(z_ref, g_ref, b_ref, ws_ref, bs_ref, o_ref):
    tr = z_ref.shape[0]
    z = jax.nn.gelu(z_ref[...])
    u = z[:, :GROUP_WIDTH]
    v = _layer_norm(z[:, GROUP_WIDTH:], g_ref[...], b_ref[...]).astype(BF16)
    causal = _row((GM_CHUNK, GM_CHUNK)) >= _lane((GM_CHUNK, GM_CHUNK))
    bias = bs_ref[...]
    for h in range(GM_HEADS):
        w = jnp.where(causal, ws_ref[h], 0.0).astype(BF16)
        cols = slice(h * LANES, (h + 1) * LANES)
        for c in range(tr // GM_CHUNK):
            rows = slice(c * GM_CHUNK, (c + 1) * GM_CHUNK)
            s = _dot(w, v[rows, cols]) + bias[:, cols]
            o_ref[rows, cols] = (u[rows, cols] * s).astype(BF16)


def gmlp(z, ln_g, ln_b, ws, bs_rows, tr):
    T = z.shape[0]
    return pl.pallas_call(
        _gmlp_body,
        grid=(T // tr,),
        in_specs=[pl.BlockSpec((tr, 2 * GROUP_WIDTH), lambda i: (i, COL_GM // (2 * GROUP_WIDTH))),
                  pl.BlockSpec((1, GROUP_WIDTH), lambda i: (0, 0)),
                  pl.BlockSpec((1, GROUP_WIDTH), lambda i: (0, 0)),
                  pl.BlockSpec((GM_HEADS, GM_CHUNK, GM_CHUNK), lambda i: (0, 0, 0)),
                  pl.BlockSpec((GM_CHUNK, GROUP_WIDTH), lambda i: (0, 0))],
        out_specs=pl.BlockSpec((tr, GROUP_WIDTH), lambda i: (i, 0)),
        out_shape=jax.ShapeDtypeStruct((T, GROUP_WIDTH), BF16),
        compiler_params=_cparams(("parallel",)),
        name="gmlp",
    )(z, ln_g, ln_b, ws, bs_rows)


CV_HALO = 32
CV_SUB = 64


def _conv_body(a_ref, g_ref, ap_ref, gp_ref, w_ref, b_ref, lg_ref, lb_ref, o_ref, hs_ref):
    tr = a_ref.shape[0]
    first = pl.program_id(0) == 0
    prev = ap_ref[...] * jax.nn.sigmoid(gp_ref[...])
    hs_ref[0:CV_HALO, :] = jnp.where(first, 0.0, prev)
    hs_ref[CV_HALO:CV_HALO + tr, :] = a_ref[...] * jax.nn.sigmoid(g_ref[...])
    hs_ref[CV_HALO + tr:, :] = jnp.zeros((SUBLANES, GROUP_WIDTH), F32)
    w = w_ref[...]
    lead = CV_HALO - (CV_WIDTH - 1)
    for r0 in range(0, tr, CV_SUB):
        acc = jnp.zeros((CV_SUB, GROUP_WIDTH), F32)
        for phase in range(SUBLANES):
            base, shift = divmod(lead + phase, SUBLANES)
            part = jnp.zeros((CV_SUB + SUBLANES, GROUP_WIDTH), F32)
            for k in range(phase, CV_WIDTH, SUBLANES):
                start = r0 + SUBLANES * (base + k // SUBLANES)
                part = part + hs_ref[start:start + CV_SUB + SUBLANES, :] * w[k:k + 1, :]
            acc = acc + part[shift:shift + CV_SUB]
        y = _layer_norm(acc + b_ref[...], lg_ref[...], lb_ref[...])
        o_ref[r0:r0 + CV_SUB, :] = (y * jax.nn.sigmoid(y)).astype(BF16)


def conformer_conv(z, dw_w, dw_b, ln_g, ln_b, tr):
    T = z.shape[0]
    ca = COL_CV // GROUP_WIDTH
    per = tr // CV_HALO

    def halo(col):
        return lambda i: (jnp.maximum(i * per - 1, 0), col)

    return pl.pallas_call(
        _conv_body,
        grid=(T // tr,),
        in_specs=[pl.BlockSpec((tr, GROUP_WIDTH), lambda i: (i, ca)),
                  pl.BlockSpec((tr, GROUP_WIDTH), lambda i: (i, ca + 1)),
                  pl.BlockSpec((CV_HALO, GROUP_WIDTH), halo(ca)),
                  pl.BlockSpec((CV_HALO, GROUP_WIDTH), halo(ca + 1)),
                  pl.BlockSpec((CV_HALO, GROUP_WIDTH), lambda i: (0, 0)),
                  pl.BlockSpec((1, GROUP_WIDTH), lambda i: (0, 0)),
                  pl.BlockSpec((1, GROUP_WIDTH), lambda i: (0, 0)),
                  pl.BlockSpec((1, GROUP_WIDTH), lambda i: (0, 0))],
        out_specs=pl.BlockSpec((tr, GROUP_WIDTH), lambda i: (i, 0)),
        out_shape=jax.ShapeDtypeStruct((T, GROUP_WIDTH), BF16),
        scratch_shapes=[pltpu.VMEM((CV_HALO + tr + SUBLANES, GROUP_WIDTH), F32)],
        compiler_params=_cparams(("parallel",)),
        name="conformer_conv",
    )(z, z, z, z, dw_w, dw_b, ln_g, ln_b)


def _rope128(x, cos, sin_signed):
    lo = (_lane(x.shape) & 63) < 32
    rot = jnp.where(lo, pltpu.roll(x, 96, 1), pltpu.roll(x, 32, 1))
    return x * cos + rot * sin_signed


def _low_half(x, fill=0.0):
    return jnp.where(_lane(x.shape) < 64, x, fill)


def _high_half_to_low(x, fill=0.0):
    return jnp.where(_lane(x.shape) < 64, pltpu.roll(x, 64, 1), fill)


def _prep_body(qd_ref, kd_ref, vd_ref, qn_ref, kvc_ref, kvs_ref, kvw_ref, gt_ref, cos_ref, sin_ref,
               qd_o, kd_o, vd_o, qn_o, kc_o, vc_o, ks_o, vs_o, kw_o, vw_o, gx_o):
    tr = cos_ref.shape[0]
    cos = cos_ref[...]
    sin = sin_ref[...]
    q_scale = DA_QK_DIM ** -0.5 * LOG2E
    ones = jnp.ones((tr, LANES), BF16)
    for h in range(DA_HEADS):
        cols = slice(h * LANES, (h + 1) * LANES)
        qd_o[h] = (_rope128(qd_ref[:, cols], cos, sin) * q_scale).astype(BF16)
        kd_o[h] = _rope128(kd_ref[:, cols], cos, sin).astype(BF16)
        vd_o[h, :, 0:LANES] = vd_ref[:, cols].astype(BF16)
        vd_o[h, :, LANES:] = ones
    for c in range(NSA_HEADS // 2):
        q = _rope128(qn_ref[:, c * LANES:(c + 1) * LANES], cos, sin) * (NSA_HEAD_DIM ** -0.5 * LOG2E)
        qn_o[2 * c] = _low_half(q).astype(BF16)
        qn_o[2 * c + 1] = _high_half_to_low(q).astype(BF16)
    kc_o[...] = _rope128(kvc_ref[:, :LANES], cos, sin)
    vc_o[...] = kvc_ref[:, LANES:]
    blk = (pl.program_id(0) * tr + _row((tr, LANES))) >> 6
    onehot = jnp.where(blk == _lane((tr, LANES)), 1.0, 0.0).astype(BF16)
    k = _rope128(kvs_ref[:, :LANES], cos, sin)
    for g, half in enumerate((_low_half, _high_half_to_low)):
        ks_o[g, :, 0:LANES] = onehot
        ks_o[g, :, LANES:] = half(k).astype(BF16)
        vs_o[g] = half(kvs_ref[:, LANES:], 1.0).astype(BF16)
    k = _rope128(kvw_ref[:, :LANES], cos, sin)
    for g, half in enumerate((_low_half, _high_half_to_low)):
        kw_o[g] = half(k).astype(BF16)
        vw_o[g] = half(kvw_ref[:, LANES:], 1.0).astype(BF16)
    gate = jax.nn.sigmoid(gt_ref[...])
    low = _lane((tr, LANES)) < 64
    for c in range(3):
        for hp in range(NSA_HEADS // 2):
            j = c * NSA_HEADS + 2 * hp
            gx_o[c, :, hp * LANES:(hp + 1) * LANES] = jnp.where(low, gate[:, j:j + 1], gate[:, j + 1:j + 2])


def prep(z, cos128, sin128, tr):
    T = z.shape[0]
    G = NSA_KV_HEADS

    def zspec(width, col):
        return pl.BlockSpec((tr, width), lambda i: (i, col // width))

    def heads(n, width=LANES):
        return pl.BlockSpec((n, tr, width), lambda i: (0, i, 0))

    def hshape(n, width=LANES):
        return jax.ShapeDtypeStruct((n, T, width), BF16)

    row128 = pl.BlockSpec((tr, LANES), lambda i: (i, 0))
    return pl.pallas_call(
        _prep_body,
        grid=(T // tr,),
        in_specs=[zspec(512, COL_QDA), zspec(512, COL_KDA), zspec(512, COL_VDA), zspec(512, COL_QNS),
                  zspec(256, COL_KVNS), zspec(256, COL_KVNS + 256), zspec(256, COL_KVNS + 512),
                  zspec(128, COL_GNS), row128, row128],
        out_specs=[heads(DA_HEADS), heads(DA_HEADS), heads(DA_HEADS, 2 * LANES),
                   heads(NSA_HEADS), row128, row128, heads(G, 2 * LANES), heads(G), heads(G), heads(G),
                   pl.BlockSpec((3, tr, GROUP_WIDTH), lambda i: (0, i, 0))],
        out_shape=[hshape(DA_HEADS), hshape(DA_HEADS), hshape(DA_HEADS, 2 * LANES),
                   hshape(NSA_HEADS), jax.ShapeDtypeStruct((T, LANES), F32),
                   jax.ShapeDtypeStruct((T, LANES), F32), hshape(G, 2 * LANES), hshape(G), hshape(G), hshape(G),
                   jax.ShapeDtypeStruct((3, T, GROUP_WIDTH), F32)],
        compiler_params=_cparams(("parallel",)),
        name="attention_prep",
    )(z, z, z, z, z, z, z, z, cos128, sin128)


def _lane_tile(x, n):
    return x if n == 1 else jnp.concatenate([x] * n, axis=-1)


def _softmax_update(s, v, m_ref, acc_ref):
    m_prev = m_ref[...]
    m_new = jnp.maximum(m_prev, jnp.max(s, axis=-1, keepdims=True))
    alpha = jnp.exp2(m_prev - m_new)
    p = jnp.exp2(s - _lane_tile(m_new, s.shape[1] // LANES))
    acc_ref[...] = _lane_tile(alpha, acc_ref.shape[1] // LANES) * acc_ref[...] + _dot(p.astype(BF16), v)
    m_ref[...] = m_new


SWEEP_UNROLL = 4


def _causal_sweep(qs_ref, k_ref, v_ref, m_ref, acc_ref, s_ref, q0, tq, tk):
    m_ref[...] = jnp.full(m_ref.shape, -jnp.inf, F32)
    acc_ref[...] = jnp.zeros(acc_ref.shape, F32)

    def keys(ref, t):
        return ref[0, pl.ds(pl.multiple_of(t * tk, tk), tk), :]

    def scores(t):
        return _dot_nt(qs_ref[...], keys(k_ref, t))

    n_full = q0 // tk
    s_ref[...] = scores(0)

    def run(t0, count):
        s_cur = s_ref[...]
        for u in range(count):
            s_next = scores(t0 + u + 1)
            _softmax_update(s_cur, keys(v_ref, t0 + u), m_ref, acc_ref)
            s_cur = s_next
        s_ref[...] = s_cur

    def several(i, c):
        run(SWEEP_UNROLL * i, SWEEP_UNROLL)
        return c

    def single(t, c):
        run(t, 1)
        return c

    n_groups = n_full // SWEEP_UNROLL
    lax.fori_loop(0, n_groups, several, 0)
    lax.fori_loop(n_groups * SWEEP_UNROLL, n_full, single, 0)
    s = s_ref[...]
    visible = n_full * tk + _lane(s.shape) <= q0 + (_row(s.shape) & (tq - 1))
    _softmax_update(jnp.where(visible, s, NEG), keys(v_ref, n_full), m_ref, acc_ref)


def _diff_attn_body(q_ref, k_ref, v_ref, lam_ref, sg_ref, o_ref, qs_ref, m_ref, acc_ref, s_ref, *,
                    tq, tk, lambda_init):
    q = q_ref[0]
    first = _lane(q.shape) < DA_QK_DIM
    qs_ref[0:tq, :] = jnp.where(first, q, jnp.zeros_like(q))
    qs_ref[tq:, :] = jnp.where(first, jnp.zeros_like(q), q)
    _causal_sweep(qs_ref, k_ref, v_ref, m_ref, acc_ref, s_ref, pl.program_id(1) * tq, tq, tk)
    lam = lam_ref[...]
    lam_full = (jnp.exp(jnp.sum(lam[0:1] * lam[1:2], axis=-1, keepdims=True))
                - jnp.exp(jnp.sum(lam[2:3] * lam[3:4], axis=-1, keepdims=True)) + lambda_init)
    o = acc_ref[:, 0:LANES] / acc_ref[:, LANES:]
    a = o[0:tq] - lam_full * o[tq:]
    o_ref[...] = (_rms(a, sg_ref[...]) * (1.0 - lambda_init)).astype(BF16)


def diff_attention(qd, kd, vd, lam, subln, lambda_init, tq, tk):
    H, T, _ = qd.shape
    return pl.pallas_call(
        functools.partial(_diff_attn_body, tq=tq, tk=tk, lambda_init=lambda_init),
        grid=(H, T // tq),
        in_specs=[pl.BlockSpec((1, tq, LANES), lambda h, qi: (h, qi, 0)),
                  pl.BlockSpec((1, T, LANES), lambda h, qi: (h, 0, 0)),
                  pl.BlockSpec((1, T, 2 * LANES), lambda h, qi: (h, 0, 0)),
                  pl.BlockSpec((4, DA_QK_DIM), lambda h, qi: (0, 0)),
                  pl.BlockSpec((1, LANES), lambda h, qi: (0, 0))],
        out_specs=pl.BlockSpec((tq, LANES), lambda h, qi: (qi, h)),
        out_shape=jax.ShapeDtypeStruct((T, GROUP_WIDTH), BF16),
        scratch_shapes=[pltpu.VMEM((2 * tq, LANES), BF16), pltpu.VMEM((2 * tq, LANES), F32),
                        pltpu.VMEM((2 * tq, 2 * LANES), F32), pltpu.VMEM((2 * tq, tk), F32)],
        compiler_params=_cparams(("parallel", "arbitrary")),
        name="diff_attention",
    )(qd, kd, vd, lam, subln)


def _compress_body(x_ref, pe_ref, w1_ref, w2_ref, o_ref):
    x = x_ref[0, 0]
    half = x.shape[1]
    a = _dot((x + pe_ref[0, 0:1, :]).astype(BF16), w1_ref[0, 0:half, :])
    b = _dot((x + pe_ref[0, 1:2, :]).astype(BF16), w1_ref[0, half:, :])
    hid = jax.nn.gelu(a + pltpu.roll(b, b.shape[0] - 1, 0))
    o_ref[0, 0] = _dot(hid.astype(BF16), w2_ref[0]).astype(BF16)


def compress(x16, pe2, w1, w2p):
    _, G, n, half = x16.shape
    hid = w1.shape[2]
    return pl.pallas_call(
        _compress_body,
        grid=(2, G),
        in_specs=[pl.BlockSpec((1, 1, n, half), lambda c, g: (c, g, 0, 0)),
                  pl.BlockSpec((1, 2, half), lambda c, g: (c, 0, 0)),
                  pl.BlockSpec((1, 2 * half, hid), lambda c, g: (c, 0, 0)),
                  pl.BlockSpec((1, hid, LANES), lambda c, g: (c, 0, 0))],
        out_specs=pl.BlockSpec((1, 1, n, LANES), lambda c, g: (c, g, 0, 0)),
        out_shape=jax.ShapeDtypeStruct((2, G, n, LANES), BF16),
        compiler_params=_cparams(("parallel", "parallel")),
        name="nsa_compress",
    )(x16, pe2, w1, w2p)


def _split3(x):
    hi = x.astype(BF16)
    r1 = x - hi.astype(F32)
    mid = r1.astype(BF16)
    lo = (r1 - mid.astype(F32)).astype(BF16)
    return hi, mid, lo


def _pack_heads(o, tq):
    pair = lambda a, b: a + pltpu.roll(b, 64, 1)
    return jnp.concatenate([pair(o[0:tq], o[tq:2 * tq]), pair(o[2 * tq:3 * tq], o[3 * tq:])], axis=-1)


def _normalize_low_half(acc):
    return jnp.where(_lane(acc.shape) < 64, acc / pltpu.roll(acc, 64, 1), 0.0)


def _cmp_select_body(q_ref, kc_ref, vc_ref, ovt_ref, gx_ref, o_ref, sb_ref, *, tq):
    qi = pl.program_id(1)
    hpg = q_ref.shape[0]
    q = q_ref[...].reshape(hpg * tq, LANES)
    s = _dot_nt(q, kc_ref[0, 0])
    t = qi * tq + (_row(s.shape) & (tq - 1))
    cmask = _lane(s.shape) * NSA_CMP_STRIDE + (NSA_CMP_LEN - 1) <= t
    s = jnp.where(cmask, s, NEG)
    e = jnp.exp2(s - jnp.max(s, axis=-1, keepdims=True))
    p = jnp.where(cmask, e / jnp.sum(e, axis=-1, keepdims=True), 0.0)
    o = _dot(p.astype(BF16), vc_ref[0, 0])
    o_ref[...] = (gx_ref[0] * _pack_heads(o, tq)).astype(BF16)

    psum = p[0:tq]
    for hh in range(1, hpg):
        psum = psum + p[hh * tq:(hh + 1) * tq]
    ovt = ovt_ref[...]
    imp = sum(_dot_nt(ovt, part) for part in _split3(psum))
    j = _row(imp.shape)
    cur = (qi * tq + _lane(imp.shape)) >> 6
    forced = (j == 0) | (j == cur) | (j == cur - 1)
    score = jnp.where(forced, FORCE, jnp.where(j <= cur, imp, -1.0))
    jf = j.astype(F32)
    bias = jnp.full(imp.shape, NEG, F32)
    for _ in range(NSA_SLC_TOPK):
        m = jnp.max(score, axis=0, keepdims=True)
        first = jnp.min(jnp.where(score == m, jf, float(LANES)), axis=0, keepdims=True)
        hit = jf == first
        bias = jnp.where(hit, jnp.where(m >= 0.0, 0.0, NEG), bias)
        score = jnp.where(hit, -2.0, score)
    sb_ref[0] = bias.T.astype(BF16)


def cmp_select(qn, kvc, overlap_t, gx, tq):
    H, T, _ = qn.shape
    G = NSA_KV_HEADS
    hpg = H // G
    n = kvc.shape[2]
    return pl.pallas_call(
        functools.partial(_cmp_select_body, tq=tq),
        grid=(G, T // tq),
        in_specs=[pl.BlockSpec((hpg, tq, LANES), lambda g, qi: (g, qi, 0)),
                  pl.BlockSpec((1, 1, n, LANES), lambda g, qi: (0, g, 0, 0)),
                  pl.BlockSpec((1, 1, n, LANES), lambda g, qi: (1, g, 0, 0)),
                  pl.BlockSpec((LANES, n), lambda g, qi: (0, 0)),
                  pl.BlockSpec((1, tq, hpg * NSA_HEAD_DIM), lambda g, qi: (0, qi, g))],
        out_specs=[pl.BlockSpec((tq, hpg * NSA_HEAD_DIM), lambda g, qi: (qi, g)),
                   pl.BlockSpec((1, tq, LANES), lambda g, qi: (g, qi, 0))],
        out_shape=[jax.ShapeDtypeStruct((T, GROUP_WIDTH), BF16),
                   jax.ShapeDtypeStruct((G, T, LANES), BF16)],
        compiler_params=_cparams(("parallel", "parallel")),
        name="nsa_compressed_select",
    )(qn, kvc, kvc, overlap_t, gx)


def _sel_attn_body(q_ref, sb_ref, k_ref, v_ref, gx_ref, o_ref, qs_ref, m_ref, acc_ref, s_ref, *, tq, tk):
    hpg = q_ref.shape[0]
    for hh in range(hpg):
        qs_ref[hh * tq:(hh + 1) * tq, 0:LANES] = sb_ref[0]
        qs_ref[hh * tq:(hh + 1) * tq, LANES:] = q_ref[hh]
    _causal_sweep(qs_ref, k_ref, v_ref, m_ref, acc_ref, s_ref, pl.program_id(1) * tq, tq, tk)
    o = _normalize_low_half(acc_ref[...])
    o_ref[...] = (gx_ref[0] * _pack_heads(o, tq)).astype(BF16)


def sel_attention(qn, selbias, ks, vs, gx, tq, tk):
    H, T, _ = qn.shape
    G = NSA_KV_HEADS
    hpg = H // G
    return pl.pallas_call(
        functools.partial(_sel_attn_body, tq=tq, tk=tk),
        grid=(G, T // tq),
        in_specs=[pl.BlockSpec((hpg, tq, LANES), lambda g, qi: (g, qi, 0)),
                  pl.BlockSpec((1, tq, LANES), lambda g, qi: (g, qi, 0)),
                  pl.BlockSpec((1, T, 2 * LANES), lambda g, qi: (g, 0, 0)),
                  pl.BlockSpec((1, T, LANES), lambda g, qi: (g, 0, 0)),
                  pl.BlockSpec((1, tq, hpg * NSA_HEAD_DIM), lambda g, qi: (1, qi, g))],
        out_specs=pl.BlockSpec((tq, hpg * NSA_HEAD_DIM), lambda g, qi: (qi, g)),
        out_shape=jax.ShapeDtypeStruct((T, GROUP_WIDTH), BF16),
        scratch_shapes=[pltpu.VMEM((hpg * tq, 2 * LANES), BF16), pltpu.VMEM((hpg * tq, LANES), F32),
                        pltpu.VMEM((hpg * tq, LANES), F32), pltpu.VMEM((hpg * tq, tk), F32)],
        compiler_params=_cparams(("parallel", "arbitrary")),
        name="nsa_selected_attention",
    )(qn, selbias, ks, vs, gx)


def _win_attn_body(q_ref, k_ref, v_ref, gx_ref, o_ref, *, tq, span):
    hpg = q_ref.shape[0]
    q0 = pl.program_id(1) * tq
    lo = pl.multiple_of(jnp.maximum(q0 + tq - span, 0), tq)
    q = q_ref[...].reshape(hpg * tq, LANES)
    s = _dot_nt(q, k_ref[0, pl.ds(lo, span), :])
    qpos = q0 + (_row(s.shape) & (tq - 1))
    kpos = lo + _lane(s.shape)
    s = jnp.where((kpos <= qpos) & (kpos > qpos - NSA_WINDOW), s, NEG)
    p = jnp.exp2(s - jnp.max(s, axis=-1, keepdims=True))
    acc = _dot(p.astype(BF16), v_ref[0, pl.ds(lo, span), :])
    o_ref[...] = (gx_ref[0] * _pack_heads(_normalize_low_half(acc), tq)).astype(BF16)


def win_attention(qn, kw, vw, gx, tq):
    H, T, _ = qn.shape
    G = NSA_KV_HEADS
    hpg = H // G
    return pl.pallas_call(
        functools.partial(_win_attn_body, tq=tq, span=NSA_WINDOW + tq),
        grid=(G, T // tq),
        in_specs=[pl.BlockSpec((hpg, tq, LANES), lambda g, qi: (g, qi, 0)),
                  pl.BlockSpec((1, T, LANES), lambda g, qi: (g, 0, 0)),
                  pl.BlockSpec((1, T, LANES), lambda g, qi: (g, 0, 0)),
                  pl.BlockSpec((1, tq, hpg * NSA_HEAD_DIM), lambda g, qi: (2, qi, g))],
        out_specs=pl.BlockSpec((tq, hpg * NSA_HEAD_DIM), lambda g, qi: (qi, g)),
        out_shape=jax.ShapeDtypeStruct((T, GROUP_WIDTH), BF16),
        compiler_params=_cparams(("parallel", "arbitrary")),
        name="nsa_window_attention",
    )(qn, kw, vw, gx)


def _out_proj_body(a_ref, b_ref, c_ref, d1_ref, d2_ref, d3_ref, w_ref, x_ref, g_ref, o_ref, hn_ref, lhs_ref):
    lhs_ref[:, 0:GROUP_WIDTH] = a_ref[...]
    lhs_ref[:, GROUP_WIDTH:2 * GROUP_WIDTH] = b_ref[...]
    lhs_ref[:, 2 * GROUP_WIDTH:3 * GROUP_WIDTH] = c_ref[...]
    d = d1_ref[...].astype(F32) + d2_ref[...].astype(F32) + d3_ref[...].astype(F32)
    lhs_ref[:, 3 * GROUP_WIDTH:] = d.astype(BF16)
    y = x_ref[...] + _dot(lhs_ref[...], w_ref[0])
    o_ref[...] = y
    hn_ref[...] = _rms(y, g_ref[...]).astype(BF16)


def out_proj(parts, w, layer, x, g_next, tm):
    T, N = x.shape
    part = pl.BlockSpec((tm, GROUP_WIDTH), lambda i: (i, 0))
    rows = pl.BlockSpec((tm, N), lambda i: (i, 0))
    return pl.pallas_call(
        _out_proj_body,
        grid=(T // tm,),
        in_specs=[part] * 6 + [pl.BlockSpec((1, 4 * GROUP_WIDTH, N), lambda i: (layer, 0, 0)), rows,
                               pl.BlockSpec((1, N), lambda i: (0, 0))],
        out_specs=[rows, rows],
        out_shape=[jax.ShapeDtypeStruct((T, N), F32), jax.ShapeDtypeStruct((T, N), BF16)],
        scratch_shapes=[pltpu.VMEM((tm, 4 * GROUP_WIDTH), BF16)],
        compiler_params=_cparams(("parallel",)),
        name="out_proj",
    )(*parts, w, x, g_next)


def _group_starts(te_ref, i):
    return (i == 0) | (te_ref[i] != te_ref[jnp.maximum(i - 1, 0)])


def _stream_group_weights(te_ref, nx_ref, w_hbms, wbuf_ref, wb_refs, sem_ref, slot_ref):
    j = pl.program_id(0)
    i = pl.program_id(1)
    tn = wbuf_ref.shape[3]

    def copies(e, jj, slot):
        cols = pl.ds(pl.multiple_of(jj * tn, tn), tn)
        return [pltpu.make_async_copy(w.at[e, :, cols], wbuf_ref.at[slot, n], sem_ref.at[slot])
                for n, w in enumerate(w_hbms)]

    @pl.when((j == 0) & (i == 0))
    def _():
        slot_ref[0] = 0
        for c in copies(te_ref[0], 0, 0):
            c.start()

    @pl.when(_group_starts(te_ref, i))
    def _():
        slot = slot_ref[0]
        for c in copies(te_ref[i], j, slot):
            c.wait()
        for n, wb_ref in enumerate(wb_refs):
            wb_ref[...] = wbuf_ref[slot, n].astype(BF16)
        in_sweep = nx_ref[i] >= 0

        @pl.when(in_sweep | (j + 1 < pl.num_programs(0)))
        def _():
            for c in copies(jnp.where(in_sweep, nx_ref[i], te_ref[0]), jnp.where(in_sweep, j, j + 1), 1 - slot):
                c.start()

        slot_ref[0] = 1 - slot


def _weight_stream_scratch(n_weights, k, tn):
    return [pltpu.VMEM((2, n_weights, k, tn), F32)] + [pltpu.VMEM((k, tn), BF16)] * n_weights + [
        pltpu.SemaphoreType.DMA((2,)), pltpu.SMEM((1,), jnp.int32)]


def _up_body(te_ref, nx_ref, na_ref, h_ref, wg_hbm, wu_hbm, o_ref, wbuf_ref, wgb_ref, wub_ref, sem_ref, slot_ref):
    i = pl.program_id(1)
    _stream_group_weights(te_ref, nx_ref, (wg_hbm, wu_hbm), wbuf_ref, (wgb_ref, wub_ref), sem_ref, slot_ref)

    @pl.when(i < na_ref[0])
    def _():
        h = h_ref[...]
        a = _dot(h, wgb_ref[...])
        o_ref[...] = (a * jax.nn.sigmoid(a) * _dot(h, wub_ref[...])).astype(BF16)

    @pl.when(i >= na_ref[0])
    def _():
        o_ref[...] = jnp.zeros(o_ref.shape, BF16)


def swiglu_up(tile_expert, next_expert, n_active, hn, wg, wu, tm, tn):
    P, K = hn.shape
    F = wg.shape[2]
    return pl.pallas_call(
        _up_body,
        grid_spec=pltpu.PrefetchScalarGridSpec(
            num_scalar_prefetch=3,
            grid=(F // tn, P // tm),
            in_specs=[pl.BlockSpec((tm, K), lambda j, i, te, nx, na: (jnp.minimum(i, na[0] - 1), 0)),
                      pl.BlockSpec(memory_space=pl.ANY), pl.BlockSpec(memory_space=pl.ANY)],
            out_specs=pl.BlockSpec((tm, tn), lambda j, i, te, nx, na: (i, j)),
            scratch_shapes=_weight_stream_scratch(2, K, tn)),
        out_shape=jax.ShapeDtypeStruct((P, F), BF16),
        compiler_params=_cparams(("arbitrary", "arbitrary")),
        name="swiglu_up",
    )(tile_expert, next_expert, n_active, hn, wg, wu)


def _down_body(te_ref, nx_ref, na_ref, a_ref, w_hbm, *rest, residual):
    r_ref = rest[0] if residual else None
    o_ref, wbuf_ref, wb_ref, sem_ref, slot_ref = rest[1:] if residual else rest
    _stream_group_weights(te_ref, nx_ref, (w_hbm,), wbuf_ref, (wb_ref,), sem_ref, slot_ref)
    y = _dot(a_ref[...], wb_ref[...])
    o_ref[...] = r_ref[...] + y if residual else y


def swiglu_down(tile_expert, next_expert, n_active, act, wd, residual, tm, tn):
    P, F = act.shape
    N = wd.shape[2]
    tile = pl.BlockSpec((tm, tn), lambda j, i, te, nx, na: (i, j))
    in_specs = [pl.BlockSpec((tm, F), lambda j, i, te, nx, na: (jnp.minimum(i, na[0] - 1), 0)),
                pl.BlockSpec(memory_space=pl.ANY)]
    operands = [act, wd]
    if residual is not None:
        in_specs.append(tile)
        operands.append(residual)
    return pl.pallas_call(
        functools.partial(_down_body, residual=residual is not None),
        grid_spec=pltpu.PrefetchScalarGridSpec(
            num_scalar_prefetch=3,
            grid=(N // tn, P // tm),
            in_specs=in_specs,
            out_specs=tile,
            scratch_shapes=_weight_stream_scratch(1, F, tn)),
        out_shape=jax.ShapeDtypeStruct((P, N), F32),
        compiler_params=_cparams(("arbitrary", "arbitrary")),
        name="swiglu_down",
    )(tile_expert, next_expert, n_active, *operands)


def _router_body(x_ref, g_ref, w_ref, o_ref):
    h = _rms(x_ref[...], g_ref[...])
    logits = jnp.dot(h, w_ref[...], preferred_element_type=F32, precision=lax.Precision.HIGHEST)
    lane = _lane(logits.shape)
    lf = lane.astype(F32)
    logits = jnp.where(lane < N_EXPERTS, logits, -jnp.inf)
    v0 = jnp.max(logits, axis=-1, keepdims=True)
    i0 = jnp.min(jnp.where(logits == v0, lf, float(LANES)), axis=-1, keepdims=True)
    rest = jnp.where(lf == i0, -jnp.inf, logits)
    v1 = jnp.max(rest, axis=-1, keepdims=True)
    i1 = jnp.min(jnp.where(rest == v1, lf, float(LANES)), axis=-1, keepdims=True)
    e1 = jnp.exp(v1 - v0)
    w0 = 1.0 / (1.0 + e1)
    w1 = e1 / (1.0 + e1)
    o_ref[...] = jnp.where(lane == 0, i0, jnp.where(lane == 1, i1, jnp.where(lane == 2, w0, w1)))


def router(x, g, w_pad, tm):
    T, K = x.shape
    return pl.pallas_call(
        _router_body,
        grid=(T // tm,),
        in_specs=[pl.BlockSpec((tm, K), lambda i: (i, 0)),
                  pl.BlockSpec((1, K), lambda i: (0, 0)),
                  pl.BlockSpec((K, LANES), lambda i: (0, 0))],
        out_specs=pl.BlockSpec((tm, LANES), lambda i: (i, 0)),
        out_shape=jax.ShapeDtypeStruct((T, LANES), F32),
        compiler_params=_cparams(("parallel",)),
        name="moe_router",
    )(x, g, w_pad)


ROW_DMA_UNROLL = 8


def _row_copy(src_hbm, row, dst_ref, r, sem):
    return pltpu.make_async_copy(src_hbm.at[pl.ds(row, 1)], dst_ref.at[pl.ds(r, 1)], sem)


def _gather_body(tok_ref, na_ref, x_hbm, g_ref, o_ref, buf_ref, sem_ref):
    i = pl.program_id(0)
    tm = buf_ref.shape[1]

    def fetch(tile):
        slot = tile % 2

        def start(r, c):
            _row_copy(x_hbm, tok_ref[tile * tm + r], buf_ref.at[slot], r, sem_ref.at[slot]).start()
            return c

        lax.fori_loop(0, tm, start, 0, unroll=ROW_DMA_UNROLL)

    @pl.when(i == 0)
    def _():
        fetch(0)

    @pl.when(i + 1 < na_ref[0])
    def _():
        fetch(i + 1)

    @pl.when(i < na_ref[0])
    def _():
        slot = i % 2

        def wait(r, c):
            _row_copy(x_hbm, 0, buf_ref.at[slot], r, sem_ref.at[slot]).wait()
            return c

        lax.fori_loop(0, tm, wait, 0, unroll=ROW_DMA_UNROLL)
        o_ref[...] = _rms(buf_ref[slot], g_ref[...]).astype(BF16)

    @pl.when(i >= na_ref[0])
    def _():
        o_ref[...] = jnp.zeros(o_ref.shape, BF16)


def gather_norm_tokens(tok_of_slot, n_active, x, g, tm):
    P = tok_of_slot.shape[0]
    K = x.shape[1]
    return pl.pallas_call(
        _gather_body,
        grid_spec=pltpu.PrefetchScalarGridSpec(
            num_scalar_prefetch=2,
            grid=(P // tm,),
            in_specs=[pl.BlockSpec(memory_space=pl.ANY), pl.BlockSpec((1, K), lambda i, tok, na: (0, 0))],
            out_specs=pl.BlockSpec((tm, K), lambda i, tok, na: (i, 0)),
            scratch_shapes=[pltpu.VMEM((2, tm, K), F32), pltpu.SemaphoreType.DMA((2,))]),
        out_shape=jax.ShapeDtypeStruct((P, K), BF16),
        compiler_params=_cparams(("arbitrary",)),
        name="moe_gather",
    )(tok_of_slot, n_active, x, g)


def _combine_body(s0_ref, s1_ref, x_ref, y_hbm, rt_ref, g_ref, o_ref, b0_ref, b1_ref, sem0, sem1, *, final):
    tm = x_ref.shape[0]
    base = pl.program_id(0) * tm

    def start(r, c):
        _row_copy(y_hbm, s0_ref[base + r], b0_ref, r, sem0).start()
        _row_copy(y_hbm, s1_ref[base + r], b1_ref, r, sem1).start()
        return c

    def wait(r, c):
        _row_copy(y_hbm, 0, b0_ref, r, sem0).wait()
        _row_copy(y_hbm, 0, b1_ref, r, sem1).wait()
        return c

    lax.fori_loop(0, tm, start, 0, unroll=ROW_DMA_UNROLL)
    lax.fori_loop(0, tm, wait, 0, unroll=ROW_DMA_UNROLL)
    gates = rt_ref[...]
    y = x_ref[...] + gates[:, 2:3] * b0_ref[...] + gates[:, 3:4] * b1_ref[...]
    o_ref[...] = _rms(y, g_ref[...]) if final else y


def moe_combine(slot0, slot1, x, ys, route, g, tm, final):
    T, K = x.shape
    return pl.pallas_call(
        functools.partial(_combine_body, final=final),
        grid_spec=pltpu.PrefetchScalarGridSpec(
            num_scalar_prefetch=2,
            grid=(T // tm,),
            in_specs=[pl.BlockSpec((tm, K), lambda i, s0, s1: (i, 0)),
                      pl.BlockSpec(memory_space=pl.ANY),
                      pl.BlockSpec((tm, LANES), lambda i, s0, s1: (i, 0)),
                      pl.BlockSpec((1, K), lambda i, s0, s1: (0, 0))],
            out_specs=pl.BlockSpec((tm, K), lambda i, s0, s1: (i, 0)),
            scratch_shapes=[pltpu.VMEM((tm, K), F32), pltpu.VMEM((tm, K), F32),
                            pltpu.SemaphoreType.DMA(()), pltpu.SemaphoreType.DMA(())]),
        out_shape=jax.ShapeDtypeStruct((T, K), F32),
        compiler_params=_cparams(("arbitrary",)),
        name="moe_combine",
    )(slot0, slot1, x, ys, route, g)


def _final_norm_body(x_ref, g_ref, o_ref):
    o_ref[...] = _rms(x_ref[...], g_ref[...])


def final_norm(x, g, tm):
    T, K = x.shape
    return pl.pallas_call(
        _final_norm_body,
        grid=(T // tm,),
        in_specs=[pl.BlockSpec((tm, K), lambda i: (i, 0)), pl.BlockSpec((1, K), lambda i: (0, 0))],
        out_specs=pl.BlockSpec((tm, K), lambda i: (i, 0)),
        out_shape=jax.ShapeDtypeStruct((T, K), F32),
        compiler_params=_cparams(("parallel",)),
        name="final_norm",
    )(x, g)


def _routing_tables(route, tm, expert_base):
    T = route.shape[0]
    top_i = route[:, 0:2].astype(jnp.int32)
    e_flat = top_i.reshape(-1)
    onehot = (e_flat[:, None] == jnp.arange(N_EXPERTS)[None, :]).astype(jnp.int32)
    rank = jnp.take_along_axis(jnp.cumsum(onehot, axis=0) - onehot, e_flat[:, None], axis=1)[:, 0]
    count = jnp.sum(onehot, axis=0)
    padded = ((count + tm - 1) // tm) * tm
    end = jnp.cumsum(padded)
    start = end - padded
    slot = start[e_flat] + rank
    n_slots = 2 * T + N_EXPERTS * tm
    n_tiles = n_slots // tm
    n_active = (end[-1] // tm).astype(jnp.int32)
    tile_start = jnp.minimum(jnp.arange(n_tiles, dtype=jnp.int32), n_active - 1) * tm
    tile_expert = jnp.minimum(jnp.sum(tile_start[:, None] >= end[None, :], axis=1), N_EXPERTS - 1)
    group_end = end[tile_expert] // tm
    next_expert = jnp.where(group_end < n_active, tile_expert[jnp.minimum(group_end, n_tiles - 1)] + expert_base, -1)
    tok_of_slot = jnp.zeros((n_slots,), jnp.int32).at[slot].set(jnp.arange(2 * T, dtype=jnp.int32) // 2)
    slot2 = slot.reshape(T, 2).astype(jnp.int32)
    return ((tile_expert + expert_base).astype(jnp.int32), next_expert.astype(jnp.int32), n_active.reshape(1),
            tok_of_slot, slot2[:, 0], slot2[:, 1])


def _rope_tables128(T):
    half = NSA_HEAD_DIM // 2
    inv = ROPE_THETA ** (-jnp.arange(0, NSA_HEAD_DIM, 2, dtype=F32) / NSA_HEAD_DIM)
    ang = jnp.arange(T, dtype=F32)[:, None] * inv[None, :]
    cos, sin = jnp.cos(ang), jnp.sin(ang)
    cos128 = jnp.tile(cos, (1, LANES // half))
    sin128 = jnp.tile(jnp.concatenate([-sin, sin], axis=1), (1, LANES // NSA_HEAD_DIM))
    return cos128, sin128


def _overlap_matrix_t(n_cmp_pad):
    sstart = jnp.arange(LANES) * NSA_SLC_LEN
    cstart = jnp.arange(n_cmp_pad) * NSA_CMP_STRIDE
    ov = (cstart[None, :] < sstart[:, None] + NSA_SLC_LEN) & (cstart[None, :] + NSA_CMP_LEN > sstart[:, None])
    return ov.astype(BF16)


def kernel(x, attn_norm, w_in, w_out, gm_ln_g, gm_ln_b, gm_ws, gm_bs, da_lambda, da_subln, cv_dw_w, cv_dw_b,
           cv_ln_g, cv_ln_b, nsa_cmp_w1, nsa_cmp_w2, nsa_cmp_pe, ffn_norm, ffn_wg, ffn_wu, ffn_wd, router_w,
           exp_wg, exp_wu, exp_wd, final_norm_g):
    B, T, D = x.shape
    assert B == 1 and D == D_MODEL and T % 1024 == 0 and T // NSA_SLC_LEN <= LANES
    depth = w_in.shape[0]
    G = NSA_KV_HEADS
    n16 = T // NSA_CMP_STRIDE
    tm = 512

    cos128, sin128 = _rope_tables128(T)
    overlap_t = _overlap_matrix_t(n16)
    dense_na = jnp.full((1,), T // tm, jnp.int32)
    row = lambda v: v.reshape(1, -1)
    merge = lambda w: w.reshape((-1,) + w.shape[2:])
    exp_wg, exp_wu, exp_wd = merge(exp_wg), merge(exp_wu), merge(exp_wd)
    w_in_b = jnp.pad(w_in, ((0, 0), (0, 0), (0, IN_WIDTH_PAD - IN_WIDTH))).astype(BF16)
    w_out_b = w_out.astype(BF16)

    xs = x[0]
    for l in range(depth):
        lambda_init = 0.8 - 0.6 * math.exp(-0.3 * l)
        z = norm_mm(xs, row(attn_norm[l]), w_in_b, l, 1024, IN_WIDTH_PAD // 3)

        bs_rows = jnp.repeat(gm_bs[l].T, GM_CHUNK, axis=1)
        o_a = gmlp(z, row(gm_ln_g[l]), row(gm_ln_b[l]), gm_ws[l], bs_rows, 512)
        dw_w = jnp.pad(cv_dw_w[l], ((0, CV_HALO - CV_WIDTH), (0, 0)))
        o_c = conformer_conv(z, dw_w, row(cv_dw_b[l]), row(cv_ln_g[l]), row(cv_ln_b[l]), 256)

        qd, kd, vd, qn, kc, vc, ks, vs, kw, vw, gx = prep(z, cos128, sin128, 256)
        o_b = diff_attention(qd, kd, vd, da_lambda[l], row(da_subln[l]), lambda_init, 512, 512)

        x16 = jnp.stack([kc, vc]).reshape(2, T, G, NSA_HEAD_DIM).transpose(0, 2, 1, 3)
        x16 = x16.reshape(2, G, n16, NSA_CMP_STRIDE * NSA_HEAD_DIM)
        pe2 = nsa_cmp_pe[l].reshape(2, 2, NSA_CMP_STRIDE * NSA_HEAD_DIM)
        w2p = jnp.pad(nsa_cmp_w2[l], ((0, 0), (0, 0), (0, LANES - NSA_HEAD_DIM))).astype(BF16)
        kvc = compress(x16, pe2, nsa_cmp_w1[l].astype(BF16), w2p)
        o_cmp, selbias = cmp_select(qn, kvc, overlap_t, gx, 256)
        o_sel = sel_attention(qn, selbias, ks, vs, gx, 256, 512)
        o_win = win_attention(qn, kw, vw, gx, 256)

        g_ffn = row(ffn_norm[l])
        xs, hn = out_proj((o_a, o_b, o_c, o_cmp, o_sel, o_win), w_out_b, l, xs, g_ffn, 256)

        e = l // 2
        if l % 2 == 0:
            dense_te = jnp.full((T // tm,), e, jnp.int32)
            dense_nx = jnp.full((T // tm,), -1, jnp.int32)
            act = swiglu_up(dense_te[:T // 1024], dense_nx[:T // 1024], dense_na // 2, hn, ffn_wg, ffn_wu, 1024, 512)
            xs = swiglu_down(dense_te, dense_nx, dense_na, act, ffn_wd, xs, tm, 512)
            if l == depth - 1:
                xs = final_norm(xs, row(final_norm_g), tm)
        else:
            w_r = jnp.pad(router_w[e], ((0, 0), (0, LANES - N_EXPERTS)))
            route = router(xs, g_ffn, w_r, tm)
            tile_expert, next_expert, n_active, tok_of_slot, slot0, slot1 = _routing_tables(
                route, tm, e * N_EXPERTS)
            hg = gather_norm_tokens(tok_of_slot, n_active, xs, g_ffn, tm)
            act = swiglu_up(tile_expert, next_expert, n_active, hg, exp_wg, exp_wu, tm, 512)
            ys = swiglu_down(tile_expert, next_expert, n_active, act, exp_wd, None, tm, 512)
            xs = moe_combine(slot0, slot1, xs, ys, route, row(final_norm_g), 256, final=(l == depth - 1))
    return xs[None]
```

```python
import functools
import math

import jax
import jax.numpy as jnp
from jax import lax
from jax.experimental import pallas as pl
from jax.experimental.pallas import tpu as pltpu

F32 = jnp.float32
BF16 = jnp.bfloat16

D_MODEL = 2048
GROUP_WIDTH = 512
GM_CHUNK = 128
GM_HEADS = 4
DA_HEADS = 4
DA_QK_DIM = 64
CV_WIDTH = 31
NSA_HEADS = 8
NSA_KV_HEADS = 2
NSA_HEAD_DIM = 64
NSA_CMP_LEN = 32
NSA_CMP_STRIDE = 16
NSA_SLC_LEN = 64
NSA_SLC_TOPK = 16
NSA_WINDOW = 512
ROPE_THETA = 10000.0
NORM_EPS = 1e-6
NEG = -1e30
FORCE = 1e9
N_EXPERTS = 8
LANES = 128
SUBLANES = 8
LOG2E = math.log2(math.e)

IN_WIDTH = 4888
IN_WIDTH_PAD = 4992
COL_GM = 0
COL_QDA = 1024
COL_KDA = 1536
COL_VDA = 2048
COL_CV = 2560
COL_QNS = 3584
COL_KVNS = 4096
COL_GNS = 4864

VMEM_LIMIT = 56 * 1024 * 1024


def _cparams(sem, **kw):
    return pltpu.CompilerParams(dimension_semantics=sem, vmem_limit_bytes=VMEM_LIMIT, **kw)


def _rms(x, g):
    ms = jnp.mean(x * x, axis=-1, keepdims=True)
    return x * lax.rsqrt(ms + NORM_EPS) * g


def _layer_norm(x, g, b):
    mu = jnp.mean(x, axis=-1, keepdims=True)
    xc = x - mu
    var = jnp.mean(xc * xc, axis=-1, keepdims=True)
    return xc * lax.rsqrt(var + NORM_EPS) * g + b


def _dot(a, b):
    return jnp.dot(a, b, preferred_element_type=F32)


def _dot_nt(a, b):
    return lax.dot_general(a, b, (((1,), (1,)), ((), ())), preferred_element_type=F32)


def _lane(shape):
    return lax.broadcasted_iota(jnp.int32, shape, len(shape) - 1)


def _row(shape):
    return lax.broadcasted_iota(jnp.int32, shape, len(shape) - 2)


def _norm_mm_body(x_ref, g_ref, w_ref, o_ref, hn_ref):
    @pl.when(pl.program_id(1) == 0)
    def _():
        hn_ref[...] = _rms(x_ref[...], g_ref[...]).astype(BF16)

    o_ref[...] = _dot(hn_ref[...], w_ref[0])


def norm_mm(x, g, w, layer, tm, tn):
    T, K = x.shape
    N = w.shape[2]
    return pl.pallas_call(
        _norm_mm_body,
        grid=(T // tm, N // tn),
        in_specs=[pl.BlockSpec((tm, K), lambda i, j: (i, 0)),
                  pl.BlockSpec((1, K), lambda i, j: (0, 0)),
                  pl.BlockSpec((1, K, tn), lambda i, j: (layer, 0, j))],
        out_specs=pl.BlockSpec((tm, tn), lambda i, j: (i, j)),
        out_shape=jax.ShapeDtypeStruct((T, N), F32),
        scratch_shapes=[pltpu.VMEM((tm, K), BF16)],
        compiler_params=_cparams(("parallel", "arbitrary")),
        name="norm_in_proj",
    )(x, g, w)


def _gmlp_body(z_ref, g_ref, b_ref, ws_ref, bs_ref, o_ref):
    tr = z_ref.shape[0]
    z = jax.nn.gelu(z_ref[...])
    u = z[:, :GROUP_WIDTH]
    v = _layer_norm(z[:, GROUP_WIDTH:], g_ref[...], b_ref[...]).astype(BF16)
    causal = _row((GM_CHUNK, GM_CHUNK)) >= _lane((GM_CHUNK, GM_CHUNK))
    bias = bs_ref[...]
    for h in range(GM_HEADS):
        w = jnp.where(causal, ws_ref[h], 0.0).astype(BF16)
        cols = slice(h * LANES, (h + 1) * LANES)
        for c in range(tr // GM_CHUNK):
            rows = slice(c * GM_CHUNK, (c + 1) * GM_CHUNK)
            s = _dot(w, v[rows, cols]) + bias[:, cols]
            o_ref[rows, cols] = (u[rows, cols] * s).astype(BF16)


def gmlp(z, ln_g, ln_b, ws, bs_rows, tr):
    T = z.shape[0]
    return pl.pallas_call(
        _gmlp_body,
        grid=(T // tr,),
        in_specs=[pl.BlockSpec((tr, 2 * GROUP_WIDTH), lambda i: (i, COL_GM // (2 * GROUP_WIDTH))),
                  pl.BlockSpec((1, GROUP_WIDTH), lambda i: (0, 0)),
                  pl.BlockSpec((1, GROUP_WIDTH), lambda i: (0, 0)),
                  pl.BlockSpec((GM_HEADS, GM_CHUNK, GM_CHUNK), lambda i: (0, 0, 0)),
                  pl.BlockSpec((GM_CHUNK, GROUP_WIDTH), lambda i: (0, 0))],
        out_specs=pl.BlockSpec((tr, GROUP_WIDTH), lambda i: (i, 0)),
        out_shape=jax.ShapeDtypeStruct((T, GROUP_WIDTH), BF16),
        compiler_params=_cparams(("parallel",)),
        name="gmlp",
    )(z, ln_g, ln_b, ws, bs_rows)


CV_HALO = 32
CV_SUB = 64


def _conv_body(a_ref, g_ref, ap_ref, gp_ref, w_ref, b_ref, lg_ref, lb_ref, o_ref, hs_ref):
    tr = a_ref.shape[0]
    first = pl.program_id(0) == 0
    prev = ap_ref[...] * jax.nn.sigmoid(gp_ref[...])
    hs_ref[0:CV_HALO, :] = jnp.where(first, 0.0, prev)
    hs_ref[CV_HALO:CV_HALO + tr, :] = a_ref[...] * jax.nn.sigmoid(g_ref[...])
    hs_ref[CV_HALO + tr:, :] = jnp.zeros((SUBLANES, GROUP_WIDTH), F32)
    w = w_ref[...]
    lead = CV_HALO - (CV_WIDTH - 1)
    for r0 in range(0, tr, CV_SUB):
        acc = jnp.zeros((CV_SUB, GROUP_WIDTH), F32)
        for phase in range(SUBLANES):
            base, shift = divmod(lead + phase, SUBLANES)
            part = jnp.zeros((CV_SUB + SUBLANES, GROUP_WIDTH), F32)
            for k in range(phase, CV_WIDTH, SUBLANES):
                start = r0 + SUBLANES * (base + k // SUBLANES)
                part = part + hs_ref[start:start + CV_SUB + SUBLANES, :] * w[k:k + 1, :]
            acc = acc + part[shift:shift + CV_SUB]
        y = _layer_norm(acc + b_ref[...], lg_ref[...], lb_ref[...])
        o_ref[r0:r0 + CV_SUB, :] = (y * jax.nn.sigmoid(y)).astype(BF16)


def conformer_conv(z, dw_w, dw_b, ln_g, ln_b, tr):
    T = z.shape[0]
    ca = COL_CV // GROUP_WIDTH
    per = tr // CV_HALO

    def halo(col):
        return lambda i: (jnp.maximum(i * per - 1, 0), col)

    return pl.pallas_call(
        _conv_body,
        grid=(T // tr,),
        in_specs=[pl.BlockSpec((tr, GROUP_WIDTH), lambda i: (i, ca)),
                  pl.BlockSpec((tr, GROUP_WIDTH), lambda i: (i, ca + 1)),
                  pl.BlockSpec((CV_HALO, GROUP_WIDTH), halo(ca)),
                  pl.BlockSpec((CV_HALO, GROUP_WIDTH), halo(ca + 1)),
                  pl.BlockSpec((CV_HALO, GROUP_WIDTH), lambda i: (0, 0)),
                  pl.BlockSpec((1, GROUP_WIDTH), lambda i: (0, 0)),
                  pl.BlockSpec((1, GROUP_WIDTH), lambda i: (0, 0)),
                  pl.BlockSpec((1, GROUP_WIDTH), lambda i: (0, 0))],
        out_specs=pl.BlockSpec((tr, GROUP_WIDTH), lambda i: (i, 0)),
        out_shape=jax.ShapeDtypeStruct((T, GROUP_WIDTH), BF16),
        scratch_shapes=[pltpu.VMEM((CV_HALO + tr + SUBLANES, GROUP_WIDTH), F32)],
        compiler_params=_cparams(("parallel",)),
        name="conformer_conv",
    )(z, z, z, z, dw_w, dw_b, ln_g, ln_b)


def _rope128(x, cos, sin_signed):
    lo = (_lane(x.shape) & 63) < 32
    rot = jnp.where(lo, pltpu.roll(x, 96, 1), pltpu.roll(x, 32, 1))
    return x * cos + rot * sin_signed


def _low_half(x, fill=0.0):
    return jnp.where(_lane(x.shape) < 64, x, fill)


def _high_half_to_low(x, fill=0.0):
    return jnp.where(_lane(x.shape) < 64, pltpu.roll(x, 64, 1), fill)


def _prep_body(qd_ref, kd_ref, vd_ref, qn_ref, kvc_ref, kvs_ref, kvw_ref, gt_ref, cos_ref, sin_ref, ge_ref,
               qd_o, kd_o, vd_o, qn_o, kc_o, vc_o, ks_o, vs_o, kw_o, vw_o, gx_o):
    tr = cos_ref.shape[0]
    cos = cos_ref[...]
    sin = sin_ref[...]
    q_scale = DA_QK_DIM ** -0.5 * LOG2E
    ones = jnp.ones((tr, LANES), BF16)
    for h in range(DA_HEADS):
        cols = slice(h * LANES, (h + 1) * LANES)
        qd_o[h] = (_rope128(qd_ref[:, cols], cos, sin) * q_scale).astype(BF16)
        kd_o[h] = _rope128(kd_ref[:, cols], cos, sin).astype(BF16)
        vd_o[h, :, 0:LANES] = vd_ref[:, cols].astype(BF16)
        vd_o[h, :, LANES:] = ones
    for c in range(NSA_HEADS // 2):
        q = _rope128(qn_ref[:, c * LANES:(c + 1) * LANES], cos, sin) * (NSA_HEAD_DIM ** -0.5 * LOG2E)
        qn_o[2 * c] = _low_half(q).astype(BF16)
        qn_o[2 * c + 1] = _high_half_to_low(q).astype(BF16)
    kc_o[...] = _rope128(kvc_ref[:, :LANES], cos, sin)
    vc_o[...] = kvc_ref[:, LANES:]
    blk = (pl.program_id(0) * tr + _row((tr, LANES))) >> 6
    onehot = jnp.where(blk == _lane((tr, LANES)), 1.0, 0.0).astype(BF16)
    k = _rope128(kvs_ref[:, :LANES], cos, sin)
    for g, half in enumerate((_low_half, _high_half_to_low)):
        ks_o[g, :, 0:LANES] = onehot
        ks_o[g, :, LANES:] = half(k).astype(BF16)
        vs_o[g] = half(kvs_ref[:, LANES:], 1.0).astype(BF16)
    k = _rope128(kvw_ref[:, :LANES], cos, sin)
    for g, half in enumerate((_low_half, _high_half_to_low)):
        kw_o[g] = half(k).astype(BF16)
        vw_o[g] = half(kvw_ref[:, LANES:], 1.0).astype(BF16)
    expand = ge_ref[...]
    gx = sum(_dot(part, expand) for part in _split3(jax.nn.sigmoid(gt_ref[...])))
    for c in range(3):
        gx_o[c] = gx[:, c * GROUP_WIDTH:(c + 1) * GROUP_WIDTH]


def _gate_expansion():
    col = jnp.arange(3 * GROUP_WIDTH)
    src = (col // GROUP_WIDTH) * NSA_HEADS + (col % GROUP_WIDTH) // NSA_HEAD_DIM
    return (jnp.arange(LANES)[:, None] == src[None, :]).astype(BF16)


def prep(z, cos128, sin128, tr):
    T = z.shape[0]
    G = NSA_KV_HEADS

    def zspec(width, col):
        return pl.BlockSpec((tr, width), lambda i: (i, col // width))

    def heads(n, width=LANES):
        return pl.BlockSpec((n, tr, width), lambda i: (0, i, 0))

    def hshape(n, width=LANES):
        return jax.ShapeDtypeStruct((n, T, width), BF16)

    row128 = pl.BlockSpec((tr, LANES), lambda i: (i, 0))
    return pl.pallas_call(
        _prep_body,
        grid=(T // tr,),
        in_specs=[zspec(512, COL_QDA), zspec(512, COL_KDA), zspec(512, COL_VDA), zspec(512, COL_QNS),
                  zspec(256, COL_KVNS), zspec(256, COL_KVNS + 256), zspec(256, COL_KVNS + 512),
                  zspec(128, COL_GNS), row128, row128,
                  pl.BlockSpec((LANES, 3 * GROUP_WIDTH), lambda i: (0, 0))],
        out_specs=[heads(DA_HEADS), heads(DA_HEADS), heads(DA_HEADS, 2 * LANES),
                   heads(NSA_HEADS), row128, row128, heads(G, 2 * LANES), heads(G), heads(G), heads(G),
                   pl.BlockSpec((3, tr, GROUP_WIDTH), lambda i: (0, i, 0))],
        out_shape=[hshape(DA_HEADS), hshape(DA_HEADS), hshape(DA_HEADS, 2 * LANES),
                   hshape(NSA_HEADS), jax.ShapeDtypeStruct((T, LANES), F32),
                   jax.ShapeDtypeStruct((T, LANES), F32), hshape(G, 2 * LANES), hshape(G), hshape(G), hshape(G),
                   jax.ShapeDtypeStruct((3, T, GROUP_WIDTH), F32)],
        compiler_params=_cparams(("parallel",)),
        name="attention_prep",
    )(z, z, z, z, z, z, z, z, cos128, sin128, _gate_expansion())


def _lane_tile(x, n):
    return x if n == 1 else jnp.concatenate([x] * n, axis=-1)


def _softmax_update(s, v, m_ref, acc_ref):
    m_prev = m_ref[...]
    m_new = jnp.maximum(m_prev, jnp.max(s, axis=-1, keepdims=True))
    alpha = jnp.exp2(m_prev - m_new)
    p = jnp.exp2(s - _lane_tile(m_new, s.shape[1] // LANES))
    acc_ref[...] = _lane_tile(alpha, acc_ref.shape[1] // LANES) * acc_ref[...] + _dot(p.astype(BF16), v)
    m_ref[...] = m_new


SWEEP_UNROLL = 4


def _causal_sweep(qs_ref, k_ref, v_ref, m_ref, acc_ref, s_ref, q0, tq, tk):
    m_ref[...] = jnp.full(m_ref.shape, -jnp.inf, F32)
    acc_ref[...] = jnp.zeros(acc_ref.shape, F32)

    def keys(ref, t):
        return ref[0, pl.ds(pl.multiple_of(t * tk, tk), tk), :]

    def scores(t):
        return _dot_nt(qs_ref[...], keys(k_ref, t))

    n_full = q0 // tk
    s_ref[...] = scores(0)

    def run(t0, count):
        s_cur = s_ref[...]
        for u in range(count):
            s_next = scores(t0 + u + 1)
            _softmax_update(s_cur, keys(v_ref, t0 + u), m_ref, acc_ref)
            s_cur = s_next
        s_ref[...] = s_cur

    def several(i, c):
        run(SWEEP_UNROLL * i, SWEEP_UNROLL)
        return c

    def single(t, c):
        run(t, 1)
        return c

    n_groups = n_full // SWEEP_UNROLL
    lax.fori_loop(0, n_groups, several, 0)
    lax.fori_loop(n_groups * SWEEP_UNROLL, n_full, single, 0)
    s = s_ref[...]
    visible = n_full * tk + _lane(s.shape) <= q0 + (_row(s.shape) & (tq - 1))
    _softmax_update(jnp.where(visible, s, NEG), keys(v_ref, n_full), m_ref, acc_ref)


def _diff_attn_body(q_ref, k_ref, v_ref, lam_ref, sg_ref, o_ref, qs_ref, m_ref, acc_ref, s_ref, *,
                    tq, tk, lambda_init):
    q = q_ref[0]
    first = _lane(q.shape) < DA_QK_DIM
    qs_ref[0:tq, :] = jnp.where(first, q, jnp.zeros_like(q))
    qs_ref[tq:, :] = jnp.where(first, jnp.zeros_like(q), q)
    _causal_sweep(qs_ref, k_ref, v_ref, m_ref, acc_ref, s_ref, pl.program_id(1) * tq, tq, tk)
    lam = lam_ref[...]
    lam_full = (jnp.exp(jnp.sum(lam[0:1] * lam[1:2], axis=-1, keepdims=True))
                - jnp.exp(jnp.sum(lam[2:3] * lam[3:4], axis=-1, keepdims=True)) + lambda_init)
    o = acc_ref[:, 0:LANES] / acc_ref[:, LANES:]
    a = o[0:tq] - lam_full * o[tq:]
    o_ref[...] = (_rms(a, sg_ref[...]) * (1.0 - lambda_init)).astype(BF16)


def diff_attention(qd, kd, vd, lam, subln, lambda_init, tq, tk):
    H, T, _ = qd.shape
    return pl.pallas_call(
        functools.partial(_diff_attn_body, tq=tq, tk=tk, lambda_init=lambda_init),
        grid=(H, T // tq),
        in_specs=[pl.BlockSpec((1, tq, LANES), lambda h, qi: (h, qi, 0)),
                  pl.BlockSpec((1, T, LANES), lambda h, qi: (h, 0, 0)),
                  pl.BlockSpec((1, T, 2 * LANES), lambda h, qi: (h, 0, 0)),
                  pl.BlockSpec((4, DA_QK_DIM), lambda h, qi: (0, 0)),
                  pl.BlockSpec((1, LANES), lambda h, qi: (0, 0))],
        out_specs=pl.BlockSpec((tq, LANES), lambda h, qi: (qi, h)),
        out_shape=jax.ShapeDtypeStruct((T, GROUP_WIDTH), BF16),
        scratch_shapes=[pltpu.VMEM((2 * tq, LANES), BF16), pltpu.VMEM((2 * tq, LANES), F32),
                        pltpu.VMEM((2 * tq, 2 * LANES), F32), pltpu.VMEM((2 * tq, tk), F32)],
        compiler_params=_cparams(("parallel", "arbitrary")),
        name="diff_attention",
    )(qd, kd, vd, lam, subln)


def _compress_body(x_ref, pe_ref, w1_ref, w2_ref, o_ref):
    x = x_ref[0, 0]
    half = x.shape[1]
    a = _dot((x + pe_ref[0, 0:1, :]).astype(BF16), w1_ref[0, 0:half, :])
    b = _dot((x + pe_ref[0, 1:2, :]).astype(BF16), w1_ref[0, half:, :])
    hid = jax.nn.gelu(a + pltpu.roll(b, b.shape[0] - 1, 0))
    o_ref[0, 0] = _dot(hid.astype(BF16), w2_ref[0]).astype(BF16)


def compress(x16, pe2, w1, w2p):
    _, G, n, half = x16.shape
    hid = w1.shape[2]
    return pl.pallas_call(
        _compress_body,
        grid=(2, G),
        in_specs=[pl.BlockSpec((1, 1, n, half), lambda c, g: (c, g, 0, 0)),
                  pl.BlockSpec((1, 2, half), lambda c, g: (c, 0, 0)),
                  pl.BlockSpec((1, 2 * half, hid), lambda c, g: (c, 0, 0)),
                  pl.BlockSpec((1, hid, LANES), lambda c, g: (c, 0, 0))],
        out_specs=pl.BlockSpec((1, 1, n, LANES), lambda c, g: (c, g, 0, 0)),
        out_shape=jax.ShapeDtypeStruct((2, G, n, LANES), BF16),
        compiler_params=_cparams(("parallel", "parallel")),
        name="nsa_compress",
    )(x16, pe2, w1, w2p)


def _split3(x):
    hi = x.astype(BF16)
    r1 = x - hi.astype(F32)
    mid = r1.astype(BF16)
    lo = (r1 - mid.astype(F32)).astype(BF16)
    return hi, mid, lo


def _pack_heads(o, tq):
    pair = lambda a, b: a + pltpu.roll(b, 64, 1)
    return jnp.concatenate([pair(o[0:tq], o[tq:2 * tq]), pair(o[2 * tq:3 * tq], o[3 * tq:])], axis=-1)


def _normalize_low_half(acc):
    return jnp.where(_lane(acc.shape) < 64, acc / pltpu.roll(acc, 64, 1), 0.0)


def _cmp_select_body(q_ref, kc_ref, vc_ref, ovt_ref, gx_ref, o_ref, sb_ref, *, tq):
    qi = pl.program_id(1)
    hpg = q_ref.shape[0]
    q = q_ref[...].reshape(hpg * tq, LANES)
    s = _dot_nt(q, kc_ref[0, 0])
    t = qi * tq + (_row(s.shape) & (tq - 1))
    cmask = _lane(s.shape) * NSA_CMP_STRIDE + (NSA_CMP_LEN - 1) <= t
    s = jnp.where(cmask, s, NEG)
    e = jnp.exp2(s - jnp.max(s, axis=-1, keepdims=True))
    p = jnp.where(cmask, e / jnp.sum(e, axis=-1, keepdims=True), 0.0)
    o = _dot(p.astype(BF16), vc_ref[0, 0])
    o_ref[...] = (gx_ref[0] * _pack_heads(o, tq)).astype(BF16)

    psum = p[0:tq]
    for hh in range(1, hpg):
        psum = psum + p[hh * tq:(hh + 1) * tq]
    ovt = ovt_ref[...]
    imp = sum(_dot_nt(ovt, part) for part in _split3(psum))
    j = _row(imp.shape)
    cur = (qi * tq + _lane(imp.shape)) >> 6
    forced = (j == 0) | (j == cur) | (j == cur - 1)
    score = jnp.where(forced, FORCE, jnp.where(j <= cur, imp, -1.0))
    jf = j.astype(F32)
    bias = jnp.full(imp.shape, NEG, F32)
    for _ in range(NSA_SLC_TOPK):
        m = jnp.max(score, axis=0, keepdims=True)
        first = jnp.min(jnp.where(score == m, jf, float(LANES)), axis=0, keepdims=True)
        hit = jf == first
        bias = jnp.where(hit, jnp.where(m >= 0.0, 0.0, NEG), bias)
        score = jnp.where(hit, -2.0, score)
    sb_ref[0] = bias.T.astype(BF16)


def cmp_select(qn, kvc, overlap_t, gx, tq):
    H, T, _ = qn.shape
    G = NSA_KV_HEADS
    hpg = H // G
    n = kvc.shape[2]
    return pl.pallas_call(
        functools.partial(_cmp_select_body, tq=tq),
        grid=(G, T // tq),
        in_specs=[pl.BlockSpec((hpg, tq, LANES), lambda g, qi: (g, qi, 0)),
                  pl.BlockSpec((1, 1, n, LANES), lambda g, qi: (0, g, 0, 0)),
                  pl.BlockSpec((1, 1, n, LANES), lambda g, qi: (1, g, 0, 0)),
                  pl.BlockSpec((LANES, n), lambda g, qi: (0, 0)),
                  pl.BlockSpec((1, tq, hpg * NSA_HEAD_DIM), lambda g, qi: (0, qi, g))],
        out_specs=[pl.BlockSpec((tq, hpg * NSA_HEAD_DIM), lambda g, qi: (qi, g)),
                   pl.BlockSpec((1, tq, LANES), lambda g, qi: (g, qi, 0))],
        out_shape=[jax.ShapeDtypeStruct((T, GROUP_WIDTH), BF16),
                   jax.ShapeDtypeStruct((G, T, LANES), BF16)],
        compiler_params=_cparams(("parallel", "parallel")),
        name="nsa_compressed_select",
    )(qn, kvc, kvc, overlap_t, gx)


def _sel_attn_body(q_ref, sb_ref, k_ref, v_ref, gx_ref, o_ref, qs_ref, m_ref, acc_ref, s_ref, *, tq, tk):
    hpg = q_ref.shape[0]
    for hh in range(hpg):
        qs_ref[hh * tq:(hh + 1) * tq, 0:LANES] = sb_ref[0]
        qs_ref[hh * tq:(hh + 1) * tq, LANES:] = q_ref[hh]
    _causal_sweep(qs_ref, k_ref, v_ref, m_ref, acc_ref, s_ref, pl.program_id(1) * tq, tq, tk)
    o = _normalize_low_half(acc_ref[...])
    o_ref[...] = (gx_ref[0] * _pack_heads(o, tq)).astype(BF16)


def sel_attention(qn, selbias, ks, vs, gx, tq, tk):
    H, T, _ = qn.shape
    G = NSA_KV_HEADS
    hpg = H // G
    return pl.pallas_call(
        functools.partial(_sel_attn_body, tq=tq, tk=tk),
        grid=(G, T // tq),
        in_specs=[pl.BlockSpec((hpg, tq, LANES), lambda g, qi: (g, qi, 0)),
                  pl.BlockSpec((1, tq, LANES), lambda g, qi: (g, qi, 0)),
                  pl.BlockSpec((1, T, 2 * LANES), lambda g, qi: (g, 0, 0)),
                  pl.BlockSpec((1, T, LANES), lambda g, qi: (g, 0, 0)),
                  pl.BlockSpec((1, tq, hpg * NSA_HEAD_DIM), lambda g, qi: (1, qi, g))],
        out_specs=pl.BlockSpec((tq, hpg * NSA_HEAD_DIM), lambda g, qi: (qi, g)),
        out_shape=jax.ShapeDtypeStruct((T, GROUP_WIDTH), BF16),
        scratch_shapes=[pltpu.VMEM((hpg * tq, 2 * LANES), BF16), pltpu.VMEM((hpg * tq, LANES), F32),
                        pltpu.VMEM((hpg * tq, LANES), F32), pltpu.VMEM((hpg * tq, tk), F32)],
        compiler_params=_cparams(("parallel", "arbitrary")),
        name="nsa_selected_attention",
    )(qn, selbias, ks, vs, gx)


def _win_attn_body(q_ref, k_ref, v_ref, b_ref, gx_ref, o_ref, *, tq, span):
    hpg = q_ref.shape[0]
    q0 = pl.program_id(1) * tq
    lo = pl.multiple_of(jnp.maximum(q0 + tq - span, 0), tq)
    q = q_ref[...].reshape(hpg * tq, LANES)
    s = _dot_nt(q, k_ref[0, pl.ds(lo, span), :])

    def finish(s):
        p = jnp.exp2(s - jnp.max(s, axis=-1, keepdims=True))
        acc = _dot(p.astype(BF16), v_ref[0, pl.ds(lo, span), :])
        o_ref[...] = (gx_ref[0] * _pack_heads(_normalize_low_half(acc), tq)).astype(BF16)

    @pl.when(q0 + tq >= span)
    def _():
        finish(s + jnp.concatenate([b_ref[...]] * hpg, axis=0))

    @pl.when(q0 + tq < span)
    def _():
        qpos = q0 + (_row(s.shape) & (tq - 1))
        kpos = lo + _lane(s.shape)
        finish(jnp.where((kpos <= qpos) & (kpos > qpos - NSA_WINDOW), s, NEG))


def _window_bias(tq, span):
    d = jnp.arange(span)[None, :] - jnp.arange(tq)[:, None] + (tq - span)
    return jnp.where((d <= 0) & (d > -NSA_WINDOW), 0.0, NEG).astype(F32)


def win_attention(qn, kw, vw, gx, tq):
    H, T, _ = qn.shape
    G = NSA_KV_HEADS
    hpg = H // G
    span = NSA_WINDOW + tq
    return pl.pallas_call(
        functools.partial(_win_attn_body, tq=tq, span=span),
        grid=(G, T // tq),
        in_specs=[pl.BlockSpec((hpg, tq, LANES), lambda g, qi: (g, qi, 0)),
                  pl.BlockSpec((1, T, LANES), lambda g, qi: (g, 0, 0)),
                  pl.BlockSpec((1, T, LANES), lambda g, qi: (g, 0, 0)),
                  pl.BlockSpec((tq, span), lambda g, qi: (0, 0)),
                  pl.BlockSpec((1, tq, hpg * NSA_HEAD_DIM), lambda g, qi: (2, qi, g))],
        out_specs=pl.BlockSpec((tq, hpg * NSA_HEAD_DIM), lambda g, qi: (qi, g)),
        out_shape=jax.ShapeDtypeStruct((T, GROUP_WIDTH), BF16),
        compiler_params=_cparams(("parallel", "arbitrary")),
        name="nsa_window_attention",
    )(qn, kw, vw, _window_bias(tq, span), gx)


def _top2_route(h, w_router):
    logits = jnp.dot(h, w_router, preferred_element_type=F32, precision=lax.Precision.HIGHEST)
    lane = _lane(logits.shape)
    lf = lane.astype(F32)
    logits = jnp.where(lane < N_EXPERTS, logits, -jnp.inf)
    v0 = jnp.max(logits, axis=-1, keepdims=True)
    i0 = jnp.min(jnp.where(logits == v0, lf, float(LANES)), axis=-1, keepdims=True)
    rest = jnp.where(lf == i0, -jnp.inf, logits)
    v1 = jnp.max(rest, axis=-1, keepdims=True)
    i1 = jnp.min(jnp.where(rest == v1, lf, float(LANES)), axis=-1, keepdims=True)
    e1 = jnp.exp(v1 - v0)
    w0 = 1.0 / (1.0 + e1)
    w1 = e1 / (1.0 + e1)
    return jnp.where(lane == 0, i0, jnp.where(lane == 1, i1, jnp.where(lane == 2, w0, w1)))


def _out_proj_body(a_ref, b_ref, c_ref, d1_ref, d2_ref, d3_ref, w_ref, x_ref, g_ref, *rest, routed):
    wr_ref = rest[0] if routed else None
    o_ref, hn_ref = rest[1:3] if routed else rest[0:2]
    lhs_ref = rest[-1]
    lhs_ref[:, 0:GROUP_WIDTH] = a_ref[...]
    lhs_ref[:, GROUP_WIDTH:2 * GROUP_WIDTH] = b_ref[...]
    lhs_ref[:, 2 * GROUP_WIDTH:3 * GROUP_WIDTH] = c_ref[...]
    d = d1_ref[...].astype(F32) + d2_ref[...].astype(F32) + d3_ref[...].astype(F32)
    lhs_ref[:, 3 * GROUP_WIDTH:] = d.astype(BF16)
    y = x_ref[...] + _dot(lhs_ref[...], w_ref[0])
    o_ref[...] = y
    h = _rms(y, g_ref[...])
    hn_ref[...] = h.astype(BF16)
    if routed:
        rest[3][...] = _top2_route(h, wr_ref[...])


def out_proj(parts, w, layer, x, g_next, w_router, tm):
    T, N = x.shape
    part = pl.BlockSpec((tm, GROUP_WIDTH), lambda i: (i, 0))
    rows = pl.BlockSpec((tm, N), lambda i: (i, 0))
    in_specs = [part] * 6 + [pl.BlockSpec((1, 4 * GROUP_WIDTH, N), lambda i: (layer, 0, 0)), rows,
                             pl.BlockSpec((1, N), lambda i: (0, 0))]
    out_specs = [rows, rows]
    out_shape = [jax.ShapeDtypeStruct((T, N), F32), jax.ShapeDtypeStruct((T, N), BF16)]
    operands = [*parts, w, x, g_next]
    if w_router is not None:
        in_specs.append(pl.BlockSpec((N, LANES), lambda i: (0, 0)))
        out_specs.append(pl.BlockSpec((tm, LANES), lambda i: (i, 0)))
        out_shape.append(jax.ShapeDtypeStruct((T, LANES), F32))
        operands.append(w_router)
    return pl.pallas_call(
        functools.partial(_out_proj_body, routed=w_router is not None),
        grid=(T // tm,),
        in_specs=in_specs,
        out_specs=out_specs,
        out_shape=out_shape,
        scratch_shapes=[pltpu.VMEM((tm, 4 * GROUP_WIDTH), BF16)],
        compiler_params=_cparams(("parallel",)),
        name="out_proj",
    )(*operands)


def _group_starts(te_ref, i):
    return (i == 0) | (te_ref[i] != te_ref[jnp.maximum(i - 1, 0)])


def _stream_group_weights(te_ref, nx_ref, w_hbms, wbuf_ref, wb_refs, sem_ref, slot_ref):
    j = pl.program_id(0)
    i = pl.program_id(1)
    tn = wbuf_ref.shape[3]

    def copies(e, jj, slot):
        cols = pl.ds(pl.multiple_of(jj * tn, tn), tn)
        return [pltpu.make_async_copy(w.at[e, :, cols], wbuf_ref.at[slot, n], sem_ref.at[slot])
                for n, w in enumerate(w_hbms)]

    @pl.when((j == 0) & (i == 0))
    def _():
        slot_ref[0] = 0
        for c in copies(te_ref[0], 0, 0):
            c.start()

    @pl.when(_group_starts(te_ref, i))
    def _():
        slot = slot_ref[0]
        for c in copies(te_ref[i], j, slot):
            c.wait()
        for n, wb_ref in enumerate(wb_refs):
            wb_ref[...] = wbuf_ref[slot, n].astype(BF16)
        in_sweep = nx_ref[i] >= 0

        @pl.when(in_sweep | (j + 1 < pl.num_programs(0)))
        def _():
            for c in copies(jnp.where(in_sweep, nx_ref[i], te_ref[0]), jnp.where(in_sweep, j, j + 1), 1 - slot):
                c.start()

        slot_ref[0] = 1 - slot


def _weight_stream_scratch(n_weights, k, tn):
    return [pltpu.VMEM((2, n_weights, k, tn), F32)] + [pltpu.VMEM((k, tn), BF16)] * n_weights + [
        pltpu.SemaphoreType.DMA((2,)), pltpu.SMEM((1,), jnp.int32)]


def _by_fill(nv_ref, o_ref, compute):
    nv = nv_ref[pl.program_id(1)]
    half = o_ref.shape[0] // 2

    @pl.when(nv > half)
    def _():
        o_ref[...] = compute(slice(None))

    @pl.when((nv > 0) & (nv <= half))
    def _():
        o_ref[0:half, :] = compute(slice(0, half))
        o_ref[half:, :] = jnp.zeros((half, o_ref.shape[1]), o_ref.dtype)

    @pl.when(nv == 0)
    def _():
        o_ref[...] = jnp.zeros(o_ref.shape, o_ref.dtype)


def _up_body(te_ref, nx_ref, nv_ref, na_ref, h_ref, wg_hbm, wu_hbm, o_ref, wbuf_ref, wgb_ref, wub_ref, sem_ref,
             slot_ref):
    _stream_group_weights(te_ref, nx_ref, (wg_hbm, wu_hbm), wbuf_ref, (wgb_ref, wub_ref), sem_ref, slot_ref)

    def swiglu(rows):
        h = h_ref[rows, :]
        a = _dot(h, wgb_ref[...])
        return (a * jax.nn.sigmoid(a) * _dot(h, wub_ref[...])).astype(BF16)

    _by_fill(nv_ref, o_ref, swiglu)


def swiglu_up(tile_expert, next_expert, valid_rows, n_active, hn, wg, wu, tm, tn):
    P, K = hn.shape
    F = wg.shape[2]
    return pl.pallas_call(
        _up_body,
        grid_spec=pltpu.PrefetchScalarGridSpec(
            num_scalar_prefetch=4,
            grid=(F // tn, P // tm),
            in_specs=[pl.BlockSpec((tm, K), lambda j, i, te, nx, nv, na: (jnp.minimum(i, na[0] - 1), 0)),
                      pl.BlockSpec(memory_space=pl.ANY), pl.BlockSpec(memory_space=pl.ANY)],
            out_specs=pl.BlockSpec((tm, tn), lambda j, i, te, nx, nv, na: (i, j)),
            scratch_shapes=_weight_stream_scratch(2, K, tn)),
        out_shape=jax.ShapeDtypeStruct((P, F), BF16),
        compiler_params=_cparams(("arbitrary", "arbitrary")),
        name="swiglu_up",
    )(tile_expert, next_expert, valid_rows, n_active, hn, wg, wu)


def _down_body(te_ref, nx_ref, nv_ref, na_ref, a_ref, w_hbm, *rest, residual):
    r_ref = rest[0] if residual else None
    o_ref, wbuf_ref, wb_ref, sem_ref, slot_ref = rest[1:] if residual else rest
    _stream_group_weights(te_ref, nx_ref, (w_hbm,), wbuf_ref, (wb_ref,), sem_ref, slot_ref)

    def project(rows):
        y = _dot(a_ref[rows, :], wb_ref[...])
        return r_ref[rows, :] + y if residual else y

    _by_fill(nv_ref, o_ref, project)


def swiglu_down(tile_expert, next_expert, valid_rows, n_active, act, wd, residual, tm, tn):
    P, F = act.shape
    N = wd.shape[2]
    tile = pl.BlockSpec((tm, tn), lambda j, i, te, nx, nv, na: (i, j))
    in_specs = [pl.BlockSpec((tm, F), lambda j, i, te, nx, nv, na: (jnp.minimum(i, na[0] - 1), 0)),
                pl.BlockSpec(memory_space=pl.ANY)]
    operands = [act, wd]
    if residual is not None:
        in_specs.append(tile)
        operands.append(residual)
    return pl.pallas_call(
        functools.partial(_down_body, residual=residual is not None),
        grid_spec=pltpu.PrefetchScalarGridSpec(
            num_scalar_prefetch=4,
            grid=(N // tn, P // tm),
            in_specs=in_specs,
            out_specs=tile,
            scratch_shapes=_weight_stream_scratch(1, F, tn)),
        out_shape=jax.ShapeDtypeStruct((P, N), F32),
        compiler_params=_cparams(("arbitrary", "arbitrary")),
        name="swiglu_down",
    )(tile_expert, next_expert, valid_rows, n_active, *operands)


ROW_DMA_UNROLL = 8


def _row_copy(src_hbm, row, dst_ref, r, sem):
    return pltpu.make_async_copy(src_hbm.at[pl.ds(row, 1)], dst_ref.at[pl.ds(r, 1)], sem)


def _gather_body(tok_ref, na_ref, x_hbm, g_ref, o_ref, buf_ref, sem_ref):
    i = pl.program_id(0)
    tm = buf_ref.shape[1]

    def fetch(tile):
        slot = tile % 2

        def start(r, c):
            _row_copy(x_hbm, tok_ref[tile * tm + r], buf_ref.at[slot], r, sem_ref.at[slot]).start()
            return c

        lax.fori_loop(0, tm, start, 0, unroll=ROW_DMA_UNROLL)

    @pl.when(i == 0)
    def _():
        fetch(0)

    @pl.when(i + 1 < na_ref[0])
    def _():
        fetch(i + 1)

    @pl.when(i < na_ref[0])
    def _():
        slot = i % 2

        def wait(r, c):
            _row_copy(x_hbm, 0, buf_ref.at[slot], r, sem_ref.at[slot]).wait()
            return c

        lax.fori_loop(0, tm, wait, 0, unroll=ROW_DMA_UNROLL)
        o_ref[...] = _rms(buf_ref[slot], g_ref[...]).astype(BF16)

    @pl.when(i >= na_ref[0])
    def _():
        o_ref[...] = jnp.zeros(o_ref.shape, BF16)


def gather_norm_tokens(tok_of_slot, n_active, x, g, tm):
    P = tok_of_slot.shape[0]
    K = x.shape[1]
    return pl.pallas_call(
        _gather_body,
        grid_spec=pltpu.PrefetchScalarGridSpec(
            num_scalar_prefetch=2,
            grid=(P // tm,),
            in_specs=[pl.BlockSpec(memory_space=pl.ANY), pl.BlockSpec((1, K), lambda i, tok, na: (0, 0))],
            out_specs=pl.BlockSpec((tm, K), lambda i, tok, na: (i, 0)),
            scratch_shapes=[pltpu.VMEM((2, tm, K), F32), pltpu.SemaphoreType.DMA((2,))]),
        out_shape=jax.ShapeDtypeStruct((P, K), BF16),
        compiler_params=_cparams(("arbitrary",)),
        name="moe_gather",
    )(tok_of_slot, n_active, x, g)


def _combine_body(s0_ref, s1_ref, x_ref, y_hbm, rt_ref, g_ref, o_ref, b0_ref, b1_ref, sem0, sem1, *, final):
    tm = x_ref.shape[0]
    base = pl.program_id(0) * tm

    def start(r, c):
        _row_copy(y_hbm, s0_ref[base + r], b0_ref, r, sem0).start()
        _row_copy(y_hbm, s1_ref[base + r], b1_ref, r, sem1).start()
        return c

    def wait(r, c):
        _row_copy(y_hbm, 0, b0_ref, r, sem0).wait()
        _row_copy(y_hbm, 0, b1_ref, r, sem1).wait()
        return c

    lax.fori_loop(0, tm, start, 0, unroll=ROW_DMA_UNROLL)
    lax.fori_loop(0, tm, wait, 0, unroll=ROW_DMA_UNROLL)
    gates = rt_ref[...]
    y = x_ref[...] + gates[:, 2:3] * b0_ref[...] + gates[:, 3:4] * b1_ref[...]
    o_ref[...] = _rms(y, g_ref[...]) if final else y


def moe_combine(slot0, slot1, x, ys, route, g, tm, final):
    T, K = x.shape
    return pl.pallas_call(
        functools.partial(_combine_body, final=final),
        grid_spec=pltpu.PrefetchScalarGridSpec(
            num_scalar_prefetch=2,
            grid=(T // tm,),
            in_specs=[pl.BlockSpec((tm, K), lambda i, s0, s1: (i, 0)),
                      pl.BlockSpec(memory_space=pl.ANY),
                      pl.BlockSpec((tm, LANES), lambda i, s0, s1: (i, 0)),
                      pl.BlockSpec((1, K), lambda i, s0, s1: (0, 0))],
            out_specs=pl.BlockSpec((tm, K), lambda i, s0, s1: (i, 0)),
            scratch_shapes=[pltpu.VMEM((tm, K), F32), pltpu.VMEM((tm, K), F32),
                            pltpu.SemaphoreType.DMA(()), pltpu.SemaphoreType.DMA(())]),
        out_shape=jax.ShapeDtypeStruct((T, K), F32),
        compiler_params=_cparams(("arbitrary",)),
        name="moe_combine",
    )(slot0, slot1, x, ys, route, g)


def _final_norm_body(x_ref, g_ref, o_ref):
    o_ref[...] = _rms(x_ref[...], g_ref[...])


def final_norm(x, g, tm):
    T, K = x.shape
    return pl.pallas_call(
        _final_norm_body,
        grid=(T // tm,),
        in_specs=[pl.BlockSpec((tm, K), lambda i: (i, 0)), pl.BlockSpec((1, K), lambda i: (0, 0))],
        out_specs=pl.BlockSpec((tm, K), lambda i: (i, 0)),
        out_shape=jax.ShapeDtypeStruct((T, K), F32),
        compiler_params=_cparams(("parallel",)),
        name="final_norm",
    )(x, g)


def _routing_tables(route, tm, expert_base):
    T = route.shape[0]
    top_i = route[:, 0:2].astype(jnp.int32)
    e_flat = top_i.reshape(-1)
    onehot = (e_flat[:, None] == jnp.arange(N_EXPERTS)[None, :]).astype(jnp.int32)
    rank = jnp.take_along_axis(jnp.cumsum(onehot, axis=0) - onehot, e_flat[:, None], axis=1)[:, 0]
    count = jnp.sum(onehot, axis=0)
    padded = ((count + tm - 1) // tm) * tm
    end = jnp.cumsum(padded)
    start = end - padded
    slot = start[e_flat] + rank
    n_slots = 2 * T + N_EXPERTS * tm
    n_tiles = n_slots // tm
    n_active = (end[-1] // tm).astype(jnp.int32)
    tile_start = jnp.minimum(jnp.arange(n_tiles, dtype=jnp.int32), n_active - 1) * tm
    tile_expert = jnp.minimum(jnp.sum(tile_start[:, None] >= end[None, :], axis=1), N_EXPERTS - 1)
    group_end = end[tile_expert] // tm
    next_expert = jnp.where(group_end < n_active, tile_expert[jnp.minimum(group_end, n_tiles - 1)] + expert_base, -1)
    tile_index = jnp.arange(n_tiles, dtype=jnp.int32)
    valid_rows = jnp.where(tile_index < n_active,
                           jnp.clip(start[tile_expert] + count[tile_expert] - tile_index * tm, 0, tm), 0)
    tok_of_slot = jnp.zeros((n_slots,), jnp.int32).at[slot].set(jnp.arange(2 * T, dtype=jnp.int32) // 2)
    slot2 = slot.reshape(T, 2).astype(jnp.int32)
    return ((tile_expert + expert_base).astype(jnp.int32), next_expert.astype(jnp.int32),
            valid_rows.astype(jnp.int32), n_active.reshape(1), tok_of_slot, slot2[:, 0], slot2[:, 1])


def _rope_tables128(T):
    half = NSA_HEAD_DIM // 2
    inv = ROPE_THETA ** (-jnp.arange(0, NSA_HEAD_DIM, 2, dtype=F32) / NSA_HEAD_DIM)
    ang = jnp.arange(T, dtype=F32)[:, None] * inv[None, :]
    cos, sin = jnp.cos(ang), jnp.sin(ang)
    cos128 = jnp.tile(cos, (1, LANES // half))
    sin128 = jnp.tile(jnp.concatenate([-sin, sin], axis=1), (1, LANES // NSA_HEAD_DIM))
    return cos128, sin128


def _overlap_matrix_t(n_cmp_pad):
    sstart = jnp.arange(LANES) * NSA_SLC_LEN
    cstart = jnp.arange(n_cmp_pad) * NSA_CMP_STRIDE
    ov = (cstart[None, :] < sstart[:, None] + NSA_SLC_LEN) & (cstart[None, :] + NSA_CMP_LEN > sstart[:, None])
    return ov.astype(BF16)


def kernel(x, attn_norm, w_in, w_out, gm_ln_g, gm_ln_b, gm_ws, gm_bs, da_lambda, da_subln, cv_dw_w, cv_dw_b,
           cv_ln_g, cv_ln_b, nsa_cmp_w1, nsa_cmp_w2, nsa_cmp_pe, ffn_norm, ffn_wg, ffn_wu, ffn_wd, router_w,
           exp_wg, exp_wu, exp_wd, final_norm_g):
    B, T, D = x.shape
    assert B == 1 and D == D_MODEL and T % 1024 == 0 and T // NSA_SLC_LEN <= LANES
    depth = w_in.shape[0]
    G = NSA_KV_HEADS
    n16 = T // NSA_CMP_STRIDE
    tm = 512

    cos128, sin128 = _rope_tables128(T)
    overlap_t = _overlap_matrix_t(n16)
    dense_na = jnp.full((1,), T // tm, jnp.int32)
    row = lambda v: v.reshape(1, -1)
    merge = lambda w: w.reshape((-1,) + w.shape[2:])
    exp_wg, exp_wu, exp_wd = merge(exp_wg), merge(exp_wu), merge(exp_wd)
    w_in_b = jnp.pad(w_in, ((0, 0), (0, 0), (0, IN_WIDTH_PAD - IN_WIDTH))).astype(BF16)
    w_out_b = w_out.astype(BF16)

    xs = x[0]
    for l in range(depth):
        lambda_init = 0.8 - 0.6 * math.exp(-0.3 * l)
        z = norm_mm(xs, row(attn_norm[l]), w_in_b, l, 1024, IN_WIDTH_PAD // 3)

        bs_rows = jnp.repeat(gm_bs[l].T, GM_CHUNK, axis=1)
        o_a = gmlp(z, row(gm_ln_g[l]), row(gm_ln_b[l]), gm_ws[l], bs_rows, 512)
        dw_w = jnp.pad(cv_dw_w[l], ((0, CV_HALO - CV_WIDTH), (0, 0)))
        o_c = conformer_conv(z, dw_w, row(cv_dw_b[l]), row(cv_ln_g[l]), row(cv_ln_b[l]), 256)

        qd, kd, vd, qn, kc, vc, ks, vs, kw, vw, gx = prep(z, cos128, sin128, 256)
        o_b = diff_attention(qd, kd, vd, da_lambda[l], row(da_subln[l]), lambda_init, 512, 512)

        x16 = jnp.stack([kc, vc]).reshape(2, T, G, NSA_HEAD_DIM).transpose(0, 2, 1, 3)
        x16 = x16.reshape(2, G, n16, NSA_CMP_STRIDE * NSA_HEAD_DIM)
        pe2 = nsa_cmp_pe[l].reshape(2, 2, NSA_CMP_STRIDE * NSA_HEAD_DIM)
        w2p = jnp.pad(nsa_cmp_w2[l], ((0, 0), (0, 0), (0, LANES - NSA_HEAD_DIM))).astype(BF16)
        kvc = compress(x16, pe2, nsa_cmp_w1[l].astype(BF16), w2p)
        o_cmp, selbias = cmp_select(qn, kvc, overlap_t, gx, 256)
        o_sel = sel_attention(qn, selbias, ks, vs, gx, 256, 512)
        o_win = win_attention(qn, kw, vw, gx, 256)

        g_ffn = row(ffn_norm[l])
        parts = (o_a, o_b, o_c, o_cmp, o_sel, o_win)
        e = l // 2
        if l % 2 == 0:
            xs, hn = out_proj(parts, w_out_b, l, xs, g_ffn, None, 256)
            dense_te = jnp.full((T // tm,), e, jnp.int32)
            dense_nx = jnp.full((T // tm,), -1, jnp.int32)
            up_tm = 2 * tm
            act = swiglu_up(dense_te[:T // up_tm], dense_nx[:T // up_tm], jnp.full((T // up_tm,), up_tm, jnp.int32),
                            dense_na // 2, hn, ffn_wg, ffn_wu, up_tm, 512)
            xs = swiglu_down(dense_te, dense_nx, jnp.full((T // tm,), tm, jnp.int32), dense_na, act, ffn_wd, xs,
                             tm, 512)
            if l == depth - 1:
                xs = final_norm(xs, row(final_norm_g), tm)
        else:
            w_r = jnp.pad(router_w[e], ((0, 0), (0, LANES - N_EXPERTS)))
            xs, _, route = out_proj(parts, w_out_b, l, xs, g_ffn, w_r, 256)
            tile_expert, next_expert, valid_rows, n_active, tok_of_slot, slot0, slot1 = _routing_tables(
                route, tm, e * N_EXPERTS)
            hg = gather_norm_tokens(tok_of_slot, n_active, xs, g_ffn, tm)
            act = swiglu_up(tile_expert, next_expert, valid_rows, n_active, hg, exp_wg, exp_wu, tm, 512)
            ys = swiglu_down(tile_expert, next_expert, valid_rows, n_active, act, exp_wd, None, tm, 512)
            xs = moe_combine(slot0, slot1, xs, ys, route, row(final_norm_g), 256, final=(l == depth - 1))
    return xs[None]
```

```python
import functools
import math

import jax
import jax.numpy as jnp
from jax import lax
from jax.experimental import pallas as pl
from jax.experimental.pallas import tpu as pltpu

F32 = jnp.float32
BF16 = jnp.bfloat16

D_MODEL = 2048
GROUP_WIDTH = 512
GM_CHUNK = 128
GM_HEADS = 4
DA_HEADS = 4
DA_QK_DIM = 64
CV_WIDTH = 31
NSA_HEADS = 8
NSA_KV_HEADS = 2
NSA_HEAD_DIM = 64
NSA_CMP_LEN = 32
NSA_CMP_STRIDE = 16
NSA_SLC_LEN = 64
NSA_SLC_TOPK = 16
NSA_WINDOW = 512
ROPE_THETA = 10000.0
NORM_EPS = 1e-6
NEG = -1e30
NSA_FORCED = 3
N_EXPERTS = 8
LANES = 128
SUBLANES = 8
LOG2E = math.log2(math.e)

IN_WIDTH = 4888
IN_WIDTH_PAD = 4992
COL_GM = 0
COL_QDA = 1024
COL_KDA = 1536
COL_VDA = 2048
COL_CV = 2560
COL_QNS = 3584
COL_KVNS = 4096
COL_GNS = 4864

VMEM_LIMIT = 56 * 1024 * 1024


def _cparams(sem, **kw):
    return pltpu.CompilerParams(dimension_semantics=sem, vmem_limit_bytes=VMEM_LIMIT, **kw)


def _rms(x, g):
    ms = jnp.mean(x * x, axis=-1, keepdims=True)
    return x * lax.rsqrt(ms + NORM_EPS) * g


def _layer_norm(x, g, b):
    mu = jnp.mean(x, axis=-1, keepdims=True)
    xc = x - mu
    var = jnp.mean(xc * xc, axis=-1, keepdims=True)
    return xc * lax.rsqrt(var + NORM_EPS) * g + b


def _dot(a, b):
    return jnp.dot(a, b, preferred_element_type=F32)


def _dot_nt(a, b):
    return lax.dot_general(a, b, (((1,), (1,)), ((), ())), preferred_element_type=F32)


def _lane(shape):
    return lax.broadcasted_iota(jnp.int32, shape, len(shape) - 1)


def _row(shape):
    return lax.broadcasted_iota(jnp.int32, shape, len(shape) - 2)


def _norm_mm_body(x_ref, g_ref, w_ref, o_ref, hn_ref):
    @pl.when(pl.program_id(1) == 0)
    def _():
        hn_ref[...] = _rms(x_ref[...], g_ref[...]).astype(BF16)

    o_ref[...] = _dot(hn_ref[...], w_ref[0]).astype(o_ref.dtype)


def norm_mm(x, g, w, layer, tm, tn):
    T, K = x.shape
    N = w.shape[2]
    return pl.pallas_call(
        _norm_mm_body,
        grid=(T // tm, N // tn),
        in_specs=[pl.BlockSpec((tm, K), lambda i, j: (i, 0)),
                  pl.BlockSpec((1, K), lambda i, j: (0, 0)),
                  pl.BlockSpec((1, K, tn), lambda i, j: (layer, 0, j))],
        out_specs=pl.BlockSpec((tm, tn), lambda i, j: (i, j)),
        out_shape=jax.ShapeDtypeStruct((T, N), BF16),
        scratch_shapes=[pltpu.VMEM((tm, K), BF16)],
        compiler_params=_cparams(("parallel", "arbitrary")),
        name="norm_in_proj",
    )(x, g, w)


def _gmlp_body(z_ref, g_ref, b_ref, ws_ref, bs_ref, o_ref):
    tr = z_ref.shape[0]
    z = jax.nn.gelu(z_ref[...].astype(F32))
    u = z[:, :GROUP_WIDTH]
    v = _layer_norm(z[:, GROUP_WIDTH:], g_ref[...], b_ref[...]).astype(BF16)
    causal = _row((GM_CHUNK, GM_CHUNK)) >= _lane((GM_CHUNK, GM_CHUNK))
    bias = bs_ref[...]
    for h in range(GM_HEADS):
        w = jnp.where(causal, ws_ref[h], 0.0).astype(BF16)
        cols = slice(h * LANES, (h + 1) * LANES)
        for c in range(tr // GM_CHUNK):
            rows = slice(c * GM_CHUNK, (c + 1) * GM_CHUNK)
            s = _dot(w, v[rows, cols]) + bias[:, cols]
            o_ref[rows, cols] = (u[rows, cols] * s).astype(BF16)


def gmlp(z, ln_g, ln_b, ws, bs_rows, tr):
    T = z.shape[0]
    return pl.pallas_call(
        _gmlp_body,
        grid=(T // tr,),
        in_specs=[pl.BlockSpec((tr, 2 * GROUP_WIDTH), lambda i: (i, COL_GM // (2 * GROUP_WIDTH))),
                  pl.BlockSpec((1, GROUP_WIDTH), lambda i: (0, 0)),
                  pl.BlockSpec((1, GROUP_WIDTH), lambda i: (0, 0)),
                  pl.BlockSpec((GM_HEADS, GM_CHUNK, GM_CHUNK), lambda i: (0, 0, 0)),
                  pl.BlockSpec((GM_CHUNK, GROUP_WIDTH), lambda i: (0, 0))],
        out_specs=pl.BlockSpec((tr, GROUP_WIDTH), lambda i: (i, 0)),
        out_shape=jax.ShapeDtypeStruct((T, GROUP_WIDTH), BF16),
        compiler_params=_cparams(("parallel",)),
        name="gmlp",
    )(z, ln_g, ln_b, ws, bs_rows)


CV_HALO = 32
CV_SUB = 64


def _conv_body(a_ref, g_ref, ap_ref, gp_ref, w_ref, b_ref, lg_ref, lb_ref, o_ref, hs_ref):
    tr = a_ref.shape[0]
    first = pl.program_id(0) == 0
    prev = ap_ref[...].astype(F32) * jax.nn.sigmoid(gp_ref[...].astype(F32))
    hs_ref[0:CV_HALO, :] = jnp.where(first, 0.0, prev)
    hs_ref[CV_HALO:CV_HALO + tr, :] = a_ref[...].astype(F32) * jax.nn.sigmoid(g_ref[...].astype(F32))
    hs_ref[CV_HALO + tr:, :] = jnp.zeros((SUBLANES, GROUP_WIDTH), F32)
    w = w_ref[...]
    lead = CV_HALO - (CV_WIDTH - 1)
    for r0 in range(0, tr, CV_SUB):
        acc = jnp.zeros((CV_SUB, GROUP_WIDTH), F32)
        for phase in range(SUBLANES):
            base, shift = divmod(lead + phase, SUBLANES)
            part = jnp.zeros((CV_SUB + SUBLANES, GROUP_WIDTH), F32)
            for k in range(phase, CV_WIDTH, SUBLANES):
                start = r0 + SUBLANES * (base + k // SUBLANES)
                part = part + hs_ref[start:start + CV_SUB + SUBLANES, :] * w[k:k + 1, :]
            acc = acc + part[shift:shift + CV_SUB]
        y = _layer_norm(acc + b_ref[...], lg_ref[...], lb_ref[...])
        o_ref[r0:r0 + CV_SUB, :] = (y * jax.nn.sigmoid(y)).astype(BF16)


def conformer_conv(z, dw_w, dw_b, ln_g, ln_b, tr):
    T = z.shape[0]
    ca = COL_CV // GROUP_WIDTH
    per = tr // CV_HALO

    def halo(col):
        return lambda i: (jnp.maximum(i * per - 1, 0), col)

    return pl.pallas_call(
        _conv_body,
        grid=(T // tr,),
        in_specs=[pl.BlockSpec((tr, GROUP_WIDTH), lambda i: (i, ca)),
                  pl.BlockSpec((tr, GROUP_WIDTH), lambda i: (i, ca + 1)),
                  pl.BlockSpec((CV_HALO, GROUP_WIDTH), halo(ca)),
                  pl.BlockSpec((CV_HALO, GROUP_WIDTH), halo(ca + 1)),
                  pl.BlockSpec((CV_HALO, GROUP_WIDTH), lambda i: (0, 0)),
                  pl.BlockSpec((1, GROUP_WIDTH), lambda i: (0, 0)),
                  pl.BlockSpec((1, GROUP_WIDTH), lambda i: (0, 0)),
                  pl.BlockSpec((1, GROUP_WIDTH), lambda i: (0, 0))],
        out_specs=pl.BlockSpec((tr, GROUP_WIDTH), lambda i: (i, 0)),
        out_shape=jax.ShapeDtypeStruct((T, GROUP_WIDTH), BF16),
        scratch_shapes=[pltpu.VMEM((CV_HALO + tr + SUBLANES, GROUP_WIDTH), F32)],
        compiler_params=_cparams(("parallel",)),
        name="conformer_conv",
    )(z, z, z, z, dw_w, dw_b, ln_g, ln_b)


def _rope128(x, cos, sin_signed):
    x = x.astype(F32)
    lo = (_lane(x.shape) & 63) < 32
    rot = jnp.where(lo, pltpu.roll(x, 96, 1), pltpu.roll(x, 32, 1))
    return x * cos + rot * sin_signed


def _low_half(x, fill=0.0):
    return jnp.where(_lane(x.shape) < 64, x, fill)


def _high_half_to_low(x, fill=0.0):
    return jnp.where(_lane(x.shape) < 64, pltpu.roll(x, 64, 1), fill)


def _prep_body(qd_ref, kd_ref, vd_ref, qn_ref, kvc_ref, kvs_ref, kvw_ref, gt_ref, cos_ref, sin_ref, ge_ref,
               qd_o, kd_o, vd_o, qn_o, kc_o, vc_o, ks_o, vs_o, kw_o, vw_o, gx_o):
    tr = cos_ref.shape[0]
    cos = cos_ref[...]
    sin = sin_ref[...]
    q_scale = DA_QK_DIM ** -0.5 * LOG2E
    ones = jnp.ones((tr, LANES), BF16)
    for h in range(DA_HEADS):
        cols = slice(h * LANES, (h + 1) * LANES)
        qd_o[h] = (_rope128(qd_ref[:, cols], cos, sin) * q_scale).astype(BF16)
        kd_o[h] = _rope128(kd_ref[:, cols], cos, sin).astype(BF16)
        vd_o[h, :, 0:LANES] = vd_ref[:, cols].astype(BF16)
        vd_o[h, :, LANES:] = ones
    for c in range(NSA_HEADS // 2):
        q = _rope128(qn_ref[:, c * LANES:(c + 1) * LANES], cos, sin) * (NSA_HEAD_DIM ** -0.5 * LOG2E)
        qn_o[2 * c] = _low_half(q).astype(BF16)
        qn_o[2 * c + 1] = _high_half_to_low(q).astype(BF16)
    kc_o[...] = _rope128(kvc_ref[:, :LANES], cos, sin)
    vc_o[...] = kvc_ref[:, LANES:].astype(F32)
    blk = (pl.program_id(0) * tr + _row((tr, LANES))) >> 6
    onehot = jnp.where(blk == _lane((tr, LANES)), 1.0, 0.0).astype(BF16)
    k = _rope128(kvs_ref[:, :LANES], cos, sin)
    for g, half in enumerate((_low_half, _high_half_to_low)):
        ks_o[g, :, 0:LANES] = onehot
        ks_o[g, :, LANES:] = half(k).astype(BF16)
        vs_o[g] = half(kvs_ref[:, LANES:].astype(F32), 1.0).astype(BF16)
    k = _rope128(kvw_ref[:, :LANES], cos, sin)
    for g, half in enumerate((_low_half, _high_half_to_low)):
        kw_o[g] = half(k).astype(BF16)
        vw_o[g] = half(kvw_ref[:, LANES:].astype(F32), 1.0).astype(BF16)
    expand = ge_ref[...]
    gx = sum(_dot(part, expand) for part in _split3(jax.nn.sigmoid(gt_ref[...].astype(F32))))
    for c in range(3):
        gx_o[c] = gx[:, c * GROUP_WIDTH:(c + 1) * GROUP_WIDTH]


def _gate_expansion():
    col = jnp.arange(3 * GROUP_WIDTH)
    src = (col // GROUP_WIDTH) * NSA_HEADS + (col % GROUP_WIDTH) // NSA_HEAD_DIM
    return (jnp.arange(LANES)[:, None] == src[None, :]).astype(BF16)


def prep(z, cos128, sin128, tr):
    T = z.shape[0]
    G = NSA_KV_HEADS

    def zspec(width, col):
        return pl.BlockSpec((tr, width), lambda i: (i, col // width))

    def heads(n, width=LANES):
        return pl.BlockSpec((n, tr, width), lambda i: (0, i, 0))

    def hshape(n, width=LANES):
        return jax.ShapeDtypeStruct((n, T, width), BF16)

    row128 = pl.BlockSpec((tr, LANES), lambda i: (i, 0))
    return pl.pallas_call(
        _prep_body,
        grid=(T // tr,),
        in_specs=[zspec(512, COL_QDA), zspec(512, COL_KDA), zspec(512, COL_VDA), zspec(512, COL_QNS),
                  zspec(256, COL_KVNS), zspec(256, COL_KVNS + 256), zspec(256, COL_KVNS + 512),
                  zspec(128, COL_GNS), row128, row128,
                  pl.BlockSpec((LANES, 3 * GROUP_WIDTH), lambda i: (0, 0))],
        out_specs=[heads(DA_HEADS), heads(DA_HEADS), heads(DA_HEADS, 2 * LANES),
                   heads(NSA_HEADS), row128, row128, heads(G, 2 * LANES), heads(G), heads(G), heads(G),
                   pl.BlockSpec((3, tr, GROUP_WIDTH), lambda i: (0, i, 0))],
        out_shape=[hshape(DA_HEADS), hshape(DA_HEADS), hshape(DA_HEADS, 2 * LANES),
                   hshape(NSA_HEADS), jax.ShapeDtypeStruct((T, LANES), F32),
                   jax.ShapeDtypeStruct((T, LANES), F32), hshape(G, 2 * LANES), hshape(G), hshape(G), hshape(G),
                   jax.ShapeDtypeStruct((3, T, GROUP_WIDTH), F32)],
        compiler_params=_cparams(("parallel",)),
        name="attention_prep",
    )(z, z, z, z, z, z, z, z, cos128, sin128, _gate_expansion())


def _lane_tile(x, n):
    return x if n == 1 else jnp.concatenate([x] * n, axis=-1)


def _softmax_update(s, v, m_ref, acc_ref):
    m_prev = m_ref[...]
    m_new = jnp.maximum(m_prev, jnp.max(s, axis=-1, keepdims=True))
    alpha = jnp.exp2(m_prev - m_new)
    p = jnp.exp2(s - _lane_tile(m_new, s.shape[1] // LANES))
    acc_ref[...] = _lane_tile(alpha, acc_ref.shape[1] // LANES) * acc_ref[...] + _dot(p.astype(BF16), v)
    m_ref[...] = m_new


SWEEP_UNROLL = 4


def _causal_sweep(qs_ref, k_ref, v_ref, m_ref, acc_ref, s_ref, q0, tq, tk):
    m_ref[...] = jnp.full(m_ref.shape, -jnp.inf, F32)
    acc_ref[...] = jnp.zeros(acc_ref.shape, F32)

    def keys(ref, t):
        return ref[0, pl.ds(pl.multiple_of(t * tk, tk), tk), :]

    def scores(t):
        return _dot_nt(qs_ref[...], keys(k_ref, t))

    n_full = q0 // tk
    s_ref[...] = scores(0)

    def run(t0, count):
        s_cur = s_ref[...]
        for u in range(count):
            s_next = scores(t0 + u + 1)
            _softmax_update(s_cur, keys(v_ref, t0 + u), m_ref, acc_ref)
            s_cur = s_next
        s_ref[...] = s_cur

    def several(i, c):
        run(SWEEP_UNROLL * i, SWEEP_UNROLL)
        return c

    def single(t, c):
        run(t, 1)
        return c

    n_groups = n_full // SWEEP_UNROLL
    lax.fori_loop(0, n_groups, several, 0)
    lax.fori_loop(n_groups * SWEEP_UNROLL, n_full, single, 0)
    s = s_ref[...]
    visible = n_full * tk + _lane(s.shape) <= q0 + (_row(s.shape) & (tq - 1))
    _softmax_update(jnp.where(visible, s, NEG), keys(v_ref, n_full), m_ref, acc_ref)


def _diff_attn_body(q_ref, k_ref, v_ref, lam_ref, sg_ref, o_ref, qs_ref, m_ref, acc_ref, s_ref, *,
                    tq, tk, lambda_init):
    q = q_ref[0]
    first = _lane(q.shape) < DA_QK_DIM
    qs_ref[0:tq, :] = jnp.where(first, q, jnp.zeros_like(q))
    qs_ref[tq:, :] = jnp.where(first, jnp.zeros_like(q), q)
    _causal_sweep(qs_ref, k_ref, v_ref, m_ref, acc_ref, s_ref, pl.program_id(1) * tq, tq, tk)
    lam = lam_ref[...]
    lam_full = (jnp.exp(jnp.sum(lam[0:1] * lam[1:2], axis=-1, keepdims=True))
                - jnp.exp(jnp.sum(lam[2:3] * lam[3:4], axis=-1, keepdims=True)) + lambda_init)
    o = acc_ref[:, 0:LANES] / acc_ref[:, LANES:]
    a = o[0:tq] - lam_full * o[tq:]
    o_ref[...] = (_rms(a, sg_ref[...]) * (1.0 - lambda_init)).astype(BF16)


def diff_attention(qd, kd, vd, lam, subln, lambda_init, tq, tk):
    H, T, _ = qd.shape
    return pl.pallas_call(
        functools.partial(_diff_attn_body, tq=tq, tk=tk, lambda_init=lambda_init),
        grid=(H, T // tq),
        in_specs=[pl.BlockSpec((1, tq, LANES), lambda h, qi: (h, qi, 0)),
                  pl.BlockSpec((1, T, LANES), lambda h, qi: (h, 0, 0)),
                  pl.BlockSpec((1, T, 2 * LANES), lambda h, qi: (h, 0, 0)),
                  pl.BlockSpec((4, DA_QK_DIM), lambda h, qi: (0, 0)),
                  pl.BlockSpec((1, LANES), lambda h, qi: (0, 0))],
        out_specs=pl.BlockSpec((tq, LANES), lambda h, qi: (qi, h)),
        out_shape=jax.ShapeDtypeStruct((T, GROUP_WIDTH), BF16),
        scratch_shapes=[pltpu.VMEM((2 * tq, LANES), BF16), pltpu.VMEM((2 * tq, LANES), F32),
                        pltpu.VMEM((2 * tq, 2 * LANES), F32), pltpu.VMEM((2 * tq, tk), F32)],
        compiler_params=_cparams(("parallel", "arbitrary")),
        name="diff_attention",
    )(qd, kd, vd, lam, subln)


def _compress_body(x_ref, pe_ref, w1_ref, w2_ref, o_ref):
    x = x_ref[0, 0]
    half = x.shape[1]
    a = _dot((x + pe_ref[0, 0:1, :]).astype(BF16), w1_ref[0, 0:half, :])
    b = _dot((x + pe_ref[0, 1:2, :]).astype(BF16), w1_ref[0, half:, :])
    hid = jax.nn.gelu(a + pltpu.roll(b, b.shape[0] - 1, 0))
    o_ref[0, 0] = _dot(hid.astype(BF16), w2_ref[0]).astype(BF16)


def compress(x16, pe2, w1, w2p):
    _, G, n, half = x16.shape
    hid = w1.shape[2]
    return pl.pallas_call(
        _compress_body,
        grid=(2, G),
        in_specs=[pl.BlockSpec((1, 1, n, half), lambda c, g: (c, g, 0, 0)),
                  pl.BlockSpec((1, 2, half), lambda c, g: (c, 0, 0)),
                  pl.BlockSpec((1, 2 * half, hid), lambda c, g: (c, 0, 0)),
                  pl.BlockSpec((1, hid, LANES), lambda c, g: (c, 0, 0))],
        out_specs=pl.BlockSpec((1, 1, n, LANES), lambda c, g: (c, g, 0, 0)),
        out_shape=jax.ShapeDtypeStruct((2, G, n, LANES), BF16),
        compiler_params=_cparams(("parallel", "parallel")),
        name="nsa_compress",
    )(x16, pe2, w1, w2p)


def _split3(x):
    hi = x.astype(BF16)
    r1 = x - hi.astype(F32)
    mid = r1.astype(BF16)
    lo = (r1 - mid.astype(F32)).astype(BF16)
    return hi, mid, lo


def _pack_heads(o, tq):
    pair = lambda a, b: a + pltpu.roll(b, 64, 1)
    return jnp.concatenate([pair(o[0:tq], o[tq:2 * tq]), pair(o[2 * tq:3 * tq], o[3 * tq:])], axis=-1)


def _normalize_low_half(acc):
    return jnp.where(_lane(acc.shape) < 64, acc / pltpu.roll(acc, 64, 1), 0.0)


def _cmp_select_body(q_ref, kc_ref, vc_ref, ovt_ref, gx_ref, o_ref, sb_ref, *, tq):
    qi = pl.program_id(1)
    hpg = q_ref.shape[0]
    q = q_ref[...].reshape(hpg * tq, LANES)
    s = _dot_nt(q, kc_ref[0, 0])
    t = qi * tq + (_row(s.shape) & (tq - 1))
    cmask = _lane(s.shape) * NSA_CMP_STRIDE + (NSA_CMP_LEN - 1) <= t
    s = jnp.where(cmask, s, NEG)
    e = jnp.exp2(s - jnp.max(s, axis=-1, keepdims=True))
    p = jnp.where(cmask, e / jnp.sum(e, axis=-1, keepdims=True), 0.0)
    o = _dot(p.astype(BF16), vc_ref[0, 0])
    o_ref[...] = (gx_ref[0] * _pack_heads(o, tq)).astype(BF16)

    psum = p[0:tq]
    for hh in range(1, hpg):
        psum = psum + p[hh * tq:(hh + 1) * tq]
    ovt = ovt_ref[...]
    imp = sum(_dot_nt(ovt, part) for part in _split3(psum))
    j = _row(imp.shape)
    cur = (qi * tq + _lane(imp.shape)) >> 6
    forced = (j == 0) | (j == cur) | (j == cur - 1)
    score = jnp.where(forced, -2.0, jnp.where(j <= cur, imp, -1.0))
    jf = j.astype(F32)
    bias = jnp.where(forced, 0.0, NEG)
    for _ in range(NSA_SLC_TOPK - NSA_FORCED):
        m = jnp.max(score, axis=0, keepdims=True)
        first = jnp.min(jnp.where(score == m, jf, float(LANES)), axis=0, keepdims=True)
        hit = jf == first
        bias = jnp.where(hit, jnp.where(m >= 0.0, 0.0, NEG), bias)
        score = jnp.where(hit, -2.0, score)
    sb_ref[0] = bias.T.astype(BF16)


def cmp_select(qn, kvc, overlap_t, gx, tq):
    H, T, _ = qn.shape
    G = NSA_KV_HEADS
    hpg = H // G
    n = kvc.shape[2]
    return pl.pallas_call(
        functools.partial(_cmp_select_body, tq=tq),
        grid=(G, T // tq),
        in_specs=[pl.BlockSpec((hpg, tq, LANES), lambda g, qi: (g, qi, 0)),
                  pl.BlockSpec((1, 1, n, LANES), lambda g, qi: (0, g, 0, 0)),
                  pl.BlockSpec((1, 1, n, LANES), lambda g, qi: (1, g, 0, 0)),
                  pl.BlockSpec((LANES, n), lambda g, qi: (0, 0)),
                  pl.BlockSpec((1, tq, hpg * NSA_HEAD_DIM), lambda g, qi: (0, qi, g))],
        out_specs=[pl.BlockSpec((tq, hpg * NSA_HEAD_DIM), lambda g, qi: (qi, g)),
                   pl.BlockSpec((1, tq, LANES), lambda g, qi: (g, qi, 0))],
        out_shape=[jax.ShapeDtypeStruct((T, GROUP_WIDTH), BF16),
                   jax.ShapeDtypeStruct((G, T, LANES), BF16)],
        compiler_params=_cparams(("parallel", "parallel")),
        name="nsa_compressed_select",
    )(qn, kvc, kvc, overlap_t, gx)


def _sel_attn_body(q_ref, sb_ref, k_ref, v_ref, gx_ref, o_ref, qs_ref, m_ref, acc_ref, s_ref, *, tq, tk):
    hpg = q_ref.shape[0]
    for hh in range(hpg):
        qs_ref[hh * tq:(hh + 1) * tq, 0:LANES] = sb_ref[0]
        qs_ref[hh * tq:(hh + 1) * tq, LANES:] = q_ref[hh]
    _causal_sweep(qs_ref, k_ref, v_ref, m_ref, acc_ref, s_ref, pl.program_id(1) * tq, tq, tk)
    o = _normalize_low_half(acc_ref[...])
    o_ref[...] = (gx_ref[0] * _pack_heads(o, tq)).astype(BF16)


def sel_attention(qn, selbias, ks, vs, gx, tq, tk):
    H, T, _ = qn.shape
    G = NSA_KV_HEADS
    hpg = H // G
    return pl.pallas_call(
        functools.partial(_sel_attn_body, tq=tq, tk=tk),
        grid=(G, T // tq),
        in_specs=[pl.BlockSpec((hpg, tq, LANES), lambda g, qi: (g, qi, 0)),
                  pl.BlockSpec((1, tq, LANES), lambda g, qi: (g, qi, 0)),
                  pl.BlockSpec((1, T, 2 * LANES), lambda g, qi: (g, 0, 0)),
                  pl.BlockSpec((1, T, LANES), lambda g, qi: (g, 0, 0)),
                  pl.BlockSpec((1, tq, hpg * NSA_HEAD_DIM), lambda g, qi: (1, qi, g))],
        out_specs=pl.BlockSpec((tq, hpg * NSA_HEAD_DIM), lambda g, qi: (qi, g)),
        out_shape=jax.ShapeDtypeStruct((T, GROUP_WIDTH), BF16),
        scratch_shapes=[pltpu.VMEM((hpg * tq, 2 * LANES), BF16), pltpu.VMEM((hpg * tq, LANES), F32),
                        pltpu.VMEM((hpg * tq, LANES), F32), pltpu.VMEM((hpg * tq, tk), F32)],
        compiler_params=_cparams(("parallel", "arbitrary")),
        name="nsa_selected_attention",
    )(qn, selbias, ks, vs, gx)


def _win_attn_body(q_ref, k_ref, v_ref, b_ref, gx_ref, o_ref, *, tq, span):
    hpg = q_ref.shape[0]
    q0 = pl.program_id(1) * tq
    lo = pl.multiple_of(jnp.maximum(q0 + tq - span, 0), tq)
    q = q_ref[...].reshape(hpg * tq, LANES)
    s = _dot_nt(q, k_ref[0, pl.ds(lo, span), :])

    def finish(s):
        p = jnp.exp2(s - jnp.max(s, axis=-1, keepdims=True))
        acc = _dot(p.astype(BF16), v_ref[0, pl.ds(lo, span), :])
        o_ref[...] = (gx_ref[0] * _pack_heads(_normalize_low_half(acc), tq)).astype(BF16)

    @pl.when(q0 + tq >= span)
    def _():
        finish(s + jnp.concatenate([b_ref[...]] * hpg, axis=0))

    @pl.when(q0 + tq < span)
    def _():
        qpos = q0 + (_row(s.shape) & (tq - 1))
        kpos = lo + _lane(s.shape)
        finish(jnp.where((kpos <= qpos) & (kpos > qpos - NSA_WINDOW), s, NEG))


def _window_bias(tq, span):
    d = jnp.arange(span)[None, :] - jnp.arange(tq)[:, None] + (tq - span)
    return jnp.where((d <= 0) & (d > -NSA_WINDOW), 0.0, NEG).astype(F32)


def win_attention(qn, kw, vw, gx, tq):
    H, T, _ = qn.shape
    G = NSA_KV_HEADS
    hpg = H // G
    span = NSA_WINDOW + tq
    return pl.pallas_call(
        functools.partial(_win_attn_body, tq=tq, span=span),
        grid=(G, T // tq),
        in_specs=[pl.BlockSpec((hpg, tq, LANES), lambda g, qi: (g, qi, 0)),
                  pl.BlockSpec((1, T, LANES), lambda g, qi: (g, 0, 0)),
                  pl.BlockSpec((1, T, LANES), lambda g, qi: (g, 0, 0)),
                  pl.BlockSpec((tq, span), lambda g, qi: (0, 0)),
                  pl.BlockSpec((1, tq, hpg * NSA_HEAD_DIM), lambda g, qi: (2, qi, g))],
        out_specs=pl.BlockSpec((tq, hpg * NSA_HEAD_DIM), lambda g, qi: (qi, g)),
        out_shape=jax.ShapeDtypeStruct((T, GROUP_WIDTH), BF16),
        compiler_params=_cparams(("parallel", "arbitrary")),
        name="nsa_window_attention",
    )(qn, kw, vw, _window_bias(tq, span), gx)


def _top2_route(h, w_router):
    logits = jnp.dot(h, w_router, preferred_element_type=F32, precision=lax.Precision.HIGHEST)
    lane = _lane(logits.shape)
    lf = lane.astype(F32)
    logits = jnp.where(lane < N_EXPERTS, logits, -jnp.inf)
    v0 = jnp.max(logits, axis=-1, keepdims=True)
    i0 = jnp.min(jnp.where(logits == v0, lf, float(LANES)), axis=-1, keepdims=True)
    rest = jnp.where(lf == i0, -jnp.inf, logits)
    v1 = jnp.max(rest, axis=-1, keepdims=True)
    i1 = jnp.min(jnp.where(rest == v1, lf, float(LANES)), axis=-1, keepdims=True)
    e1 = jnp.exp(v1 - v0)
    w0 = 1.0 / (1.0 + e1)
    w1 = e1 / (1.0 + e1)
    return jnp.where(lane == 0, i0, jnp.where(lane == 1, i1, jnp.where(lane == 2, w0, w1)))


def _out_proj_body(a_ref, b_ref, c_ref, d1_ref, d2_ref, d3_ref, w_ref, x_ref, *rest, normed):
    o_ref, lhs_ref = rest[-3 if normed else -2], rest[-1]
    lhs_ref[:, 0:GROUP_WIDTH] = a_ref[...]
    lhs_ref[:, GROUP_WIDTH:2 * GROUP_WIDTH] = b_ref[...]
    lhs_ref[:, 2 * GROUP_WIDTH:3 * GROUP_WIDTH] = c_ref[...]
    d = d1_ref[...].astype(F32) + d2_ref[...].astype(F32) + d3_ref[...].astype(F32)
    lhs_ref[:, 3 * GROUP_WIDTH:] = d.astype(BF16)
    y = x_ref[...] + _dot(lhs_ref[...], w_ref[0])
    o_ref[...] = y
    if normed:
        g_ref, hn_ref = rest[0], rest[2]
        hn_ref[...] = _rms(y, g_ref[...]).astype(BF16)


def out_proj(parts, w, layer, x, g_next, tm):
    T, N = x.shape
    part = pl.BlockSpec((tm, GROUP_WIDTH), lambda i: (i, 0))
    rows = pl.BlockSpec((tm, N), lambda i: (i, 0))
    in_specs = [part] * 6 + [pl.BlockSpec((1, 4 * GROUP_WIDTH, N), lambda i: (layer, 0, 0)), rows]
    out_specs = [rows]
    out_shape = [jax.ShapeDtypeStruct((T, N), F32)]
    operands = [*parts, w, x]
    if g_next is not None:
        in_specs.append(pl.BlockSpec((1, N), lambda i: (0, 0)))
        out_specs.append(rows)
        out_shape.append(jax.ShapeDtypeStruct((T, N), BF16))
        operands.append(g_next)
    return pl.pallas_call(
        functools.partial(_out_proj_body, normed=g_next is not None),
        grid=(T // tm,),
        in_specs=in_specs,
        out_specs=out_specs,
        out_shape=out_shape,
        scratch_shapes=[pltpu.VMEM((tm, 4 * GROUP_WIDTH), BF16)],
        compiler_params=_cparams(("parallel",)),
        name="out_proj",
    )(*operands)


def _group_starts(te_ref, i):
    return (i == 0) | (te_ref[i] != te_ref[jnp.maximum(i - 1, 0)])


def _stream_group_weights(te_ref, nx_ref, w_hbms, wbuf_ref, wb_refs, sem_ref, slot_ref):
    j = pl.program_id(0)
    i = pl.program_id(1)
    tn = wbuf_ref.shape[3]

    def copies(e, jj, slot):
        cols = pl.ds(pl.multiple_of(jj * tn, tn), tn)
        return [pltpu.make_async_copy(w.at[e, :, cols], wbuf_ref.at[slot, n], sem_ref.at[slot])
                for n, w in enumerate(w_hbms)]

    @pl.when((j == 0) & (i == 0))
    def _():
        slot_ref[0] = 0
        for c in copies(te_ref[0], 0, 0):
            c.start()

    @pl.when(_group_starts(te_ref, i))
    def _():
        slot = slot_ref[0]
        for c in copies(te_ref[i], j, slot):
            c.wait()
        for n, wb_ref in enumerate(wb_refs):
            wb_ref[...] = wbuf_ref[slot, n].astype(BF16)
        in_sweep = nx_ref[i] >= 0

        @pl.when(in_sweep | (j + 1 < pl.num_programs(0)))
        def _():
            for c in copies(jnp.where(in_sweep, nx_ref[i], te_ref[0]), jnp.where(in_sweep, j, j + 1), 1 - slot):
                c.start()

        slot_ref[0] = 1 - slot


def _weight_stream_scratch(n_weights, k, tn):
    return [pltpu.VMEM((2, n_weights, k, tn), F32)] + [pltpu.VMEM((k, tn), BF16)] * n_weights + [
        pltpu.SemaphoreType.DMA((2,)), pltpu.SMEM((1,), jnp.int32)]


def _by_fill(nv_ref, o_ref, compute):
    nv = nv_ref[pl.program_id(1)]
    half = o_ref.shape[0] // 2

    @pl.when(nv > half)
    def _():
        o_ref[...] = compute(slice(None))

    @pl.when((nv > 0) & (nv <= half))
    def _():
        o_ref[0:half, :] = compute(slice(0, half))
        o_ref[half:, :] = jnp.zeros((half, o_ref.shape[1]), o_ref.dtype)

    @pl.when(nv == 0)
    def _():
        o_ref[...] = jnp.zeros(o_ref.shape, o_ref.dtype)


def _up_body(te_ref, nx_ref, nv_ref, na_ref, h_ref, wg_hbm, wu_hbm, o_ref, wbuf_ref, wgb_ref, wub_ref, sem_ref,
             slot_ref):
    _stream_group_weights(te_ref, nx_ref, (wg_hbm, wu_hbm), wbuf_ref, (wgb_ref, wub_ref), sem_ref, slot_ref)

    def swiglu(rows):
        h = h_ref[rows, :]
        a = _dot(h, wgb_ref[...])
        return (a * jax.nn.sigmoid(a) * _dot(h, wub_ref[...])).astype(BF16)

    _by_fill(nv_ref, o_ref, swiglu)


def swiglu_up(tile_expert, next_expert, valid_rows, n_active, hn, wg, wu, tm, tn):
    P, K = hn.shape
    F = wg.shape[2]
    return pl.pallas_call(
        _up_body,
        grid_spec=pltpu.PrefetchScalarGridSpec(
            num_scalar_prefetch=4,
            grid=(F // tn, P // tm),
            in_specs=[pl.BlockSpec((tm, K), lambda j, i, te, nx, nv, na: (jnp.minimum(i, na[0] - 1), 0)),
                      pl.BlockSpec(memory_space=pl.ANY), pl.BlockSpec(memory_space=pl.ANY)],
            out_specs=pl.BlockSpec((tm, tn), lambda j, i, te, nx, nv, na: (i, j)),
            scratch_shapes=_weight_stream_scratch(2, K, tn)),
        out_shape=jax.ShapeDtypeStruct((P, F), BF16),
        compiler_params=_cparams(("arbitrary", "arbitrary")),
        name="swiglu_up",
    )(tile_expert, next_expert, valid_rows, n_active, hn, wg, wu)


def _down_body(te_ref, nx_ref, nv_ref, na_ref, a_ref, w_hbm, *rest, residual):
    r_ref = rest[0] if residual else None
    o_ref, wbuf_ref, wb_ref, sem_ref, slot_ref = rest[1:] if residual else rest
    _stream_group_weights(te_ref, nx_ref, (w_hbm,), wbuf_ref, (wb_ref,), sem_ref, slot_ref)

    def project(rows):
        y = _dot(a_ref[rows, :], wb_ref[...])
        return r_ref[rows, :] + y if residual else y

    _by_fill(nv_ref, o_ref, project)


def swiglu_down(tile_expert, next_expert, valid_rows, n_active, act, wd, residual, tm, tn):
    P, F = act.shape
    N = wd.shape[2]
    tile = pl.BlockSpec((tm, tn), lambda j, i, te, nx, nv, na: (i, j))
    in_specs = [pl.BlockSpec((tm, F), lambda j, i, te, nx, nv, na: (jnp.minimum(i, na[0] - 1), 0)),
                pl.BlockSpec(memory_space=pl.ANY)]
    operands = [act, wd]
    if residual is not None:
        in_specs.append(tile)
        operands.append(residual)
    return pl.pallas_call(
        functools.partial(_down_body, residual=residual is not None),
        grid_spec=pltpu.PrefetchScalarGridSpec(
            num_scalar_prefetch=4,
            grid=(N // tn, P // tm),
            in_specs=in_specs,
            out_specs=tile,
            scratch_shapes=_weight_stream_scratch(1, F, tn)),
        out_shape=jax.ShapeDtypeStruct((P, N), F32),
        compiler_params=_cparams(("arbitrary", "arbitrary")),
        name="swiglu_down",
    )(tile_expert, next_expert, valid_rows, n_active, *operands)


def _router_body(x_ref, g_ref, w_ref, o_ref):
    o_ref[...] = _top2_route(_rms(x_ref[...], g_ref[...]), w_ref[...])


def router(x, g, w_pad, tm):
    T, K = x.shape
    return pl.pallas_call(
        _router_body,
        grid=(T // tm,),
        in_specs=[pl.BlockSpec((tm, K), lambda i: (i, 0)),
                  pl.BlockSpec((1, K), lambda i: (0, 0)),
                  pl.BlockSpec((K, LANES), lambda i: (0, 0))],
        out_specs=pl.BlockSpec((tm, LANES), lambda i: (i, 0)),
        out_shape=jax.ShapeDtypeStruct((T, LANES), F32),
        compiler_params=_cparams(("parallel",)),
        name="moe_router",
    )(x, g, w_pad)


ROW_DMA_UNROLL = 8


def _row_copy(src_hbm, row, dst_ref, r, sem):
    return pltpu.make_async_copy(src_hbm.at[pl.ds(row, 1)], dst_ref.at[pl.ds(r, 1)], sem)


def _gather_body(tok_ref, na_ref, x_hbm, g_ref, o_ref, buf_ref, sem_ref):
    i = pl.program_id(0)
    tm = buf_ref.shape[1]

    def fetch(tile):
        slot = tile % 2

        def start(r, c):
            _row_copy(x_hbm, tok_ref[tile * tm + r], buf_ref.at[slot], r, sem_ref.at[slot]).start()
            return c

        lax.fori_loop(0, tm, start, 0, unroll=ROW_DMA_UNROLL)

    @pl.when(i == 0)
    def _():
        fetch(0)

    @pl.when(i + 1 < na_ref[0])
    def _():
        fetch(i + 1)

    @pl.when(i < na_ref[0])
    def _():
        slot = i % 2

        def wait(r, c):
            _row_copy(x_hbm, 0, buf_ref.at[slot], r, sem_ref.at[slot]).wait()
            return c

        lax.fori_loop(0, tm, wait, 0, unroll=ROW_DMA_UNROLL)
        o_ref[...] = _rms(buf_ref[slot], g_ref[...]).astype(BF16)

    @pl.when(i >= na_ref[0])
    def _():
        o_ref[...] = jnp.zeros(o_ref.shape, BF16)


def gather_norm_tokens(tok_of_slot, n_active, x, g, tm):
    P = tok_of_slot.shape[0]
    K = x.shape[1]
    return pl.pallas_call(
        _gather_body,
        grid_spec=pltpu.PrefetchScalarGridSpec(
            num_scalar_prefetch=2,
            grid=(P // tm,),
            in_specs=[pl.BlockSpec(memory_space=pl.ANY), pl.BlockSpec((1, K), lambda i, tok, na: (0, 0))],
            out_specs=pl.BlockSpec((tm, K), lambda i, tok, na: (i, 0)),
            scratch_shapes=[pltpu.VMEM((2, tm, K), F32), pltpu.SemaphoreType.DMA((2,))]),
        out_shape=jax.ShapeDtypeStruct((P, K), BF16),
        compiler_params=_cparams(("arbitrary",)),
        name="moe_gather",
    )(tok_of_slot, n_active, x, g)


def _combine_body(s0_ref, s1_ref, x_ref, y_hbm, rt_ref, g_ref, o_ref, b0_ref, b1_ref, sem0, sem1, *, final):
    tm = x_ref.shape[0]
    base = pl.program_id(0) * tm

    def start(r, c):
        _row_copy(y_hbm, s0_ref[base + r], b0_ref, r, sem0).start()
        _row_copy(y_hbm, s1_ref[base + r], b1_ref, r, sem1).start()
        return c

    def wait(r, c):
        _row_copy(y_hbm, 0, b0_ref, r, sem0).wait()
        _row_copy(y_hbm, 0, b1_ref, r, sem1).wait()
        return c

    lax.fori_loop(0, tm, start, 0, unroll=ROW_DMA_UNROLL)
    lax.fori_loop(0, tm, wait, 0, unroll=ROW_DMA_UNROLL)
    gates = rt_ref[...]
    y = x_ref[...] + gates[:, 2:3] * b0_ref[...] + gates[:, 3:4] * b1_ref[...]
    o_ref[...] = _rms(y, g_ref[...]) if final else y


def moe_combine(slot0, slot1, x, ys, route, g, tm, final):
    T, K = x.shape
    return pl.pallas_call(
        functools.partial(_combine_body, final=final),
        grid_spec=pltpu.PrefetchScalarGridSpec(
            num_scalar_prefetch=2,
            grid=(T // tm,),
            in_specs=[pl.BlockSpec((tm, K), lambda i, s0, s1: (i, 0)),
                      pl.BlockSpec(memory_space=pl.ANY),
                      pl.BlockSpec((tm, LANES), lambda i, s0, s1: (i, 0)),
                      pl.BlockSpec((1, K), lambda i, s0, s1: (0, 0))],
            out_specs=pl.BlockSpec((tm, K), lambda i, s0, s1: (i, 0)),
            scratch_shapes=[pltpu.VMEM((tm, K), F32), pltpu.VMEM((tm, K), F32),
                            pltpu.SemaphoreType.DMA(()), pltpu.SemaphoreType.DMA(())]),
        out_shape=jax.ShapeDtypeStruct((T, K), F32),
        compiler_params=_cparams(("arbitrary",)),
        name="moe_combine",
    )(slot0, slot1, x, ys, route, g)


def _final_norm_body(x_ref, g_ref, o_ref):
    o_ref[...] = _rms(x_ref[...], g_ref[...])


def final_norm(x, g, tm):
    T, K = x.shape
    return pl.pallas_call(
        _final_norm_body,
        grid=(T // tm,),
        in_specs=[pl.BlockSpec((tm, K), lambda i: (i, 0)), pl.BlockSpec((1, K), lambda i: (0, 0))],
        out_specs=pl.BlockSpec((tm, K), lambda i: (i, 0)),
        out_shape=jax.ShapeDtypeStruct((T, K), F32),
        compiler_params=_cparams(("parallel",)),
        name="final_norm",
    )(x, g)


def _routing_tables(route, tm, expert_base):
    T = route.shape[0]
    top_i = route[:, 0:2].astype(jnp.int32)
    e_flat = top_i.reshape(-1)
    onehot = (e_flat[:, None] == jnp.arange(N_EXPERTS)[None, :]).astype(jnp.int32)
    rank = jnp.take_along_axis(jnp.cumsum(onehot, axis=0) - onehot, e_flat[:, None], axis=1)[:, 0]
    count = jnp.sum(onehot, axis=0)
    padded = ((count + tm - 1) // tm) * tm
    end = jnp.cumsum(padded)
    start = end - padded
    slot = start[e_flat] + rank
    n_slots = 2 * T + N_EXPERTS * tm
    n_tiles = n_slots // tm
    n_active = (end[-1] // tm).astype(jnp.int32)
    tile_start = jnp.minimum(jnp.arange(n_tiles, dtype=jnp.int32), n_active - 1) * tm
    tile_expert = jnp.minimum(jnp.sum(tile_start[:, None] >= end[None, :], axis=1), N_EXPERTS - 1)
    group_end = end[tile_expert] // tm
    next_expert = jnp.where(group_end < n_active, tile_expert[jnp.minimum(group_end, n_tiles - 1)] + expert_base, -1)
    tile_index = jnp.arange(n_tiles, dtype=jnp.int32)
    valid_rows = jnp.where(tile_index < n_active,
                           jnp.clip(start[tile_expert] + count[tile_expert] - tile_index * tm, 0, tm), 0)
    tok_of_slot = jnp.zeros((n_slots,), jnp.int32).at[slot].set(jnp.arange(2 * T, dtype=jnp.int32) // 2)
    slot2 = slot.reshape(T, 2).astype(jnp.int32)
    return ((tile_expert + expert_base).astype(jnp.int32), next_expert.astype(jnp.int32),
            valid_rows.astype(jnp.int32), n_active.reshape(1), tok_of_slot, slot2[:, 0], slot2[:, 1])


def _rope_tables128(T):
    half = NSA_HEAD_DIM // 2
    inv = ROPE_THETA ** (-jnp.arange(0, NSA_HEAD_DIM, 2, dtype=F32) / NSA_HEAD_DIM)
    ang = jnp.arange(T, dtype=F32)[:, None] * inv[None, :]
    cos, sin = jnp.cos(ang), jnp.sin(ang)
    cos128 = jnp.tile(cos, (1, LANES // half))
    sin128 = jnp.tile(jnp.concatenate([-sin, sin], axis=1), (1, LANES // NSA_HEAD_DIM))
    return cos128, sin128


def _overlap_matrix_t(n_cmp_pad):
    sstart = jnp.arange(LANES) * NSA_SLC_LEN
    cstart = jnp.arange(n_cmp_pad) * NSA_CMP_STRIDE
    ov = (cstart[None, :] < sstart[:, None] + NSA_SLC_LEN) & (cstart[None, :] + NSA_CMP_LEN > sstart[:, None])
    return ov.astype(BF16)


def kernel(x, attn_norm, w_in, w_out, gm_ln_g, gm_ln_b, gm_ws, gm_bs, da_lambda, da_subln, cv_dw_w, cv_dw_b,
           cv_ln_g, cv_ln_b, nsa_cmp_w1, nsa_cmp_w2, nsa_cmp_pe, ffn_norm, ffn_wg, ffn_wu, ffn_wd, router_w,
           exp_wg, exp_wu, exp_wd, final_norm_g):
    B, T, D = x.shape
    assert B == 1 and D == D_MODEL and T % 1024 == 0 and T // NSA_SLC_LEN <= LANES
    depth = w_in.shape[0]
    G = NSA_KV_HEADS
    n16 = T // NSA_CMP_STRIDE
    tm = 512

    cos128, sin128 = _rope_tables128(T)
    overlap_t = _overlap_matrix_t(n16)
    dense_na = jnp.full((1,), T // tm, jnp.int32)
    row = lambda v: v.reshape(1, -1)
    merge = lambda w: w.reshape((-1,) + w.shape[2:])
    exp_wg, exp_wu, exp_wd = merge(exp_wg), merge(exp_wu), merge(exp_wd)
    w_in_b = jnp.pad(w_in, ((0, 0), (0, 0), (0, IN_WIDTH_PAD - IN_WIDTH))).astype(BF16)
    w_out_b = w_out.astype(BF16)

    xs = x[0]
    for l in range(depth):
        lambda_init = 0.8 - 0.6 * math.exp(-0.3 * l)
        z = norm_mm(xs, row(attn_norm[l]), w_in_b, l, 1024, IN_WIDTH_PAD // 3)

        bs_rows = jnp.repeat(gm_bs[l].T, GM_CHUNK, axis=1)
        o_a = gmlp(z, row(gm_ln_g[l]), row(gm_ln_b[l]), gm_ws[l], bs_rows, 512)
        dw_w = jnp.pad(cv_dw_w[l], ((0, CV_HALO - CV_WIDTH), (0, 0)))
        o_c = conformer_conv(z, dw_w, row(cv_dw_b[l]), row(cv_ln_g[l]), row(cv_ln_b[l]), 256)

        qd, kd, vd, qn, kc, vc, ks, vs, kw, vw, gx = prep(z, cos128, sin128, 256)
        o_b = diff_attention(qd, kd, vd, da_lambda[l], row(da_subln[l]), lambda_init, 512, 512)

        x16 = jnp.stack([kc, vc]).reshape(2, T, G, NSA_HEAD_DIM).transpose(0, 2, 1, 3)
        x16 = x16.reshape(2, G, n16, NSA_CMP_STRIDE * NSA_HEAD_DIM)
        pe2 = nsa_cmp_pe[l].reshape(2, 2, NSA_CMP_STRIDE * NSA_HEAD_DIM)
        w2p = jnp.pad(nsa_cmp_w2[l], ((0, 0), (0, 0), (0, LANES - NSA_HEAD_DIM))).astype(BF16)
        kvc = compress(x16, pe2, nsa_cmp_w1[l].astype(BF16), w2p)
        o_cmp, selbias = cmp_select(qn, kvc, overlap_t, gx, 256)
        o_sel = sel_attention(qn, selbias, ks, vs, gx, 256, 512)
        o_win = win_attention(qn, kw, vw, gx, 256)

        g_ffn = row(ffn_norm[l])
        parts = (o_a, o_b, o_c, o_cmp, o_sel, o_win)
        e = l // 2
        if l % 2 == 0:
            xs, hn = out_proj(parts, w_out_b, l, xs, g_ffn, 256)
            dense_te = jnp.full((T // tm,), e, jnp.int32)
            dense_nx = jnp.full((T // tm,), -1, jnp.int32)
            up_tm = 2 * tm
            act = swiglu_up(dense_te[:T // up_tm], dense_nx[:T // up_tm], jnp.full((T // up_tm,), up_tm, jnp.int32),
                            dense_na // 2, hn, ffn_wg, ffn_wu, up_tm, 512)
            xs = swiglu_down(dense_te, dense_nx, jnp.full((T // tm,), tm, jnp.int32), dense_na, act, ffn_wd, xs,
                             tm, 512)
            if l == depth - 1:
                xs = final_norm(xs, row(final_norm_g), tm)
        else:
            w_r = jnp.pad(router_w[e], ((0, 0), (0, LANES - N_EXPERTS)))
            xs, = out_proj(parts, w_out_b, l, xs, None, 256)
            route = router(xs, g_ffn, w_r, tm)
            tile_expert, next_expert, valid_rows, n_active, tok_of_slot, slot0, slot1 = _routing_tables(
                route, tm, e * N_EXPERTS)
            hg = gather_norm_tokens(tok_of_slot, n_active, xs, g_ffn, tm)
            act = swiglu_up(tile_expert, next_expert, valid_rows, n_active, hg, exp_wg, exp_wu, tm, 512)
            ys = swiglu_down(tile_expert, next_expert, valid_rows, n_active, act, exp_wd, None, tm, 512)
            xs = moe_combine(slot0, slot1, xs, ys, route, row(final_norm_g), 256, final=(l == depth - 1))
    return xs[None]
```

```python
import functools
import math

import jax
import jax.numpy as jnp
from jax import lax
from jax.experimental import pallas as pl
from jax.experimental.pallas import tpu as pltpu

F32 = jnp.float32
BF16 = jnp.bfloat16

D_MODEL = 2048
GROUP_WIDTH = 512
GM_CHUNK = 128
GM_HEADS = 4
DA_HEADS = 4
DA_QK_DIM = 64
CV_WIDTH = 31
NSA_HEADS = 8
NSA_KV_HEADS = 2
NSA_HEAD_DIM = 64
NSA_CMP_LEN = 32
NSA_CMP_STRIDE = 16
NSA_SLC_LEN = 64
NSA_SLC_TOPK = 16
NSA_WINDOW = 512
ROPE_THETA = 10000.0
NORM_EPS = 1e-6
NEG = -1e30
NSA_FORCED = 3
N_EXPERTS = 8
LANES = 128
SUBLANES = 8
LOG2E = math.log2(math.e)

IN_WIDTH = 4888
IN_WIDTH_PAD = 4992
COL_GM = 0
COL_QDA = 1024
COL_KDA = 1536
COL_VDA = 2048
COL_CV = 2560
COL_QNS = 3584
COL_KVNS = 4096
COL_GNS = 4864

VMEM_LIMIT = 56 * 1024 * 1024


def _cparams(sem, **kw):
    return pltpu.CompilerParams(dimension_semantics=sem, vmem_limit_bytes=VMEM_LIMIT, **kw)


def _rms(x, g):
    ms = jnp.mean(x * x, axis=-1, keepdims=True)
    return x * lax.rsqrt(ms + NORM_EPS) * g


def _layer_norm(x, g, b):
    mu = jnp.mean(x, axis=-1, keepdims=True)
    xc = x - mu
    var = jnp.mean(xc * xc, axis=-1, keepdims=True)
    return xc * lax.rsqrt(var + NORM_EPS) * g + b


def _dot(a, b):
    return jnp.dot(a, b, preferred_element_type=F32)


def _dot_nt(a, b):
    return lax.dot_general(a, b, (((1,), (1,)), ((), ())), preferred_element_type=F32)


def _lane(shape):
    return lax.broadcasted_iota(jnp.int32, shape, len(shape) - 1)


def _row(shape):
    return lax.broadcasted_iota(jnp.int32, shape, len(shape) - 2)


def _norm_mm_body(x_ref, g_ref, w_ref, o_ref, hn_ref):
    @pl.when(pl.program_id(1) == 0)
    def _():
        hn_ref[...] = _rms(x_ref[...], g_ref[...]).astype(BF16)

    o_ref[...] = _dot(hn_ref[...], w_ref[0]).astype(o_ref.dtype)


def norm_mm(x, g, w, layer, tm, tn):
    T, K = x.shape
    N = w.shape[2]
    return pl.pallas_call(
        _norm_mm_body,
        grid=(T // tm, N // tn),
        in_specs=[pl.BlockSpec((tm, K), lambda i, j: (i, 0)),
                  pl.BlockSpec((1, K), lambda i, j: (0, 0)),
                  pl.BlockSpec((1, K, tn), lambda i, j: (layer, 0, j))],
        out_specs=pl.BlockSpec((tm, tn), lambda i, j: (i, j)),
        out_shape=jax.ShapeDtypeStruct((T, N), BF16),
        scratch_shapes=[pltpu.VMEM((tm, K), BF16)],
        compiler_params=_cparams(("parallel", "arbitrary")),
        name="norm_in_proj",
    )(x, g, w)


def _gmlp_body(z_ref, g_ref, b_ref, ws_ref, bs_ref, o_ref):
    tr = z_ref.shape[0]
    z = jax.nn.gelu(z_ref[...].astype(F32))
    u = z[:, :GROUP_WIDTH]
    v = _layer_norm(z[:, GROUP_WIDTH:], g_ref[...], b_ref[...]).astype(BF16)
    causal = _row((GM_CHUNK, GM_CHUNK)) >= _lane((GM_CHUNK, GM_CHUNK))
    bias = bs_ref[...]
    for h in range(GM_HEADS):
        w = jnp.where(causal, ws_ref[h], 0.0).astype(BF16)
        cols = slice(h * LANES, (h + 1) * LANES)
        for c in range(tr // GM_CHUNK):
            rows = slice(c * GM_CHUNK, (c + 1) * GM_CHUNK)
            s = _dot(w, v[rows, cols]) + bias[:, cols]
            o_ref[rows, cols] = (u[rows, cols] * s).astype(BF16)


def gmlp(z, ln_g, ln_b, ws, bs_rows, tr):
    T = z.shape[0]
    return pl.pallas_call(
        _gmlp_body,
        grid=(T // tr,),
        in_specs=[pl.BlockSpec((tr, 2 * GROUP_WIDTH), lambda i: (i, COL_GM // (2 * GROUP_WIDTH))),
                  pl.BlockSpec((1, GROUP_WIDTH), lambda i: (0, 0)),
                  pl.BlockSpec((1, GROUP_WIDTH), lambda i: (0, 0)),
                  pl.BlockSpec((GM_HEADS, GM_CHUNK, GM_CHUNK), lambda i: (0, 0, 0)),
                  pl.BlockSpec((GM_CHUNK, GROUP_WIDTH), lambda i: (0, 0))],
        out_specs=pl.BlockSpec((tr, GROUP_WIDTH), lambda i: (i, 0)),
        out_shape=jax.ShapeDtypeStruct((T, GROUP_WIDTH), BF16),
        compiler_params=_cparams(("parallel",)),
        name="gmlp",
    )(z, ln_g, ln_b, ws, bs_rows)


CV_HALO = 32
CV_SUB = 64


def _conv_body(a_ref, g_ref, ap_ref, gp_ref, w_ref, b_ref, lg_ref, lb_ref, o_ref, hs_ref):
    tr = a_ref.shape[0]
    first = pl.program_id(0) == 0
    prev = ap_ref[...].astype(F32) * jax.nn.sigmoid(gp_ref[...].astype(F32))
    hs_ref[0:CV_HALO, :] = jnp.where(first, 0.0, prev)
    hs_ref[CV_HALO:CV_HALO + tr, :] = a_ref[...].astype(F32) * jax.nn.sigmoid(g_ref[...].astype(F32))
    hs_ref[CV_HALO + tr:, :] = jnp.zeros((SUBLANES, GROUP_WIDTH), F32)
    w = w_ref[...]
    lead = CV_HALO - (CV_WIDTH - 1)
    for r0 in range(0, tr, CV_SUB):
        acc = jnp.zeros((CV_SUB, GROUP_WIDTH), F32)
        for phase in range(SUBLANES):
            base, shift = divmod(lead + phase, SUBLANES)
            part = jnp.zeros((CV_SUB + SUBLANES, GROUP_WIDTH), F32)
            for k in range(phase, CV_WIDTH, SUBLANES):
                start = r0 + SUBLANES * (base + k // SUBLANES)
                part = part + hs_ref[start:start + CV_SUB + SUBLANES, :] * w[k:k + 1, :]
            acc = acc + part[shift:shift + CV_SUB]
        y = _layer_norm(acc + b_ref[...], lg_ref[...], lb_ref[...])
        o_ref[r0:r0 + CV_SUB, :] = (y * jax.nn.sigmoid(y)).astype(BF16)


def conformer_conv(z, dw_w, dw_b, ln_g, ln_b, tr):
    T = z.shape[0]
    ca = COL_CV // GROUP_WIDTH
    per = tr // CV_HALO

    def halo(col):
        return lambda i: (jnp.maximum(i * per - 1, 0), col)

    return pl.pallas_call(
        _conv_body,
        grid=(T // tr,),
        in_specs=[pl.BlockSpec((tr, GROUP_WIDTH), lambda i: (i, ca)),
                  pl.BlockSpec((tr, GROUP_WIDTH), lambda i: (i, ca + 1)),
                  pl.BlockSpec((CV_HALO, GROUP_WIDTH), halo(ca)),
                  pl.BlockSpec((CV_HALO, GROUP_WIDTH), halo(ca + 1)),
                  pl.BlockSpec((CV_HALO, GROUP_WIDTH), lambda i: (0, 0)),
                  pl.BlockSpec((1, GROUP_WIDTH), lambda i: (0, 0)),
                  pl.BlockSpec((1, GROUP_WIDTH), lambda i: (0, 0)),
                  pl.BlockSpec((1, GROUP_WIDTH), lambda i: (0, 0))],
        out_specs=pl.BlockSpec((tr, GROUP_WIDTH), lambda i: (i, 0)),
        out_shape=jax.ShapeDtypeStruct((T, GROUP_WIDTH), BF16),
        scratch_shapes=[pltpu.VMEM((CV_HALO + tr + SUBLANES, GROUP_WIDTH), F32)],
        compiler_params=_cparams(("parallel",)),
        name="conformer_conv",
    )(z, z, z, z, dw_w, dw_b, ln_g, ln_b)


def _rope128(x, cos, sin_signed):
    x = x.astype(F32)
    lo = (_lane(x.shape) & 63) < 32
    rot = jnp.where(lo, pltpu.roll(x, 96, 1), pltpu.roll(x, 32, 1))
    return x * cos + rot * sin_signed


def _low_half(x, fill=0.0):
    return jnp.where(_lane(x.shape) < 64, x, fill)


def _high_half_to_low(x, fill=0.0):
    return jnp.where(_lane(x.shape) < 64, pltpu.roll(x, 64, 1), fill)


def _prep_body(qd_ref, kd_ref, vd_ref, qn_ref, kvc_ref, kvs_ref, kvw_ref, gt_ref, cos_ref, sin_ref, ge_ref,
               qd_o, kd_o, vd_o, qn_o, kc_o, vc_o, ks_o, vs_o, kw_o, vw_o, gx_o):
    tr = cos_ref.shape[0]
    cos = cos_ref[...]
    sin = sin_ref[...]
    q_scale = DA_QK_DIM ** -0.5 * LOG2E
    ones = jnp.ones((tr, LANES), BF16)
    for h in range(DA_HEADS):
        cols = slice(h * LANES, (h + 1) * LANES)
        qd_o[h] = (_rope128(qd_ref[:, cols], cos, sin) * q_scale).astype(BF16)
        kd_o[h] = _rope128(kd_ref[:, cols], cos, sin).astype(BF16)
        vd_o[h, :, 0:LANES] = vd_ref[:, cols].astype(BF16)
        vd_o[h, :, LANES:] = ones
    for c in range(NSA_HEADS // 2):
        q = _rope128(qn_ref[:, c * LANES:(c + 1) * LANES], cos, sin) * (NSA_HEAD_DIM ** -0.5 * LOG2E)
        qn_o[2 * c] = _low_half(q).astype(BF16)
        qn_o[2 * c + 1] = _high_half_to_low(q).astype(BF16)
    kc_o[...] = _rope128(kvc_ref[:, :LANES], cos, sin)
    vc_o[...] = kvc_ref[:, LANES:].astype(F32)
    blk = (pl.program_id(0) * tr + _row((tr, LANES))) >> 6
    onehot = jnp.where(blk == _lane((tr, LANES)), 1.0, 0.0).astype(BF16)
    k = _rope128(kvs_ref[:, :LANES], cos, sin)
    for g, half in enumerate((_low_half, _high_half_to_low)):
        ks_o[g, :, 0:LANES] = onehot
        ks_o[g, :, LANES:] = half(k).astype(BF16)
        vs_o[g] = half(kvs_ref[:, LANES:].astype(F32), 1.0).astype(BF16)
    k = _rope128(kvw_ref[:, :LANES], cos, sin)
    for g, half in enumerate((_low_half, _high_half_to_low)):
        kw_o[g] = half(k).astype(BF16)
        vw_o[g] = half(kvw_ref[:, LANES:].astype(F32), 1.0).astype(BF16)
    expand = ge_ref[...]
    gx = sum(_dot(part, expand) for part in _split3(jax.nn.sigmoid(gt_ref[...].astype(F32))))
    for c in range(3):
        gx_o[c] = gx[:, c * GROUP_WIDTH:(c + 1) * GROUP_WIDTH]


def _gate_expansion():
    col = jnp.arange(3 * GROUP_WIDTH)
    src = (col // GROUP_WIDTH) * NSA_HEADS + (col % GROUP_WIDTH) // NSA_HEAD_DIM
    return (jnp.arange(LANES)[:, None] == src[None, :]).astype(BF16)


def prep(z, cos128, sin128, tr):
    T = z.shape[0]
    G = NSA_KV_HEADS

    def zspec(width, col):
        return pl.BlockSpec((tr, width), lambda i: (i, col // width))

    def heads(n, width=LANES):
        return pl.BlockSpec((n, tr, width), lambda i: (0, i, 0))

    def hshape(n, width=LANES):
        return jax.ShapeDtypeStruct((n, T, width), BF16)

    row128 = pl.BlockSpec((tr, LANES), lambda i: (i, 0))
    return pl.pallas_call(
        _prep_body,
        grid=(T // tr,),
        in_specs=[zspec(512, COL_QDA), zspec(512, COL_KDA), zspec(512, COL_VDA), zspec(512, COL_QNS),
                  zspec(256, COL_KVNS), zspec(256, COL_KVNS + 256), zspec(256, COL_KVNS + 512),
                  zspec(128, COL_GNS), row128, row128,
                  pl.BlockSpec((LANES, 3 * GROUP_WIDTH), lambda i: (0, 0))],
        out_specs=[heads(DA_HEADS), heads(DA_HEADS), heads(DA_HEADS, 2 * LANES),
                   heads(NSA_HEADS), row128, row128, heads(G, 2 * LANES), heads(G), heads(G), heads(G),
                   pl.BlockSpec((3, tr, GROUP_WIDTH), lambda i: (0, i, 0))],
        out_shape=[hshape(DA_HEADS), hshape(DA_HEADS), hshape(DA_HEADS, 2 * LANES),
                   hshape(NSA_HEADS), jax.ShapeDtypeStruct((T, LANES), F32),
                   jax.ShapeDtypeStruct((T, LANES), F32), hshape(G, 2 * LANES), hshape(G), hshape(G), hshape(G),
                   jax.ShapeDtypeStruct((3, T, GROUP_WIDTH), F32)],
        compiler_params=_cparams(("parallel",)),
        name="attention_prep",
    )(z, z, z, z, z, z, z, z, cos128, sin128, _gate_expansion())


def _lane_tile(x, n):
    return x if n == 1 else jnp.concatenate([x] * n, axis=-1)


def _softmax_update(s, v, m_ref, acc_ref):
    m_prev = m_ref[...]
    m_new = jnp.maximum(m_prev, jnp.max(s, axis=-1, keepdims=True))
    alpha = jnp.exp2(m_prev - m_new)
    p = jnp.exp2(s - _lane_tile(m_new, s.shape[1] // LANES))
    acc_ref[...] = _lane_tile(alpha, acc_ref.shape[1] // LANES) * acc_ref[...] + _dot(p.astype(BF16), v)
    m_ref[...] = m_new


SWEEP_UNROLL = 4


def _causal_sweep(qs_ref, k_ref, v_ref, m_ref, acc_ref, s_ref, q0, tq, tk):
    m_ref[...] = jnp.full(m_ref.shape, -jnp.inf, F32)
    acc_ref[...] = jnp.zeros(acc_ref.shape, F32)

    def keys(ref, t):
        return ref[0, pl.ds(pl.multiple_of(t * tk, tk), tk), :]

    def scores(t):
        return _dot_nt(qs_ref[...], keys(k_ref, t))

    n_full = q0 // tk
    s_ref[...] = scores(0)

    def run(t0, count):
        s_cur = s_ref[...]
        for u in range(count):
            s_next = scores(t0 + u + 1)
            _softmax_update(s_cur, keys(v_ref, t0 + u), m_ref, acc_ref)
            s_cur = s_next
        s_ref[...] = s_cur

    def several(i, c):
        run(SWEEP_UNROLL * i, SWEEP_UNROLL)
        return c

    def single(t, c):
        run(t, 1)
        return c

    n_groups = n_full // SWEEP_UNROLL
    lax.fori_loop(0, n_groups, several, 0)
    lax.fori_loop(n_groups * SWEEP_UNROLL, n_full, single, 0)
    s = s_ref[...]
    visible = n_full * tk + _lane(s.shape) <= q0 + (_row(s.shape) & (tq - 1))
    _softmax_update(jnp.where(visible, s, NEG), keys(v_ref, n_full), m_ref, acc_ref)


def _diff_attn_body(q_ref, k_ref, v_ref, lam_ref, sg_ref, o_ref, qs_ref, m_ref, acc_ref, s_ref, *,
                    tq, tk, lambda_init):
    q = q_ref[0]
    first = _lane(q.shape) < DA_QK_DIM
    qs_ref[0:tq, :] = jnp.where(first, q, jnp.zeros_like(q))
    qs_ref[tq:, :] = jnp.where(first, jnp.zeros_like(q), q)
    _causal_sweep(qs_ref, k_ref, v_ref, m_ref, acc_ref, s_ref, pl.program_id(1) * tq, tq, tk)
    lam = lam_ref[...]
    lam_full = (jnp.exp(jnp.sum(lam[0:1] * lam[1:2], axis=-1, keepdims=True))
                - jnp.exp(jnp.sum(lam[2:3] * lam[3:4], axis=-1, keepdims=True)) + lambda_init)
    o = acc_ref[:, 0:LANES] / acc_ref[:, LANES:]
    a = o[0:tq] - lam_full * o[tq:]
    o_ref[...] = (_rms(a, sg_ref[...]) * (1.0 - lambda_init)).astype(BF16)


def diff_attention(qd, kd, vd, lam, subln, lambda_init, tq, tk):
    H, T, _ = qd.shape
    return pl.pallas_call(
        functools.partial(_diff_attn_body, tq=tq, tk=tk, lambda_init=lambda_init),
        grid=(H, T // tq),
        in_specs=[pl.BlockSpec((1, tq, LANES), lambda h, qi: (h, qi, 0)),
                  pl.BlockSpec((1, T, LANES), lambda h, qi: (h, 0, 0)),
                  pl.BlockSpec((1, T, 2 * LANES), lambda h, qi: (h, 0, 0)),
                  pl.BlockSpec((4, DA_QK_DIM), lambda h, qi: (0, 0)),
                  pl.BlockSpec((1, LANES), lambda h, qi: (0, 0))],
        out_specs=pl.BlockSpec((tq, LANES), lambda h, qi: (qi, h)),
        out_shape=jax.ShapeDtypeStruct((T, GROUP_WIDTH), BF16),
        scratch_shapes=[pltpu.VMEM((2 * tq, LANES), BF16), pltpu.VMEM((2 * tq, LANES), F32),
                        pltpu.VMEM((2 * tq, 2 * LANES), F32), pltpu.VMEM((2 * tq, tk), F32)],
        compiler_params=_cparams(("parallel", "arbitrary")),
        name="diff_attention",
    )(qd, kd, vd, lam, subln)


def _compress_body(x_ref, pe_ref, w1_ref, w2_ref, o_ref):
    x = x_ref[0, 0]
    half = x.shape[1]
    a = _dot((x + pe_ref[0, 0:1, :]).astype(BF16), w1_ref[0, 0:half, :])
    b = _dot((x + pe_ref[0, 1:2, :]).astype(BF16), w1_ref[0, half:, :])
    hid = jax.nn.gelu(a + pltpu.roll(b, b.shape[0] - 1, 0))
    o_ref[0, 0] = _dot(hid.astype(BF16), w2_ref[0]).astype(BF16)


def compress(x16, pe2, w1, w2p):
    _, G, n, half = x16.shape
    hid = w1.shape[2]
    return pl.pallas_call(
        _compress_body,
        grid=(2, G),
        in_specs=[pl.BlockSpec((1, 1, n, half), lambda c, g: (c, g, 0, 0)),
                  pl.BlockSpec((1, 2, half), lambda c, g: (c, 0, 0)),
                  pl.BlockSpec((1, 2 * half, hid), lambda c, g: (c, 0, 0)),
                  pl.BlockSpec((1, hid, LANES), lambda c, g: (c, 0, 0))],
        out_specs=pl.BlockSpec((1, 1, n, LANES), lambda c, g: (c, g, 0, 0)),
        out_shape=jax.ShapeDtypeStruct((2, G, n, LANES), BF16),
        compiler_params=_cparams(("parallel", "parallel")),
        name="nsa_compress",
    )(x16, pe2, w1, w2p)


def _split3(x):
    hi = x.astype(BF16)
    r1 = x - hi.astype(F32)
    mid = r1.astype(BF16)
    lo = (r1 - mid.astype(F32)).astype(BF16)
    return hi, mid, lo


def _pack_heads(o, tq):
    pair = lambda a, b: a + pltpu.roll(b, 64, 1)
    return jnp.concatenate([pair(o[0:tq], o[tq:2 * tq]), pair(o[2 * tq:3 * tq], o[3 * tq:])], axis=-1)


def _normalize_low_half(acc):
    return jnp.where(_lane(acc.shape) < 64, acc / pltpu.roll(acc, 64, 1), 0.0)


def _cmp_select_body(q_ref, kc_ref, vc_ref, ovt_ref, gx_ref, o_ref, sb_ref, *, tq):
    qi = pl.program_id(1)
    hpg = q_ref.shape[0]
    q = q_ref[...].reshape(hpg * tq, LANES)
    s = _dot_nt(q, kc_ref[0, 0])
    t = qi * tq + (_row(s.shape) & (tq - 1))
    cmask = _lane(s.shape) * NSA_CMP_STRIDE + (NSA_CMP_LEN - 1) <= t
    s = jnp.where(cmask, s, NEG)
    e = jnp.exp2(s - jnp.max(s, axis=-1, keepdims=True))
    p = jnp.where(cmask, e / jnp.sum(e, axis=-1, keepdims=True), 0.0)
    o = _dot(p.astype(BF16), vc_ref[0, 0])
    o_ref[...] = (gx_ref[0] * _pack_heads(o, tq)).astype(BF16)

    psum = p[0:tq]
    for hh in range(1, hpg):
        psum = psum + p[hh * tq:(hh + 1) * tq]
    ovt = ovt_ref[...]
    imp = sum(_dot_nt(ovt, part) for part in _split3(psum))
    j = _row(imp.shape)
    cur = (qi * tq + _lane(imp.shape)) >> 6
    forced = (j == 0) | (j == cur) | (j == cur - 1)
    score = jnp.where(forced, -2.0, jnp.where(j <= cur, imp, -1.0))
    jf = j.astype(F32)
    bias = jnp.where(forced, 0.0, NEG)
    for _ in range(NSA_SLC_TOPK - NSA_FORCED):
        m = jnp.max(score, axis=0, keepdims=True)
        first = jnp.min(jnp.where(score == m, jf, float(LANES)), axis=0, keepdims=True)
        hit = jf == first
        bias = jnp.where(hit, jnp.where(m >= 0.0, 0.0, NEG), bias)
        score = jnp.where(hit, -2.0, score)
    sb_ref[0] = bias.T.astype(BF16)


def cmp_select(qn, kvc, overlap_t, gx, tq):
    H, T, _ = qn.shape
    G = NSA_KV_HEADS
    hpg = H // G
    n = kvc.shape[2]
    return pl.pallas_call(
        functools.partial(_cmp_select_body, tq=tq),
        grid=(G, T // tq),
        in_specs=[pl.BlockSpec((hpg, tq, LANES), lambda g, qi: (g, qi, 0)),
                  pl.BlockSpec((1, 1, n, LANES), lambda g, qi: (0, g, 0, 0)),
                  pl.BlockSpec((1, 1, n, LANES), lambda g, qi: (1, g, 0, 0)),
                  pl.BlockSpec((LANES, n), lambda g, qi: (0, 0)),
                  pl.BlockSpec((1, tq, hpg * NSA_HEAD_DIM), lambda g, qi: (0, qi, g))],
        out_specs=[pl.BlockSpec((tq, hpg * NSA_HEAD_DIM), lambda g, qi: (qi, g)),
                   pl.BlockSpec((1, tq, LANES), lambda g, qi: (g, qi, 0))],
        out_shape=[jax.ShapeDtypeStruct((T, GROUP_WIDTH), BF16),
                   jax.ShapeDtypeStruct((G, T, LANES), BF16)],
        compiler_params=_cparams(("parallel", "parallel")),
        name="nsa_compressed_select",
    )(qn, kvc, kvc, overlap_t, gx)


def _sel_attn_body(q_ref, sb_ref, k_ref, v_ref, gx_ref, o_ref, qs_ref, m_ref, acc_ref, s_ref, *, tq, tk):
    hpg = q_ref.shape[0]
    for hh in range(hpg):
        qs_ref[hh * tq:(hh + 1) * tq, 0:LANES] = sb_ref[0]
        qs_ref[hh * tq:(hh + 1) * tq, LANES:] = q_ref[hh]
    _causal_sweep(qs_ref, k_ref, v_ref, m_ref, acc_ref, s_ref, pl.program_id(1) * tq, tq, tk)
    o = _normalize_low_half(acc_ref[...])
    o_ref[...] = (gx_ref[0] * _pack_heads(o, tq)).astype(BF16)


def sel_attention(qn, selbias, ks, vs, gx, tq, tk):
    H, T, _ = qn.shape
    G = NSA_KV_HEADS
    hpg = H // G
    return pl.pallas_call(
        functools.partial(_sel_attn_body, tq=tq, tk=tk),
        grid=(G, T // tq),
        in_specs=[pl.BlockSpec((hpg, tq, LANES), lambda g, qi: (g, qi, 0)),
                  pl.BlockSpec((1, tq, LANES), lambda g, qi: (g, qi, 0)),
                  pl.BlockSpec((1, T, 2 * LANES), lambda g, qi: (g, 0, 0)),
                  pl.BlockSpec((1, T, LANES), lambda g, qi: (g, 0, 0)),
                  pl.BlockSpec((1, tq, hpg * NSA_HEAD_DIM), lambda g, qi: (1, qi, g))],
        out_specs=pl.BlockSpec((tq, hpg * NSA_HEAD_DIM), lambda g, qi: (qi, g)),
        out_shape=jax.ShapeDtypeStruct((T, GROUP_WIDTH), BF16),
        scratch_shapes=[pltpu.VMEM((hpg * tq, 2 * LANES), BF16), pltpu.VMEM((hpg * tq, LANES), F32),
                        pltpu.VMEM((hpg * tq, LANES), F32), pltpu.VMEM((hpg * tq, tk), F32)],
        compiler_params=_cparams(("parallel", "arbitrary")),
        name="nsa_selected_attention",
    )(qn, selbias, ks, vs, gx)


def _win_attn_body(q_ref, k_ref, v_ref, b_ref, gx_ref, o_ref, *, tq, span):
    hpg = q_ref.shape[0]
    q0 = pl.program_id(1) * tq
    lo = pl.multiple_of(jnp.maximum(q0 + tq - span, 0), tq)
    q = q_ref[...].reshape(hpg * tq, LANES)
    s = _dot_nt(q, k_ref[0, pl.ds(lo, span), :])

    def finish(s):
        p = jnp.exp2(s - jnp.max(s, axis=-1, keepdims=True))
        acc = _dot(p.astype(BF16), v_ref[0, pl.ds(lo, span), :])
        o_ref[...] = (gx_ref[0] * _pack_heads(_normalize_low_half(acc), tq)).astype(BF16)

    @pl.when(q0 + tq >= span)
    def _():
        finish(s + jnp.concatenate([b_ref[...]] * hpg, axis=0))

    @pl.when(q0 + tq < span)
    def _():
        qpos = q0 + (_row(s.shape) & (tq - 1))
        kpos = lo + _lane(s.shape)
        finish(jnp.where((kpos <= qpos) & (kpos > qpos - NSA_WINDOW), s, NEG))


def _window_bias(tq, span):
    d = jnp.arange(span)[None, :] - jnp.arange(tq)[:, None] + (tq - span)
    return jnp.where((d <= 0) & (d > -NSA_WINDOW), 0.0, NEG).astype(F32)


def win_attention(qn, kw, vw, gx, tq):
    H, T, _ = qn.shape
    G = NSA_KV_HEADS
    hpg = H // G
    span = NSA_WINDOW + tq
    return pl.pallas_call(
        functools.partial(_win_attn_body, tq=tq, span=span),
        grid=(G, T // tq),
        in_specs=[pl.BlockSpec((hpg, tq, LANES), lambda g, qi: (g, qi, 0)),
                  pl.BlockSpec((1, T, LANES), lambda g, qi: (g, 0, 0)),
                  pl.BlockSpec((1, T, LANES), lambda g, qi: (g, 0, 0)),
                  pl.BlockSpec((tq, span), lambda g, qi: (0, 0)),
                  pl.BlockSpec((1, tq, hpg * NSA_HEAD_DIM), lambda g, qi: (2, qi, g))],
        out_specs=pl.BlockSpec((tq, hpg * NSA_HEAD_DIM), lambda g, qi: (qi, g)),
        out_shape=jax.ShapeDtypeStruct((T, GROUP_WIDTH), BF16),
        compiler_params=_cparams(("parallel", "arbitrary")),
        name="nsa_window_attention",
    )(qn, kw, vw, _window_bias(tq, span), gx)


def _top2_route(h, w_router):
    logits = jnp.dot(h, w_router, preferred_element_type=F32, precision=lax.Precision.HIGHEST)
    lane = _lane(logits.shape)
    lf = lane.astype(F32)
    logits = jnp.where(lane < N_EXPERTS, logits, -jnp.inf)
    v0 = jnp.max(logits, axis=-1, keepdims=True)
    i0 = jnp.min(jnp.where(logits == v0, lf, float(LANES)), axis=-1, keepdims=True)
    rest = jnp.where(lf == i0, -jnp.inf, logits)
    v1 = jnp.max(rest, axis=-1, keepdims=True)
    i1 = jnp.min(jnp.where(rest == v1, lf, float(LANES)), axis=-1, keepdims=True)
    e1 = jnp.exp(v1 - v0)
    w0 = 1.0 / (1.0 + e1)
    w1 = e1 / (1.0 + e1)
    return jnp.where(lane == 0, i0, jnp.where(lane == 1, i1, jnp.where(lane == 2, w0, w1)))


def _out_proj_body(a_ref, b_ref, c_ref, d1_ref, d2_ref, d3_ref, w_ref, x_ref, *rest, normed):
    o_ref, lhs_ref = rest[-3 if normed else -2], rest[-1]
    lhs_ref[:, 0:GROUP_WIDTH] = a_ref[...]
    lhs_ref[:, GROUP_WIDTH:2 * GROUP_WIDTH] = b_ref[...]
    lhs_ref[:, 2 * GROUP_WIDTH:3 * GROUP_WIDTH] = c_ref[...]
    d = d1_ref[...].astype(F32) + d2_ref[...].astype(F32) + d3_ref[...].astype(F32)
    lhs_ref[:, 3 * GROUP_WIDTH:] = d.astype(BF16)
    y = x_ref[...] + _dot(lhs_ref[...], w_ref[0])
    o_ref[...] = y
    if normed:
        g_ref, hn_ref = rest[0], rest[2]
        hn_ref[...] = _rms(y, g_ref[...]).astype(BF16)


def out_proj(parts, w, layer, x, g_next, tm):
    T, N = x.shape
    part = pl.BlockSpec((tm, GROUP_WIDTH), lambda i: (i, 0))
    rows = pl.BlockSpec((tm, N), lambda i: (i, 0))
    in_specs = [part] * 6 + [pl.BlockSpec((1, 4 * GROUP_WIDTH, N), lambda i: (layer, 0, 0)), rows]
    out_specs = [rows]
    out_shape = [jax.ShapeDtypeStruct((T, N), F32)]
    operands = [*parts, w, x]
    if g_next is not None:
        in_specs.append(pl.BlockSpec((1, N), lambda i: (0, 0)))
        out_specs.append(rows)
        out_shape.append(jax.ShapeDtypeStruct((T, N), BF16))
        operands.append(g_next)
    return pl.pallas_call(
        functools.partial(_out_proj_body, normed=g_next is not None),
        grid=(T // tm,),
        in_specs=in_specs,
        out_specs=out_specs,
        out_shape=out_shape,
        scratch_shapes=[pltpu.VMEM((tm, 4 * GROUP_WIDTH), BF16)],
        compiler_params=_cparams(("parallel",)),
        name="out_proj",
    )(*operands)


UP_TN = 1408
UP_CHUNK = 512
CAST_ROWS = 256


def _group_starts(te_ref, i):
    return (i == 0) | (te_ref[i] != te_ref[jnp.maximum(i - 1, 0)])


def _stream_group_weights(te_ref, nx_ref, w_hbms, wst_ref, wb_refs, sem_ref):
    j = pl.program_id(0)
    i = pl.program_id(1)
    tn = wst_ref.shape[2]

    def copies(e, jj):
        cols = pl.ds(pl.multiple_of(jj * tn, tn), tn)
        return [pltpu.make_async_copy(w.at[e, :, cols], wst_ref.at[n], sem_ref) for n, w in enumerate(w_hbms)]

    @pl.when((j == 0) & (i == 0))
    def _():
        for c in copies(te_ref[0], 0):
            c.start()

    @pl.when(_group_starts(te_ref, i))
    def _():
        for c in copies(te_ref[i], j):
            c.wait()
        def cast_rows(c, carry):
            rows = pl.ds(pl.multiple_of(c * CAST_ROWS, CAST_ROWS), CAST_ROWS)
            for n, wb_ref in enumerate(wb_refs):
                wb_ref[rows, :] = wst_ref[n, rows, :].astype(BF16)
            return carry

        lax.fori_loop(0, wst_ref.shape[1] // CAST_ROWS, cast_rows, 0)
        in_sweep = nx_ref[i] >= 0

        @pl.when(in_sweep | (j + 1 < pl.num_programs(0)))
        def _():
            for c in copies(jnp.where(in_sweep, nx_ref[i], te_ref[0]), jnp.where(in_sweep, j, j + 1)):
                c.start()


def _weight_stream_scratch(n_weights, k, tn):
    return [pltpu.VMEM((n_weights, k, tn), F32)] + [pltpu.VMEM((k, tn), BF16)] * n_weights + [
        pltpu.SemaphoreType.DMA(())]


def _by_fill(nv_ref, o_ref, compute):
    nv = nv_ref[pl.program_id(1)]
    half = o_ref.shape[0] // 2

    @pl.when(nv > half)
    def _():
        o_ref[...] = compute(slice(None))

    @pl.when((nv > 0) & (nv <= half))
    def _():
        o_ref[0:half, :] = compute(slice(0, half))
        o_ref[half:, :] = jnp.zeros((half, o_ref.shape[1]), o_ref.dtype)

    @pl.when(nv == 0)
    def _():
        o_ref[...] = jnp.zeros(o_ref.shape, o_ref.dtype)


def _up_body(te_ref, nx_ref, nv_ref, na_ref, h_ref, wg_hbm, wu_hbm, o_ref, wst_ref, wgb_ref, wub_ref, sem_ref):
    _stream_group_weights(te_ref, nx_ref, (wg_hbm, wu_hbm), wst_ref, (wgb_ref, wub_ref), sem_ref)

    def swiglu(rows):
        h = h_ref[rows, :]
        parts = []
        for c0 in range(0, o_ref.shape[1], UP_CHUNK):
            cols = slice(c0, min(c0 + UP_CHUNK, o_ref.shape[1]))
            a = _dot(h, wgb_ref[:, cols])
            parts.append((a * jax.nn.sigmoid(a) * _dot(h, wub_ref[:, cols])).astype(BF16))
        return jnp.concatenate(parts, axis=-1)

    _by_fill(nv_ref, o_ref, swiglu)


def swiglu_up(tile_expert, next_expert, valid_rows, n_active, hn, wg, wu, tm, tn):
    P, K = hn.shape
    F = wg.shape[2]
    return pl.pallas_call(
        _up_body,
        grid_spec=pltpu.PrefetchScalarGridSpec(
            num_scalar_prefetch=4,
            grid=(F // tn, P // tm),
            in_specs=[pl.BlockSpec((tm, K), lambda j, i, te, nx, nv, na: (jnp.minimum(i, na[0] - 1), 0)),
                      pl.BlockSpec(memory_space=pl.ANY), pl.BlockSpec(memory_space=pl.ANY)],
            out_specs=pl.BlockSpec((tm, tn), lambda j, i, te, nx, nv, na: (i, j)),
            scratch_shapes=_weight_stream_scratch(2, K, tn)),
        out_shape=jax.ShapeDtypeStruct((P, F), BF16),
        compiler_params=_cparams(("arbitrary", "arbitrary")),
        name="swiglu_up",
    )(tile_expert, next_expert, valid_rows, n_active, hn, wg, wu)


def _down_body(te_ref, nx_ref, nv_ref, na_ref, a_ref, w_hbm, *rest, residual):
    r_ref = rest[0] if residual else None
    o_ref, wst_ref, wb_ref, sem_ref = rest[1:] if residual else rest
    _stream_group_weights(te_ref, nx_ref, (w_hbm,), wst_ref, (wb_ref,), sem_ref)

    def project(rows):
        y = _dot(a_ref[rows, :], wb_ref[...])
        return r_ref[rows, :] + y if residual else y

    _by_fill(nv_ref, o_ref, project)


def swiglu_down(tile_expert, next_expert, valid_rows, n_active, act, wd, residual, tm, tn):
    P, F = act.shape
    N = wd.shape[2]
    tile = pl.BlockSpec((tm, tn), lambda j, i, te, nx, nv, na: (i, j))
    in_specs = [pl.BlockSpec((tm, F), lambda j, i, te, nx, nv, na: (jnp.minimum(i, na[0] - 1), 0)),
                pl.BlockSpec(memory_space=pl.ANY)]
    operands = [act, wd]
    if residual is not None:
        in_specs.append(tile)
        operands.append(residual)
    return pl.pallas_call(
        functools.partial(_down_body, residual=residual is not None),
        grid_spec=pltpu.PrefetchScalarGridSpec(
            num_scalar_prefetch=4,
            grid=(N // tn, P // tm),
            in_specs=in_specs,
            out_specs=tile,
            scratch_shapes=_weight_stream_scratch(1, F, tn)),
        out_shape=jax.ShapeDtypeStruct((P, N), F32),
        compiler_params=_cparams(("arbitrary", "arbitrary")),
        name="swiglu_down",
    )(tile_expert, next_expert, valid_rows, n_active, *operands)


def _router_body(x_ref, g_ref, w_ref, o_ref):
    o_ref[...] = _top2_route(_rms(x_ref[...], g_ref[...]), w_ref[...])


def router(x, g, w_pad, tm):
    T, K = x.shape
    return pl.pallas_call(
        _router_body,
        grid=(T // tm,),
        in_specs=[pl.BlockSpec((tm, K), lambda i: (i, 0)),
                  pl.BlockSpec((1, K), lambda i: (0, 0)),
                  pl.BlockSpec((K, LANES), lambda i: (0, 0))],
        out_specs=pl.BlockSpec((tm, LANES), lambda i: (i, 0)),
        out_shape=jax.ShapeDtypeStruct((T, LANES), F32),
        compiler_params=_cparams(("parallel",)),
        name="moe_router",
    )(x, g, w_pad)


ROW_DMA_UNROLL = 8


def _row_copy(src_hbm, row, dst_ref, r, sem):
    return pltpu.make_async_copy(src_hbm.at[pl.ds(row, 1)], dst_ref.at[pl.ds(r, 1)], sem)


def _gather_body(tok_ref, na_ref, x_hbm, g_ref, o_ref, buf_ref, sem_ref):
    i = pl.program_id(0)
    tm = buf_ref.shape[1]

    def fetch(tile):
        slot = tile % 2

        def start(r, c):
            _row_copy(x_hbm, tok_ref[tile * tm + r], buf_ref.at[slot], r, sem_ref.at[slot]).start()
            return c

        lax.fori_loop(0, tm, start, 0, unroll=ROW_DMA_UNROLL)

    @pl.when(i == 0)
    def _():
        fetch(0)

    @pl.when(i + 1 < na_ref[0])
    def _():
        fetch(i + 1)

    @pl.when(i < na_ref[0])
    def _():
        slot = i % 2

        def wait(r, c):
            _row_copy(x_hbm, 0, buf_ref.at[slot], r, sem_ref.at[slot]).wait()
            return c

        lax.fori_loop(0, tm, wait, 0, unroll=ROW_DMA_UNROLL)
        o_ref[...] = _rms(buf_ref[slot], g_ref[...]).astype(BF16)

    @pl.when(i >= na_ref[0])
    def _():
        o_ref[...] = jnp.zeros(o_ref.shape, BF16)


def gather_norm_tokens(tok_of_slot, n_active, x, g, tm):
    P = tok_of_slot.shape[0]
    K = x.shape[1]
    return pl.pallas_call(
        _gather_body,
        grid_spec=pltpu.PrefetchScalarGridSpec(
            num_scalar_prefetch=2,
            grid=(P // tm,),
            in_specs=[pl.BlockSpec(memory_space=pl.ANY), pl.BlockSpec((1, K), lambda i, tok, na: (0, 0))],
            out_specs=pl.BlockSpec((tm, K), lambda i, tok, na: (i, 0)),
            scratch_shapes=[pltpu.VMEM((2, tm, K), F32), pltpu.SemaphoreType.DMA((2,))]),
        out_shape=jax.ShapeDtypeStruct((P, K), BF16),
        compiler_params=_cparams(("arbitrary",)),
        name="moe_gather",
    )(tok_of_slot, n_active, x, g)


def _combine_body(s0_ref, s1_ref, x_ref, y_hbm, rt_ref, g_ref, o_ref, b0_ref, b1_ref, sem0, sem1, *, final):
    tm = x_ref.shape[0]
    base = pl.program_id(0) * tm

    def start(r, c):
        _row_copy(y_hbm, s0_ref[base + r], b0_ref, r, sem0).start()
        _row_copy(y_hbm, s1_ref[base + r], b1_ref, r, sem1).start()
        return c

    def wait(r, c):
        _row_copy(y_hbm, 0, b0_ref, r, sem0).wait()
        _row_copy(y_hbm, 0, b1_ref, r, sem1).wait()
        return c

    lax.fori_loop(0, tm, start, 0, unroll=ROW_DMA_UNROLL)
    lax.fori_loop(0, tm, wait, 0, unroll=ROW_DMA_UNROLL)
    gates = rt_ref[...]
    y = x_ref[...] + gates[:, 2:3] * b0_ref[...] + gates[:, 3:4] * b1_ref[...]
    o_ref[...] = _rms(y, g_ref[...]) if final else y


def moe_combine(slot0, slot1, x, ys, route, g, tm, final):
    T, K = x.shape
    return pl.pallas_call(
        functools.partial(_combine_body, final=final),
        grid_spec=pltpu.PrefetchScalarGridSpec(
            num_scalar_prefetch=2,
            grid=(T // tm,),
            in_specs=[pl.BlockSpec((tm, K), lambda i, s0, s1: (i, 0)),
                      pl.BlockSpec(memory_space=pl.ANY),
                      pl.BlockSpec((tm, LANES), lambda i, s0, s1: (i, 0)),
                      pl.BlockSpec((1, K), lambda i, s0, s1: (0, 0))],
            out_specs=pl.BlockSpec((tm, K), lambda i, s0, s1: (i, 0)),
            scratch_shapes=[pltpu.VMEM((tm, K), F32), pltpu.VMEM((tm, K), F32),
                            pltpu.SemaphoreType.DMA(()), pltpu.SemaphoreType.DMA(())]),
        out_shape=jax.ShapeDtypeStruct((T, K), F32),
        compiler_params=_cparams(("arbitrary",)),
        name="moe_combine",
    )(slot0, slot1, x, ys, route, g)


def _final_norm_body(x_ref, g_ref, o_ref):
    o_ref[...] = _rms(x_ref[...], g_ref[...])


def final_norm(x, g, tm):
    T, K = x.shape
    return pl.pallas_call(
        _final_norm_body,
        grid=(T // tm,),
        in_specs=[pl.BlockSpec((tm, K), lambda i: (i, 0)), pl.BlockSpec((1, K), lambda i: (0, 0))],
        out_specs=pl.BlockSpec((tm, K), lambda i: (i, 0)),
        out_shape=jax.ShapeDtypeStruct((T, K), F32),
        compiler_params=_cparams(("parallel",)),
        name="final_norm",
    )(x, g)


def _routing_tables(route, tm, expert_base):
    T = route.shape[0]
    top_i = route[:, 0:2].astype(jnp.int32)
    e_flat = top_i.reshape(-1)
    onehot = (e_flat[:, None] == jnp.arange(N_EXPERTS)[None, :]).astype(jnp.int32)
    rank = jnp.take_along_axis(jnp.cumsum(onehot, axis=0) - onehot, e_flat[:, None], axis=1)[:, 0]
    count = jnp.sum(onehot, axis=0)
    padded = ((count + tm - 1) // tm) * tm
    end = jnp.cumsum(padded)
    start = end - padded
    slot = start[e_flat] + rank
    n_slots = 2 * T + N_EXPERTS * tm
    n_tiles = n_slots // tm
    n_active = (end[-1] // tm).astype(jnp.int32)
    tile_start = jnp.minimum(jnp.arange(n_tiles, dtype=jnp.int32), n_active - 1) * tm
    tile_expert = jnp.minimum(jnp.sum(tile_start[:, None] >= end[None, :], axis=1), N_EXPERTS - 1)
    group_end = end[tile_expert] // tm
    next_expert = jnp.where(group_end < n_active, tile_expert[jnp.minimum(group_end, n_tiles - 1)] + expert_base, -1)
    tile_index = jnp.arange(n_tiles, dtype=jnp.int32)
    valid_rows = jnp.where(tile_index < n_active,
                           jnp.clip(start[tile_expert] + count[tile_expert] - tile_index * tm, 0, tm), 0)
    tok_of_slot = jnp.zeros((n_slots,), jnp.int32).at[slot].set(jnp.arange(2 * T, dtype=jnp.int32) // 2)
    slot2 = slot.reshape(T, 2).astype(jnp.int32)
    return ((tile_expert + expert_base).astype(jnp.int32), next_expert.astype(jnp.int32),
            valid_rows.astype(jnp.int32), n_active.reshape(1), tok_of_slot, slot2[:, 0], slot2[:, 1])


def _rope_tables128(T):
    half = NSA_HEAD_DIM // 2
    inv = ROPE_THETA ** (-jnp.arange(0, NSA_HEAD_DIM, 2, dtype=F32) / NSA_HEAD_DIM)
    ang = jnp.arange(T, dtype=F32)[:, None] * inv[None, :]
    cos, sin = jnp.cos(ang), jnp.sin(ang)
    cos128 = jnp.tile(cos, (1, LANES // half))
    sin128 = jnp.tile(jnp.concatenate([-sin, sin], axis=1), (1, LANES // NSA_HEAD_DIM))
    return cos128, sin128


def _overlap_matrix_t(n_cmp_pad):
    sstart = jnp.arange(LANES) * NSA_SLC_LEN
    cstart = jnp.arange(n_cmp_pad) * NSA_CMP_STRIDE
    ov = (cstart[None, :] < sstart[:, None] + NSA_SLC_LEN) & (cstart[None, :] + NSA_CMP_LEN > sstart[:, None])
    return ov.astype(BF16)


def kernel(x, attn_norm, w_in, w_out, gm_ln_g, gm_ln_b, gm_ws, gm_bs, da_lambda, da_subln, cv_dw_w, cv_dw_b,
           cv_ln_g, cv_ln_b, nsa_cmp_w1, nsa_cmp_w2, nsa_cmp_pe, ffn_norm, ffn_wg, ffn_wu, ffn_wd, router_w,
           exp_wg, exp_wu, exp_wd, final_norm_g):
    B, T, D = x.shape
    assert B == 1 and D == D_MODEL and T % 1024 == 0 and T // NSA_SLC_LEN <= LANES
    depth = w_in.shape[0]
    G = NSA_KV_HEADS
    n16 = T // NSA_CMP_STRIDE
    tm = 512

    cos128, sin128 = _rope_tables128(T)
    overlap_t = _overlap_matrix_t(n16)
    dense_na = jnp.full((1,), T // tm, jnp.int32)
    row = lambda v: v.reshape(1, -1)
    merge = lambda w: w.reshape((-1,) + w.shape[2:])
    exp_wg, exp_wu, exp_wd = merge(exp_wg), merge(exp_wu), merge(exp_wd)
    w_in_b = jnp.pad(w_in, ((0, 0), (0, 0), (0, IN_WIDTH_PAD - IN_WIDTH))).astype(BF16)
    w_out_b = w_out.astype(BF16)

    xs = x[0]
    for l in range(depth):
        lambda_init = 0.8 - 0.6 * math.exp(-0.3 * l)
        z = norm_mm(xs, row(attn_norm[l]), w_in_b, l, 1024, IN_WIDTH_PAD // 3)

        bs_rows = jnp.repeat(gm_bs[l].T, GM_CHUNK, axis=1)
        o_a = gmlp(z, row(gm_ln_g[l]), row(gm_ln_b[l]), gm_ws[l], bs_rows, 512)
        dw_w = jnp.pad(cv_dw_w[l], ((0, CV_HALO - CV_WIDTH), (0, 0)))
        o_c = conformer_conv(z, dw_w, row(cv_dw_b[l]), row(cv_ln_g[l]), row(cv_ln_b[l]), 256)

        qd, kd, vd, qn, kc, vc, ks, vs, kw, vw, gx = prep(z, cos128, sin128, 256)
        o_b = diff_attention(qd, kd, vd, da_lambda[l], row(da_subln[l]), lambda_init, 512, 512)

        x16 = jnp.stack([kc, vc]).reshape(2, T, G, NSA_HEAD_DIM).transpose(0, 2, 1, 3)
        x16 = x16.reshape(2, G, n16, NSA_CMP_STRIDE * NSA_HEAD_DIM)
        pe2 = nsa_cmp_pe[l].reshape(2, 2, NSA_CMP_STRIDE * NSA_HEAD_DIM)
        w2p = jnp.pad(nsa_cmp_w2[l], ((0, 0), (0, 0), (0, LANES - NSA_HEAD_DIM))).astype(BF16)
        kvc = compress(x16, pe2, nsa_cmp_w1[l].astype(BF16), w2p)
        o_cmp, selbias = cmp_select(qn, kvc, overlap_t, gx, 256)
        o_sel = sel_attention(qn, selbias, ks, vs, gx, 256, 512)
        o_win = win_attention(qn, kw, vw, gx, 256)

        g_ffn = row(ffn_norm[l])
        parts = (o_a, o_b, o_c, o_cmp, o_sel, o_win)
        e = l // 2
        if l % 2 == 0:
            xs, hn = out_proj(parts, w_out_b, l, xs, g_ffn, 256)
            dense_te = jnp.full((T // tm,), e, jnp.int32)
            dense_nx = jnp.full((T // tm,), -1, jnp.int32)
            dense_nv = jnp.full((T // tm,), tm, jnp.int32)
            act = swiglu_up(dense_te, dense_nx, dense_nv, dense_na, hn, ffn_wg, ffn_wu, tm, UP_TN)
            xs = swiglu_down(dense_te, dense_nx, dense_nv, dense_na, act, ffn_wd, xs, tm, 512)
            if l == depth - 1:
                xs = final_norm(xs, row(final_norm_g), tm)
        else:
            w_r = jnp.pad(router_w[e], ((0, 0), (0, LANES - N_EXPERTS)))
            xs, = out_proj(parts, w_out_b, l, xs, None, 256)
            route = router(xs, g_ffn, w_r, tm)
            tile_expert, next_expert, valid_rows, n_active, tok_of_slot, slot0, slot1 = _routing_tables(
                route, tm, e * N_EXPERTS)
            hg = gather_norm_tokens(tok_of_slot, n_active, xs, g_ffn, tm)
            act = swiglu_up(tile_expert, next_expert, valid_rows, n_active, hg, exp_wg, exp_wu, tm, UP_TN)
            ys = swiglu_down(tile_expert, next_expert, valid_rows, n_active, act, exp_wd, None, tm, 1024)
            xs = moe_combine(slot0, slot1, xs, ys, route, row(final_norm_g), 256, final=(l == depth - 1))
    return xs[None]
```

```python
import functools
import math

import jax
import jax.numpy as jnp
from jax import lax
from jax.experimental import pallas as pl
from jax.experimental.pallas import tpu as pltpu

F32 = jnp.float32
BF16 = jnp.bfloat16

D_MODEL = 2048
GROUP_WIDTH = 512
GM_CHUNK = 128
GM_HEADS = 4
DA_HEADS = 4
DA_QK_DIM = 64
CV_WIDTH = 31
NSA_HEADS = 8
NSA_KV_HEADS = 2
NSA_HEAD_DIM = 64
NSA_CMP_LEN = 32
NSA_CMP_STRIDE = 16
NSA_SLC_LEN = 64
NSA_SLC_TOPK = 16
NSA_WINDOW = 512
ROPE_THETA = 10000.0
NORM_EPS = 1e-6
NEG = -1e30
NSA_FORCED = 3
N_EXPERTS = 8
LANES = 128
SUBLANES = 8
LOG2E = math.log2(math.e)

IN_WIDTH = 4888
IN_WIDTH_PAD = 4992
COL_GM = 0
COL_QDA = 1024
COL_KDA = 1536
COL_VDA = 2048
COL_CV = 2560
COL_QNS = 3584
COL_KVNS = 4096
COL_GNS = 4864

VMEM_LIMIT = 56 * 1024 * 1024


def _cparams(sem, **kw):
    return pltpu.CompilerParams(dimension_semantics=sem, vmem_limit_bytes=VMEM_LIMIT, **kw)


def _rms(x, g):
    ms = jnp.mean(x * x, axis=-1, keepdims=True)
    return x * lax.rsqrt(ms + NORM_EPS) * g


def _layer_norm(x, g, b):
    mu = jnp.mean(x, axis=-1, keepdims=True)
    xc = x - mu
    var = jnp.mean(xc * xc, axis=-1, keepdims=True)
    return xc * lax.rsqrt(var + NORM_EPS) * g + b


def _dot(a, b):
    return jnp.dot(a, b, preferred_element_type=F32)


def _dot_nt(a, b):
    return lax.dot_general(a, b, (((1,), (1,)), ((), ())), preferred_element_type=F32)


def _lane(shape):
    return lax.broadcasted_iota(jnp.int32, shape, len(shape) - 1)


def _row(shape):
    return lax.broadcasted_iota(jnp.int32, shape, len(shape) - 2)


def _norm_mm_body(x_ref, g_ref, w_ref, o_ref, hn_ref):
    @pl.when(pl.program_id(1) == 0)
    def _():
        hn_ref[...] = _rms(x_ref[...], g_ref[...]).astype(BF16)

    o_ref[...] = _dot(hn_ref[...], w_ref[0]).astype(o_ref.dtype)


def norm_mm(x, g, w, layer, tm, tn):
    T, K = x.shape
    N = w.shape[2]
    return pl.pallas_call(
        _norm_mm_body,
        grid=(T // tm, N // tn),
        in_specs=[pl.BlockSpec((tm, K), lambda i, j: (i, 0)),
                  pl.BlockSpec((1, K), lambda i, j: (0, 0)),
                  pl.BlockSpec((1, K, tn), lambda i, j: (layer, 0, j))],
        out_specs=pl.BlockSpec((tm, tn), lambda i, j: (i, j)),
        out_shape=jax.ShapeDtypeStruct((T, N), BF16),
        scratch_shapes=[pltpu.VMEM((tm, K), BF16)],
        compiler_params=_cparams(("parallel", "arbitrary")),
        name="norm_in_proj",
    )(x, g, w)


def _gmlp_body(z_ref, g_ref, b_ref, ws_ref, bs_ref, o_ref):
    tr = z_ref.shape[0]
    z = jax.nn.gelu(z_ref[...].astype(F32))
    u = z[:, :GROUP_WIDTH]
    v = _layer_norm(z[:, GROUP_WIDTH:], g_ref[...], b_ref[...]).astype(BF16)
    causal = _row((GM_CHUNK, GM_CHUNK)) >= _lane((GM_CHUNK, GM_CHUNK))
    bias = bs_ref[...]
    for h in range(GM_HEADS):
        w = jnp.where(causal, ws_ref[h], 0.0).astype(BF16)
        cols = slice(h * LANES, (h + 1) * LANES)
        for c in range(tr // GM_CHUNK):
            rows = slice(c * GM_CHUNK, (c + 1) * GM_CHUNK)
            s = _dot(w, v[rows, cols]) + bias[:, cols]
            o_ref[rows, cols] = (u[rows, cols] * s).astype(BF16)


def gmlp(z, ln_g, ln_b, ws, bs_rows, tr):
    T = z.shape[0]
    return pl.pallas_call(
        _gmlp_body,
        grid=(T // tr,),
        in_specs=[pl.BlockSpec((tr, 2 * GROUP_WIDTH), lambda i: (i, COL_GM // (2 * GROUP_WIDTH))),
                  pl.BlockSpec((1, GROUP_WIDTH), lambda i: (0, 0)),
                  pl.BlockSpec((1, GROUP_WIDTH), lambda i: (0, 0)),
                  pl.BlockSpec((GM_HEADS, GM_CHUNK, GM_CHUNK), lambda i: (0, 0, 0)),
                  pl.BlockSpec((GM_CHUNK, GROUP_WIDTH), lambda i: (0, 0))],
        out_specs=pl.BlockSpec((tr, GROUP_WIDTH), lambda i: (i, 0)),
        out_shape=jax.ShapeDtypeStruct((T, GROUP_WIDTH), BF16),
        compiler_params=_cparams(("parallel",)),
        name="gmlp",
    )(z, ln_g, ln_b, ws, bs_rows)


CV_HALO = 32
CV_SUB = 64


def _conv_body(a_ref, g_ref, ap_ref, gp_ref, w_ref, b_ref, lg_ref, lb_ref, o_ref, hs_ref):
    tr = a_ref.shape[0]
    first = pl.program_id(0) == 0
    prev = ap_ref[...].astype(F32) * jax.nn.sigmoid(gp_ref[...].astype(F32))
    hs_ref[0:CV_HALO, :] = jnp.where(first, 0.0, prev)
    hs_ref[CV_HALO:CV_HALO + tr, :] = a_ref[...].astype(F32) * jax.nn.sigmoid(g_ref[...].astype(F32))
    hs_ref[CV_HALO + tr:, :] = jnp.zeros((SUBLANES, GROUP_WIDTH), F32)
    w = w_ref[...]
    lead = CV_HALO - (CV_WIDTH - 1)
    for r0 in range(0, tr, CV_SUB):
        acc = jnp.zeros((CV_SUB, GROUP_WIDTH), F32)
        for phase in range(SUBLANES):
            base, shift = divmod(lead + phase, SUBLANES)
            part = jnp.zeros((CV_SUB + SUBLANES, GROUP_WIDTH), F32)
            for k in range(phase, CV_WIDTH, SUBLANES):
                start = r0 + SUBLANES * (base + k // SUBLANES)
                part = part + hs_ref[start:start + CV_SUB + SUBLANES, :] * w[k:k + 1, :]
            acc = acc + part[shift:shift + CV_SUB]
        y = _layer_norm(acc + b_ref[...], lg_ref[...], lb_ref[...])
        o_ref[r0:r0 + CV_SUB, :] = (y * jax.nn.sigmoid(y)).astype(BF16)


def conformer_conv(z, dw_w, dw_b, ln_g, ln_b, tr):
    T = z.shape[0]
    ca = COL_CV // GROUP_WIDTH
    per = tr // CV_HALO

    def halo(col):
        return lambda i: (jnp.maximum(i * per - 1, 0), col)

    return pl.pallas_call(
        _conv_body,
        grid=(T // tr,),
        in_specs=[pl.BlockSpec((tr, GROUP_WIDTH), lambda i: (i, ca)),
                  pl.BlockSpec((tr, GROUP_WIDTH), lambda i: (i, ca + 1)),
                  pl.BlockSpec((CV_HALO, GROUP_WIDTH), halo(ca)),
                  pl.BlockSpec((CV_HALO, GROUP_WIDTH), halo(ca + 1)),
                  pl.BlockSpec((CV_HALO, GROUP_WIDTH), lambda i: (0, 0)),
                  pl.BlockSpec((1, GROUP_WIDTH), lambda i: (0, 0)),
                  pl.BlockSpec((1, GROUP_WIDTH), lambda i: (0, 0)),
                  pl.BlockSpec((1, GROUP_WIDTH), lambda i: (0, 0))],
        out_specs=pl.BlockSpec((tr, GROUP_WIDTH), lambda i: (i, 0)),
        out_shape=jax.ShapeDtypeStruct((T, GROUP_WIDTH), BF16),
        scratch_shapes=[pltpu.VMEM((CV_HALO + tr + SUBLANES, GROUP_WIDTH), F32)],
        compiler_params=_cparams(("parallel",)),
        name="conformer_conv",
    )(z, z, z, z, dw_w, dw_b, ln_g, ln_b)


def _rope128(x, cos, sin_signed):
    x = x.astype(F32)
    lo = (_lane(x.shape) & 63) < 32
    rot = jnp.where(lo, pltpu.roll(x, 96, 1), pltpu.roll(x, 32, 1))
    return x * cos + rot * sin_signed


def _low_half(x, fill=0.0):
    return jnp.where(_lane(x.shape) < 64, x, fill)


def _high_half_to_low(x, fill=0.0):
    return jnp.where(_lane(x.shape) < 64, pltpu.roll(x, 64, 1), fill)


def _prep_body(qd_ref, kd_ref, vd_ref, qn_ref, kvc_ref, kvs_ref, kvw_ref, gt_ref, cos_ref, sin_ref, ge_ref,
               qd_o, kd_o, vd_o, qn_o, kc_o, vc_o, ks_o, vs_o, kw_o, vw_o, gx_o):
    tr = cos_ref.shape[0]
    cos = cos_ref[...]
    sin = sin_ref[...]
    q_scale = DA_QK_DIM ** -0.5 * LOG2E
    ones = jnp.ones((tr, LANES), BF16)
    for h in range(DA_HEADS):
        cols = slice(h * LANES, (h + 1) * LANES)
        qd_o[h] = (_rope128(qd_ref[:, cols], cos, sin) * q_scale).astype(BF16)
        kd_o[h] = _rope128(kd_ref[:, cols], cos, sin).astype(BF16)
        vd_o[h, :, 0:LANES] = vd_ref[:, cols].astype(BF16)
        vd_o[h, :, LANES:] = ones
    for c in range(NSA_HEADS // 2):
        q = _rope128(qn_ref[:, c * LANES:(c + 1) * LANES], cos, sin) * (NSA_HEAD_DIM ** -0.5 * LOG2E)
        qn_o[2 * c] = _low_half(q).astype(BF16)
        qn_o[2 * c + 1] = _high_half_to_low(q).astype(BF16)
    kc_o[...] = _rope128(kvc_ref[:, :LANES], cos, sin)
    vc_o[...] = kvc_ref[:, LANES:].astype(F32)
    blk = (pl.program_id(0) * tr + _row((tr, LANES))) >> 6
    onehot = jnp.where(blk == _lane((tr, LANES)), 1.0, 0.0).astype(BF16)
    k = _rope128(kvs_ref[:, :LANES], cos, sin)
    for g, half in enumerate((_low_half, _high_half_to_low)):
        ks_o[g, :, 0:LANES] = onehot
        ks_o[g, :, LANES:] = half(k).astype(BF16)
        vs_o[g] = half(kvs_ref[:, LANES:].astype(F32), 1.0).astype(BF16)
    k = _rope128(kvw_ref[:, :LANES], cos, sin)
    for g, half in enumerate((_low_half, _high_half_to_low)):
        kw_o[g] = half(k).astype(BF16)
        vw_o[g] = half(kvw_ref[:, LANES:].astype(F32), 1.0).astype(BF16)
    expand = ge_ref[...]
    gx = sum(_dot(part, expand) for part in _split3(jax.nn.sigmoid(gt_ref[...].astype(F32))))
    for c in range(3):
        gx_o[c] = gx[:, c * GROUP_WIDTH:(c + 1) * GROUP_WIDTH]


def _gate_expansion():
    col = jnp.arange(3 * GROUP_WIDTH)
    src = (col // GROUP_WIDTH) * NSA_HEADS + (col % GROUP_WIDTH) // NSA_HEAD_DIM
    return (jnp.arange(LANES)[:, None] == src[None, :]).astype(BF16)


def prep(z, cos128, sin128, tr):
    T = z.shape[0]
    G = NSA_KV_HEADS

    def zspec(width, col):
        return pl.BlockSpec((tr, width), lambda i: (i, col // width))

    def heads(n, width=LANES):
        return pl.BlockSpec((n, tr, width), lambda i: (0, i, 0))

    def hshape(n, width=LANES):
        return jax.ShapeDtypeStruct((n, T, width), BF16)

    row128 = pl.BlockSpec((tr, LANES), lambda i: (i, 0))
    return pl.pallas_call(
        _prep_body,
        grid=(T // tr,),
        in_specs=[zspec(512, COL_QDA), zspec(512, COL_KDA), zspec(512, COL_VDA), zspec(512, COL_QNS),
                  zspec(256, COL_KVNS), zspec(256, COL_KVNS + 256), zspec(256, COL_KVNS + 512),
                  zspec(128, COL_GNS), row128, row128,
                  pl.BlockSpec((LANES, 3 * GROUP_WIDTH), lambda i: (0, 0))],
        out_specs=[heads(DA_HEADS), heads(DA_HEADS), heads(DA_HEADS, 2 * LANES),
                   heads(NSA_HEADS), row128, row128, heads(G, 2 * LANES), heads(G), heads(G), heads(G),
                   pl.BlockSpec((3, tr, GROUP_WIDTH), lambda i: (0, i, 0))],
        out_shape=[hshape(DA_HEADS), hshape(DA_HEADS), hshape(DA_HEADS, 2 * LANES),
                   hshape(NSA_HEADS), jax.ShapeDtypeStruct((T, LANES), F32),
                   jax.ShapeDtypeStruct((T, LANES), F32), hshape(G, 2 * LANES), hshape(G), hshape(G), hshape(G),
                   jax.ShapeDtypeStruct((3, T, GROUP_WIDTH), F32)],
        compiler_params=_cparams(("parallel",)),
        name="attention_prep",
    )(z, z, z, z, z, z, z, z, cos128, sin128, _gate_expansion())


def _lane_tile(x, n):
    return x if n == 1 else jnp.concatenate([x] * n, axis=-1)


def _softmax_update(s, v, m_ref, acc_ref):
    m_prev = m_ref[...]
    m_new = jnp.maximum(m_prev, jnp.max(s, axis=-1, keepdims=True))
    alpha = jnp.exp2(m_prev - m_new)
    p = jnp.exp2(s - _lane_tile(m_new, s.shape[1] // LANES))
    acc_ref[...] = _lane_tile(alpha, acc_ref.shape[1] // LANES) * acc_ref[...] + _dot(p.astype(BF16), v)
    m_ref[...] = m_new


SWEEP_UNROLL = 4


def _causal_sweep(qs_ref, k_ref, v_ref, m_ref, acc_ref, s_ref, q0, tq, tk):
    m_ref[...] = jnp.full(m_ref.shape, -jnp.inf, F32)
    acc_ref[...] = jnp.zeros(acc_ref.shape, F32)

    def keys(ref, t):
        return ref[0, pl.ds(pl.multiple_of(t * tk, tk), tk), :]

    def scores(t):
        return _dot_nt(qs_ref[...], keys(k_ref, t))

    n_full = q0 // tk
    s_ref[...] = scores(0)

    def run(t0, count):
        s_cur = s_ref[...]
        for u in range(count):
            s_next = scores(t0 + u + 1)
            _softmax_update(s_cur, keys(v_ref, t0 + u), m_ref, acc_ref)
            s_cur = s_next
        s_ref[...] = s_cur

    def several(i, c):
        run(SWEEP_UNROLL * i, SWEEP_UNROLL)
        return c

    def single(t, c):
        run(t, 1)
        return c

    n_groups = n_full // SWEEP_UNROLL
    lax.fori_loop(0, n_groups, several, 0)
    lax.fori_loop(n_groups * SWEEP_UNROLL, n_full, single, 0)
    s = s_ref[...]
    visible = n_full * tk + _lane(s.shape) <= q0 + (_row(s.shape) & (tq - 1))
    _softmax_update(jnp.where(visible, s, NEG), keys(v_ref, n_full), m_ref, acc_ref)


def _diff_attn_body(q_ref, k_ref, v_ref, lam_ref, sg_ref, o_ref, qs_ref, m_ref, acc_ref, s_ref, *,
                    tq, tk, lambda_init):
    q = q_ref[0]
    first = _lane(q.shape) < DA_QK_DIM
    qs_ref[0:tq, :] = jnp.where(first, q, jnp.zeros_like(q))
    qs_ref[tq:, :] = jnp.where(first, jnp.zeros_like(q), q)
    _causal_sweep(qs_ref, k_ref, v_ref, m_ref, acc_ref, s_ref, pl.program_id(1) * tq, tq, tk)
    lam = lam_ref[...]
    lam_full = (jnp.exp(jnp.sum(lam[0:1] * lam[1:2], axis=-1, keepdims=True))
                - jnp.exp(jnp.sum(lam[2:3] * lam[3:4], axis=-1, keepdims=True)) + lambda_init)
    o = acc_ref[:, 0:LANES] / acc_ref[:, LANES:]
    a = o[0:tq] - lam_full * o[tq:]
    o_ref[...] = (_rms(a, sg_ref[...]) * (1.0 - lambda_init)).astype(BF16)


def diff_attention(qd, kd, vd, lam, subln, lambda_init, tq, tk):
    H, T, _ = qd.shape
    return pl.pallas_call(
        functools.partial(_diff_attn_body, tq=tq, tk=tk, lambda_init=lambda_init),
        grid=(H, T // tq),
        in_specs=[pl.BlockSpec((1, tq, LANES), lambda h, qi: (h, qi, 0)),
                  pl.BlockSpec((1, T, LANES), lambda h, qi: (h, 0, 0)),
                  pl.BlockSpec((1, T, 2 * LANES), lambda h, qi: (h, 0, 0)),
                  pl.BlockSpec((4, DA_QK_DIM), lambda h, qi: (0, 0)),
                  pl.BlockSpec((1, LANES), lambda h, qi: (0, 0))],
        out_specs=pl.BlockSpec((tq, LANES), lambda h, qi: (qi, h)),
        out_shape=jax.ShapeDtypeStruct((T, GROUP_WIDTH), BF16),
        scratch_shapes=[pltpu.VMEM((2 * tq, LANES), BF16), pltpu.VMEM((2 * tq, LANES), F32),
                        pltpu.VMEM((2 * tq, 2 * LANES), F32), pltpu.VMEM((2 * tq, tk), F32)],
        compiler_params=_cparams(("parallel", "arbitrary")),
        name="diff_attention",
    )(qd, kd, vd, lam, subln)


def _compress_body(x_ref, pe_ref, w1_ref, w2_ref, o_ref):
    x = x_ref[0, 0]
    half = x.shape[1]
    a = _dot((x + pe_ref[0, 0:1, :]).astype(BF16), w1_ref[0, 0:half, :])
    b = _dot((x + pe_ref[0, 1:2, :]).astype(BF16), w1_ref[0, half:, :])
    hid = jax.nn.gelu(a + pltpu.roll(b, b.shape[0] - 1, 0))
    o_ref[0, 0] = _dot(hid.astype(BF16), w2_ref[0]).astype(BF16)


def compress(x16, pe2, w1, w2p):
    _, G, n, half = x16.shape
    hid = w1.shape[2]
    return pl.pallas_call(
        _compress_body,
        grid=(2, G),
        in_specs=[pl.BlockSpec((1, 1, n, half), lambda c, g: (c, g, 0, 0)),
                  pl.BlockSpec((1, 2, half), lambda c, g: (c, 0, 0)),
                  pl.BlockSpec((1, 2 * half, hid), lambda c, g: (c, 0, 0)),
                  pl.BlockSpec((1, hid, LANES), lambda c, g: (c, 0, 0))],
        out_specs=pl.BlockSpec((1, 1, n, LANES), lambda c, g: (c, g, 0, 0)),
        out_shape=jax.ShapeDtypeStruct((2, G, n, LANES), BF16),
        compiler_params=_cparams(("parallel", "parallel")),
        name="nsa_compress",
    )(x16, pe2, w1, w2p)


def _split3(x):
    hi = x.astype(BF16)
    r1 = x - hi.astype(F32)
    mid = r1.astype(BF16)
    lo = (r1 - mid.astype(F32)).astype(BF16)
    return hi, mid, lo


def _pack_heads(o, tq):
    pair = lambda a, b: a + pltpu.roll(b, 64, 1)
    return jnp.concatenate([pair(o[0:tq], o[tq:2 * tq]), pair(o[2 * tq:3 * tq], o[3 * tq:])], axis=-1)


def _normalize_low_half(acc):
    return jnp.where(_lane(acc.shape) < 64, acc / pltpu.roll(acc, 64, 1), 0.0)


def _cmp_select_body(q_ref, kc_ref, vc_ref, ovt_ref, gx_ref, o_ref, sb_ref, *, tq):
    qi = pl.program_id(1)
    hpg = q_ref.shape[0]
    q = q_ref[...].reshape(hpg * tq, LANES)
    s = _dot_nt(q, kc_ref[0, 0])
    t = qi * tq + (_row(s.shape) & (tq - 1))
    cmask = _lane(s.shape) * NSA_CMP_STRIDE + (NSA_CMP_LEN - 1) <= t
    s = jnp.where(cmask, s, NEG)
    e = jnp.exp2(s - jnp.max(s, axis=-1, keepdims=True))
    p = jnp.where(cmask, e / jnp.sum(e, axis=-1, keepdims=True), 0.0)
    o = _dot(p.astype(BF16), vc_ref[0, 0])
    o_ref[...] = (gx_ref[0] * _pack_heads(o, tq)).astype(BF16)

    psum = p[0:tq]
    for hh in range(1, hpg):
        psum = psum + p[hh * tq:(hh + 1) * tq]
    ovt = ovt_ref[...]
    imp = sum(_dot_nt(ovt, part) for part in _split3(psum))
    j = _row(imp.shape)
    cur = (qi * tq + _lane(imp.shape)) >> 6
    forced = (j == 0) | (j == cur) | (j == cur - 1)
    score = jnp.where(forced, -2.0, jnp.where(j <= cur, imp, -1.0))
    jf = j.astype(F32)
    bias = jnp.where(forced, 0.0, NEG)
    for _ in range(NSA_SLC_TOPK - NSA_FORCED):
        m = jnp.max(score, axis=0, keepdims=True)
        first = jnp.min(jnp.where(score == m, jf, float(LANES)), axis=0, keepdims=True)
        hit = jf == first
        bias = jnp.where(hit, jnp.where(m >= 0.0, 0.0, NEG), bias)
        score = jnp.where(hit, -2.0, score)
    sb_ref[0] = bias.T.astype(BF16)


def cmp_select(qn, kvc, overlap_t, gx, tq):
    H, T, _ = qn.shape
    G = NSA_KV_HEADS
    hpg = H // G
    n = kvc.shape[2]
    return pl.pallas_call(
        functools.partial(_cmp_select_body, tq=tq),
        grid=(G, T // tq),
        in_specs=[pl.BlockSpec((hpg, tq, LANES), lambda g, qi: (g, qi, 0)),
                  pl.BlockSpec((1, 1, n, LANES), lambda g, qi: (0, g, 0, 0)),
                  pl.BlockSpec((1, 1, n, LANES), lambda g, qi: (1, g, 0, 0)),
                  pl.BlockSpec((LANES, n), lambda g, qi: (0, 0)),
                  pl.BlockSpec((1, tq, hpg * NSA_HEAD_DIM), lambda g, qi: (0, qi, g))],
        out_specs=[pl.BlockSpec((tq, hpg * NSA_HEAD_DIM), lambda g, qi: (qi, g)),
                   pl.BlockSpec((1, tq, LANES), lambda g, qi: (g, qi, 0))],
        out_shape=[jax.ShapeDtypeStruct((T, GROUP_WIDTH), BF16),
                   jax.ShapeDtypeStruct((G, T, LANES), BF16)],
        compiler_params=_cparams(("parallel", "parallel")),
        name="nsa_compressed_select",
    )(qn, kvc, kvc, overlap_t, gx)


def _sel_attn_body(q_ref, sb_ref, k_ref, v_ref, gx_ref, o_ref, qs_ref, m_ref, acc_ref, s_ref, *, tq, tk):
    hpg = q_ref.shape[0]
    for hh in range(hpg):
        qs_ref[hh * tq:(hh + 1) * tq, 0:LANES] = sb_ref[0]
        qs_ref[hh * tq:(hh + 1) * tq, LANES:] = q_ref[hh]
    _causal_sweep(qs_ref, k_ref, v_ref, m_ref, acc_ref, s_ref, pl.program_id(1) * tq, tq, tk)
    o = _normalize_low_half(acc_ref[...])
    o_ref[...] = (gx_ref[0] * _pack_heads(o, tq)).astype(BF16)


def sel_attention(qn, selbias, ks, vs, gx, tq, tk):
    H, T, _ = qn.shape
    G = NSA_KV_HEADS
    hpg = H // G
    return pl.pallas_call(
        functools.partial(_sel_attn_body, tq=tq, tk=tk),
        grid=(G, T // tq),
        in_specs=[pl.BlockSpec((hpg, tq, LANES), lambda g, qi: (g, qi, 0)),
                  pl.BlockSpec((1, tq, LANES), lambda g, qi: (g, qi, 0)),
                  pl.BlockSpec((1, T, 2 * LANES), lambda g, qi: (g, 0, 0)),
                  pl.BlockSpec((1, T, LANES), lambda g, qi: (g, 0, 0)),
                  pl.BlockSpec((1, tq, hpg * NSA_HEAD_DIM), lambda g, qi: (1, qi, g))],
        out_specs=pl.BlockSpec((tq, hpg * NSA_HEAD_DIM), lambda g, qi: (qi, g)),
        out_shape=jax.ShapeDtypeStruct((T, GROUP_WIDTH), BF16),
        scratch_shapes=[pltpu.VMEM((hpg * tq, 2 * LANES), BF16), pltpu.VMEM((hpg * tq, LANES), F32),
                        pltpu.VMEM((hpg * tq, LANES), F32), pltpu.VMEM((hpg * tq, tk), F32)],
        compiler_params=_cparams(("parallel", "arbitrary")),
        name="nsa_selected_attention",
    )(qn, selbias, ks, vs, gx)


def _win_attn_body(q_ref, k_ref, v_ref, b_ref, gx_ref, o_ref, *, tq, span):
    hpg = q_ref.shape[0]
    q0 = pl.program_id(1) * tq
    lo = pl.multiple_of(jnp.maximum(q0 + tq - span, 0), tq)
    q = q_ref[...].reshape(hpg * tq, LANES)
    s = _dot_nt(q, k_ref[0, pl.ds(lo, span), :])

    def finish(s):
        p = jnp.exp2(s - jnp.max(s, axis=-1, keepdims=True))
        acc = _dot(p.astype(BF16), v_ref[0, pl.ds(lo, span), :])
        o_ref[...] = (gx_ref[0] * _pack_heads(_normalize_low_half(acc), tq)).astype(BF16)

    @pl.when(q0 + tq >= span)
    def _():
        finish(s + jnp.concatenate([b_ref[...]] * hpg, axis=0))

    @pl.when(q0 + tq < span)
    def _():
        qpos = q0 + (_row(s.shape) & (tq - 1))
        kpos = lo + _lane(s.shape)
        finish(jnp.where((kpos <= qpos) & (kpos > qpos - NSA_WINDOW), s, NEG))


def _window_bias(tq, span):
    d = jnp.arange(span)[None, :] - jnp.arange(tq)[:, None] + (tq - span)
    return jnp.where((d <= 0) & (d > -NSA_WINDOW), 0.0, NEG).astype(F32)


def win_attention(qn, kw, vw, gx, tq):
    H, T, _ = qn.shape
    G = NSA_KV_HEADS
    hpg = H // G
    span = NSA_WINDOW + tq
    return pl.pallas_call(
        functools.partial(_win_attn_body, tq=tq, span=span),
        grid=(G, T // tq),
        in_specs=[pl.BlockSpec((hpg, tq, LANES), lambda g, qi: (g, qi, 0)),
                  pl.BlockSpec((1, T, LANES), lambda g, qi: (g, 0, 0)),
                  pl.BlockSpec((1, T, LANES), lambda g, qi: (g, 0, 0)),
                  pl.BlockSpec((tq, span), lambda g, qi: (0, 0)),
                  pl.BlockSpec((1, tq, hpg * NSA_HEAD_DIM), lambda g, qi: (2, qi, g))],
        out_specs=pl.BlockSpec((tq, hpg * NSA_HEAD_DIM), lambda g, qi: (qi, g)),
        out_shape=jax.ShapeDtypeStruct((T, GROUP_WIDTH), BF16),
        compiler_params=_cparams(("parallel", "arbitrary")),
        name="nsa_window_attention",
    )(qn, kw, vw, _window_bias(tq, span), gx)


def _top2_route(h, w_router):
    logits = jnp.dot(h, w_router, preferred_element_type=F32, precision=lax.Precision.HIGHEST)
    lane = _lane(logits.shape)
    lf = lane.astype(F32)
    logits = jnp.where(lane < N_EXPERTS, logits, -jnp.inf)
    v0 = jnp.max(logits, axis=-1, keepdims=True)
    i0 = jnp.min(jnp.where(logits == v0, lf, float(LANES)), axis=-1, keepdims=True)
    rest = jnp.where(lf == i0, -jnp.inf, logits)
    v1 = jnp.max(rest, axis=-1, keepdims=True)
    i1 = jnp.min(jnp.where(rest == v1, lf, float(LANES)), axis=-1, keepdims=True)
    e1 = jnp.exp(v1 - v0)
    w0 = 1.0 / (1.0 + e1)
    w1 = e1 / (1.0 + e1)
    return jnp.where(lane == 0, i0, jnp.where(lane == 1, i1, jnp.where(lane == 2, w0, w1)))


def _out_proj_body(a_ref, b_ref, c_ref, d1_ref, d2_ref, d3_ref, w_ref, x_ref, *rest, normed):
    o_ref, lhs_ref = rest[-3 if normed else -2], rest[-1]
    lhs_ref[:, 0:GROUP_WIDTH] = a_ref[...]
    lhs_ref[:, GROUP_WIDTH:2 * GROUP_WIDTH] = b_ref[...]
    lhs_ref[:, 2 * GROUP_WIDTH:3 * GROUP_WIDTH] = c_ref[...]
    d = d1_ref[...].astype(F32) + d2_ref[...].astype(F32) + d3_ref[...].astype(F32)
    lhs_ref[:, 3 * GROUP_WIDTH:] = d.astype(BF16)
    y = x_ref[...] + _dot(lhs_ref[...], w_ref[0])
    o_ref[...] = y
    if normed:
        g_ref, hn_ref = rest[0], rest[2]
        hn_ref[...] = _rms(y, g_ref[...]).astype(BF16)


def out_proj(parts, w, layer, x, g_next, tm):
    T, N = x.shape
    part = pl.BlockSpec((tm, GROUP_WIDTH), lambda i: (i, 0))
    rows = pl.BlockSpec((tm, N), lambda i: (i, 0))
    in_specs = [part] * 6 + [pl.BlockSpec((1, 4 * GROUP_WIDTH, N), lambda i: (layer, 0, 0)), rows]
    out_specs = [rows]
    out_shape = [jax.ShapeDtypeStruct((T, N), F32)]
    operands = [*parts, w, x]
    if g_next is not None:
        in_specs.append(pl.BlockSpec((1, N), lambda i: (0, 0)))
        out_specs.append(rows)
        out_shape.append(jax.ShapeDtypeStruct((T, N), BF16))
        operands.append(g_next)
    return pl.pallas_call(
        functools.partial(_out_proj_body, normed=g_next is not None),
        grid=(T // tm,),
        in_specs=in_specs,
        out_specs=out_specs,
        out_shape=out_shape,
        scratch_shapes=[pltpu.VMEM((tm, 4 * GROUP_WIDTH), BF16)],
        compiler_params=_cparams(("parallel",)),
        name="out_proj",
    )(*operands)


UP_TN = 1408
UP_CHUNK = 512
CAST_ROWS = 256


def _group_starts(te_ref, i):
    return (i == 0) | (te_ref[i] != te_ref[jnp.maximum(i - 1, 0)])


def _stream_group_weights(te_ref, nx_ref, w_hbms, wst_ref, wb_refs, sem_ref):
    j = pl.program_id(0)
    i = pl.program_id(1)
    tn = wst_ref.shape[2]

    def copies(e, jj):
        cols = pl.ds(pl.multiple_of(jj * tn, tn), tn)
        return [pltpu.make_async_copy(w.at[e, :, cols], wst_ref.at[n], sem_ref) for n, w in enumerate(w_hbms)]

    @pl.when((j == 0) & (i == 0))
    def _():
        for c in copies(te_ref[0], 0):
            c.start()

    @pl.when(_group_starts(te_ref, i))
    def _():
        for c in copies(te_ref[i], j):
            c.wait()
        def cast_rows(c, carry):
            rows = pl.ds(pl.multiple_of(c * CAST_ROWS, CAST_ROWS), CAST_ROWS)
            for n, wb_ref in enumerate(wb_refs):
                wb_ref[rows, :] = wst_ref[n, rows, :].astype(BF16)
            return carry

        lax.fori_loop(0, wst_ref.shape[1] // CAST_ROWS, cast_rows, 0)
        in_sweep = nx_ref[i] >= 0

        @pl.when(in_sweep | (j + 1 < pl.num_programs(0)))
        def _():
            for c in copies(jnp.where(in_sweep, nx_ref[i], te_ref[0]), jnp.where(in_sweep, j, j + 1)):
                c.start()


def _weight_stream_scratch(n_weights, k, tn):
    return [pltpu.VMEM((n_weights, k, tn), F32)] + [pltpu.VMEM((k, tn), BF16)] * n_weights + [
        pltpu.SemaphoreType.DMA(())]


def _by_fill(nv_ref, o_ref, compute):
    nv = nv_ref[pl.program_id(1)]
    half = o_ref.shape[0] // 2

    @pl.when(nv > half)
    def _():
        o_ref[...] = compute(slice(None))

    @pl.when((nv > 0) & (nv <= half))
    def _():
        o_ref[0:half, :] = compute(slice(0, half))
        o_ref[half:, :] = jnp.zeros((half, o_ref.shape[1]), o_ref.dtype)

    @pl.when(nv == 0)
    def _():
        o_ref[...] = jnp.zeros(o_ref.shape, o_ref.dtype)


def _up_body(te_ref, nx_ref, nv_ref, na_ref, h_ref, wg_hbm, wu_hbm, o_ref, wst_ref, wgb_ref, wub_ref, sem_ref):
    _stream_group_weights(te_ref, nx_ref, (wg_hbm, wu_hbm), wst_ref, (wgb_ref, wub_ref), sem_ref)

    def swiglu(rows):
        h = h_ref[rows, :]
        parts = []
        for c0 in range(0, o_ref.shape[1], UP_CHUNK):
            cols = slice(c0, min(c0 + UP_CHUNK, o_ref.shape[1]))
            a = _dot(h, wgb_ref[:, cols])
            parts.append((a * jax.nn.sigmoid(a) * _dot(h, wub_ref[:, cols])).astype(BF16))
        return jnp.concatenate(parts, axis=-1)

    _by_fill(nv_ref, o_ref, swiglu)


def swiglu_up(tile_expert, next_expert, valid_rows, n_active, hn, wg, wu, tm, tn):
    P, K = hn.shape
    F = wg.shape[2]
    return pl.pallas_call(
        _up_body,
        grid_spec=pltpu.PrefetchScalarGridSpec(
            num_scalar_prefetch=4,
            grid=(F // tn, P // tm),
            in_specs=[pl.BlockSpec((tm, K), lambda j, i, te, nx, nv, na: (jnp.minimum(i, na[0] - 1), 0)),
                      pl.BlockSpec(memory_space=pl.ANY), pl.BlockSpec(memory_space=pl.ANY)],
            out_specs=pl.BlockSpec((tm, tn), lambda j, i, te, nx, nv, na: (i, j)),
            scratch_shapes=_weight_stream_scratch(2, K, tn)),
        out_shape=jax.ShapeDtypeStruct((P, F), BF16),
        compiler_params=_cparams(("arbitrary", "arbitrary")),
        name="swiglu_up",
    )(tile_expert, next_expert, valid_rows, n_active, hn, wg, wu)


def _down_body(te_ref, nx_ref, nv_ref, na_ref, a_ref, w_hbm, *rest, residual):
    r_ref = rest[0] if residual else None
    o_ref, wst_ref, wb_ref, sem_ref = rest[1:] if residual else rest
    _stream_group_weights(te_ref, nx_ref, (w_hbm,), wst_ref, (wb_ref,), sem_ref)

    def project(rows):
        y = _dot(a_ref[rows, :], wb_ref[...])
        return r_ref[rows, :] + y if residual else y

    _by_fill(nv_ref, o_ref, project)


def swiglu_down(tile_expert, next_expert, valid_rows, n_active, act, wd, residual, tm, tn):
    P, F = act.shape
    N = wd.shape[2]
    tile = pl.BlockSpec((tm, tn), lambda j, i, te, nx, nv, na: (i, j))
    in_specs = [pl.BlockSpec((tm, F), lambda j, i, te, nx, nv, na: (jnp.minimum(i, na[0] - 1), 0)),
                pl.BlockSpec(memory_space=pl.ANY)]
    operands = [act, wd]
    if residual is not None:
        in_specs.append(tile)
        operands.append(residual)
    return pl.pallas_call(
        functools.partial(_down_body, residual=residual is not None),
        grid_spec=pltpu.PrefetchScalarGridSpec(
            num_scalar_prefetch=4,
            grid=(N // tn, P // tm),
            in_specs=in_specs,
            out_specs=tile,
            scratch_shapes=_weight_stream_scratch(1, F, tn)),
        out_shape=jax.ShapeDtypeStruct((P, N), F32),
        compiler_params=_cparams(("arbitrary", "arbitrary")),
        name="swiglu_down",
    )(tile_expert, next_expert, valid_rows, n_active, *operands)


def _router_body(x_ref, g_ref, w_ref, o_ref):
    o_ref[...] = _top2_route(_rms(x_ref[...], g_ref[...]), w_ref[...])


def router(x, g, w_pad, tm):
    T, K = x.shape
    return pl.pallas_call(
        _router_body,
        grid=(T // tm,),
        in_specs=[pl.BlockSpec((tm, K), lambda i: (i, 0)),
                  pl.BlockSpec((1, K), lambda i: (0, 0)),
                  pl.BlockSpec((K, LANES), lambda i: (0, 0))],
        out_specs=pl.BlockSpec((tm, LANES), lambda i: (i, 0)),
        out_shape=jax.ShapeDtypeStruct((T, LANES), F32),
        compiler_params=_cparams(("parallel",)),
        name="moe_router",
    )(x, g, w_pad)


ROW_DMA_UNROLL = 8


def _row_copy(src_hbm, row, dst_ref, r, sem):
    return pltpu.make_async_copy(src_hbm.at[pl.ds(row, 1)], dst_ref.at[pl.ds(r, 1)], sem)


def _gather_body(tok_ref, na_ref, x_hbm, g_ref, o_ref, buf_ref, sem_ref):
    i = pl.program_id(0)
    tm = buf_ref.shape[1]

    def fetch(tile):
        slot = tile % 2

        def start(r, c):
            _row_copy(x_hbm, tok_ref[tile * tm + r], buf_ref.at[slot], r, sem_ref.at[slot]).start()
            return c

        lax.fori_loop(0, tm, start, 0, unroll=ROW_DMA_UNROLL)

    @pl.when(i == 0)
    def _():
        fetch(0)

    @pl.when(i + 1 < na_ref[0])
    def _():
        fetch(i + 1)

    @pl.when(i < na_ref[0])
    def _():
        slot = i % 2

        def wait(r, c):
            _row_copy(x_hbm, 0, buf_ref.at[slot], r, sem_ref.at[slot]).wait()
            return c

        lax.fori_loop(0, tm, wait, 0, unroll=ROW_DMA_UNROLL)
        o_ref[...] = _rms(buf_ref[slot], g_ref[...]).astype(BF16)

    @pl.when(i >= na_ref[0])
    def _():
        o_ref[...] = jnp.zeros(o_ref.shape, BF16)


def gather_norm_tokens(tok_of_slot, n_active, x, g, tm):
    P = tok_of_slot.shape[0]
    K = x.shape[1]
    return pl.pallas_call(
        _gather_body,
        grid_spec=pltpu.PrefetchScalarGridSpec(
            num_scalar_prefetch=2,
            grid=(P // tm,),
            in_specs=[pl.BlockSpec(memory_space=pl.ANY), pl.BlockSpec((1, K), lambda i, tok, na: (0, 0))],
            out_specs=pl.BlockSpec((tm, K), lambda i, tok, na: (i, 0)),
            scratch_shapes=[pltpu.VMEM((2, tm, K), F32), pltpu.SemaphoreType.DMA((2,))]),
        out_shape=jax.ShapeDtypeStruct((P, K), BF16),
        compiler_params=_cparams(("arbitrary",)),
        name="moe_gather",
    )(tok_of_slot, n_active, x, g)


def _combine_body(s0_ref, s1_ref, x_ref, y_hbm, rt_ref, g_ref, o_ref, b0_ref, b1_ref, sem0, sem1, *, final):
    tm = x_ref.shape[0]
    base = pl.program_id(0) * tm

    def start(r, c):
        _row_copy(y_hbm, s0_ref[base + r], b0_ref, r, sem0).start()
        _row_copy(y_hbm, s1_ref[base + r], b1_ref, r, sem1).start()
        return c

    def wait(r, c):
        _row_copy(y_hbm, 0, b0_ref, r, sem0).wait()
        _row_copy(y_hbm, 0, b1_ref, r, sem1).wait()
        return c

    lax.fori_loop(0, tm, start, 0, unroll=ROW_DMA_UNROLL)
    lax.fori_loop(0, tm, wait, 0, unroll=ROW_DMA_UNROLL)
    gates = rt_ref[...]
    y = x_ref[...] + gates[:, 2:3] * b0_ref[...] + gates[:, 3:4] * b1_ref[...]
    o_ref[...] = _rms(y, g_ref[...]) if final else y


def moe_combine(slot0, slot1, x, ys, route, g, tm, final):
    T, K = x.shape
    return pl.pallas_call(
        functools.partial(_combine_body, final=final),
        grid_spec=pltpu.PrefetchScalarGridSpec(
            num_scalar_prefetch=2,
            grid=(T // tm,),
            in_specs=[pl.BlockSpec((tm, K), lambda i, s0, s1: (i, 0)),
                      pl.BlockSpec(memory_space=pl.ANY),
                      pl.BlockSpec((tm, LANES), lambda i, s0, s1: (i, 0)),
                      pl.BlockSpec((1, K), lambda i, s0, s1: (0, 0))],
            out_specs=pl.BlockSpec((tm, K), lambda i, s0, s1: (i, 0)),
            scratch_shapes=[pltpu.VMEM((tm, K), F32), pltpu.VMEM((tm, K), F32),
                            pltpu.SemaphoreType.DMA(()), pltpu.SemaphoreType.DMA(())]),
        out_shape=jax.ShapeDtypeStruct((T, K), F32),
        compiler_params=_cparams(("arbitrary",)),
        name="moe_combine",
    )(slot0, slot1, x, ys, route, g)


def _final_norm_body(x_ref, g_ref, o_ref):
    o_ref[...] = _rms(x_ref[...], g_ref[...])


def final_norm(x, g, tm):
    T, K = x.shape
    return pl.pallas_call(
        _final_norm_body,
        grid=(T // tm,),
        in_specs=[pl.BlockSpec((tm, K), lambda i: (i, 0)), pl.BlockSpec((1, K), lambda i: (0, 0))],
        out_specs=pl.BlockSpec((tm, K), lambda i: (i, 0)),
        out_shape=jax.ShapeDtypeStruct((T, K), F32),
        compiler_params=_cparams(("parallel",)),
        name="final_norm",
    )(x, g)


def _routing_tables(route, tm, expert_base):
    T = route.shape[0]
    top_i = route[:, 0:2].astype(jnp.int32)
    e_flat = top_i.reshape(-1)
    onehot = (e_flat[:, None] == jnp.arange(N_EXPERTS)[None, :]).astype(jnp.int32)
    rank = jnp.take_along_axis(jnp.cumsum(onehot, axis=0) - onehot, e_flat[:, None], axis=1)[:, 0]
    count = jnp.sum(onehot, axis=0)
    padded = ((count + tm - 1) // tm) * tm
    end = jnp.cumsum(padded)
    start = end - padded
    slot = start[e_flat] + rank
    n_slots = 2 * T + N_EXPERTS * tm
    n_tiles = n_slots // tm
    n_active = (end[-1] // tm).astype(jnp.int32)
    tile_start = jnp.minimum(jnp.arange(n_tiles, dtype=jnp.int32), n_active - 1) * tm
    tile_expert = jnp.minimum(jnp.sum(tile_start[:, None] >= end[None, :], axis=1), N_EXPERTS - 1)
    group_end = end[tile_expert] // tm
    next_expert = jnp.where(group_end < n_active, tile_expert[jnp.minimum(group_end, n_tiles - 1)] + expert_base, -1)
    tile_index = jnp.arange(n_tiles, dtype=jnp.int32)
    valid_rows = jnp.where(tile_index < n_active,
                           jnp.clip(start[tile_expert] + count[tile_expert] - tile_index * tm, 0, tm), 0)
    tok_of_slot = jnp.zeros((n_slots,), jnp.int32).at[slot].set(jnp.arange(2 * T, dtype=jnp.int32) // 2)
    slot2 = slot.reshape(T, 2).astype(jnp.int32)
    return ((tile_expert + expert_base).astype(jnp.int32), next_expert.astype(jnp.int32),
            valid_rows.astype(jnp.int32), n_active.reshape(1), tok_of_slot, slot2[:, 0], slot2[:, 1])


def _rope_tables128(T):
    half = NSA_HEAD_DIM // 2
    inv = ROPE_THETA ** (-jnp.arange(0, NSA_HEAD_DIM, 2, dtype=F32) / NSA_HEAD_DIM)
    ang = jnp.arange(T, dtype=F32)[:, None] * inv[None, :]
    cos, sin = jnp.cos(ang), jnp.sin(ang)
    cos128 = jnp.tile(cos, (1, LANES // half))
    sin128 = jnp.tile(jnp.concatenate([-sin, sin], axis=1), (1, LANES // NSA_HEAD_DIM))
    return cos128, sin128


def _overlap_matrix_t(n_cmp_pad):
    sstart = jnp.arange(LANES) * NSA_SLC_LEN
    cstart = jnp.arange(n_cmp_pad) * NSA_CMP_STRIDE
    ov = (cstart[None, :] < sstart[:, None] + NSA_SLC_LEN) & (cstart[None, :] + NSA_CMP_LEN > sstart[:, None])
    return ov.astype(BF16)


def kernel(x, attn_norm, w_in, w_out, gm_ln_g, gm_ln_b, gm_ws, gm_bs, da_lambda, da_subln, cv_dw_w, cv_dw_b,
           cv_ln_g, cv_ln_b, nsa_cmp_w1, nsa_cmp_w2, nsa_cmp_pe, ffn_norm, ffn_wg, ffn_wu, ffn_wd, router_w,
           exp_wg, exp_wu, exp_wd, final_norm_g):
    B, T, D = x.shape
    assert B == 1 and D == D_MODEL and T % 1024 == 0 and T // NSA_SLC_LEN <= LANES
    depth = w_in.shape[0]
    G = NSA_KV_HEADS
    n16 = T // NSA_CMP_STRIDE
    tm = 512

    cos128, sin128 = _rope_tables128(T)
    overlap_t = _overlap_matrix_t(n16)
    dense_na = jnp.full((1,), T // tm, jnp.int32)
    row = lambda v: v.reshape(1, -1)
    merge = lambda w: w.reshape((-1,) + w.shape[2:])
    exp_wg, exp_wu, exp_wd = merge(exp_wg), merge(exp_wu), merge(exp_wd)
    w_in_b = jnp.pad(w_in, ((0, 0), (0, 0), (0, IN_WIDTH_PAD - IN_WIDTH))).astype(BF16)
    w_out_b = w_out.astype(BF16)

    xs = x[0]
    for l in range(depth):
        lambda_init = 0.8 - 0.6 * math.exp(-0.3 * l)
        z = norm_mm(xs, row(attn_norm[l]), w_in_b, l, 1024, IN_WIDTH_PAD // 3)

        bs_rows = jnp.repeat(gm_bs[l].T, GM_CHUNK, axis=1)
        o_a = gmlp(z, row(gm_ln_g[l]), row(gm_ln_b[l]), gm_ws[l], bs_rows, 512)
        dw_w = jnp.pad(cv_dw_w[l], ((0, CV_HALO - CV_WIDTH), (0, 0)))
        o_c = conformer_conv(z, dw_w, row(cv_dw_b[l]), row(cv_ln_g[l]), row(cv_ln_b[l]), 256)

        qd, kd, vd, qn, kc, vc, ks, vs, kw, vw, gx = prep(z, cos128, sin128, 256)
        o_b = diff_attention(qd, kd, vd, da_lambda[l], row(da_subln[l]), lambda_init, 512, 512)

        x16 = jnp.stack([kc, vc]).reshape(2, T, G, NSA_HEAD_DIM).transpose(0, 2, 1, 3)
        x16 = x16.reshape(2, G, n16, NSA_CMP_STRIDE * NSA_HEAD_DIM)
        pe2 = nsa_cmp_pe[l].reshape(2, 2, NSA_CMP_STRIDE * NSA_HEAD_DIM)
        w2p = jnp.pad(nsa_cmp_w2[l], ((0, 0), (0, 0), (0, LANES - NSA_HEAD_DIM))).astype(BF16)
        kvc = compress(x16, pe2, nsa_cmp_w1[l].astype(BF16), w2p)
        o_cmp, selbias = cmp_select(qn, kvc, overlap_t, gx, 256)
        o_sel = sel_attention(qn, selbias, ks, vs, gx, 256, 512)
        o_win = win_attention(qn, kw, vw, gx, 256)

        g_ffn = row(ffn_norm[l])
        parts = (o_a, o_b, o_c, o_cmp, o_sel, o_win)
        e = l // 2
        if l % 2 == 0:
            xs, hn = out_proj(parts, w_out_b, l, xs, g_ffn, 256)
            dense_te = jnp.full((T // tm,), e, jnp.int32)
            dense_nx = jnp.full((T // tm,), -1, jnp.int32)
            dense_nv = jnp.full((T // tm,), tm, jnp.int32)
            act = swiglu_up(dense_te[::2], dense_nx[::2], 2 * dense_nv[::2], dense_na // 2, hn, ffn_wg, ffn_wu,
                            2 * tm, UP_TN)
            xs = swiglu_down(dense_te, dense_nx, dense_nv, dense_na, act, ffn_wd, xs, tm, 512)
            if l == depth - 1:
                xs = final_norm(xs, row(final_norm_g), tm)
        else:
            w_r = jnp.pad(router_w[e], ((0, 0), (0, LANES - N_EXPERTS)))
            xs, = out_proj(parts, w_out_b, l, xs, None, 256)
            route = router(xs, g_ffn, w_r, tm)
            tile_expert, next_expert, valid_rows, n_active, tok_of_slot, slot0, slot1 = _routing_tables(
                route, tm, e * N_EXPERTS)
            hg = gather_norm_tokens(tok_of_slot, n_active, xs, g_ffn, tm)
            act = swiglu_up(tile_expert, next_expert, valid_rows, n_active, hg, exp_wg, exp_wu, tm, UP_TN)
            ys = swiglu_down(tile_expert, next_expert, valid_rows, n_active, act, exp_wd, None, tm, 1024)
            xs = moe_combine(slot0, slot1, xs, ys, route, row(final_norm_g), 256, final=(l == depth - 1))
    return xs[None]
```

```python
import functools
import math

import jax
import jax.numpy as jnp
from jax import lax
from jax.experimental import pallas as pl
from jax.experimental.pallas import tpu as pltpu

F32 = jnp.float32
BF16 = jnp.bfloat16

D_MODEL = 2048
GROUP_WIDTH = 512
GM_CHUNK = 128
GM_HEADS = 4
DA_HEADS = 4
DA_QK_DIM = 64
CV_WIDTH = 31
NSA_HEADS = 8
NSA_KV_HEADS = 2
NSA_HEAD_DIM = 64
NSA_CMP_LEN = 32
NSA_CMP_STRIDE = 16
NSA_SLC_LEN = 64
NSA_SLC_TOPK = 16
NSA_WINDOW = 512
ROPE_THETA = 10000.0
NORM_EPS = 1e-6
NEG = -1e30
NSA_FORCED = 3
N_EXPERTS = 8
LANES = 128
SUBLANES = 8
LOG2E = math.log2(math.e)

IN_WIDTH = 4888
IN_WIDTH_PAD = 4992
COL_GM = 0
COL_QDA = 1024
COL_KDA = 1536
COL_VDA = 2048
COL_CV = 2560
COL_QNS = 3584
COL_KVNS = 4096
COL_GNS = 4864

VMEM_LIMIT = 56 * 1024 * 1024


def _cparams(sem, **kw):
    return pltpu.CompilerParams(dimension_semantics=sem, vmem_limit_bytes=VMEM_LIMIT, **kw)


def _rms(x, g):
    ms = jnp.mean(x * x, axis=-1, keepdims=True)
    return x * lax.rsqrt(ms + NORM_EPS) * g


def _layer_norm(x, g, b):
    mu = jnp.mean(x, axis=-1, keepdims=True)
    xc = x - mu
    var = jnp.mean(xc * xc, axis=-1, keepdims=True)
    return xc * lax.rsqrt(var + NORM_EPS) * g + b


def _dot(a, b):
    return jnp.dot(a, b, preferred_element_type=F32)


def _dot_nt(a, b):
    return lax.dot_general(a, b, (((1,), (1,)), ((), ())), preferred_element_type=F32)


def _lane(shape):
    return lax.broadcasted_iota(jnp.int32, shape, len(shape) - 1)


def _row(shape):
    return lax.broadcasted_iota(jnp.int32, shape, len(shape) - 2)


def _norm_mm_body(x_ref, g_ref, w_ref, o_ref, hn_ref):
    @pl.when(pl.program_id(1) == 0)
    def _():
        hn_ref[...] = _rms(x_ref[...], g_ref[...]).astype(BF16)

    o_ref[...] = _dot(hn_ref[...], w_ref[0]).astype(o_ref.dtype)


def norm_mm(x, g, w, layer, tm, tn):
    T, K = x.shape
    N = w.shape[2]
    return pl.pallas_call(
        _norm_mm_body,
        grid=(T // tm, N // tn),
        in_specs=[pl.BlockSpec((tm, K), lambda i, j: (i, 0)),
                  pl.BlockSpec((1, K), lambda i, j: (0, 0)),
                  pl.BlockSpec((1, K, tn), lambda i, j: (layer, 0, j))],
        out_specs=pl.BlockSpec((tm, tn), lambda i, j: (i, j)),
        out_shape=jax.ShapeDtypeStruct((T, N), BF16),
        scratch_shapes=[pltpu.VMEM((tm, K), BF16)],
        compiler_params=_cparams(("parallel", "arbitrary")),
        name="norm_in_proj",
    )(x, g, w)


def _gmlp_body(z_ref, g_ref, b_ref, ws_ref, bs_ref, o_ref):
    tr = z_ref.shape[0]
    z = jax.nn.gelu(z_ref[...].astype(F32))
    u = z[:, :GROUP_WIDTH]
    v = _layer_norm(z[:, GROUP_WIDTH:], g_ref[...], b_ref[...]).astype(BF16)
    causal = _row((GM_CHUNK, GM_CHUNK)) >= _lane((GM_CHUNK, GM_CHUNK))
    bias = bs_ref[...]
    for h in range(GM_HEADS):
        w = jnp.where(causal, ws_ref[h], 0.0).astype(BF16)
        cols = slice(h * LANES, (h + 1) * LANES)
        for c in range(tr // GM_CHUNK):
            rows = slice(c * GM_CHUNK, (c + 1) * GM_CHUNK)
            s = _dot(w, v[rows, cols]) + bias[:, cols]
            o_ref[rows, cols] = (u[rows, cols] * s).astype(BF16)


def gmlp(z, ln_g, ln_b, ws, bs_rows, tr):
    T = z.shape[0]
    return pl.pallas_call(
        _gmlp_body,
        grid=(T // tr,),
        in_specs=[pl.BlockSpec((tr, 2 * GROUP_WIDTH), lambda i: (i, COL_GM // (2 * GROUP_WIDTH))),
                  pl.BlockSpec((1, GROUP_WIDTH), lambda i: (0, 0)),
                  pl.BlockSpec((1, GROUP_WIDTH), lambda i: (0, 0)),
                  pl.BlockSpec((GM_HEADS, GM_CHUNK, GM_CHUNK), lambda i: (0, 0, 0)),
                  pl.BlockSpec((GM_CHUNK, GROUP_WIDTH), lambda i: (0, 0))],
        out_specs=pl.BlockSpec((tr, GROUP_WIDTH), lambda i: (i, 0)),
        out_shape=jax.ShapeDtypeStruct((T, GROUP_WIDTH), BF16),
        compiler_params=_cparams(("parallel",)),
        name="gmlp",
    )(z, ln_g, ln_b, ws, bs_rows)


CV_HALO = 32
CV_SUB = 64


def _conv_body(a_ref, g_ref, ap_ref, gp_ref, w_ref, b_ref, lg_ref, lb_ref, o_ref, hs_ref):
    tr = a_ref.shape[0]
    first = pl.program_id(0) == 0
    prev = ap_ref[...].astype(F32) * jax.nn.sigmoid(gp_ref[...].astype(F32))
    hs_ref[0:CV_HALO, :] = jnp.where(first, 0.0, prev)
    hs_ref[CV_HALO:CV_HALO + tr, :] = a_ref[...].astype(F32) * jax.nn.sigmoid(g_ref[...].astype(F32))
    hs_ref[CV_HALO + tr:, :] = jnp.zeros((SUBLANES, GROUP_WIDTH), F32)
    w = w_ref[...]
    lead = CV_HALO - (CV_WIDTH - 1)
    for r0 in range(0, tr, CV_SUB):
        acc = jnp.zeros((CV_SUB, GROUP_WIDTH), F32)
        for phase in range(SUBLANES):
            base, shift = divmod(lead + phase, SUBLANES)
            part = jnp.zeros((CV_SUB + SUBLANES, GROUP_WIDTH), F32)
            for k in range(phase, CV_WIDTH, SUBLANES):
                start = r0 + SUBLANES * (base + k // SUBLANES)
                part = part + hs_ref[start:start + CV_SUB + SUBLANES, :] * w[k:k + 1, :]
            acc = acc + part[shift:shift + CV_SUB]
        y = _layer_norm(acc + b_ref[...], lg_ref[...], lb_ref[...])
        o_ref[r0:r0 + CV_SUB, :] = (y * jax.nn.sigmoid(y)).astype(BF16)


def conformer_conv(z, dw_w, dw_b, ln_g, ln_b, tr):
    T = z.shape[0]
    ca = COL_CV // GROUP_WIDTH
    per = tr // CV_HALO

    def halo(col):
        return lambda i: (jnp.maximum(i * per - 1, 0), col)

    return pl.pallas_call(
        _conv_body,
        grid=(T // tr,),
        in_specs=[pl.BlockSpec((tr, GROUP_WIDTH), lambda i: (i, ca)),
                  pl.BlockSpec((tr, GROUP_WIDTH), lambda i: (i, ca + 1)),
                  pl.BlockSpec((CV_HALO, GROUP_WIDTH), halo(ca)),
                  pl.BlockSpec((CV_HALO, GROUP_WIDTH), halo(ca + 1)),
                  pl.BlockSpec((CV_HALO, GROUP_WIDTH), lambda i: (0, 0)),
                  pl.BlockSpec((1, GROUP_WIDTH), lambda i: (0, 0)),
                  pl.BlockSpec((1, GROUP_WIDTH), lambda i: (0, 0)),
                  pl.BlockSpec((1, GROUP_WIDTH), lambda i: (0, 0))],
        out_specs=pl.BlockSpec((tr, GROUP_WIDTH), lambda i: (i, 0)),
        out_shape=jax.ShapeDtypeStruct((T, GROUP_WIDTH), BF16),
        scratch_shapes=[pltpu.VMEM((CV_HALO + tr + SUBLANES, GROUP_WIDTH), F32)],
        compiler_params=_cparams(("parallel",)),
        name="conformer_conv",
    )(z, z, z, z, dw_w, dw_b, ln_g, ln_b)


def _rope128(x, cos, sin_signed):
    x = x.astype(F32)
    lo = (_lane(x.shape) & 63) < 32
    rot = jnp.where(lo, pltpu.roll(x, 96, 1), pltpu.roll(x, 32, 1))
    return x * cos + rot * sin_signed


def _low_half(x, fill=0.0):
    return jnp.where(_lane(x.shape) < 64, x, fill)


def _high_half_to_low(x, fill=0.0):
    return jnp.where(_lane(x.shape) < 64, pltpu.roll(x, 64, 1), fill)


def _prep_body(qd_ref, kd_ref, vd_ref, qn_ref, kvc_ref, kvs_ref, kvw_ref, gt_ref, cos_ref, sin_ref, ge_ref,
               qd_o, kd_o, vd_o, qn_o, kc_o, vc_o, ks_o, vs_o, kw_o, vw_o, gx_o):
    tr = cos_ref.shape[0]
    cos = cos_ref[...]
    sin = sin_ref[...]
    q_scale = DA_QK_DIM ** -0.5 * LOG2E
    ones = jnp.ones((tr, LANES), BF16)
    for h in range(DA_HEADS):
        cols = slice(h * LANES, (h + 1) * LANES)
        qd_o[h] = (_rope128(qd_ref[:, cols], cos, sin) * q_scale).astype(BF16)
        kd_o[h] = _rope128(kd_ref[:, cols], cos, sin).astype(BF16)
        vd_o[h, :, 0:LANES] = vd_ref[:, cols].astype(BF16)
        vd_o[h, :, LANES:] = ones
    for c in range(NSA_HEADS // 2):
        q = _rope128(qn_ref[:, c * LANES:(c + 1) * LANES], cos, sin) * (NSA_HEAD_DIM ** -0.5 * LOG2E)
        qn_o[2 * c] = _low_half(q).astype(BF16)
        qn_o[2 * c + 1] = _high_half_to_low(q).astype(BF16)
    kc_o[...] = _rope128(kvc_ref[:, :LANES], cos, sin)
    vc_o[...] = kvc_ref[:, LANES:].astype(F32)
    blk = (pl.program_id(0) * tr + _row((tr, LANES))) >> 6
    onehot = jnp.where(blk == _lane((tr, LANES)), 1.0, 0.0).astype(BF16)
    k = _rope128(kvs_ref[:, :LANES], cos, sin)
    for g, half in enumerate((_low_half, _high_half_to_low)):
        ks_o[g, :, 0:LANES] = onehot
        ks_o[g, :, LANES:] = half(k).astype(BF16)
        vs_o[g] = half(kvs_ref[:, LANES:].astype(F32), 1.0).astype(BF16)
    k = _rope128(kvw_ref[:, :LANES], cos, sin)
    for g, half in enumerate((_low_half, _high_half_to_low)):
        kw_o[g] = half(k).astype(BF16)
        vw_o[g] = half(kvw_ref[:, LANES:].astype(F32), 1.0).astype(BF16)
    expand = ge_ref[...]
    gx = sum(_dot(part, expand) for part in _split3(jax.nn.sigmoid(gt_ref[...].astype(F32))))
    for c in range(3):
        gx_o[c] = gx[:, c * GROUP_WIDTH:(c + 1) * GROUP_WIDTH]


def _gate_expansion():
    col = jnp.arange(3 * GROUP_WIDTH)
    src = (col // GROUP_WIDTH) * NSA_HEADS + (col % GROUP_WIDTH) // NSA_HEAD_DIM
    return (jnp.arange(LANES)[:, None] == src[None, :]).astype(BF16)


def prep(z, cos128, sin128, tr):
    T = z.shape[0]
    G = NSA_KV_HEADS

    def zspec(width, col):
        return pl.BlockSpec((tr, width), lambda i: (i, col // width))

    def heads(n, width=LANES):
        return pl.BlockSpec((n, tr, width), lambda i: (0, i, 0))

    def hshape(n, width=LANES):
        return jax.ShapeDtypeStruct((n, T, width), BF16)

    row128 = pl.BlockSpec((tr, LANES), lambda i: (i, 0))
    return pl.pallas_call(
        _prep_body,
        grid=(T // tr,),
        in_specs=[zspec(512, COL_QDA), zspec(512, COL_KDA), zspec(512, COL_VDA), zspec(512, COL_QNS),
                  zspec(256, COL_KVNS), zspec(256, COL_KVNS + 256), zspec(256, COL_KVNS + 512),
                  zspec(128, COL_GNS), row128, row128,
                  pl.BlockSpec((LANES, 3 * GROUP_WIDTH), lambda i: (0, 0))],
        out_specs=[heads(DA_HEADS), heads(DA_HEADS), heads(DA_HEADS, 2 * LANES),
                   heads(NSA_HEADS), row128, row128, heads(G, 2 * LANES), heads(G), heads(G), heads(G),
                   pl.BlockSpec((3, tr, GROUP_WIDTH), lambda i: (0, i, 0))],
        out_shape=[hshape(DA_HEADS), hshape(DA_HEADS), hshape(DA_HEADS, 2 * LANES),
                   hshape(NSA_HEADS), jax.ShapeDtypeStruct((T, LANES), F32),
                   jax.ShapeDtypeStruct((T, LANES), F32), hshape(G, 2 * LANES), hshape(G), hshape(G), hshape(G),
                   jax.ShapeDtypeStruct((3, T, GROUP_WIDTH), F32)],
        compiler_params=_cparams(("parallel",)),
        name="attention_prep",
    )(z, z, z, z, z, z, z, z, cos128, sin128, _gate_expansion())


def _lane_tile(x, n):
    return x if n == 1 else jnp.concatenate([x] * n, axis=-1)


def _softmax_update(s, v, m_ref, acc_ref):
    m_prev = m_ref[...]
    m_new = jnp.maximum(m_prev, jnp.max(s, axis=-1, keepdims=True))
    alpha = jnp.exp2(m_prev - m_new)
    p = jnp.exp2(s - _lane_tile(m_new, s.shape[1] // LANES))
    acc_ref[...] = _lane_tile(alpha, acc_ref.shape[1] // LANES) * acc_ref[...] + _dot(p.astype(BF16), v)
    m_ref[...] = m_new


SWEEP_UNROLL = 4


def _causal_sweep(qs_ref, k_ref, v_ref, m_ref, acc_ref, s_ref, q0, tq, tk):
    m_ref[...] = jnp.full(m_ref.shape, -jnp.inf, F32)
    acc_ref[...] = jnp.zeros(acc_ref.shape, F32)

    def keys(ref, t):
        return ref[0, pl.ds(pl.multiple_of(t * tk, tk), tk), :]

    def scores(t):
        return _dot_nt(qs_ref[...], keys(k_ref, t))

    n_full = q0 // tk
    s_ref[...] = scores(0)

    def run(t0, count):
        s_cur = s_ref[...]
        for u in range(count):
            s_next = scores(t0 + u + 1)
            _softmax_update(s_cur, keys(v_ref, t0 + u), m_ref, acc_ref)
            s_cur = s_next
        s_ref[...] = s_cur

    def several(i, c):
        run(SWEEP_UNROLL * i, SWEEP_UNROLL)
        return c

    def single(t, c):
        run(t, 1)
        return c

    n_groups = n_full // SWEEP_UNROLL
    lax.fori_loop(0, n_groups, several, 0)
    lax.fori_loop(n_groups * SWEEP_UNROLL, n_full, single, 0)
    s = s_ref[...]
    visible = n_full * tk + _lane(s.shape) <= q0 + (_row(s.shape) & (tq - 1))
    _softmax_update(jnp.where(visible, s, NEG), keys(v_ref, n_full), m_ref, acc_ref)


def _diff_attn_body(q_ref, k_ref, v_ref, lam_ref, sg_ref, o_ref, qs_ref, m_ref, acc_ref, s_ref, *,
                    tq, tk, lambda_init):
    q = q_ref[0]
    first = _lane(q.shape) < DA_QK_DIM
    qs_ref[0:tq, :] = jnp.where(first, q, jnp.zeros_like(q))
    qs_ref[tq:, :] = jnp.where(first, jnp.zeros_like(q), q)
    _causal_sweep(qs_ref, k_ref, v_ref, m_ref, acc_ref, s_ref, pl.program_id(1) * tq, tq, tk)
    lam = lam_ref[...]
    lam_full = (jnp.exp(jnp.sum(lam[0:1] * lam[1:2], axis=-1, keepdims=True))
                - jnp.exp(jnp.sum(lam[2:3] * lam[3:4], axis=-1, keepdims=True)) + lambda_init)
    o = acc_ref[:, 0:LANES] / acc_ref[:, LANES:]
    a = o[0:tq] - lam_full * o[tq:]
    o_ref[...] = (_rms(a, sg_ref[...]) * (1.0 - lambda_init)).astype(BF16)


def diff_attention(qd, kd, vd, lam, subln, lambda_init, tq, tk):
    H, T, _ = qd.shape
    return pl.pallas_call(
        functools.partial(_diff_attn_body, tq=tq, tk=tk, lambda_init=lambda_init),
        grid=(H, T // tq),
        in_specs=[pl.BlockSpec((1, tq, LANES), lambda h, qi: (h, qi, 0)),
                  pl.BlockSpec((1, T, LANES), lambda h, qi: (h, 0, 0)),
                  pl.BlockSpec((1, T, 2 * LANES), lambda h, qi: (h, 0, 0)),
                  pl.BlockSpec((4, DA_QK_DIM), lambda h, qi: (0, 0)),
                  pl.BlockSpec((1, LANES), lambda h, qi: (0, 0))],
        out_specs=pl.BlockSpec((tq, LANES), lambda h, qi: (qi, h)),
        out_shape=jax.ShapeDtypeStruct((T, GROUP_WIDTH), BF16),
        scratch_shapes=[pltpu.VMEM((2 * tq, LANES), BF16), pltpu.VMEM((2 * tq, LANES), F32),
                        pltpu.VMEM((2 * tq, 2 * LANES), F32), pltpu.VMEM((2 * tq, tk), F32)],
        compiler_params=_cparams(("parallel", "arbitrary")),
        name="diff_attention",
    )(qd, kd, vd, lam, subln)


def _compress_body(x_ref, pe_ref, w1_ref, w2_ref, o_ref):
    x = x_ref[0, 0]
    half = x.shape[1]
    a = _dot((x + pe_ref[0, 0:1, :]).astype(BF16), w1_ref[0, 0:half, :])
    b = _dot((x + pe_ref[0, 1:2, :]).astype(BF16), w1_ref[0, half:, :])
    hid = jax.nn.gelu(a + pltpu.roll(b, b.shape[0] - 1, 0))
    o_ref[0, 0] = _dot(hid.astype(BF16), w2_ref[0]).astype(BF16)


def compress(x16, pe2, w1, w2p):
    _, G, n, half = x16.shape
    hid = w1.shape[2]
    return pl.pallas_call(
        _compress_body,
        grid=(2, G),
        in_specs=[pl.BlockSpec((1, 1, n, half), lambda c, g: (c, g, 0, 0)),
                  pl.BlockSpec((1, 2, half), lambda c, g: (c, 0, 0)),
                  pl.BlockSpec((1, 2 * half, hid), lambda c, g: (c, 0, 0)),
                  pl.BlockSpec((1, hid, LANES), lambda c, g: (c, 0, 0))],
        out_specs=pl.BlockSpec((1, 1, n, LANES), lambda c, g: (c, g, 0, 0)),
        out_shape=jax.ShapeDtypeStruct((2, G, n, LANES), BF16),
        compiler_params=_cparams(("parallel", "parallel")),
        name="nsa_compress",
    )(x16, pe2, w1, w2p)


def _split3(x):
    hi = x.astype(BF16)
    r1 = x - hi.astype(F32)
    mid = r1.astype(BF16)
    lo = (r1 - mid.astype(F32)).astype(BF16)
    return hi, mid, lo


def _pack_heads(o, tq):
    pair = lambda a, b: a + pltpu.roll(b, 64, 1)
    return jnp.concatenate([pair(o[0:tq], o[tq:2 * tq]), pair(o[2 * tq:3 * tq], o[3 * tq:])], axis=-1)


def _normalize_low_half(acc):
    return jnp.where(_lane(acc.shape) < 64, acc / pltpu.roll(acc, 64, 1), 0.0)


def _cmp_select_body(q_ref, kc_ref, vc_ref, ovt_ref, gx_ref, o_ref, sb_ref, *, tq):
    qi = pl.program_id(1)
    hpg = q_ref.shape[0]
    q = q_ref[...].reshape(hpg * tq, LANES)
    s = _dot_nt(q, kc_ref[0, 0])
    t = qi * tq + (_row(s.shape) & (tq - 1))
    cmask = _lane(s.shape) * NSA_CMP_STRIDE + (NSA_CMP_LEN - 1) <= t
    s = jnp.where(cmask, s, NEG)
    e = jnp.exp2(s - jnp.max(s, axis=-1, keepdims=True))
    p = jnp.where(cmask, e / jnp.sum(e, axis=-1, keepdims=True), 0.0)
    o = _dot(p.astype(BF16), vc_ref[0, 0])
    o_ref[...] = (gx_ref[0] * _pack_heads(o, tq)).astype(BF16)

    psum = p[0:tq]
    for hh in range(1, hpg):
        psum = psum + p[hh * tq:(hh + 1) * tq]
    ovt = ovt_ref[...]
    imp = sum(_dot_nt(ovt, part) for part in _split3(psum))
    j = _row(imp.shape)
    cur = (qi * tq + _lane(imp.shape)) >> 6
    forced = (j == 0) | (j == cur) | (j == cur - 1)
    score = jnp.where(forced, -2.0, jnp.where(j <= cur, imp, -1.0))
    jf = j.astype(F32)
    bias = jnp.where(forced, 0.0, NEG)
    for _ in range(NSA_SLC_TOPK - NSA_FORCED):
        m = jnp.max(score, axis=0, keepdims=True)
        first = jnp.min(jnp.where(score == m, jf, float(LANES)), axis=0, keepdims=True)
        hit = jf == first
        bias = jnp.where(hit, jnp.where(m >= 0.0, 0.0, NEG), bias)
        score = jnp.where(hit, -2.0, score)
    sb_ref[0] = bias.T.astype(BF16)


def cmp_select(qn, kvc, overlap_t, gx, tq):
    H, T, _ = qn.shape
    G = NSA_KV_HEADS
    hpg = H // G
    n = kvc.shape[2]
    return pl.pallas_call(
        functools.partial(_cmp_select_body, tq=tq),
        grid=(G, T // tq),
        in_specs=[pl.BlockSpec((hpg, tq, LANES), lambda g, qi: (g, qi, 0)),
                  pl.BlockSpec((1, 1, n, LANES), lambda g, qi: (0, g, 0, 0)),
                  pl.BlockSpec((1, 1, n, LANES), lambda g, qi: (1, g, 0, 0)),
                  pl.BlockSpec((LANES, n), lambda g, qi: (0, 0)),
                  pl.BlockSpec((1, tq, hpg * NSA_HEAD_DIM), lambda g, qi: (0, qi, g))],
        out_specs=[pl.BlockSpec((tq, hpg * NSA_HEAD_DIM), lambda g, qi: (qi, g)),
                   pl.BlockSpec((1, tq, LANES), lambda g, qi: (g, qi, 0))],
        out_shape=[jax.ShapeDtypeStruct((T, GROUP_WIDTH), BF16),
                   jax.ShapeDtypeStruct((G, T, LANES), BF16)],
        compiler_params=_cparams(("parallel", "parallel")),
        name="nsa_compressed_select",
    )(qn, kvc, kvc, overlap_t, gx)


def _sel_attn_body(q_ref, sb_ref, k_ref, v_ref, gx_ref, o_ref, qs_ref, m_ref, acc_ref, s_ref, *, tq, tk):
    hpg = q_ref.shape[0]
    for hh in range(hpg):
        qs_ref[hh * tq:(hh + 1) * tq, 0:LANES] = sb_ref[0]
        qs_ref[hh * tq:(hh + 1) * tq, LANES:] = q_ref[hh]
    _causal_sweep(qs_ref, k_ref, v_ref, m_ref, acc_ref, s_ref, pl.program_id(1) * tq, tq, tk)
    o = _normalize_low_half(acc_ref[...])
    o_ref[...] = (gx_ref[0] * _pack_heads(o, tq)).astype(BF16)


def sel_attention(qn, selbias, ks, vs, gx, tq, tk):
    H, T, _ = qn.shape
    G = NSA_KV_HEADS
    hpg = H // G
    return pl.pallas_call(
        functools.partial(_sel_attn_body, tq=tq, tk=tk),
        grid=(G, T // tq),
        in_specs=[pl.BlockSpec((hpg, tq, LANES), lambda g, qi: (g, qi, 0)),
                  pl.BlockSpec((1, tq, LANES), lambda g, qi: (g, qi, 0)),
                  pl.BlockSpec((1, T, 2 * LANES), lambda g, qi: (g, 0, 0)),
                  pl.BlockSpec((1, T, LANES), lambda g, qi: (g, 0, 0)),
                  pl.BlockSpec((1, tq, hpg * NSA_HEAD_DIM), lambda g, qi: (1, qi, g))],
        out_specs=pl.BlockSpec((tq, hpg * NSA_HEAD_DIM), lambda g, qi: (qi, g)),
        out_shape=jax.ShapeDtypeStruct((T, GROUP_WIDTH), BF16),
        scratch_shapes=[pltpu.VMEM((hpg * tq, 2 * LANES), BF16), pltpu.VMEM((hpg * tq, LANES), F32),
                        pltpu.VMEM((hpg * tq, LANES), F32), pltpu.VMEM((hpg * tq, tk), F32)],
        compiler_params=_cparams(("parallel", "arbitrary")),
        name="nsa_selected_attention",
    )(qn, selbias, ks, vs, gx)


def _win_attn_body(q_ref, k_ref, v_ref, b_ref, gx_ref, o_ref, *, tq, span):
    hpg = q_ref.shape[0]
    q0 = pl.program_id(1) * tq
    lo = pl.multiple_of(jnp.maximum(q0 + tq - span, 0), tq)
    q = q_ref[...].reshape(hpg * tq, LANES)
    s = _dot_nt(q, k_ref[0, pl.ds(lo, span), :])

    def finish(s):
        p = jnp.exp2(s - jnp.max(s, axis=-1, keepdims=True))
        acc = _dot(p.astype(BF16), v_ref[0, pl.ds(lo, span), :])
        o_ref[...] = (gx_ref[0] * _pack_heads(_normalize_low_half(acc), tq)).astype(BF16)

    @pl.when(q0 + tq >= span)
    def _():
        finish(s + jnp.concatenate([b_ref[...]] * hpg, axis=0))

    @pl.when(q0 + tq < span)
    def _():
        qpos = q0 + (_row(s.shape) & (tq - 1))
        kpos = lo + _lane(s.shape)
        finish(jnp.where((kpos <= qpos) & (kpos > qpos - NSA_WINDOW), s, NEG))


def _window_bias(tq, span):
    d = jnp.arange(span)[None, :] - jnp.arange(tq)[:, None] + (tq - span)
    return jnp.where((d <= 0) & (d > -NSA_WINDOW), 0.0, NEG).astype(F32)


def win_attention(qn, kw, vw, gx, tq):
    H, T, _ = qn.shape
    G = NSA_KV_HEADS
    hpg = H // G
    span = NSA_WINDOW + tq
    return pl.pallas_call(
        functools.partial(_win_attn_body, tq=tq, span=span),
        grid=(G, T // tq),
        in_specs=[pl.BlockSpec((hpg, tq, LANES), lambda g, qi: (g, qi, 0)),
                  pl.BlockSpec((1, T, LANES), lambda g, qi: (g, 0, 0)),
                  pl.BlockSpec((1, T, LANES), lambda g, qi: (g, 0, 0)),
                  pl.BlockSpec((tq, span), lambda g, qi: (0, 0)),
                  pl.BlockSpec((1, tq, hpg * NSA_HEAD_DIM), lambda g, qi: (2, qi, g))],
        out_specs=pl.BlockSpec((tq, hpg * NSA_HEAD_DIM), lambda g, qi: (qi, g)),
        out_shape=jax.ShapeDtypeStruct((T, GROUP_WIDTH), BF16),
        compiler_params=_cparams(("parallel", "arbitrary")),
        name="nsa_window_attention",
    )(qn, kw, vw, _window_bias(tq, span), gx)


def _top2_route(h, w_router):
    logits = jnp.dot(h, w_router, preferred_element_type=F32, precision=lax.Precision.HIGHEST)
    lane = _lane(logits.shape)
    lf = lane.astype(F32)
    logits = jnp.where(lane < N_EXPERTS, logits, -jnp.inf)
    v0 = jnp.max(logits, axis=-1, keepdims=True)
    i0 = jnp.min(jnp.where(logits == v0, lf, float(LANES)), axis=-1, keepdims=True)
    rest = jnp.where(lf == i0, -jnp.inf, logits)
    v1 = jnp.max(rest, axis=-1, keepdims=True)
    i1 = jnp.min(jnp.where(rest == v1, lf, float(LANES)), axis=-1, keepdims=True)
    e1 = jnp.exp(v1 - v0)
    w0 = 1.0 / (1.0 + e1)
    w1 = e1 / (1.0 + e1)
    return jnp.where(lane == 0, i0, jnp.where(lane == 1, i1, jnp.where(lane == 2, w0, w1)))


def _out_proj_body(a_ref, b_ref, c_ref, d1_ref, d2_ref, d3_ref, w_ref, x_ref, *rest, normed):
    o_ref, lhs_ref = rest[-3 if normed else -2], rest[-1]
    lhs_ref[:, 0:GROUP_WIDTH] = a_ref[...]
    lhs_ref[:, GROUP_WIDTH:2 * GROUP_WIDTH] = b_ref[...]
    lhs_ref[:, 2 * GROUP_WIDTH:3 * GROUP_WIDTH] = c_ref[...]
    d = d1_ref[...].astype(F32) + d2_ref[...].astype(F32) + d3_ref[...].astype(F32)
    lhs_ref[:, 3 * GROUP_WIDTH:] = d.astype(BF16)
    y = x_ref[...] + _dot(lhs_ref[...], w_ref[0])
    o_ref[...] = y
    if normed:
        g_ref, hn_ref = rest[0], rest[2]
        hn_ref[...] = _rms(y, g_ref[...]).astype(BF16)


def out_proj(parts, w, layer, x, g_next, tm):
    T, N = x.shape
    part = pl.BlockSpec((tm, GROUP_WIDTH), lambda i: (i, 0))
    rows = pl.BlockSpec((tm, N), lambda i: (i, 0))
    in_specs = [part] * 6 + [pl.BlockSpec((1, 4 * GROUP_WIDTH, N), lambda i: (layer, 0, 0)), rows]
    out_specs = [rows]
    out_shape = [jax.ShapeDtypeStruct((T, N), F32)]
    operands = [*parts, w, x]
    if g_next is not None:
        in_specs.append(pl.BlockSpec((1, N), lambda i: (0, 0)))
        out_specs.append(rows)
        out_shape.append(jax.ShapeDtypeStruct((T, N), BF16))
        operands.append(g_next)
    return pl.pallas_call(
        functools.partial(_out_proj_body, normed=g_next is not None),
        grid=(T // tm,),
        in_specs=in_specs,
        out_specs=out_specs,
        out_shape=out_shape,
        scratch_shapes=[pltpu.VMEM((tm, 4 * GROUP_WIDTH), BF16)],
        compiler_params=_cparams(("parallel",)),
        name="out_proj",
    )(*operands)


UP_TN = 1408
UP_CHUNK = 512
CAST_ROWS = 256


def _group_starts(te_ref, i):
    return (i == 0) | (te_ref[i] != te_ref[jnp.maximum(i - 1, 0)])


def _stream_group_weights(te_ref, nx_ref, w_hbms, wst_ref, wb_refs, sem_ref):
    j = pl.program_id(0)
    i = pl.program_id(1)
    tn = wst_ref.shape[2]

    def copies(e, jj):
        cols = pl.ds(pl.multiple_of(jj * tn, tn), tn)
        return [pltpu.make_async_copy(w.at[e, :, cols], wst_ref.at[n], sem_ref) for n, w in enumerate(w_hbms)]

    @pl.when((j == 0) & (i == 0))
    def _():
        for c in copies(te_ref[0], 0):
            c.start()

    @pl.when(_group_starts(te_ref, i))
    def _():
        for c in copies(te_ref[i], j):
            c.wait()
        def cast_rows(c, carry):
            rows = pl.ds(pl.multiple_of(c * CAST_ROWS, CAST_ROWS), CAST_ROWS)
            for n, wb_ref in enumerate(wb_refs):
                wb_ref[rows, :] = wst_ref[n, rows, :].astype(BF16)
            return carry

        lax.fori_loop(0, wst_ref.shape[1] // CAST_ROWS, cast_rows, 0)
        in_sweep = nx_ref[i] >= 0

        @pl.when(in_sweep | (j + 1 < pl.num_programs(0)))
        def _():
            for c in copies(jnp.where(in_sweep, nx_ref[i], te_ref[0]), jnp.where(in_sweep, j, j + 1)):
                c.start()


def _weight_stream_scratch(n_weights, k, tn):
    return [pltpu.VMEM((n_weights, k, tn), F32)] + [pltpu.VMEM((k, tn), BF16)] * n_weights + [
        pltpu.SemaphoreType.DMA(())]


def _by_fill(nv_ref, o_ref, compute):
    nv = nv_ref[pl.program_id(1)]
    half = o_ref.shape[0] // 2

    @pl.when(nv > half)
    def _():
        o_ref[...] = compute(slice(None))

    @pl.when((nv > 0) & (nv <= half))
    def _():
        o_ref[0:half, :] = compute(slice(0, half))
        o_ref[half:, :] = jnp.zeros((half, o_ref.shape[1]), o_ref.dtype)

    @pl.when(nv == 0)
    def _():
        o_ref[...] = jnp.zeros(o_ref.shape, o_ref.dtype)


def _up_body(te_ref, nx_ref, nv_ref, na_ref, h_ref, wg_hbm, wu_hbm, o_ref, wst_ref, wgb_ref, wub_ref, sem_ref):
    _stream_group_weights(te_ref, nx_ref, (wg_hbm, wu_hbm), wst_ref, (wgb_ref, wub_ref), sem_ref)

    def swiglu(rows):
        h = h_ref[rows, :]
        parts = []
        for c0 in range(0, o_ref.shape[1], UP_CHUNK):
            cols = slice(c0, min(c0 + UP_CHUNK, o_ref.shape[1]))
            a = _dot(h, wgb_ref[:, cols])
            parts.append((a * jax.nn.sigmoid(a) * _dot(h, wub_ref[:, cols])).astype(BF16))
        return jnp.concatenate(parts, axis=-1)

    _by_fill(nv_ref, o_ref, swiglu)


def swiglu_up(tile_expert, next_expert, valid_rows, n_active, hn, wg, wu, tm, tn):
    P, K = hn.shape
    F = wg.shape[2]
    return pl.pallas_call(
        _up_body,
        grid_spec=pltpu.PrefetchScalarGridSpec(
            num_scalar_prefetch=4,
            grid=(F // tn, P // tm),
            in_specs=[pl.BlockSpec((tm, K), lambda j, i, te, nx, nv, na: (jnp.minimum(i, na[0] - 1), 0)),
                      pl.BlockSpec(memory_space=pl.ANY), pl.BlockSpec(memory_space=pl.ANY)],
            out_specs=pl.BlockSpec((tm, tn), lambda j, i, te, nx, nv, na: (i, j)),
            scratch_shapes=_weight_stream_scratch(2, K, tn)),
        out_shape=jax.ShapeDtypeStruct((P, F), BF16),
        compiler_params=_cparams(("arbitrary", "arbitrary")),
        name="swiglu_up",
    )(tile_expert, next_expert, valid_rows, n_active, hn, wg, wu)


def _down_body(te_ref, nx_ref, nv_ref, na_ref, a_ref, w_hbm, *rest, residual):
    r_ref = rest[0] if residual else None
    o_ref, wst_ref, wb_ref, sem_ref = rest[1:] if residual else rest
    _stream_group_weights(te_ref, nx_ref, (w_hbm,), wst_ref, (wb_ref,), sem_ref)

    def project(rows):
        y = _dot(a_ref[rows, :], wb_ref[...])
        return r_ref[rows, :] + y if residual else y

    _by_fill(nv_ref, o_ref, project)


def swiglu_down(tile_expert, next_expert, valid_rows, n_active, act, wd, residual, tm, tn):
    P, F = act.shape
    N = wd.shape[2]
    tile = pl.BlockSpec((tm, tn), lambda j, i, te, nx, nv, na: (i, j))
    in_specs = [pl.BlockSpec((tm, F), lambda j, i, te, nx, nv, na: (jnp.minimum(i, na[0] - 1), 0)),
                pl.BlockSpec(memory_space=pl.ANY)]
    operands = [act, wd]
    if residual is not None:
        in_specs.append(tile)
        operands.append(residual)
    return pl.pallas_call(
        functools.partial(_down_body, residual=residual is not None),
        grid_spec=pltpu.PrefetchScalarGridSpec(
            num_scalar_prefetch=4,
            grid=(N // tn, P // tm),
            in_specs=in_specs,
            out_specs=tile,
            scratch_shapes=_weight_stream_scratch(1, F, tn)),
        out_shape=jax.ShapeDtypeStruct((P, N), F32),
        compiler_params=_cparams(("arbitrary", "arbitrary")),
        name="swiglu_down",
    )(tile_expert, next_expert, valid_rows, n_active, *operands)


def _router_body(x_ref, g_ref, w_ref, o_ref):
    o_ref[...] = _top2_route(_rms(x_ref[...], g_ref[...]), w_ref[...])


def router(x, g, w_pad, tm):
    T, K = x.shape
    return pl.pallas_call(
        _router_body,
        grid=(T // tm,),
        in_specs=[pl.BlockSpec((tm, K), lambda i: (i, 0)),
                  pl.BlockSpec((1, K), lambda i: (0, 0)),
                  pl.BlockSpec((K, LANES), lambda i: (0, 0))],
        out_specs=pl.BlockSpec((tm, LANES), lambda i: (i, 0)),
        out_shape=jax.ShapeDtypeStruct((T, LANES), F32),
        compiler_params=_cparams(("parallel",)),
        name="moe_router",
    )(x, g, w_pad)


ROW_DMA_UNROLL = 8


def _row_copy(src_hbm, row, dst_ref, r, sem):
    return pltpu.make_async_copy(src_hbm.at[pl.ds(row, 1)], dst_ref.at[pl.ds(r, 1)], sem)


def _gather_body(tok_ref, na_ref, x_hbm, g_ref, o_ref, buf_ref, sem_ref):
    i = pl.program_id(0)
    tm = buf_ref.shape[1]

    def fetch(tile):
        slot = tile % 2

        def start(r, c):
            _row_copy(x_hbm, tok_ref[tile * tm + r], buf_ref.at[slot], r, sem_ref.at[slot]).start()
            return c

        lax.fori_loop(0, tm, start, 0, unroll=ROW_DMA_UNROLL)

    @pl.when(i == 0)
    def _():
        fetch(0)

    @pl.when(i + 1 < na_ref[0])
    def _():
        fetch(i + 1)

    @pl.when(i < na_ref[0])
    def _():
        slot = i % 2

        def wait(r, c):
            _row_copy(x_hbm, 0, buf_ref.at[slot], r, sem_ref.at[slot]).wait()
            return c

        lax.fori_loop(0, tm, wait, 0, unroll=ROW_DMA_UNROLL)
        o_ref[...] = _rms(buf_ref[slot], g_ref[...]).astype(BF16)

    @pl.when(i >= na_ref[0])
    def _():
        o_ref[...] = jnp.zeros(o_ref.shape, BF16)


def gather_norm_tokens(tok_of_slot, n_active, x, g, tm):
    P = tok_of_slot.shape[0]
    K = x.shape[1]
    return pl.pallas_call(
        _gather_body,
        grid_spec=pltpu.PrefetchScalarGridSpec(
            num_scalar_prefetch=2,
            grid=(P // tm,),
            in_specs=[pl.BlockSpec(memory_space=pl.ANY), pl.BlockSpec((1, K), lambda i, tok, na: (0, 0))],
            out_specs=pl.BlockSpec((tm, K), lambda i, tok, na: (i, 0)),
            scratch_shapes=[pltpu.VMEM((2, tm, K), F32), pltpu.SemaphoreType.DMA((2,))]),
        out_shape=jax.ShapeDtypeStruct((P, K), BF16),
        compiler_params=_cparams(("arbitrary",)),
        name="moe_gather",
    )(tok_of_slot, n_active, x, g)


def _combine_body(s0_ref, s1_ref, x_ref, y_hbm, rt_ref, g_ref, o_ref, b0_ref, b1_ref, sem0, sem1, *, final):
    tm = x_ref.shape[0]
    base = pl.program_id(0) * tm

    def start(r, c):
        _row_copy(y_hbm, s0_ref[base + r], b0_ref, r, sem0).start()
        _row_copy(y_hbm, s1_ref[base + r], b1_ref, r, sem1).start()
        return c

    def wait(r, c):
        _row_copy(y_hbm, 0, b0_ref, r, sem0).wait()
        _row_copy(y_hbm, 0, b1_ref, r, sem1).wait()
        return c

    lax.fori_loop(0, tm, start, 0, unroll=ROW_DMA_UNROLL)
    lax.fori_loop(0, tm, wait, 0, unroll=ROW_DMA_UNROLL)
    gates = rt_ref[...]
    y = x_ref[...] + gates[:, 2:3] * b0_ref[...] + gates[:, 3:4] * b1_ref[...]
    o_ref[...] = _rms(y, g_ref[...]) if final else y


def moe_combine(slot0, slot1, x, ys, route, g, tm, final):
    T, K = x.shape
    return pl.pallas_call(
        functools.partial(_combine_body, final=final),
        grid_spec=pltpu.PrefetchScalarGridSpec(
            num_scalar_prefetch=2,
            grid=(T // tm,),
            in_specs=[pl.BlockSpec((tm, K), lambda i, s0, s1: (i, 0)),
                      pl.BlockSpec(memory_space=pl.ANY),
                      pl.BlockSpec((tm, LANES), lambda i, s0, s1: (i, 0)),
                      pl.BlockSpec((1, K), lambda i, s0, s1: (0, 0))],
            out_specs=pl.BlockSpec((tm, K), lambda i, s0, s1: (i, 0)),
            scratch_shapes=[pltpu.VMEM((tm, K), F32), pltpu.VMEM((tm, K), F32),
                            pltpu.SemaphoreType.DMA(()), pltpu.SemaphoreType.DMA(())]),
        out_shape=jax.ShapeDtypeStruct((T, K), F32),
        compiler_params=_cparams(("arbitrary",)),
        name="moe_combine",
    )(slot0, slot1, x, ys, route, g)


def _final_norm_body(x_ref, g_ref, o_ref):
    o_ref[...] = _rms(x_ref[...], g_ref[...])


def final_norm(x, g, tm):
    T, K = x.shape
    return pl.pallas_call(
        _final_norm_body,
        grid=(T // tm,),
        in_specs=[pl.BlockSpec((tm, K), lambda i: (i, 0)), pl.BlockSpec((1, K), lambda i: (0, 0))],
        out_specs=pl.BlockSpec((tm, K), lambda i: (i, 0)),
        out_shape=jax.ShapeDtypeStruct((T, K), F32),
        compiler_params=_cparams(("parallel",)),
        name="final_norm",
    )(x, g)


def _routing_tables(route, tm, expert_base):
    T = route.shape[0]
    top_i = route[:, 0:2].astype(jnp.int32)
    e_flat = top_i.reshape(-1)
    onehot = (e_flat[:, None] == jnp.arange(N_EXPERTS)[None, :]).astype(jnp.int32)
    rank = jnp.take_along_axis(jnp.cumsum(onehot, axis=0) - onehot, e_flat[:, None], axis=1)[:, 0]
    count = jnp.sum(onehot, axis=0)
    padded = ((count + tm - 1) // tm) * tm
    end = jnp.cumsum(padded)
    start = end - padded
    slot = start[e_flat] + rank
    n_slots = 2 * T + N_EXPERTS * tm
    n_tiles = n_slots // tm
    n_active = (end[-1] // tm).astype(jnp.int32)
    tile_start = jnp.minimum(jnp.arange(n_tiles, dtype=jnp.int32), n_active - 1) * tm
    tile_expert = jnp.minimum(jnp.sum(tile_start[:, None] >= end[None, :], axis=1), N_EXPERTS - 1)
    group_end = end[tile_expert] // tm
    next_expert = jnp.where(group_end < n_active, tile_expert[jnp.minimum(group_end, n_tiles - 1)] + expert_base, -1)
    tile_index = jnp.arange(n_tiles, dtype=jnp.int32)
    valid_rows = jnp.where(tile_index < n_active,
                           jnp.clip(start[tile_expert] + count[tile_expert] - tile_index * tm, 0, tm), 0)
    tok_of_slot = jnp.zeros((n_slots,), jnp.int32).at[slot].set(jnp.arange(2 * T, dtype=jnp.int32) // 2)
    slot2 = slot.reshape(T, 2).astype(jnp.int32)
    return ((tile_expert + expert_base).astype(jnp.int32), next_expert.astype(jnp.int32),
            valid_rows.astype(jnp.int32), n_active.reshape(1), tok_of_slot, slot2[:, 0], slot2[:, 1])


def _rope_tables128(T):
    half = NSA_HEAD_DIM // 2
    inv = ROPE_THETA ** (-jnp.arange(0, NSA_HEAD_DIM, 2, dtype=F32) / NSA_HEAD_DIM)
    ang = jnp.arange(T, dtype=F32)[:, None] * inv[None, :]
    cos, sin = jnp.cos(ang), jnp.sin(ang)
    cos128 = jnp.tile(cos, (1, LANES // half))
    sin128 = jnp.tile(jnp.concatenate([-sin, sin], axis=1), (1, LANES // NSA_HEAD_DIM))
    return cos128, sin128


def _overlap_matrix_t(n_cmp_pad):
    sstart = jnp.arange(LANES) * NSA_SLC_LEN
    cstart = jnp.arange(n_cmp_pad) * NSA_CMP_STRIDE
    ov = (cstart[None, :] < sstart[:, None] + NSA_SLC_LEN) & (cstart[None, :] + NSA_CMP_LEN > sstart[:, None])
    return ov.astype(BF16)


def kernel(x, attn_norm, w_in, w_out, gm_ln_g, gm_ln_b, gm_ws, gm_bs, da_lambda, da_subln, cv_dw_w, cv_dw_b,
           cv_ln_g, cv_ln_b, nsa_cmp_w1, nsa_cmp_w2, nsa_cmp_pe, ffn_norm, ffn_wg, ffn_wu, ffn_wd, router_w,
           exp_wg, exp_wu, exp_wd, final_norm_g):
    B, T, D = x.shape
    assert B == 1 and D == D_MODEL and T % 1024 == 0 and T // NSA_SLC_LEN <= LANES
    depth = w_in.shape[0]
    G = NSA_KV_HEADS
    n16 = T // NSA_CMP_STRIDE
    tm = 512

    cos128, sin128 = _rope_tables128(T)
    overlap_t = _overlap_matrix_t(n16)
    dense_na = jnp.full((1,), T // tm, jnp.int32)
    row = lambda v: v.reshape(1, -1)
    merge = lambda w: w.reshape((-1,) + w.shape[2:])
    exp_wg, exp_wu, exp_wd = merge(exp_wg), merge(exp_wu), merge(exp_wd)
    w_in_b = jnp.concatenate(
        [w_in.astype(BF16), jnp.zeros(w_in.shape[:2] + (IN_WIDTH_PAD - IN_WIDTH,), BF16)], axis=-1)
    w_out_b = w_out.astype(BF16)

    xs = x[0]
    for l in range(depth):
        lambda_init = 0.8 - 0.6 * math.exp(-0.3 * l)
        z = norm_mm(xs, row(attn_norm[l]), w_in_b, l, 1024, IN_WIDTH_PAD // 3)

        bs_rows = jnp.repeat(gm_bs[l].T, GM_CHUNK, axis=1)
        o_a = gmlp(z, row(gm_ln_g[l]), row(gm_ln_b[l]), gm_ws[l], bs_rows, 512)
        dw_w = jnp.pad(cv_dw_w[l], ((0, CV_HALO - CV_WIDTH), (0, 0)))
        o_c = conformer_conv(z, dw_w, row(cv_dw_b[l]), row(cv_ln_g[l]), row(cv_ln_b[l]), 256)

        qd, kd, vd, qn, kc, vc, ks, vs, kw, vw, gx = prep(z, cos128, sin128, 256)
        o_b = diff_attention(qd, kd, vd, da_lambda[l], row(da_subln[l]), lambda_init, 512, 512)

        x16 = jnp.stack([kc, vc]).reshape(2, T, G, NSA_HEAD_DIM).transpose(0, 2, 1, 3)
        x16 = x16.reshape(2, G, n16, NSA_CMP_STRIDE * NSA_HEAD_DIM)
        pe2 = nsa_cmp_pe[l].reshape(2, 2, NSA_CMP_STRIDE * NSA_HEAD_DIM)
        w2p = jnp.pad(nsa_cmp_w2[l], ((0, 0), (0, 0), (0, LANES - NSA_HEAD_DIM))).astype(BF16)
        kvc = compress(x16, pe2, nsa_cmp_w1[l].astype(BF16), w2p)
        o_cmp, selbias = cmp_select(qn, kvc, overlap_t, gx, 256)
        o_sel = sel_attention(qn, selbias, ks, vs, gx, 256, 512)
        o_win = win_attention(qn, kw, vw, gx, 256)

        g_ffn = row(ffn_norm[l])
        parts = (o_a, o_b, o_c, o_cmp, o_sel, o_win)
        e = l // 2
        if l % 2 == 0:
            xs, hn = out_proj(parts, w_out_b, l, xs, g_ffn, 256)
            dense_te = jnp.full((T // tm,), e, jnp.int32)
            dense_nx = jnp.full((T // tm,), -1, jnp.int32)
            dense_nv = jnp.full((T // tm,), tm, jnp.int32)
            act = swiglu_up(dense_te[::2], dense_nx[::2], 2 * dense_nv[::2], dense_na // 2, hn, ffn_wg, ffn_wu,
                            2 * tm, UP_CHUNK)
            xs = swiglu_down(dense_te, dense_nx, dense_nv, dense_na, act, ffn_wd, xs, tm, 512)
            if l == depth - 1:
                xs = final_norm(xs, row(final_norm_g), tm)
        else:
            w_r = jnp.pad(router_w[e], ((0, 0), (0, LANES - N_EXPERTS)))
            xs, = out_proj(parts, w_out_b, l, xs, None, 256)
            route = router(xs, g_ffn, w_r, tm)
            tile_expert, next_expert, valid_rows, n_active, tok_of_slot, slot0, slot1 = _routing_tables(
                route, tm, e * N_EXPERTS)
            hg = gather_norm_tokens(tok_of_slot, n_active, xs, g_ffn, tm)
            act = swiglu_up(tile_expert, next_expert, valid_rows, n_active, hg, exp_wg, exp_wu, tm, UP_TN)
            ys = swiglu_down(tile_expert, next_expert, valid_rows, n_active, act, exp_wd, None, tm, 1024)
            xs = moe_combine(slot0, slot1, xs, ys, route, row(final_norm_g), 256, final=(l == depth - 1))
    return xs[None]
```

```python
import functools
import math

import jax
import jax.numpy as jnp
from jax import lax
from jax.experimental import pallas as pl
from jax.experimental.pallas import tpu as pltpu

F32 = jnp.float32
BF16 = jnp.bfloat16

D_MODEL = 2048
GROUP_WIDTH = 512
GM_CHUNK = 128
GM_HEADS = 4
DA_HEADS = 4
DA_QK_DIM = 64
CV_WIDTH = 31
NSA_HEADS = 8
NSA_KV_HEADS = 2
NSA_HEAD_DIM = 64
NSA_CMP_LEN = 32
NSA_CMP_STRIDE = 16
NSA_SLC_LEN = 64
NSA_SLC_TOPK = 16
NSA_WINDOW = 512
ROPE_THETA = 10000.0
NORM_EPS = 1e-6
NEG = -1e30
NSA_FORCED = 3
N_EXPERTS = 8
LANES = 128
SUBLANES = 8
LOG2E = math.log2(math.e)

IN_WIDTH = 4888
IN_WIDTH_PAD = 4992
COL_GM = 0
COL_QDA = 1024
COL_KDA = 1536
COL_VDA = 2048
COL_CV = 2560
COL_QNS = 3584
COL_KVNS = 4096
COL_GNS = 4864

VMEM_LIMIT = 56 * 1024 * 1024


def _cparams(sem, **kw):
    return pltpu.CompilerParams(dimension_semantics=sem, vmem_limit_bytes=VMEM_LIMIT, **kw)


def _rms(x, g):
    ms = jnp.mean(x * x, axis=-1, keepdims=True)
    return x * lax.rsqrt(ms + NORM_EPS) * g


def _layer_norm(x, g, b):
    mu = jnp.mean(x, axis=-1, keepdims=True)
    xc = x - mu
    var = jnp.mean(xc * xc, axis=-1, keepdims=True)
    return xc * lax.rsqrt(var + NORM_EPS) * g + b


def _dot(a, b):
    return jnp.dot(a, b, preferred_element_type=F32)


def _dot_nt(a, b):
    return lax.dot_general(a, b, (((1,), (1,)), ((), ())), preferred_element_type=F32)


def _lane(shape):
    return lax.broadcasted_iota(jnp.int32, shape, len(shape) - 1)


def _row(shape):
    return lax.broadcasted_iota(jnp.int32, shape, len(shape) - 2)


def _norm_mm_body(x_ref, g_ref, w_ref, o_ref, hn_ref):
    @pl.when(pl.program_id(1) == 0)
    def _():
        hn_ref[...] = _rms(x_ref[...], g_ref[...]).astype(BF16)

    o_ref[...] = _dot(hn_ref[...], w_ref[0]).astype(o_ref.dtype)


def norm_mm(x, g, w, layer, tm, tn):
    T, K = x.shape
    N = w.shape[2]
    return pl.pallas_call(
        _norm_mm_body,
        grid=(T // tm, N // tn),
        in_specs=[pl.BlockSpec((tm, K), lambda i, j: (i, 0)),
                  pl.BlockSpec((1, K), lambda i, j: (0, 0)),
                  pl.BlockSpec((1, K, tn), lambda i, j: (layer, 0, j))],
        out_specs=pl.BlockSpec((tm, tn), lambda i, j: (i, j)),
        out_shape=jax.ShapeDtypeStruct((T, N), BF16),
        scratch_shapes=[pltpu.VMEM((tm, K), BF16)],
        compiler_params=_cparams(("parallel", "arbitrary")),
        name="norm_in_proj",
    )(x, g, w)


def _gmlp_body(z_ref, g_ref, b_ref, ws_ref, bs_ref, o_ref):
    tr = z_ref.shape[0]
    z = jax.nn.gelu(z_ref[...].astype(F32))
    u = z[:, :GROUP_WIDTH]
    v = _layer_norm(z[:, GROUP_WIDTH:], g_ref[...], b_ref[...]).astype(BF16)
    causal = _row((GM_CHUNK, GM_CHUNK)) >= _lane((GM_CHUNK, GM_CHUNK))
    bias = bs_ref[...]
    for h in range(GM_HEADS):
        w = jnp.where(causal, ws_ref[h], 0.0).astype(BF16)
        cols = slice(h * LANES, (h + 1) * LANES)
        for c in range(tr // GM_CHUNK):
            rows = slice(c * GM_CHUNK, (c + 1) * GM_CHUNK)
            s = _dot(w, v[rows, cols]) + bias[:, cols]
            o_ref[rows, cols] = (u[rows, cols] * s).astype(BF16)


def gmlp(z, ln_g, ln_b, ws, bs_rows, tr):
    T = z.shape[0]
    return pl.pallas_call(
        _gmlp_body,
        grid=(T // tr,),
        in_specs=[pl.BlockSpec((tr, 2 * GROUP_WIDTH), lambda i: (i, COL_GM // (2 * GROUP_WIDTH))),
                  pl.BlockSpec((1, GROUP_WIDTH), lambda i: (0, 0)),
                  pl.BlockSpec((1, GROUP_WIDTH), lambda i: (0, 0)),
                  pl.BlockSpec((GM_HEADS, GM_CHUNK, GM_CHUNK), lambda i: (0, 0, 0)),
                  pl.BlockSpec((GM_CHUNK, GROUP_WIDTH), lambda i: (0, 0))],
        out_specs=pl.BlockSpec((tr, GROUP_WIDTH), lambda i: (i, 0)),
        out_shape=jax.ShapeDtypeStruct((T, GROUP_WIDTH), BF16),
        compiler_params=_cparams(("parallel",)),
        name="gmlp",
    )(z, ln_g, ln_b, ws, bs_rows)


CV_HALO = 32
CV_SUB = 64


def _conv_body(a_ref, g_ref, ap_ref, gp_ref, w_ref, b_ref, lg_ref, lb_ref, o_ref, hs_ref):
    tr = a_ref.shape[0]
    first = pl.program_id(0) == 0
    prev = ap_ref[...].astype(F32) * jax.nn.sigmoid(gp_ref[...].astype(F32))
    hs_ref[0:CV_HALO, :] = jnp.where(first, 0.0, prev)
    hs_ref[CV_HALO:CV_HALO + tr, :] = a_ref[...].astype(F32) * jax.nn.sigmoid(g_ref[...].astype(F32))
    hs_ref[CV_HALO + tr:, :] = jnp.zeros((SUBLANES, GROUP_WIDTH), F32)
    w = w_ref[...]
    lead = CV_HALO - (CV_WIDTH - 1)
    for r0 in range(0, tr, CV_SUB):
        acc = jnp.zeros((CV_SUB, GROUP_WIDTH), F32)
        for phase in range(SUBLANES):
            base, shift = divmod(lead + phase, SUBLANES)
            part = jnp.zeros((CV_SUB + SUBLANES, GROUP_WIDTH), F32)
            for k in range(phase, CV_WIDTH, SUBLANES):
                start = r0 + SUBLANES * (base + k // SUBLANES)
                part = part + hs_ref[start:start + CV_SUB + SUBLANES, :] * w[k:k + 1, :]
            acc = acc + part[shift:shift + CV_SUB]
        y = _layer_norm(acc + b_ref[...], lg_ref[...], lb_ref[...])
        o_ref[r0:r0 + CV_SUB, :] = (y * jax.nn.sigmoid(y)).astype(BF16)


def conformer_conv(z, dw_w, dw_b, ln_g, ln_b, tr):
    T = z.shape[0]
    ca = COL_CV // GROUP_WIDTH
    per = tr // CV_HALO

    def halo(col):
        return lambda i: (jnp.maximum(i * per - 1, 0), col)

    return pl.pallas_call(
        _conv_body,
        grid=(T // tr,),
        in_specs=[pl.BlockSpec((tr, GROUP_WIDTH), lambda i: (i, ca)),
                  pl.BlockSpec((tr, GROUP_WIDTH), lambda i: (i, ca + 1)),
                  pl.BlockSpec((CV_HALO, GROUP_WIDTH), halo(ca)),
                  pl.BlockSpec((CV_HALO, GROUP_WIDTH), halo(ca + 1)),
                  pl.BlockSpec((CV_HALO, GROUP_WIDTH), lambda i: (0, 0)),
                  pl.BlockSpec((1, GROUP_WIDTH), lambda i: (0, 0)),
                  pl.BlockSpec((1, GROUP_WIDTH), lambda i: (0, 0)),
                  pl.BlockSpec((1, GROUP_WIDTH), lambda i: (0, 0))],
        out_specs=pl.BlockSpec((tr, GROUP_WIDTH), lambda i: (i, 0)),
        out_shape=jax.ShapeDtypeStruct((T, GROUP_WIDTH), BF16),
        scratch_shapes=[pltpu.VMEM((CV_HALO + tr + SUBLANES, GROUP_WIDTH), F32)],
        compiler_params=_cparams(("parallel",)),
        name="conformer_conv",
    )(z, z, z, z, dw_w, dw_b, ln_g, ln_b)


def _rope128(x, cos, sin_signed):
    x = x.astype(F32)
    lo = (_lane(x.shape) & 63) < 32
    rot = jnp.where(lo, pltpu.roll(x, 96, 1), pltpu.roll(x, 32, 1))
    return x * cos + rot * sin_signed


def _low_half(x, fill=0.0):
    return jnp.where(_lane(x.shape) < 64, x, fill)


def _high_half_to_low(x, fill=0.0):
    return jnp.where(_lane(x.shape) < 64, pltpu.roll(x, 64, 1), fill)


def _prep_body(qd_ref, kd_ref, vd_ref, qn_ref, kvc_ref, kvs_ref, kvw_ref, gt_ref, cos_ref, sin_ref, ge_ref,
               qd_o, kd_o, vd_o, qn_o, kc_o, vc_o, ks_o, vs_o, kw_o, vw_o, gx_o):
    tr = cos_ref.shape[0]
    cos = cos_ref[...]
    sin = sin_ref[...]
    q_scale = DA_QK_DIM ** -0.5 * LOG2E
    ones = jnp.ones((tr, LANES), BF16)
    for h in range(DA_HEADS):
        cols = slice(h * LANES, (h + 1) * LANES)
        qd_o[h] = (_rope128(qd_ref[:, cols], cos, sin) * q_scale).astype(BF16)
        kd_o[h] = _rope128(kd_ref[:, cols], cos, sin).astype(BF16)
        vd_o[h, :, 0:LANES] = vd_ref[:, cols].astype(BF16)
        vd_o[h, :, LANES:] = ones
    for c in range(NSA_HEADS // 2):
        q = _rope128(qn_ref[:, c * LANES:(c + 1) * LANES], cos, sin) * (NSA_HEAD_DIM ** -0.5 * LOG2E)
        qn_o[2 * c] = _low_half(q).astype(BF16)
        qn_o[2 * c + 1] = _high_half_to_low(q).astype(BF16)
    kc_o[...] = _rope128(kvc_ref[:, :LANES], cos, sin)
    vc_o[...] = kvc_ref[:, LANES:].astype(F32)
    blk = (pl.program_id(0) * tr + _row((tr, LANES))) >> 6
    onehot = jnp.where(blk == _lane((tr, LANES)), 1.0, 0.0).astype(BF16)
    k = _rope128(kvs_ref[:, :LANES], cos, sin)
    for g, half in enumerate((_low_half, _high_half_to_low)):
        ks_o[g, :, 0:LANES] = onehot
        ks_o[g, :, LANES:] = half(k).astype(BF16)
        vs_o[g] = half(kvs_ref[:, LANES:].astype(F32), 1.0).astype(BF16)
    k = _rope128(kvw_ref[:, :LANES], cos, sin)
    for g, half in enumerate((_low_half, _high_half_to_low)):
        kw_o[g] = half(k).astype(BF16)
        vw_o[g] = half(kvw_ref[:, LANES:].astype(F32), 1.0).astype(BF16)
    expand = ge_ref[...]
    gx = sum(_dot(part, expand) for part in _split3(jax.nn.sigmoid(gt_ref[...].astype(F32))))
    for c in range(3):
        gx_o[c] = gx[:, c * GROUP_WIDTH:(c + 1) * GROUP_WIDTH]


def _gate_expansion():
    col = jnp.arange(3 * GROUP_WIDTH)
    src = (col // GROUP_WIDTH) * NSA_HEADS + (col % GROUP_WIDTH) // NSA_HEAD_DIM
    return (jnp.arange(LANES)[:, None] == src[None, :]).astype(BF16)


def prep(z, cos128, sin128, tr):
    T = z.shape[0]
    G = NSA_KV_HEADS

    def zspec(width, col):
        return pl.BlockSpec((tr, width), lambda i: (i, col // width))

    def heads(n, width=LANES):
        return pl.BlockSpec((n, tr, width), lambda i: (0, i, 0))

    def hshape(n, width=LANES):
        return jax.ShapeDtypeStruct((n, T, width), BF16)

    row128 = pl.BlockSpec((tr, LANES), lambda i: (i, 0))
    return pl.pallas_call(
        _prep_body,
        grid=(T // tr,),
        in_specs=[zspec(512, COL_QDA), zspec(512, COL_KDA), zspec(512, COL_VDA), zspec(512, COL_QNS),
                  zspec(256, COL_KVNS), zspec(256, COL_KVNS + 256), zspec(256, COL_KVNS + 512),
                  zspec(128, COL_GNS), row128, row128,
                  pl.BlockSpec((LANES, 3 * GROUP_WIDTH), lambda i: (0, 0))],
        out_specs=[heads(DA_HEADS), heads(DA_HEADS), heads(DA_HEADS, 2 * LANES),
                   heads(NSA_HEADS), row128, row128, heads(G, 2 * LANES), heads(G), heads(G), heads(G),
                   pl.BlockSpec((3, tr, GROUP_WIDTH), lambda i: (0, i, 0))],
        out_shape=[hshape(DA_HEADS), hshape(DA_HEADS), hshape(DA_HEADS, 2 * LANES),
                   hshape(NSA_HEADS), jax.ShapeDtypeStruct((T, LANES), F32),
                   jax.ShapeDtypeStruct((T, LANES), F32), hshape(G, 2 * LANES), hshape(G), hshape(G), hshape(G),
                   jax.ShapeDtypeStruct((3, T, GROUP_WIDTH), F32)],
        compiler_params=_cparams(("parallel",)),
        name="attention_prep",
    )(z, z, z, z, z, z, z, z, cos128, sin128, _gate_expansion())


def _lane_tile(x, n):
    return x if n == 1 else jnp.concatenate([x] * n, axis=-1)


def _softmax_update(s, v, m_ref, acc_ref):
    m_prev = m_ref[...]
    m_new = jnp.maximum(m_prev, jnp.max(s, axis=-1, keepdims=True))
    alpha = jnp.exp2(m_prev - m_new)
    p = jnp.exp2(s - _lane_tile(m_new, s.shape[1] // LANES))
    acc_ref[...] = _lane_tile(alpha, acc_ref.shape[1] // LANES) * acc_ref[...] + _dot(p.astype(BF16), v)
    m_ref[...] = m_new


SWEEP_UNROLL = 4


def _causal_sweep(qs_ref, k_ref, v_ref, m_ref, acc_ref, s_ref, q0, tq, tk, next_queries):
    m_ref[...] = jnp.full(m_ref.shape, -jnp.inf, F32)
    acc_ref[...] = jnp.zeros(acc_ref.shape, F32)

    def keys(ref, t):
        return ref[0, pl.ds(pl.multiple_of(t * tk, tk), tk), :]

    def scores(t):
        return _dot_nt(qs_ref[...], keys(k_ref, t))

    n_full = q0 // tk

    @pl.when(q0 == 0)
    def _():
        s_ref[...] = scores(0)

    def run(t0, count):
        s_cur = s_ref[...]
        for u in range(count):
            s_next = scores(t0 + u + 1)
            _softmax_update(s_cur, keys(v_ref, t0 + u), m_ref, acc_ref)
            s_cur = s_next
        s_ref[...] = s_cur

    def several(i, c):
        run(SWEEP_UNROLL * i, SWEEP_UNROLL)
        return c

    def single(t, c):
        run(t, 1)
        return c

    n_groups = n_full // SWEEP_UNROLL
    lax.fori_loop(0, n_groups, several, 0)
    lax.fori_loop(n_groups * SWEEP_UNROLL, n_full, single, 0)
    s = s_ref[...]
    visible = n_full * tk + _lane(s.shape) <= q0 + (_row(s.shape) & (tq - 1))
    s_ref[...] = _dot_nt(next_queries(), keys(k_ref, 0))
    _softmax_update(jnp.where(visible, s, NEG), keys(v_ref, n_full), m_ref, acc_ref)


def _diff_attn_body(q_ref, qnext_ref, k_ref, v_ref, lam_ref, sg_ref, o_ref, qs_ref, m_ref, acc_ref, s_ref, *,
                    tq, tk, lambda_init):
    def stacked(q):
        first = _lane(q.shape) < DA_QK_DIM
        return jnp.where(first, q, jnp.zeros_like(q)), jnp.where(first, jnp.zeros_like(q), q)

    qs_ref[0:tq, :], qs_ref[tq:, :] = stacked(q_ref[0])
    _causal_sweep(qs_ref, k_ref, v_ref, m_ref, acc_ref, s_ref, pl.program_id(1) * tq, tq, tk,
                  lambda: jnp.concatenate(stacked(qnext_ref[0]), axis=0))
    lam = lam_ref[...]
    lam_full = (jnp.exp(jnp.sum(lam[0:1] * lam[1:2], axis=-1, keepdims=True))
                - jnp.exp(jnp.sum(lam[2:3] * lam[3:4], axis=-1, keepdims=True)) + lambda_init)
    o = acc_ref[:, 0:LANES] / acc_ref[:, LANES:]
    a = o[0:tq] - lam_full * o[tq:]
    o_ref[...] = (_rms(a, sg_ref[...]) * (1.0 - lambda_init)).astype(BF16)


def diff_attention(qd, kd, vd, lam, subln, lambda_init, tq, tk):
    H, T, _ = qd.shape
    last = T // tq - 1
    return pl.pallas_call(
        functools.partial(_diff_attn_body, tq=tq, tk=tk, lambda_init=lambda_init),
        grid=(H, T // tq),
        in_specs=[pl.BlockSpec((1, tq, LANES), lambda h, qi: (h, qi, 0)),
                  pl.BlockSpec((1, tq, LANES), lambda h, qi: (h, jnp.minimum(qi + 1, last), 0)),
                  pl.BlockSpec((1, T, LANES), lambda h, qi: (h, 0, 0)),
                  pl.BlockSpec((1, T, 2 * LANES), lambda h, qi: (h, 0, 0)),
                  pl.BlockSpec((4, DA_QK_DIM), lambda h, qi: (0, 0)),
                  pl.BlockSpec((1, LANES), lambda h, qi: (0, 0))],
        out_specs=pl.BlockSpec((tq, LANES), lambda h, qi: (qi, h)),
        out_shape=jax.ShapeDtypeStruct((T, GROUP_WIDTH), BF16),
        scratch_shapes=[pltpu.VMEM((2 * tq, LANES), BF16), pltpu.VMEM((2 * tq, LANES), F32),
                        pltpu.VMEM((2 * tq, 2 * LANES), F32), pltpu.VMEM((2 * tq, tk), F32)],
        compiler_params=_cparams(("arbitrary", "arbitrary")),
        name="diff_attention",
    )(qd, qd, kd, vd, lam, subln)


def _compress_body(x_ref, pe_ref, w1_ref, w2_ref, o_ref):
    x = x_ref[0, 0]
    half = x.shape[1]
    a = _dot((x + pe_ref[0, 0:1, :]).astype(BF16), w1_ref[0, 0:half, :])
    b = _dot((x + pe_ref[0, 1:2, :]).astype(BF16), w1_ref[0, half:, :])
    hid = jax.nn.gelu(a + pltpu.roll(b, b.shape[0] - 1, 0))
    o_ref[0, 0] = _dot(hid.astype(BF16), w2_ref[0]).astype(BF16)


def compress(x16, pe2, w1, w2p):
    _, G, n, half = x16.shape
    hid = w1.shape[2]
    return pl.pallas_call(
        _compress_body,
        grid=(2, G),
        in_specs=[pl.BlockSpec((1, 1, n, half), lambda c, g: (c, g, 0, 0)),
                  pl.BlockSpec((1, 2, half), lambda c, g: (c, 0, 0)),
                  pl.BlockSpec((1, 2 * half, hid), lambda c, g: (c, 0, 0)),
                  pl.BlockSpec((1, hid, LANES), lambda c, g: (c, 0, 0))],
        out_specs=pl.BlockSpec((1, 1, n, LANES), lambda c, g: (c, g, 0, 0)),
        out_shape=jax.ShapeDtypeStruct((2, G, n, LANES), BF16),
        compiler_params=_cparams(("parallel", "parallel")),
        name="nsa_compress",
    )(x16, pe2, w1, w2p)


def _split3(x):
    hi = x.astype(BF16)
    r1 = x - hi.astype(F32)
    mid = r1.astype(BF16)
    lo = (r1 - mid.astype(F32)).astype(BF16)
    return hi, mid, lo


def _pack_heads(o, tq):
    pair = lambda a, b: a + pltpu.roll(b, 64, 1)
    return jnp.concatenate([pair(o[0:tq], o[tq:2 * tq]), pair(o[2 * tq:3 * tq], o[3 * tq:])], axis=-1)


def _normalize_low_half(acc):
    return jnp.where(_lane(acc.shape) < 64, acc / pltpu.roll(acc, 64, 1), 0.0)


def _cmp_select_body(q_ref, kc_ref, vc_ref, ovt_ref, gx_ref, o_ref, sb_ref, *, tq):
    qi = pl.program_id(1)
    hpg = q_ref.shape[0]
    q = q_ref[...].reshape(hpg * tq, LANES)
    s = _dot_nt(q, kc_ref[0, 0])
    t = qi * tq + (_row(s.shape) & (tq - 1))
    cmask = _lane(s.shape) * NSA_CMP_STRIDE + (NSA_CMP_LEN - 1) <= t
    s = jnp.where(cmask, s, NEG)
    e = jnp.exp2(s - jnp.max(s, axis=-1, keepdims=True))
    p = jnp.where(cmask, e / jnp.sum(e, axis=-1, keepdims=True), 0.0)
    o = _dot(p.astype(BF16), vc_ref[0, 0])
    o_ref[...] = (gx_ref[0] * _pack_heads(o, tq)).astype(BF16)

    psum = p[0:tq]
    for hh in range(1, hpg):
        psum = psum + p[hh * tq:(hh + 1) * tq]
    ovt = ovt_ref[...]
    imp = sum(_dot_nt(ovt, part) for part in _split3(psum))
    j = _row(imp.shape)
    cur = (qi * tq + _lane(imp.shape)) >> 6
    forced = (j == 0) | (j == cur) | (j == cur - 1)
    score = jnp.where(forced, -2.0, jnp.where(j <= cur, imp, -1.0))
    jf = j.astype(F32)
    bias = jnp.where(forced, 0.0, NEG)
    for _ in range(NSA_SLC_TOPK - NSA_FORCED):
        m = jnp.max(score, axis=0, keepdims=True)
        first = jnp.min(jnp.where(score == m, jf, float(LANES)), axis=0, keepdims=True)
        hit = jf == first
        bias = jnp.where(hit, jnp.where(m >= 0.0, 0.0, NEG), bias)
        score = jnp.where(hit, -2.0, score)
    sb_ref[0] = bias.T.astype(BF16)


def cmp_select(qn, kvc, overlap_t, gx, tq):
    H, T, _ = qn.shape
    G = NSA_KV_HEADS
    hpg = H // G
    n = kvc.shape[2]
    return pl.pallas_call(
        functools.partial(_cmp_select_body, tq=tq),
        grid=(G, T // tq),
        in_specs=[pl.BlockSpec((hpg, tq, LANES), lambda g, qi: (g, qi, 0)),
                  pl.BlockSpec((1, 1, n, LANES), lambda g, qi: (0, g, 0, 0)),
                  pl.BlockSpec((1, 1, n, LANES), lambda g, qi: (1, g, 0, 0)),
                  pl.BlockSpec((LANES, n), lambda g, qi: (0, 0)),
                  pl.BlockSpec((1, tq, hpg * NSA_HEAD_DIM), lambda g, qi: (0, qi, g))],
        out_specs=[pl.BlockSpec((tq, hpg * NSA_HEAD_DIM), lambda g, qi: (qi, g)),
                   pl.BlockSpec((1, tq, LANES), lambda g, qi: (g, qi, 0))],
        out_shape=[jax.ShapeDtypeStruct((T, GROUP_WIDTH), BF16),
                   jax.ShapeDtypeStruct((G, T, LANES), BF16)],
        compiler_params=_cparams(("parallel", "parallel")),
        name="nsa_compressed_select",
    )(qn, kvc, kvc, overlap_t, gx)


def _sel_attn_body(q_ref, sb_ref, qnext_ref, sbnext_ref, k_ref, v_ref, gx_ref, o_ref, qs_ref, m_ref, acc_ref,
                   s_ref, *, tq, tk):
    hpg = q_ref.shape[0]
    for hh in range(hpg):
        qs_ref[hh * tq:(hh + 1) * tq, 0:LANES] = sb_ref[0]
        qs_ref[hh * tq:(hh + 1) * tq, LANES:] = q_ref[hh]

    def next_queries():
        bias = sbnext_ref[0]
        return jnp.concatenate([jnp.concatenate([bias, qnext_ref[hh]], axis=-1) for hh in range(hpg)], axis=0)

    _causal_sweep(qs_ref, k_ref, v_ref, m_ref, acc_ref, s_ref, pl.program_id(1) * tq, tq, tk, next_queries)
    o = _normalize_low_half(acc_ref[...])
    o_ref[...] = (gx_ref[0] * _pack_heads(o, tq)).astype(BF16)


def sel_attention(qn, selbias, ks, vs, gx, tq, tk):
    H, T, _ = qn.shape
    G = NSA_KV_HEADS
    hpg = H // G
    last = T // tq - 1
    return pl.pallas_call(
        functools.partial(_sel_attn_body, tq=tq, tk=tk),
        grid=(G, T // tq),
        in_specs=[pl.BlockSpec((hpg, tq, LANES), lambda g, qi: (g, qi, 0)),
                  pl.BlockSpec((1, tq, LANES), lambda g, qi: (g, qi, 0)),
                  pl.BlockSpec((hpg, tq, LANES), lambda g, qi: (g, jnp.minimum(qi + 1, last), 0)),
                  pl.BlockSpec((1, tq, LANES), lambda g, qi: (g, jnp.minimum(qi + 1, last), 0)),
                  pl.BlockSpec((1, T, 2 * LANES), lambda g, qi: (g, 0, 0)),
                  pl.BlockSpec((1, T, LANES), lambda g, qi: (g, 0, 0)),
                  pl.BlockSpec((1, tq, hpg * NSA_HEAD_DIM), lambda g, qi: (1, qi, g))],
        out_specs=pl.BlockSpec((tq, hpg * NSA_HEAD_DIM), lambda g, qi: (qi, g)),
        out_shape=jax.ShapeDtypeStruct((T, GROUP_WIDTH), BF16),
        scratch_shapes=[pltpu.VMEM((hpg * tq, 2 * LANES), BF16), pltpu.VMEM((hpg * tq, LANES), F32),
                        pltpu.VMEM((hpg * tq, LANES), F32), pltpu.VMEM((hpg * tq, tk), F32)],
        compiler_params=_cparams(("arbitrary", "arbitrary")),
        name="nsa_selected_attention",
    )(qn, selbias, qn, selbias, ks, vs, gx)


def _win_attn_body(q_ref, k_ref, v_ref, b_ref, gx_ref, o_ref, *, tq, span):
    hpg = q_ref.shape[0]
    q0 = pl.program_id(1) * tq
    lo = pl.multiple_of(jnp.maximum(q0 + tq - span, 0), tq)
    q = q_ref[...].reshape(hpg * tq, LANES)
    s = _dot_nt(q, k_ref[0, pl.ds(lo, span), :])

    def finish(s):
        p = jnp.exp2(s - jnp.max(s, axis=-1, keepdims=True))
        acc = _dot(p.astype(BF16), v_ref[0, pl.ds(lo, span), :])
        o_ref[...] = (gx_ref[0] * _pack_heads(_normalize_low_half(acc), tq)).astype(BF16)

    @pl.when(q0 + tq >= span)
    def _():
        finish(s + jnp.concatenate([b_ref[...]] * hpg, axis=0))

    @pl.when(q0 + tq < span)
    def _():
        qpos = q0 + (_row(s.shape) & (tq - 1))
        kpos = lo + _lane(s.shape)
        finish(jnp.where((kpos <= qpos) & (kpos > qpos - NSA_WINDOW), s, NEG))


def _window_bias(tq, span):
    d = jnp.arange(span)[None, :] - jnp.arange(tq)[:, None] + (tq - span)
    return jnp.where((d <= 0) & (d > -NSA_WINDOW), 0.0, NEG).astype(F32)


def win_attention(qn, kw, vw, gx, tq):
    H, T, _ = qn.shape
    G = NSA_KV_HEADS
    hpg = H // G
    span = NSA_WINDOW + tq
    return pl.pallas_call(
        functools.partial(_win_attn_body, tq=tq, span=span),
        grid=(G, T // tq),
        in_specs=[pl.BlockSpec((hpg, tq, LANES), lambda g, qi: (g, qi, 0)),
                  pl.BlockSpec((1, T, LANES), lambda g, qi: (g, 0, 0)),
                  pl.BlockSpec((1, T, LANES), lambda g, qi: (g, 0, 0)),
                  pl.BlockSpec((tq, span), lambda g, qi: (0, 0)),
                  pl.BlockSpec((1, tq, hpg * NSA_HEAD_DIM), lambda g, qi: (2, qi, g))],
        out_specs=pl.BlockSpec((tq, hpg * NSA_HEAD_DIM), lambda g, qi: (qi, g)),
        out_shape=jax.ShapeDtypeStruct((T, GROUP_WIDTH), BF16),
        compiler_params=_cparams(("parallel", "arbitrary")),
        name="nsa_window_attention",
    )(qn, kw, vw, _window_bias(tq, span), gx)


def _top2_route(h, w_router):
    logits = jnp.dot(h, w_router, preferred_element_type=F32, precision=lax.Precision.HIGHEST)
    lane = _lane(logits.shape)
    lf = lane.astype(F32)
    logits = jnp.where(lane < N_EXPERTS, logits, -jnp.inf)
    v0 = jnp.max(logits, axis=-1, keepdims=True)
    i0 = jnp.min(jnp.where(logits == v0, lf, float(LANES)), axis=-1, keepdims=True)
    rest = jnp.where(lf == i0, -jnp.inf, logits)
    v1 = jnp.max(rest, axis=-1, keepdims=True)
    i1 = jnp.min(jnp.where(rest == v1, lf, float(LANES)), axis=-1, keepdims=True)
    e1 = jnp.exp(v1 - v0)
    w0 = 1.0 / (1.0 + e1)
    w1 = e1 / (1.0 + e1)
    return jnp.where(lane == 0, i0, jnp.where(lane == 1, i1, jnp.where(lane == 2, w0, w1)))


def _out_proj_body(a_ref, b_ref, c_ref, d1_ref, d2_ref, d3_ref, w_ref, x_ref, *rest, normed):
    o_ref, lhs_ref = rest[-3 if normed else -2], rest[-1]
    lhs_ref[:, 0:GROUP_WIDTH] = a_ref[...]
    lhs_ref[:, GROUP_WIDTH:2 * GROUP_WIDTH] = b_ref[...]
    lhs_ref[:, 2 * GROUP_WIDTH:3 * GROUP_WIDTH] = c_ref[...]
    d = d1_ref[...].astype(F32) + d2_ref[...].astype(F32) + d3_ref[...].astype(F32)
    lhs_ref[:, 3 * GROUP_WIDTH:] = d.astype(BF16)
    y = x_ref[...] + _dot(lhs_ref[...], w_ref[0])
    o_ref[...] = y
    if normed:
        g_ref, hn_ref = rest[0], rest[2]
        hn_ref[...] = _rms(y, g_ref[...]).astype(BF16)


def out_proj(parts, w, layer, x, g_next, tm):
    T, N = x.shape
    part = pl.BlockSpec((tm, GROUP_WIDTH), lambda i: (i, 0))
    rows = pl.BlockSpec((tm, N), lambda i: (i, 0))
    in_specs = [part] * 6 + [pl.BlockSpec((1, 4 * GROUP_WIDTH, N), lambda i: (layer, 0, 0)), rows]
    out_specs = [rows]
    out_shape = [jax.ShapeDtypeStruct((T, N), F32)]
    operands = [*parts, w, x]
    if g_next is not None:
        in_specs.append(pl.BlockSpec((1, N), lambda i: (0, 0)))
        out_specs.append(rows)
        out_shape.append(jax.ShapeDtypeStruct((T, N), BF16))
        operands.append(g_next)
    return pl.pallas_call(
        functools.partial(_out_proj_body, normed=g_next is not None),
        grid=(T // tm,),
        in_specs=in_specs,
        out_specs=out_specs,
        out_shape=out_shape,
        scratch_shapes=[pltpu.VMEM((tm, 4 * GROUP_WIDTH), BF16)],
        compiler_params=_cparams(("parallel",)),
        name="out_proj",
    )(*operands)


UP_TN = 1408
UP_CHUNK = 512
CAST_ROWS = 256


def _group_starts(te_ref, i):
    return (i == 0) | (te_ref[i] != te_ref[jnp.maximum(i - 1, 0)])


def _stream_group_weights(te_ref, nx_ref, w_hbms, wst_ref, wb_refs, sem_ref):
    j = pl.program_id(0)
    i = pl.program_id(1)
    tn = wst_ref.shape[2]

    def copies(e, jj):
        cols = pl.ds(pl.multiple_of(jj * tn, tn), tn)
        return [pltpu.make_async_copy(w.at[e, :, cols], wst_ref.at[n], sem_ref) for n, w in enumerate(w_hbms)]

    @pl.when((j == 0) & (i == 0))
    def _():
        for c in copies(te_ref[0], 0):
            c.start()

    @pl.when(_group_starts(te_ref, i))
    def _():
        for c in copies(te_ref[i], j):
            c.wait()
        def cast_rows(c, carry):
            rows = pl.ds(pl.multiple_of(c * CAST_ROWS, CAST_ROWS), CAST_ROWS)
            for n, wb_ref in enumerate(wb_refs):
                wb_ref[rows, :] = wst_ref[n, rows, :].astype(BF16)
            return carry

        lax.fori_loop(0, wst_ref.shape[1] // CAST_ROWS, cast_rows, 0)
        in_sweep = nx_ref[i] >= 0

        @pl.when(in_sweep | (j + 1 < pl.num_programs(0)))
        def _():
            for c in copies(jnp.where(in_sweep, nx_ref[i], te_ref[0]), jnp.where(in_sweep, j, j + 1)):
                c.start()


def _weight_stream_scratch(n_weights, k, tn):
    return [pltpu.VMEM((n_weights, k, tn), F32)] + [pltpu.VMEM((k, tn), BF16)] * n_weights + [
        pltpu.SemaphoreType.DMA(())]


def _by_fill(nv_ref, o_ref, compute):
    nv = nv_ref[pl.program_id(1)]
    half = o_ref.shape[0] // 2

    @pl.when(nv > half)
    def _():
        o_ref[...] = compute(slice(None))

    @pl.when((nv > 0) & (nv <= half))
    def _():
        o_ref[0:half, :] = compute(slice(0, half))
        o_ref[half:, :] = jnp.zeros((half, o_ref.shape[1]), o_ref.dtype)

    @pl.when(nv == 0)
    def _():
        o_ref[...] = jnp.zeros(o_ref.shape, o_ref.dtype)


def _up_body(te_ref, nx_ref, nv_ref, na_ref, h_ref, wg_hbm, wu_hbm, o_ref, wst_ref, wgb_ref, wub_ref, sem_ref):
    _stream_group_weights(te_ref, nx_ref, (wg_hbm, wu_hbm), wst_ref, (wgb_ref, wub_ref), sem_ref)

    def swiglu(rows):
        h = h_ref[rows, :]
        parts = []
        for c0 in range(0, o_ref.shape[1], UP_CHUNK):
            cols = slice(c0, min(c0 + UP_CHUNK, o_ref.shape[1]))
            a = _dot(h, wgb_ref[:, cols])
            parts.append((a * jax.nn.sigmoid(a) * _dot(h, wub_ref[:, cols])).astype(BF16))
        return jnp.concatenate(parts, axis=-1)

    _by_fill(nv_ref, o_ref, swiglu)


def swiglu_up(tile_expert, next_expert, valid_rows, n_active, hn, wg, wu, tm, tn):
    P, K = hn.shape
    F = wg.shape[2]
    return pl.pallas_call(
        _up_body,
        grid_spec=pltpu.PrefetchScalarGridSpec(
            num_scalar_prefetch=4,
            grid=(F // tn, P // tm),
            in_specs=[pl.BlockSpec((tm, K), lambda j, i, te, nx, nv, na: (jnp.minimum(i, na[0] - 1), 0)),
                      pl.BlockSpec(memory_space=pl.ANY), pl.BlockSpec(memory_space=pl.ANY)],
            out_specs=pl.BlockSpec((tm, tn), lambda j, i, te, nx, nv, na: (i, j)),
            scratch_shapes=_weight_stream_scratch(2, K, tn)),
        out_shape=jax.ShapeDtypeStruct((P, F), BF16),
        compiler_params=_cparams(("arbitrary", "arbitrary")),
        name="swiglu_up",
    )(tile_expert, next_expert, valid_rows, n_active, hn, wg, wu)


def _down_body(te_ref, nx_ref, nv_ref, na_ref, a_ref, w_hbm, *rest, residual):
    r_ref = rest[0] if residual else None
    o_ref, wst_ref, wb_ref, sem_ref = rest[1:] if residual else rest
    _stream_group_weights(te_ref, nx_ref, (w_hbm,), wst_ref, (wb_ref,), sem_ref)

    def project(rows):
        y = _dot(a_ref[rows, :], wb_ref[...])
        return r_ref[rows, :] + y if residual else y

    _by_fill(nv_ref, o_ref, project)


def swiglu_down(tile_expert, next_expert, valid_rows, n_active, act, wd, residual, tm, tn):
    P, F = act.shape
    N = wd.shape[2]
    tile = pl.BlockSpec((tm, tn), lambda j, i, te, nx, nv, na: (i, j))
    in_specs = [pl.BlockSpec((tm, F), lambda j, i, te, nx, nv, na: (jnp.minimum(i, na[0] - 1), 0)),
                pl.BlockSpec(memory_space=pl.ANY)]
    operands = [act, wd]
    if residual is not None:
        in_specs.append(tile)
        operands.append(residual)
    return pl.pallas_call(
        functools.partial(_down_body, residual=residual is not None),
        grid_spec=pltpu.PrefetchScalarGridSpec(
            num_scalar_prefetch=4,
            grid=(N // tn, P // tm),
            in_specs=in_specs,
            out_specs=tile,
            scratch_shapes=_weight_stream_scratch(1, F, tn)),
        out_shape=jax.ShapeDtypeStruct((P, N), F32),
        compiler_params=_cparams(("arbitrary", "arbitrary")),
        name="swiglu_down",
    )(tile_expert, next_expert, valid_rows, n_active, *operands)


def _router_body(x_ref, g_ref, w_ref, o_ref):
    o_ref[...] = _top2_route(_rms(x_ref[...], g_ref[...]), w_ref[...])


def router(x, g, w_pad, tm):
    T, K = x.shape
    return pl.pallas_call(
        _router_body,
        grid=(T // tm,),
        in_specs=[pl.BlockSpec((tm, K), lambda i: (i, 0)),
                  pl.BlockSpec((1, K), lambda i: (0, 0)),
                  pl.BlockSpec((K, LANES), lambda i: (0, 0))],
        out_specs=pl.BlockSpec((tm, LANES), lambda i: (i, 0)),
        out_shape=jax.ShapeDtypeStruct((T, LANES), F32),
        compiler_params=_cparams(("parallel",)),
        name="moe_router",
    )(x, g, w_pad)


ROW_DMA_UNROLL = 8


def _row_copy(src_hbm, row, dst_ref, r, sem):
    return pltpu.make_async_copy(src_hbm.at[pl.ds(row, 1)], dst_ref.at[pl.ds(r, 1)], sem)


def _gather_body(tok_ref, na_ref, x_hbm, g_ref, o_ref, buf_ref, sem_ref):
    i = pl.program_id(0)
    tm = buf_ref.shape[1]

    def fetch(tile):
        slot = tile % 2

        def start(r, c):
            _row_copy(x_hbm, tok_ref[tile * tm + r], buf_ref.at[slot], r, sem_ref.at[slot]).start()
            return c

        lax.fori_loop(0, tm, start, 0, unroll=ROW_DMA_UNROLL)

    @pl.when(i == 0)
    def _():
        fetch(0)

    @pl.when(i + 1 < na_ref[0])
    def _():
        fetch(i + 1)

    @pl.when(i < na_ref[0])
    def _():
        slot = i % 2

        def wait(r, c):
            _row_copy(x_hbm, 0, buf_ref.at[slot], r, sem_ref.at[slot]).wait()
            return c

        lax.fori_loop(0, tm, wait, 0, unroll=ROW_DMA_UNROLL)
        o_ref[...] = _rms(buf_ref[slot], g_ref[...]).astype(BF16)

    @pl.when(i >= na_ref[0])
    def _():
        o_ref[...] = jnp.zeros(o_ref.shape, BF16)


def gather_norm_tokens(tok_of_slot, n_active, x, g, tm):
    P = tok_of_slot.shape[0]
    K = x.shape[1]
    return pl.pallas_call(
        _gather_body,
        grid_spec=pltpu.PrefetchScalarGridSpec(
            num_scalar_prefetch=2,
            grid=(P // tm,),
            in_specs=[pl.BlockSpec(memory_space=pl.ANY), pl.BlockSpec((1, K), lambda i, tok, na: (0, 0))],
            out_specs=pl.BlockSpec((tm, K), lambda i, tok, na: (i, 0)),
            scratch_shapes=[pltpu.VMEM((2, tm, K), F32), pltpu.SemaphoreType.DMA((2,))]),
        out_shape=jax.ShapeDtypeStruct((P, K), BF16),
        compiler_params=_cparams(("arbitrary",)),
        name="moe_gather",
    )(tok_of_slot, n_active, x, g)


def _combine_body(s0_ref, s1_ref, x_ref, y_hbm, rt_ref, g_ref, o_ref, b0_ref, b1_ref, sem0, sem1, *, final):
    tm = x_ref.shape[0]
    base = pl.program_id(0) * tm

    def start(r, c):
        _row_copy(y_hbm, s0_ref[base + r], b0_ref, r, sem0).start()
        _row_copy(y_hbm, s1_ref[base + r], b1_ref, r, sem1).start()
        return c

    def wait(r, c):
        _row_copy(y_hbm, 0, b0_ref, r, sem0).wait()
        _row_copy(y_hbm, 0, b1_ref, r, sem1).wait()
        return c

    lax.fori_loop(0, tm, start, 0, unroll=ROW_DMA_UNROLL)
    lax.fori_loop(0, tm, wait, 0, unroll=ROW_DMA_UNROLL)
    gates = rt_ref[...]
    y = x_ref[...] + gates[:, 2:3] * b0_ref[...] + gates[:, 3:4] * b1_ref[...]
    o_ref[...] = _rms(y, g_ref[...]) if final else y


def moe_combine(slot0, slot1, x, ys, route, g, tm, final):
    T, K = x.shape
    return pl.pallas_call(
        functools.partial(_combine_body, final=final),
        grid_spec=pltpu.PrefetchScalarGridSpec(
            num_scalar_prefetch=2,
            grid=(T // tm,),
            in_specs=[pl.BlockSpec((tm, K), lambda i, s0, s1: (i, 0)),
                      pl.BlockSpec(memory_space=pl.ANY),
                      pl.BlockSpec((tm, LANES), lambda i, s0, s1: (i, 0)),
                      pl.BlockSpec((1, K), lambda i, s0, s1: (0, 0))],
            out_specs=pl.BlockSpec((tm, K), lambda i, s0, s1: (i, 0)),
            scratch_shapes=[pltpu.VMEM((tm, K), F32), pltpu.VMEM((tm, K), F32),
                            pltpu.SemaphoreType.DMA(()), pltpu.SemaphoreType.DMA(())]),
        out_shape=jax.ShapeDtypeStruct((T, K), F32),
        compiler_params=_cparams(("arbitrary",)),
        name="moe_combine",
    )(slot0, slot1, x, ys, route, g)


def _final_norm_body(x_ref, g_ref, o_ref):
    o_ref[...] = _rms(x_ref[...], g_ref[...])


def final_norm(x, g, tm):
    T, K = x.shape
    return pl.pallas_call(
        _final_norm_body,
        grid=(T // tm,),
        in_specs=[pl.BlockSpec((tm, K), lambda i: (i, 0)), pl.BlockSpec((1, K), lambda i: (0, 0))],
        out_specs=pl.BlockSpec((tm, K), lambda i: (i, 0)),
        out_shape=jax.ShapeDtypeStruct((T, K), F32),
        compiler_params=_cparams(("parallel",)),
        name="final_norm",
    )(x, g)


def _routing_tables(route, tm, expert_base):
    T = route.shape[0]
    top_i = route[:, 0:2].astype(jnp.int32)
    e_flat = top_i.reshape(-1)
    onehot = (e_flat[:, None] == jnp.arange(N_EXPERTS)[None, :]).astype(jnp.int32)
    rank = jnp.take_along_axis(jnp.cumsum(onehot, axis=0) - onehot, e_flat[:, None], axis=1)[:, 0]
    count = jnp.sum(onehot, axis=0)
    padded = ((count + tm - 1) // tm) * tm
    end = jnp.cumsum(padded)
    start = end - padded
    slot = start[e_flat] + rank
    n_slots = 2 * T + N_EXPERTS * tm
    n_tiles = n_slots // tm
    n_active = (end[-1] // tm).astype(jnp.int32)
    tile_start = jnp.minimum(jnp.arange(n_tiles, dtype=jnp.int32), n_active - 1) * tm
    tile_expert = jnp.minimum(jnp.sum(tile_start[:, None] >= end[None, :], axis=1), N_EXPERTS - 1)
    group_end = end[tile_expert] // tm
    next_expert = jnp.where(group_end < n_active, tile_expert[jnp.minimum(group_end, n_tiles - 1)] + expert_base, -1)
    tile_index = jnp.arange(n_tiles, dtype=jnp.int32)
    valid_rows = jnp.where(tile_index < n_active,
                           jnp.clip(start[tile_expert] + count[tile_expert] - tile_index * tm, 0, tm), 0)
    tok_of_slot = jnp.zeros((n_slots,), jnp.int32).at[slot].set(jnp.arange(2 * T, dtype=jnp.int32) // 2)
    slot2 = slot.reshape(T, 2).astype(jnp.int32)
    return ((tile_expert + expert_base).astype(jnp.int32), next_expert.astype(jnp.int32),
            valid_rows.astype(jnp.int32), n_active.reshape(1), tok_of_slot, slot2[:, 0], slot2[:, 1])


def _rope_tables128(T):
    half = NSA_HEAD_DIM // 2
    inv = ROPE_THETA ** (-jnp.arange(0, NSA_HEAD_DIM, 2, dtype=F32) / NSA_HEAD_DIM)
    ang = jnp.arange(T, dtype=F32)[:, None] * inv[None, :]
    cos, sin = jnp.cos(ang), jnp.sin(ang)
    cos128 = jnp.tile(cos, (1, LANES // half))
    sin128 = jnp.tile(jnp.concatenate([-sin, sin], axis=1), (1, LANES // NSA_HEAD_DIM))
    return cos128, sin128


def _overlap_matrix_t(n_cmp_pad):
    sstart = jnp.arange(LANES) * NSA_SLC_LEN
    cstart = jnp.arange(n_cmp_pad) * NSA_CMP_STRIDE
    ov = (cstart[None, :] < sstart[:, None] + NSA_SLC_LEN) & (cstart[None, :] + NSA_CMP_LEN > sstart[:, None])
    return ov.astype(BF16)


def kernel(x, attn_norm, w_in, w_out, gm_ln_g, gm_ln_b, gm_ws, gm_bs, da_lambda, da_subln, cv_dw_w, cv_dw_b,
           cv_ln_g, cv_ln_b, nsa_cmp_w1, nsa_cmp_w2, nsa_cmp_pe, ffn_norm, ffn_wg, ffn_wu, ffn_wd, router_w,
           exp_wg, exp_wu, exp_wd, final_norm_g):
    B, T, D = x.shape
    assert B == 1 and D == D_MODEL and T % 1024 == 0 and T // NSA_SLC_LEN <= LANES
    depth = w_in.shape[0]
    G = NSA_KV_HEADS
    n16 = T // NSA_CMP_STRIDE
    tm = 512

    cos128, sin128 = _rope_tables128(T)
    overlap_t = _overlap_matrix_t(n16)
    dense_na = jnp.full((1,), T // tm, jnp.int32)
    row = lambda v: v.reshape(1, -1)
    merge = lambda w: w.reshape((-1,) + w.shape[2:])
    exp_wg, exp_wu, exp_wd = merge(exp_wg), merge(exp_wu), merge(exp_wd)
    w_in_b = jnp.concatenate(
        [w_in.astype(BF16), jnp.zeros(w_in.shape[:2] + (IN_WIDTH_PAD - IN_WIDTH,), BF16)], axis=-1)
    w_out_b = w_out.astype(BF16)

    xs = x[0]
    for l in range(depth):
        lambda_init = 0.8 - 0.6 * math.exp(-0.3 * l)
        z = norm_mm(xs, row(attn_norm[l]), w_in_b, l, 1024, IN_WIDTH_PAD // 3)

        bs_rows = jnp.repeat(gm_bs[l].T, GM_CHUNK, axis=1)
        o_a = gmlp(z, row(gm_ln_g[l]), row(gm_ln_b[l]), gm_ws[l], bs_rows, 512)
        dw_w = jnp.pad(cv_dw_w[l], ((0, CV_HALO - CV_WIDTH), (0, 0)))
        o_c = conformer_conv(z, dw_w, row(cv_dw_b[l]), row(cv_ln_g[l]), row(cv_ln_b[l]), 256)

        qd, kd, vd, qn, kc, vc, ks, vs, kw, vw, gx = prep(z, cos128, sin128, 256)
        o_b = diff_attention(qd, kd, vd, da_lambda[l], row(da_subln[l]), lambda_init, 512, 512)

        x16 = jnp.stack([kc, vc]).reshape(2, T, G, NSA_HEAD_DIM).transpose(0, 2, 1, 3)
        x16 = x16.reshape(2, G, n16, NSA_CMP_STRIDE * NSA_HEAD_DIM)
        pe2 = nsa_cmp_pe[l].reshape(2, 2, NSA_CMP_STRIDE * NSA_HEAD_DIM)
        w2p = jnp.pad(nsa_cmp_w2[l], ((0, 0), (0, 0), (0, LANES - NSA_HEAD_DIM))).astype(BF16)
        kvc = compress(x16, pe2, nsa_cmp_w1[l].astype(BF16), w2p)
        o_cmp, selbias = cmp_select(qn, kvc, overlap_t, gx, 256)
        o_sel = sel_attention(qn, selbias, ks, vs, gx, 256, 512)
        o_win = win_attention(qn, kw, vw, gx, 256)

        g_ffn = row(ffn_norm[l])
        parts = (o_a, o_b, o_c, o_cmp, o_sel, o_win)
        e = l // 2
        if l % 2 == 0:
            xs, hn = out_proj(parts, w_out_b, l, xs, g_ffn, 256)
            dense_te = jnp.full((T // tm,), e, jnp.int32)
            dense_nx = jnp.full((T // tm,), -1, jnp.int32)
            dense_nv = jnp.full((T // tm,), tm, jnp.int32)
            act = swiglu_up(dense_te[::2], dense_nx[::2], 2 * dense_nv[::2], dense_na // 2, hn, ffn_wg, ffn_wu,
                            2 * tm, UP_CHUNK)
            xs = swiglu_down(dense_te, dense_nx, dense_nv, dense_na, act, ffn_wd, xs, tm, 512)
            if l == depth - 1:
                xs = final_norm(xs, row(final_norm_g), tm)
        else:
            w_r = jnp.pad(router_w[e], ((0, 0), (0, LANES - N_EXPERTS)))
            xs, = out_proj(parts, w_out_b, l, xs, None, 256)
            route = router(xs, g_ffn, w_r, tm)
            tile_expert, next_expert, valid_rows, n_active, tok_of_slot, slot0, slot1 = _routing_tables(
                route, tm, e * N_EXPERTS)
            hg = gather_norm_tokens(tok_of_slot, n_active, xs, g_ffn, tm)
            act = swiglu_up(tile_expert, next_expert, valid_rows, n_active, hg, exp_wg, exp_wu, tm, UP_TN)
            ys = swiglu_down(tile_expert, next_expert, valid_rows, n_active, act, exp_wd, None, tm, 1024)
            xs = moe_combine(slot0, slot1, xs, ys, route, row(final_norm_g), 256, final=(l == depth - 1))
    return xs[None]
```

```python
import functools
import math

import jax
import jax.numpy as jnp
from jax import lax
from jax.experimental import pallas as pl
from jax.experimental.pallas import tpu as pltpu

F32 = jnp.float32
BF16 = jnp.bfloat16

D_MODEL = 2048
GROUP_WIDTH = 512
GM_CHUNK = 128
GM_HEADS = 4
DA_HEADS = 4
DA_QK_DIM = 64
CV_WIDTH = 31
NSA_HEADS = 8
NSA_KV_HEADS = 2
NSA_HEAD_DIM = 64
NSA_CMP_LEN = 32
NSA_CMP_STRIDE = 16
NSA_SLC_LEN = 64
NSA_SLC_TOPK = 16
NSA_WINDOW = 512
ROPE_THETA = 10000.0
NORM_EPS = 1e-6
NEG = -1e30
NSA_FORCED = 3
N_EXPERTS = 8
LANES = 128
SUBLANES = 8
LOG2E = math.log2(math.e)

IN_WIDTH = 4888
IN_WIDTH_PAD = 4992
COL_GM = 0
COL_QDA = 1024
COL_KDA = 1536
COL_VDA = 2048
COL_CV = 2560
COL_QNS = 3584
COL_KVNS = 4096
COL_GNS = 4864

VMEM_LIMIT = 56 * 1024 * 1024


def _cparams(sem, **kw):
    return pltpu.CompilerParams(dimension_semantics=sem, vmem_limit_bytes=VMEM_LIMIT, **kw)


def _rms(x, g):
    ms = jnp.mean(x * x, axis=-1, keepdims=True)
    return x * lax.rsqrt(ms + NORM_EPS) * g


def _layer_norm(x, g, b):
    mu = jnp.mean(x, axis=-1, keepdims=True)
    xc = x - mu
    var = jnp.mean(xc * xc, axis=-1, keepdims=True)
    return xc * lax.rsqrt(var + NORM_EPS) * g + b


def _dot(a, b):
    return jnp.dot(a, b, preferred_element_type=F32)


def _dot_nt(a, b):
    return lax.dot_general(a, b, (((1,), (1,)), ((), ())), preferred_element_type=F32)


def _lane(shape):
    return lax.broadcasted_iota(jnp.int32, shape, len(shape) - 1)


def _row(shape):
    return lax.broadcasted_iota(jnp.int32, shape, len(shape) - 2)


def _norm_mm_body(x_ref, g_ref, w_ref, o_ref, hn_ref):
    @pl.when(pl.program_id(1) == 0)
    def _():
        hn_ref[...] = _rms(x_ref[...], g_ref[...]).astype(BF16)

    o_ref[...] = _dot(hn_ref[...], w_ref[0]).astype(o_ref.dtype)


def norm_mm(x, g, w, layer, tm, tn):
    T, K = x.shape
    N = w.shape[2]
    return pl.pallas_call(
        _norm_mm_body,
        grid=(T // tm, N // tn),
        in_specs=[pl.BlockSpec((tm, K), lambda i, j: (i, 0)),
                  pl.BlockSpec((1, K), lambda i, j: (0, 0)),
                  pl.BlockSpec((1, K, tn), lambda i, j: (layer, 0, j))],
        out_specs=pl.BlockSpec((tm, tn), lambda i, j: (i, j)),
        out_shape=jax.ShapeDtypeStruct((T, N), BF16),
        scratch_shapes=[pltpu.VMEM((tm, K), BF16)],
        compiler_params=_cparams(("parallel", "arbitrary")),
        name="norm_in_proj",
    )(x, g, w)


def _gmlp_body(z_ref, g_ref, b_ref, ws_ref, bs_ref, o_ref):
    tr = z_ref.shape[0]
    z = jax.nn.gelu(z_ref[...].astype(F32))
    u = z[:, :GROUP_WIDTH]
    v = _layer_norm(z[:, GROUP_WIDTH:], g_ref[...], b_ref[...]).astype(BF16)
    causal = _row((GM_CHUNK, GM_CHUNK)) >= _lane((GM_CHUNK, GM_CHUNK))
    bias = bs_ref[...]
    for h in range(GM_HEADS):
        w = jnp.where(causal, ws_ref[h], 0.0).astype(BF16)
        cols = slice(h * LANES, (h + 1) * LANES)
        for c in range(tr // GM_CHUNK):
            rows = slice(c * GM_CHUNK, (c + 1) * GM_CHUNK)
            s = _dot(w, v[rows, cols]) + bias[:, cols]
            o_ref[rows, cols] = (u[rows, cols] * s).astype(BF16)


def gmlp(z, ln_g, ln_b, ws, bs_rows, tr):
    T = z.shape[0]
    return pl.pallas_call(
        _gmlp_body,
        grid=(T // tr,),
        in_specs=[pl.BlockSpec((tr, 2 * GROUP_WIDTH), lambda i: (i, COL_GM // (2 * GROUP_WIDTH))),
                  pl.BlockSpec((1, GROUP_WIDTH), lambda i: (0, 0)),
                  pl.BlockSpec((1, GROUP_WIDTH), lambda i: (0, 0)),
                  pl.BlockSpec((GM_HEADS, GM_CHUNK, GM_CHUNK), lambda i: (0, 0, 0)),
                  pl.BlockSpec((GM_CHUNK, GROUP_WIDTH), lambda i: (0, 0))],
        out_specs=pl.BlockSpec((tr, GROUP_WIDTH), lambda i: (i, 0)),
        out_shape=jax.ShapeDtypeStruct((T, GROUP_WIDTH), BF16),
        compiler_params=_cparams(("parallel",)),
        name="gmlp",
    )(z, ln_g, ln_b, ws, bs_rows)


CV_HALO = 32
CV_SUB = 64


def _conv_body(a_ref, g_ref, ap_ref, gp_ref, w_ref, b_ref, lg_ref, lb_ref, o_ref, hs_ref):
    tr = a_ref.shape[0]
    first = pl.program_id(0) == 0
    prev = ap_ref[...].astype(F32) * jax.nn.sigmoid(gp_ref[...].astype(F32))
    hs_ref[0:CV_HALO, :] = jnp.where(first, 0.0, prev)
    hs_ref[CV_HALO:CV_HALO + tr, :] = a_ref[...].astype(F32) * jax.nn.sigmoid(g_ref[...].astype(F32))
    hs_ref[CV_HALO + tr:, :] = jnp.zeros((SUBLANES, GROUP_WIDTH), F32)
    w = w_ref[...]
    lead = CV_HALO - (CV_WIDTH - 1)
    for r0 in range(0, tr, CV_SUB):
        acc = jnp.zeros((CV_SUB, GROUP_WIDTH), F32)
        for phase in range(SUBLANES):
            base, shift = divmod(lead + phase, SUBLANES)
            part = jnp.zeros((CV_SUB + SUBLANES, GROUP_WIDTH), F32)
            for k in range(phase, CV_WIDTH, SUBLANES):
                start = r0 + SUBLANES * (base + k // SUBLANES)
                part = part + hs_ref[start:start + CV_SUB + SUBLANES, :] * w[k:k + 1, :]
            acc = acc + part[shift:shift + CV_SUB]
        y = _layer_norm(acc + b_ref[...], lg_ref[...], lb_ref[...])
        o_ref[r0:r0 + CV_SUB, :] = (y * jax.nn.sigmoid(y)).astype(BF16)


def conformer_conv(z, dw_w, dw_b, ln_g, ln_b, tr):
    T = z.shape[0]
    ca = COL_CV // GROUP_WIDTH
    per = tr // CV_HALO

    def halo(col):
        return lambda i: (jnp.maximum(i * per - 1, 0), col)

    return pl.pallas_call(
        _conv_body,
        grid=(T // tr,),
        in_specs=[pl.BlockSpec((tr, GROUP_WIDTH), lambda i: (i, ca)),
                  pl.BlockSpec((tr, GROUP_WIDTH), lambda i: (i, ca + 1)),
                  pl.BlockSpec((CV_HALO, GROUP_WIDTH), halo(ca)),
                  pl.BlockSpec((CV_HALO, GROUP_WIDTH), halo(ca + 1)),
                  pl.BlockSpec((CV_HALO, GROUP_WIDTH), lambda i: (0, 0)),
                  pl.BlockSpec((1, GROUP_WIDTH), lambda i: (0, 0)),
                  pl.BlockSpec((1, GROUP_WIDTH), lambda i: (0, 0)),
                  pl.BlockSpec((1, GROUP_WIDTH), lambda i: (0, 0))],
        out_specs=pl.BlockSpec((tr, GROUP_WIDTH), lambda i: (i, 0)),
        out_shape=jax.ShapeDtypeStruct((T, GROUP_WIDTH), BF16),
        scratch_shapes=[pltpu.VMEM((CV_HALO + tr + SUBLANES, GROUP_WIDTH), F32)],
        compiler_params=_cparams(("parallel",)),
        name="conformer_conv",
    )(z, z, z, z, dw_w, dw_b, ln_g, ln_b)


def _rope128(x, cos, sin_signed):
    x = x.astype(F32)
    lo = (_lane(x.shape) & 63) < 32
    rot = jnp.where(lo, pltpu.roll(x, 96, 1), pltpu.roll(x, 32, 1))
    return x * cos + rot * sin_signed


def _low_half(x, fill=0.0):
    return jnp.where(_lane(x.shape) < 64, x, fill)


def _high_half_to_low(x, fill=0.0):
    return jnp.where(_lane(x.shape) < 64, pltpu.roll(x, 64, 1), fill)


def _prep_body(qd_ref, kd_ref, vd_ref, qn_ref, kvc_ref, kvs_ref, kvw_ref, gt_ref, cos_ref, sin_ref, ge_ref,
               qd_o, kd_o, vd_o, qn_o, kc_o, vc_o, ks_o, vs_o, kw_o, vw_o, gx_o):
    tr = cos_ref.shape[0]
    cos = cos_ref[...]
    sin = sin_ref[...]
    q_scale = DA_QK_DIM ** -0.5 * LOG2E
    ones = jnp.ones((tr, LANES), BF16)
    for h in range(DA_HEADS):
        cols = slice(h * LANES, (h + 1) * LANES)
        qd_o[h] = (_rope128(qd_ref[:, cols], cos, sin) * q_scale).astype(BF16)
        kd_o[h] = _rope128(kd_ref[:, cols], cos, sin).astype(BF16)
        vd_o[h, :, 0:LANES] = vd_ref[:, cols].astype(BF16)
        vd_o[h, :, LANES:] = ones
    for c in range(NSA_HEADS // 2):
        q = _rope128(qn_ref[:, c * LANES:(c + 1) * LANES], cos, sin) * (NSA_HEAD_DIM ** -0.5 * LOG2E)
        qn_o[2 * c] = _low_half(q).astype(BF16)
        qn_o[2 * c + 1] = _high_half_to_low(q).astype(BF16)
    kc_o[...] = _rope128(kvc_ref[:, :LANES], cos, sin)
    vc_o[...] = kvc_ref[:, LANES:].astype(F32)
    blk = (pl.program_id(0) * tr + _row((tr, LANES))) >> 6
    onehot = jnp.where(blk == _lane((tr, LANES)), 1.0, 0.0).astype(BF16)
    k = _rope128(kvs_ref[:, :LANES], cos, sin)
    for g, half in enumerate((_low_half, _high_half_to_low)):
        ks_o[g, :, 0:LANES] = onehot
        ks_o[g, :, LANES:] = half(k).astype(BF16)
        vs_o[g] = half(kvs_ref[:, LANES:].astype(F32), 1.0).astype(BF16)
    k = _rope128(kvw_ref[:, :LANES], cos, sin)
    for g, half in enumerate((_low_half, _high_half_to_low)):
        kw_o[g] = half(k).astype(BF16)
        vw_o[g] = half(kvw_ref[:, LANES:].astype(F32), 1.0).astype(BF16)
    expand = ge_ref[...]
    gx = sum(_dot(part, expand) for part in _split3(jax.nn.sigmoid(gt_ref[...].astype(F32))))
    for c in range(3):
        gx_o[c] = gx[:, c * GROUP_WIDTH:(c + 1) * GROUP_WIDTH]


def _gate_expansion():
    col = jnp.arange(3 * GROUP_WIDTH)
    src = (col // GROUP_WIDTH) * NSA_HEADS + (col % GROUP_WIDTH) // NSA_HEAD_DIM
    return (jnp.arange(LANES)[:, None] == src[None, :]).astype(BF16)


def prep(z, cos128, sin128, tr):
    T = z.shape[0]
    G = NSA_KV_HEADS

    def zspec(width, col):
        return pl.BlockSpec((tr, width), lambda i: (i, col // width))

    def heads(n, width=LANES):
        return pl.BlockSpec((n, tr, width), lambda i: (0, i, 0))

    def hshape(n, width=LANES):
        return jax.ShapeDtypeStruct((n, T, width), BF16)

    row128 = pl.BlockSpec((tr, LANES), lambda i: (i, 0))
    return pl.pallas_call(
        _prep_body,
        grid=(T // tr,),
        in_specs=[zspec(512, COL_QDA), zspec(512, COL_KDA), zspec(512, COL_VDA), zspec(512, COL_QNS),
                  zspec(256, COL_KVNS), zspec(256, COL_KVNS + 256), zspec(256, COL_KVNS + 512),
                  zspec(128, COL_GNS), row128, row128,
                  pl.BlockSpec((LANES, 3 * GROUP_WIDTH), lambda i: (0, 0))],
        out_specs=[heads(DA_HEADS), heads(DA_HEADS), heads(DA_HEADS, 2 * LANES),
                   heads(NSA_HEADS), row128, row128, heads(G, 2 * LANES), heads(G), heads(G), heads(G),
                   pl.BlockSpec((3, tr, GROUP_WIDTH), lambda i: (0, i, 0))],
        out_shape=[hshape(DA_HEADS), hshape(DA_HEADS), hshape(DA_HEADS, 2 * LANES),
                   hshape(NSA_HEADS), jax.ShapeDtypeStruct((T, LANES), F32),
                   jax.ShapeDtypeStruct((T, LANES), F32), hshape(G, 2 * LANES), hshape(G), hshape(G), hshape(G),
                   jax.ShapeDtypeStruct((3, T, GROUP_WIDTH), F32)],
        compiler_params=_cparams(("parallel",)),
        name="attention_prep",
    )(z, z, z, z, z, z, z, z, cos128, sin128, _gate_expansion())


def _lane_tile(x, n):
    return x if n == 1 else jnp.concatenate([x] * n, axis=-1)


def _softmax_update(s, v, m_ref, acc_ref):
    m_prev = m_ref[...]
    m_new = jnp.maximum(m_prev, jnp.max(s, axis=-1, keepdims=True))
    alpha = jnp.exp2(m_prev - m_new)
    p = jnp.exp2(s - _lane_tile(m_new, s.shape[1] // LANES))
    acc_ref[...] = _lane_tile(alpha, acc_ref.shape[1] // LANES) * acc_ref[...] + _dot(p.astype(BF16), v)
    m_ref[...] = m_new


SWEEP_UNROLL = 4


def _causal_sweep(qs_ref, k_ref, v_ref, m_ref, acc_ref, s_ref, q0, tq, tk, next_queries):
    m_ref[...] = jnp.full(m_ref.shape, -jnp.inf, F32)
    acc_ref[...] = jnp.zeros(acc_ref.shape, F32)

    def keys(ref, t):
        return ref[0, pl.ds(pl.multiple_of(t * tk, tk), tk), :]

    def scores(t):
        return _dot_nt(qs_ref[...], keys(k_ref, t))

    n_full = q0 // tk

    @pl.when(q0 == 0)
    def _():
        s_ref[...] = scores(0)

    def run(t0, count):
        s_cur = s_ref[...]
        for u in range(count):
            s_next = scores(t0 + u + 1)
            _softmax_update(s_cur, keys(v_ref, t0 + u), m_ref, acc_ref)
            s_cur = s_next
        s_ref[...] = s_cur

    def several(i, c):
        run(SWEEP_UNROLL * i, SWEEP_UNROLL)
        return c

    def single(t, c):
        run(t, 1)
        return c

    n_groups = n_full // SWEEP_UNROLL
    lax.fori_loop(0, n_groups, several, 0)
    lax.fori_loop(n_groups * SWEEP_UNROLL, n_full, single, 0)
    s = s_ref[...]
    visible = n_full * tk + _lane(s.shape) <= q0 + (_row(s.shape) & (tq - 1))
    s_ref[...] = _dot_nt(next_queries(), keys(k_ref, 0))
    _softmax_update(jnp.where(visible, s, NEG), keys(v_ref, n_full), m_ref, acc_ref)


def _diff_attn_body(q_ref, qnext_ref, k_ref, v_ref, lam_ref, sg_ref, o_ref, qs_ref, m_ref, acc_ref, s_ref, *,
                    tq, tk, lambda_init):
    def stacked(q):
        first = _lane(q.shape) < DA_QK_DIM
        return jnp.where(first, q, jnp.zeros_like(q)), jnp.where(first, jnp.zeros_like(q), q)

    qs_ref[0:tq, :], qs_ref[tq:, :] = stacked(q_ref[0])
    _causal_sweep(qs_ref, k_ref, v_ref, m_ref, acc_ref, s_ref, pl.program_id(1) * tq, tq, tk,
                  lambda: jnp.concatenate(stacked(qnext_ref[0]), axis=0))
    lam = lam_ref[...]
    lam_full = (jnp.exp(jnp.sum(lam[0:1] * lam[1:2], axis=-1, keepdims=True))
                - jnp.exp(jnp.sum(lam[2:3] * lam[3:4], axis=-1, keepdims=True)) + lambda_init)
    o = acc_ref[:, 0:LANES] / acc_ref[:, LANES:]
    a = o[0:tq] - lam_full * o[tq:]
    o_ref[...] = (_rms(a, sg_ref[...]) * (1.0 - lambda_init)).astype(BF16)


def diff_attention(qd, kd, vd, lam, subln, lambda_init, tq, tk):
    H, T, _ = qd.shape
    last = T // tq - 1
    return pl.pallas_call(
        functools.partial(_diff_attn_body, tq=tq, tk=tk, lambda_init=lambda_init),
        grid=(H, T // tq),
        in_specs=[pl.BlockSpec((1, tq, LANES), lambda h, qi: (h, qi, 0)),
                  pl.BlockSpec((1, tq, LANES), lambda h, qi: (h, jnp.minimum(qi + 1, last), 0)),
                  pl.BlockSpec((1, T, LANES), lambda h, qi: (h, 0, 0)),
                  pl.BlockSpec((1, T, 2 * LANES), lambda h, qi: (h, 0, 0)),
                  pl.BlockSpec((4, DA_QK_DIM), lambda h, qi: (0, 0)),
                  pl.BlockSpec((1, LANES), lambda h, qi: (0, 0))],
        out_specs=pl.BlockSpec((tq, LANES), lambda h, qi: (qi, h)),
        out_shape=jax.ShapeDtypeStruct((T, GROUP_WIDTH), BF16),
        scratch_shapes=[pltpu.VMEM((2 * tq, LANES), BF16), pltpu.VMEM((2 * tq, LANES), F32),
                        pltpu.VMEM((2 * tq, 2 * LANES), F32), pltpu.VMEM((2 * tq, tk), F32)],
        compiler_params=_cparams(("arbitrary", "arbitrary")),
        name="diff_attention",
    )(qd, qd, kd, vd, lam, subln)


def _compress_body(x_ref, pe_ref, w1_ref, w2_ref, o_ref):
    x = x_ref[0, 0]
    half = x.shape[1]
    a = _dot((x + pe_ref[0, 0:1, :]).astype(BF16), w1_ref[0, 0:half, :])
    b = _dot((x + pe_ref[0, 1:2, :]).astype(BF16), w1_ref[0, half:, :])
    hid = jax.nn.gelu(a + pltpu.roll(b, b.shape[0] - 1, 0))
    o_ref[0, 0] = _dot(hid.astype(BF16), w2_ref[0]).astype(BF16)


def compress(x16, pe2, w1, w2p):
    _, G, n, half = x16.shape
    hid = w1.shape[2]
    return pl.pallas_call(
        _compress_body,
        grid=(2, G),
        in_specs=[pl.BlockSpec((1, 1, n, half), lambda c, g: (c, g, 0, 0)),
                  pl.BlockSpec((1, 2, half), lambda c, g: (c, 0, 0)),
                  pl.BlockSpec((1, 2 * half, hid), lambda c, g: (c, 0, 0)),
                  pl.BlockSpec((1, hid, LANES), lambda c, g: (c, 0, 0))],
        out_specs=pl.BlockSpec((1, 1, n, LANES), lambda c, g: (c, g, 0, 0)),
        out_shape=jax.ShapeDtypeStruct((2, G, n, LANES), BF16),
        compiler_params=_cparams(("parallel", "parallel")),
        name="nsa_compress",
    )(x16, pe2, w1, w2p)


def _split3(x):
    hi = x.astype(BF16)
    r1 = x - hi.astype(F32)
    mid = r1.astype(BF16)
    lo = (r1 - mid.astype(F32)).astype(BF16)
    return hi, mid, lo


def _pack_heads(o, tq):
    pair = lambda a, b: a + pltpu.roll(b, 64, 1)
    return jnp.concatenate([pair(o[0:tq], o[tq:2 * tq]), pair(o[2 * tq:3 * tq], o[3 * tq:])], axis=-1)


def _normalize_low_half(acc):
    return jnp.where(_lane(acc.shape) < 64, acc / pltpu.roll(acc, 64, 1), 0.0)


def _cmp_select_body(q_ref, kc_ref, vc_ref, ovt_ref, gx_ref, o_ref, sb_ref, *, tq):
    qi = pl.program_id(1)
    hpg = q_ref.shape[0]
    q = q_ref[...].reshape(hpg * tq, LANES)
    s = _dot_nt(q, kc_ref[0, 0])
    t = qi * tq + (_row(s.shape) & (tq - 1))
    cmask = _lane(s.shape) * NSA_CMP_STRIDE + (NSA_CMP_LEN - 1) <= t
    s = jnp.where(cmask, s, NEG)
    e = jnp.exp2(s - jnp.max(s, axis=-1, keepdims=True))
    p = jnp.where(cmask, e / jnp.sum(e, axis=-1, keepdims=True), 0.0)
    o = _dot(p.astype(BF16), vc_ref[0, 0])
    o_ref[...] = (gx_ref[0] * _pack_heads(o, tq)).astype(BF16)

    psum = p[0:tq]
    for hh in range(1, hpg):
        psum = psum + p[hh * tq:(hh + 1) * tq]
    ovt = ovt_ref[...]
    imp = sum(_dot_nt(ovt, part) for part in _split3(psum))
    j = _row(imp.shape)
    cur = (qi * tq + _lane(imp.shape)) >> 6
    forced = (j == 0) | (j == cur) | (j == cur - 1)
    score = jnp.where(forced, -2.0, jnp.where(j <= cur, imp, -1.0))
    jf = j.astype(F32)
    bias = jnp.where(forced, 0.0, NEG)
    for _ in range(NSA_SLC_TOPK - NSA_FORCED):
        m = jnp.max(score, axis=0, keepdims=True)
        first = jnp.min(jnp.where(score == m, jf, float(LANES)), axis=0, keepdims=True)
        hit = jf == first
        bias = jnp.where(hit, jnp.where(m >= 0.0, 0.0, NEG), bias)
        score = jnp.where(hit, -2.0, score)
    sb_ref[0] = bias.T.astype(BF16)


def cmp_select(qn, kvc, overlap_t, gx, tq):
    H, T, _ = qn.shape
    G = NSA_KV_HEADS
    hpg = H // G
    n = kvc.shape[2]
    return pl.pallas_call(
        functools.partial(_cmp_select_body, tq=tq),
        grid=(G, T // tq),
        in_specs=[pl.BlockSpec((hpg, tq, LANES), lambda g, qi: (g, qi, 0)),
                  pl.BlockSpec((1, 1, n, LANES), lambda g, qi: (0, g, 0, 0)),
                  pl.BlockSpec((1, 1, n, LANES), lambda g, qi: (1, g, 0, 0)),
                  pl.BlockSpec((LANES, n), lambda g, qi: (0, 0)),
                  pl.BlockSpec((1, tq, hpg * NSA_HEAD_DIM), lambda g, qi: (0, qi, g))],
        out_specs=[pl.BlockSpec((tq, hpg * NSA_HEAD_DIM), lambda g, qi: (qi, g)),
                   pl.BlockSpec((1, tq, LANES), lambda g, qi: (g, qi, 0))],
        out_shape=[jax.ShapeDtypeStruct((T, GROUP_WIDTH), BF16),
                   jax.ShapeDtypeStruct((G, T, LANES), BF16)],
        compiler_params=_cparams(("parallel", "parallel")),
        name="nsa_compressed_select",
    )(qn, kvc, kvc, overlap_t, gx)


def _sel_attn_body(q_ref, sb_ref, qnext_ref, sbnext_ref, k_ref, v_ref, gx_ref, o_ref, qs_ref, m_ref, acc_ref,
                   s_ref, *, tq, tk):
    hpg = q_ref.shape[0]
    for hh in range(hpg):
        qs_ref[hh * tq:(hh + 1) * tq, 0:LANES] = sb_ref[0]
        qs_ref[hh * tq:(hh + 1) * tq, LANES:] = q_ref[hh]

    def next_queries():
        bias = sbnext_ref[0]
        return jnp.concatenate([jnp.concatenate([bias, qnext_ref[hh]], axis=-1) for hh in range(hpg)], axis=0)

    _causal_sweep(qs_ref, k_ref, v_ref, m_ref, acc_ref, s_ref, pl.program_id(1) * tq, tq, tk, next_queries)
    o = _normalize_low_half(acc_ref[...])
    o_ref[...] = (gx_ref[0] * _pack_heads(o, tq)).astype(BF16)


def sel_attention(qn, selbias, ks, vs, gx, tq, tk):
    H, T, _ = qn.shape
    G = NSA_KV_HEADS
    hpg = H // G
    last = T // tq - 1
    return pl.pallas_call(
        functools.partial(_sel_attn_body, tq=tq, tk=tk),
        grid=(G, T // tq),
        in_specs=[pl.BlockSpec((hpg, tq, LANES), lambda g, qi: (g, qi, 0)),
                  pl.BlockSpec((1, tq, LANES), lambda g, qi: (g, qi, 0)),
                  pl.BlockSpec((hpg, tq, LANES), lambda g, qi: (g, jnp.minimum(qi + 1, last), 0)),
                  pl.BlockSpec((1, tq, LANES), lambda g, qi: (g, jnp.minimum(qi + 1, last), 0)),
                  pl.BlockSpec((1, T, 2 * LANES), lambda g, qi: (g, 0, 0)),
                  pl.BlockSpec((1, T, LANES), lambda g, qi: (g, 0, 0)),
                  pl.BlockSpec((1, tq, hpg * NSA_HEAD_DIM), lambda g, qi: (1, qi, g))],
        out_specs=pl.BlockSpec((tq, hpg * NSA_HEAD_DIM), lambda g, qi: (qi, g)),
        out_shape=jax.ShapeDtypeStruct((T, GROUP_WIDTH), BF16),
        scratch_shapes=[pltpu.VMEM((hpg * tq, 2 * LANES), BF16), pltpu.VMEM((hpg * tq, LANES), F32),
                        pltpu.VMEM((hpg * tq, LANES), F32), pltpu.VMEM((hpg * tq, tk), F32)],
        compiler_params=_cparams(("arbitrary", "arbitrary")),
        name="nsa_selected_attention",
    )(qn, selbias, qn, selbias, ks, vs, gx)


def _win_attn_body(q_ref, k_ref, v_ref, b_ref, gx_ref, o_ref, *, tq, span):
    hpg = q_ref.shape[0]
    q0 = pl.program_id(1) * tq
    lo = pl.multiple_of(jnp.maximum(q0 + tq - span, 0), tq)
    q = q_ref[...].reshape(hpg * tq, LANES)
    s = _dot_nt(q, k_ref[0, pl.ds(lo, span), :])

    def finish(s):
        p = jnp.exp2(s - jnp.max(s, axis=-1, keepdims=True))
        acc = _dot(p.astype(BF16), v_ref[0, pl.ds(lo, span), :])
        o_ref[...] = (gx_ref[0] * _pack_heads(_normalize_low_half(acc), tq)).astype(BF16)

    @pl.when(q0 + tq >= span)
    def _():
        finish(s + jnp.concatenate([b_ref[...]] * hpg, axis=0))

    @pl.when(q0 + tq < span)
    def _():
        qpos = q0 + (_row(s.shape) & (tq - 1))
        kpos = lo + _lane(s.shape)
        finish(jnp.where((kpos <= qpos) & (kpos > qpos - NSA_WINDOW), s, NEG))


def _window_bias(tq, span):
    d = jnp.arange(span)[None, :] - jnp.arange(tq)[:, None] + (tq - span)
    return jnp.where((d <= 0) & (d > -NSA_WINDOW), 0.0, NEG).astype(F32)


def win_attention(qn, kw, vw, gx, tq):
    H, T, _ = qn.shape
    G = NSA_KV_HEADS
    hpg = H // G
    span = NSA_WINDOW + tq
    return pl.pallas_call(
        functools.partial(_win_attn_body, tq=tq, span=span),
        grid=(G, T // tq),
        in_specs=[pl.BlockSpec((hpg, tq, LANES), lambda g, qi: (g, qi, 0)),
                  pl.BlockSpec((1, T, LANES), lambda g, qi: (g, 0, 0)),
                  pl.BlockSpec((1, T, LANES), lambda g, qi: (g, 0, 0)),
                  pl.BlockSpec((tq, span), lambda g, qi: (0, 0)),
                  pl.BlockSpec((1, tq, hpg * NSA_HEAD_DIM), lambda g, qi: (2, qi, g))],
        out_specs=pl.BlockSpec((tq, hpg * NSA_HEAD_DIM), lambda g, qi: (qi, g)),
        out_shape=jax.ShapeDtypeStruct((T, GROUP_WIDTH), BF16),
        compiler_params=_cparams(("parallel", "arbitrary")),
        name="nsa_window_attention",
    )(qn, kw, vw, _window_bias(tq, span), gx)


def _top2_route(h, w_router):
    logits = jnp.dot(h, w_router, preferred_element_type=F32, precision=lax.Precision.HIGHEST)
    lane = _lane(logits.shape)
    lf = lane.astype(F32)
    logits = jnp.where(lane < N_EXPERTS, logits, -jnp.inf)
    v0 = jnp.max(logits, axis=-1, keepdims=True)
    i0 = jnp.min(jnp.where(logits == v0, lf, float(LANES)), axis=-1, keepdims=True)
    rest = jnp.where(lf == i0, -jnp.inf, logits)
    v1 = jnp.max(rest, axis=-1, keepdims=True)
    i1 = jnp.min(jnp.where(rest == v1, lf, float(LANES)), axis=-1, keepdims=True)
    e1 = jnp.exp(v1 - v0)
    w0 = 1.0 / (1.0 + e1)
    w1 = e1 / (1.0 + e1)
    return jnp.where(lane == 0, i0, jnp.where(lane == 1, i1, jnp.where(lane == 2, w0, w1)))


def _out_proj_body(a_ref, b_ref, c_ref, d1_ref, d2_ref, d3_ref, w_ref, x_ref, *rest, normed):
    o_ref, lhs_ref = rest[-3 if normed else -2], rest[-1]
    lhs_ref[:, 0:GROUP_WIDTH] = a_ref[...]
    lhs_ref[:, GROUP_WIDTH:2 * GROUP_WIDTH] = b_ref[...]
    lhs_ref[:, 2 * GROUP_WIDTH:3 * GROUP_WIDTH] = c_ref[...]
    d = d1_ref[...].astype(F32) + d2_ref[...].astype(F32) + d3_ref[...].astype(F32)
    lhs_ref[:, 3 * GROUP_WIDTH:] = d.astype(BF16)
    y = x_ref[...] + _dot(lhs_ref[...], w_ref[0])
    o_ref[...] = y
    if normed:
        g_ref, hn_ref = rest[0], rest[2]
        hn_ref[...] = _rms(y, g_ref[...]).astype(BF16)


def out_proj(parts, w, layer, x, g_next, tm):
    T, N = x.shape
    part = pl.BlockSpec((tm, GROUP_WIDTH), lambda i: (i, 0))
    rows = pl.BlockSpec((tm, N), lambda i: (i, 0))
    in_specs = [part] * 6 + [pl.BlockSpec((1, 4 * GROUP_WIDTH, N), lambda i: (layer, 0, 0)), rows]
    out_specs = [rows]
    out_shape = [jax.ShapeDtypeStruct((T, N), F32)]
    operands = [*parts, w, x]
    if g_next is not None:
        in_specs.append(pl.BlockSpec((1, N), lambda i: (0, 0)))
        out_specs.append(rows)
        out_shape.append(jax.ShapeDtypeStruct((T, N), BF16))
        operands.append(g_next)
    return pl.pallas_call(
        functools.partial(_out_proj_body, normed=g_next is not None),
        grid=(T // tm,),
        in_specs=in_specs,
        out_specs=out_specs,
        out_shape=out_shape,
        scratch_shapes=[pltpu.VMEM((tm, 4 * GROUP_WIDTH), BF16)],
        compiler_params=_cparams(("parallel",)),
        name="out_proj",
    )(*operands)


UP_TN = 1408
UP_CHUNK = 512
CAST_ROWS = 256


def _group_starts(te_ref, i):
    return (i == 0) | (te_ref[i] != te_ref[jnp.maximum(i - 1, 0)])


def _stream_group_weights(te_ref, nx_ref, w_hbms, wst_ref, wb_refs, sem_ref):
    j = pl.program_id(0)
    i = pl.program_id(1)
    tn = wst_ref.shape[2]

    def copies(e, jj):
        cols = pl.ds(pl.multiple_of(jj * tn, tn), tn)
        return [pltpu.make_async_copy(w.at[e, :, cols], wst_ref.at[n], sem_ref) for n, w in enumerate(w_hbms)]

    @pl.when((j == 0) & (i == 0))
    def _():
        for c in copies(te_ref[0], 0):
            c.start()

    @pl.when(_group_starts(te_ref, i))
    def _():
        for c in copies(te_ref[i], j):
            c.wait()
        def cast_rows(c, carry):
            rows = pl.ds(pl.multiple_of(c * CAST_ROWS, CAST_ROWS), CAST_ROWS)
            for n, wb_ref in enumerate(wb_refs):
                wb_ref[rows, :] = wst_ref[n, rows, :].astype(BF16)
            return carry

        lax.fori_loop(0, wst_ref.shape[1] // CAST_ROWS, cast_rows, 0)
        in_sweep = nx_ref[i] >= 0

        @pl.when(in_sweep | (j + 1 < pl.num_programs(0)))
        def _():
            for c in copies(jnp.where(in_sweep, nx_ref[i], te_ref[0]), jnp.where(in_sweep, j, j + 1)):
                c.start()


def _weight_stream_scratch(n_weights, k, tn):
    return [pltpu.VMEM((n_weights, k, tn), F32)] + [pltpu.VMEM((k, tn), BF16)] * n_weights + [
        pltpu.SemaphoreType.DMA(())]


def _by_fill(nv_ref, o_ref, compute):
    nv = nv_ref[pl.program_id(1)]
    half = o_ref.shape[0] // 2

    @pl.when(nv > half)
    def _():
        o_ref[...] = compute(slice(None))

    @pl.when((nv > 0) & (nv <= half))
    def _():
        o_ref[0:half, :] = compute(slice(0, half))
        o_ref[half:, :] = jnp.zeros((half, o_ref.shape[1]), o_ref.dtype)

    @pl.when(nv == 0)
    def _():
        o_ref[...] = jnp.zeros(o_ref.shape, o_ref.dtype)


def _up_body(te_ref, nx_ref, nv_ref, na_ref, h_ref, wg_hbm, wu_hbm, o_ref, wst_ref, wgb_ref, wub_ref, sem_ref):
    _stream_group_weights(te_ref, nx_ref, (wg_hbm, wu_hbm), wst_ref, (wgb_ref, wub_ref), sem_ref)

    def swiglu(rows):
        h = h_ref[rows, :]
        parts = []
        for c0 in range(0, o_ref.shape[1], UP_CHUNK):
            cols = slice(c0, min(c0 + UP_CHUNK, o_ref.shape[1]))
            a = _dot(h, wgb_ref[:, cols])
            parts.append((a * jax.nn.sigmoid(a) * _dot(h, wub_ref[:, cols])).astype(BF16))
        return jnp.concatenate(parts, axis=-1)

    _by_fill(nv_ref, o_ref, swiglu)


def swiglu_up(tile_expert, next_expert, valid_rows, n_active, hn, wg, wu, tm, tn):
    P, K = hn.shape
    F = wg.shape[2]
    return pl.pallas_call(
        _up_body,
        grid_spec=pltpu.PrefetchScalarGridSpec(
            num_scalar_prefetch=4,
            grid=(F // tn, P // tm),
            in_specs=[pl.BlockSpec((tm, K), lambda j, i, te, nx, nv, na: (jnp.minimum(i, na[0] - 1), 0)),
                      pl.BlockSpec(memory_space=pl.ANY), pl.BlockSpec(memory_space=pl.ANY)],
            out_specs=pl.BlockSpec((tm, tn), lambda j, i, te, nx, nv, na: (i, j)),
            scratch_shapes=_weight_stream_scratch(2, K, tn)),
        out_shape=jax.ShapeDtypeStruct((P, F), BF16),
        compiler_params=_cparams(("arbitrary", "arbitrary")),
        name="swiglu_up",
    )(tile_expert, next_expert, valid_rows, n_active, hn, wg, wu)


def _down_body(te_ref, nx_ref, nv_ref, na_ref, a_ref, w_hbm, *rest, residual):
    r_ref = rest[0] if residual else None
    o_ref, wst_ref, wb_ref, sem_ref = rest[1:] if residual else rest
    _stream_group_weights(te_ref, nx_ref, (w_hbm,), wst_ref, (wb_ref,), sem_ref)

    def project(rows):
        y = _dot(a_ref[rows, :], wb_ref[...])
        return r_ref[rows, :] + y if residual else y

    _by_fill(nv_ref, o_ref, project)


def swiglu_down(tile_expert, next_expert, valid_rows, n_active, act, wd, residual, tm, tn):
    P, F = act.shape
    N = wd.shape[2]
    tile = pl.BlockSpec((tm, tn), lambda j, i, te, nx, nv, na: (i, j))
    in_specs = [pl.BlockSpec((tm, F), lambda j, i, te, nx, nv, na: (jnp.minimum(i, na[0] - 1), 0)),
                pl.BlockSpec(memory_space=pl.ANY)]
    operands = [act, wd]
    if residual is not None:
        in_specs.append(tile)
        operands.append(residual)
    return pl.pallas_call(
        functools.partial(_down_body, residual=residual is not None),
        grid_spec=pltpu.PrefetchScalarGridSpec(
            num_scalar_prefetch=4,
            grid=(N // tn, P // tm),
            in_specs=in_specs,
            out_specs=tile,
            scratch_shapes=_weight_stream_scratch(1, F, tn)),
        out_shape=jax.ShapeDtypeStruct((P, N), F32),
        compiler_params=_cparams(("arbitrary", "arbitrary")),
        name="swiglu_down",
    )(tile_expert, next_expert, valid_rows, n_active, *operands)


def _router_body(x_ref, g_ref, w_ref, o_ref):
    o_ref[...] = _top2_route(_rms(x_ref[...], g_ref[...]), w_ref[...])


def router(x, g, w_pad, tm):
    T, K = x.shape
    return pl.pallas_call(
        _router_body,
        grid=(T // tm,),
        in_specs=[pl.BlockSpec((tm, K), lambda i: (i, 0)),
                  pl.BlockSpec((1, K), lambda i: (0, 0)),
                  pl.BlockSpec((K, LANES), lambda i: (0, 0))],
        out_specs=pl.BlockSpec((tm, LANES), lambda i: (i, 0)),
        out_shape=jax.ShapeDtypeStruct((T, LANES), F32),
        compiler_params=_cparams(("parallel",)),
        name="moe_router",
    )(x, g, w_pad)


ROW_DMA_UNROLL = 8


def _row_copy(src_hbm, row, dst_ref, r, sem):
    return pltpu.make_async_copy(src_hbm.at[pl.ds(row, 1)], dst_ref.at[pl.ds(r, 1)], sem)


def _gather_body(tok_ref, na_ref, x_hbm, g_ref, o_ref, buf_ref, sem_ref):
    i = pl.program_id(0)
    tm = buf_ref.shape[1]

    def fetch(tile):
        slot = tile % 2

        def start(p, c):
            for priority in range(2):
                r = 2 * p + priority
                _row_copy(x_hbm, tok_ref[tile * tm + r], buf_ref.at[slot], r, sem_ref.at[slot]).start(priority)
            return c

        lax.fori_loop(0, tm // 2, start, 0, unroll=ROW_DMA_UNROLL // 2)

    @pl.when(i == 0)
    def _():
        fetch(0)

    @pl.when(i + 1 < na_ref[0])
    def _():
        fetch(i + 1)

    @pl.when(i < na_ref[0])
    def _():
        slot = i % 2

        def wait(r, c):
            _row_copy(x_hbm, 0, buf_ref.at[slot], r, sem_ref.at[slot]).wait()
            return c

        lax.fori_loop(0, tm, wait, 0, unroll=ROW_DMA_UNROLL)
        o_ref[...] = _rms(buf_ref[slot], g_ref[...]).astype(BF16)

    @pl.when(i >= na_ref[0])
    def _():
        o_ref[...] = jnp.zeros(o_ref.shape, BF16)


def gather_norm_tokens(tok_of_slot, n_active, x, g, tm):
    P = tok_of_slot.shape[0]
    K = x.shape[1]
    return pl.pallas_call(
        _gather_body,
        grid_spec=pltpu.PrefetchScalarGridSpec(
            num_scalar_prefetch=2,
            grid=(P // tm,),
            in_specs=[pl.BlockSpec(memory_space=pl.ANY), pl.BlockSpec((1, K), lambda i, tok, na: (0, 0))],
            out_specs=pl.BlockSpec((tm, K), lambda i, tok, na: (i, 0)),
            scratch_shapes=[pltpu.VMEM((2, tm, K), F32), pltpu.SemaphoreType.DMA((2,))]),
        out_shape=jax.ShapeDtypeStruct((P, K), BF16),
        compiler_params=_cparams(("arbitrary",)),
        name="moe_gather",
    )(tok_of_slot, n_active, x, g)


def _combine_body(s0_ref, s1_ref, x_ref, y_hbm, rt_ref, g_ref, o_ref, b0_ref, b1_ref, sem0, sem1, *, final):
    tm = x_ref.shape[0]
    base = pl.program_id(0) * tm

    def start(r, c):
        _row_copy(y_hbm, s0_ref[base + r], b0_ref, r, sem0).start(0)
        _row_copy(y_hbm, s1_ref[base + r], b1_ref, r, sem1).start(1)
        return c

    def wait(r, c):
        _row_copy(y_hbm, 0, b0_ref, r, sem0).wait()
        _row_copy(y_hbm, 0, b1_ref, r, sem1).wait()
        return c

    lax.fori_loop(0, tm, start, 0, unroll=ROW_DMA_UNROLL)
    lax.fori_loop(0, tm, wait, 0, unroll=ROW_DMA_UNROLL)
    gates = rt_ref[...]
    y = x_ref[...] + gates[:, 2:3] * b0_ref[...] + gates[:, 3:4] * b1_ref[...]
    o_ref[...] = _rms(y, g_ref[...]) if final else y


def moe_combine(slot0, slot1, x, ys, route, g, tm, final):
    T, K = x.shape
    return pl.pallas_call(
        functools.partial(_combine_body, final=final),
        grid_spec=pltpu.PrefetchScalarGridSpec(
            num_scalar_prefetch=2,
            grid=(T // tm,),
            in_specs=[pl.BlockSpec((tm, K), lambda i, s0, s1: (i, 0)),
                      pl.BlockSpec(memory_space=pl.ANY),
                      pl.BlockSpec((tm, LANES), lambda i, s0, s1: (i, 0)),
                      pl.BlockSpec((1, K), lambda i, s0, s1: (0, 0))],
            out_specs=pl.BlockSpec((tm, K), lambda i, s0, s1: (i, 0)),
            scratch_shapes=[pltpu.VMEM((tm, K), F32), pltpu.VMEM((tm, K), F32),
                            pltpu.SemaphoreType.DMA(()), pltpu.SemaphoreType.DMA(())]),
        out_shape=jax.ShapeDtypeStruct((T, K), F32),
        compiler_params=_cparams(("arbitrary",)),
        name="moe_combine",
    )(slot0, slot1, x, ys, route, g)


def _final_norm_body(x_ref, g_ref, o_ref):
    o_ref[...] = _rms(x_ref[...], g_ref[...])


def final_norm(x, g, tm):
    T, K = x.shape
    return pl.pallas_call(
        _final_norm_body,
        grid=(T // tm,),
        in_specs=[pl.BlockSpec((tm, K), lambda i: (i, 0)), pl.BlockSpec((1, K), lambda i: (0, 0))],
        out_specs=pl.BlockSpec((tm, K), lambda i: (i, 0)),
        out_shape=jax.ShapeDtypeStruct((T, K), F32),
        compiler_params=_cparams(("parallel",)),
        name="final_norm",
    )(x, g)


def _routing_tables(route, tm, expert_base):
    T = route.shape[0]
    top_i = route[:, 0:2].astype(jnp.int32)
    e_flat = top_i.reshape(-1)
    onehot = (e_flat[:, None] == jnp.arange(N_EXPERTS)[None, :]).astype(jnp.int32)
    rank = jnp.take_along_axis(jnp.cumsum(onehot, axis=0) - onehot, e_flat[:, None], axis=1)[:, 0]
    count = jnp.sum(onehot, axis=0)
    padded = ((count + tm - 1) // tm) * tm
    end = jnp.cumsum(padded)
    start = end - padded
    slot = start[e_flat] + rank
    n_slots = 2 * T + N_EXPERTS * tm
    n_tiles = n_slots // tm
    n_active = (end[-1] // tm).astype(jnp.int32)
    tile_start = jnp.minimum(jnp.arange(n_tiles, dtype=jnp.int32), n_active - 1) * tm
    tile_expert = jnp.minimum(jnp.sum(tile_start[:, None] >= end[None, :], axis=1), N_EXPERTS - 1)
    group_end = end[tile_expert] // tm
    next_expert = jnp.where(group_end < n_active, tile_expert[jnp.minimum(group_end, n_tiles - 1)] + expert_base, -1)
    tile_index = jnp.arange(n_tiles, dtype=jnp.int32)
    valid_rows = jnp.where(tile_index < n_active,
                           jnp.clip(start[tile_expert] + count[tile_expert] - tile_index * tm, 0, tm), 0)
    tok_of_slot = jnp.zeros((n_slots,), jnp.int32).at[slot].set(jnp.arange(2 * T, dtype=jnp.int32) // 2)
    slot2 = slot.reshape(T, 2).astype(jnp.int32)
    return ((tile_expert + expert_base).astype(jnp.int32), next_expert.astype(jnp.int32),
            valid_rows.astype(jnp.int32), n_active.reshape(1), tok_of_slot, slot2[:, 0], slot2[:, 1])


def _rope_tables128(T):
    half = NSA_HEAD_DIM // 2
    inv = ROPE_THETA ** (-jnp.arange(0, NSA_HEAD_DIM, 2, dtype=F32) / NSA_HEAD_DIM)
    ang = jnp.arange(T, dtype=F32)[:, None] * inv[None, :]
    cos, sin = jnp.cos(ang), jnp.sin(ang)
    cos128 = jnp.tile(cos, (1, LANES // half))
    sin128 = jnp.tile(jnp.concatenate([-sin, sin], axis=1), (1, LANES // NSA_HEAD_DIM))
    return cos128, sin128


def _overlap_matrix_t(n_cmp_pad):
    sstart = jnp.arange(LANES) * NSA_SLC_LEN
    cstart = jnp.arange(n_cmp_pad) * NSA_CMP_STRIDE
    ov = (cstart[None, :] < sstart[:, None] + NSA_SLC_LEN) & (cstart[None, :] + NSA_CMP_LEN > sstart[:, None])
    return ov.astype(BF16)


def kernel(x, attn_norm, w_in, w_out, gm_ln_g, gm_ln_b, gm_ws, gm_bs, da_lambda, da_subln, cv_dw_w, cv_dw_b,
           cv_ln_g, cv_ln_b, nsa_cmp_w1, nsa_cmp_w2, nsa_cmp_pe, ffn_norm, ffn_wg, ffn_wu, ffn_wd, router_w,
           exp_wg, exp_wu, exp_wd, final_norm_g):
    B, T, D = x.shape
    assert B == 1 and D == D_MODEL and T % 1024 == 0 and T // NSA_SLC_LEN <= LANES
    depth = w_in.shape[0]
    G = NSA_KV_HEADS
    n16 = T // NSA_CMP_STRIDE
    tm = 512

    cos128, sin128 = _rope_tables128(T)
    overlap_t = _overlap_matrix_t(n16)
    dense_na = jnp.full((1,), T // tm, jnp.int32)
    row = lambda v: v.reshape(1, -1)
    merge = lambda w: w.reshape((-1,) + w.shape[2:])
    exp_wg, exp_wu, exp_wd = merge(exp_wg), merge(exp_wu), merge(exp_wd)
    w_in_b = jnp.concatenate(
        [w_in.astype(BF16), jnp.zeros(w_in.shape[:2] + (IN_WIDTH_PAD - IN_WIDTH,), BF16)], axis=-1)
    w_out_b = w_out.astype(BF16)

    xs = x[0]
    for l in range(depth):
        lambda_init = 0.8 - 0.6 * math.exp(-0.3 * l)
        z = norm_mm(xs, row(attn_norm[l]), w_in_b, l, 1024, IN_WIDTH_PAD // 3)

        bs_rows = jnp.repeat(gm_bs[l].T, GM_CHUNK, axis=1)
        o_a = gmlp(z, row(gm_ln_g[l]), row(gm_ln_b[l]), gm_ws[l], bs_rows, 512)
        dw_w = jnp.pad(cv_dw_w[l], ((0, CV_HALO - CV_WIDTH), (0, 0)))
        o_c = conformer_conv(z, dw_w, row(cv_dw_b[l]), row(cv_ln_g[l]), row(cv_ln_b[l]), 256)

        qd, kd, vd, qn, kc, vc, ks, vs, kw, vw, gx = prep(z, cos128, sin128, 256)
        o_b = diff_attention(qd, kd, vd, da_lambda[l], row(da_subln[l]), lambda_init, 512, 512)

        x16 = jnp.stack([kc, vc]).reshape(2, T, G, NSA_HEAD_DIM).transpose(0, 2, 1, 3)
        x16 = x16.reshape(2, G, n16, NSA_CMP_STRIDE * NSA_HEAD_DIM)
        pe2 = nsa_cmp_pe[l].reshape(2, 2, NSA_CMP_STRIDE * NSA_HEAD_DIM)
        w2p = jnp.pad(nsa_cmp_w2[l], ((0, 0), (0, 0), (0, LANES - NSA_HEAD_DIM))).astype(BF16)
        kvc = compress(x16, pe2, nsa_cmp_w1[l].astype(BF16), w2p)
        o_cmp, selbias = cmp_select(qn, kvc, overlap_t, gx, 256)
        o_sel = sel_attention(qn, selbias, ks, vs, gx, 256, 512)
        o_win = win_attention(qn, kw, vw, gx, 256)

        g_ffn = row(ffn_norm[l])
        parts = (o_a, o_b, o_c, o_cmp, o_sel, o_win)
        e = l // 2
        if l % 2 == 0:
            xs, hn = out_proj(parts, w_out_b, l, xs, g_ffn, 256)
            dense_te = jnp.full((T // tm,), e, jnp.int32)
            dense_nx = jnp.full((T // tm,), -1, jnp.int32)
            dense_nv = jnp.full((T // tm,), tm, jnp.int32)
            act = swiglu_up(dense_te[::2], dense_nx[::2], 2 * dense_nv[::2], dense_na // 2, hn, ffn_wg, ffn_wu,
                            2 * tm, UP_CHUNK)
            xs = swiglu_down(dense_te, dense_nx, dense_nv, dense_na, act, ffn_wd, xs, tm, 512)
            if l == depth - 1:
                xs = final_norm(xs, row(final_norm_g), tm)
        else:
            w_r = jnp.pad(router_w[e], ((0, 0), (0, LANES - N_EXPERTS)))
            xs, = out_proj(parts, w_out_b, l, xs, None, 256)
            route = router(xs, g_ffn, w_r, tm)
            tile_expert, next_expert, valid_rows, n_active, tok_of_slot, slot0, slot1 = _routing_tables(
                route, tm, e * N_EXPERTS)
            hg = gather_norm_tokens(tok_of_slot, n_active, xs, g_ffn, tm)
            act = swiglu_up(tile_expert, next_expert, valid_rows, n_active, hg, exp_wg, exp_wu, tm, UP_TN)
            ys = swiglu_down(tile_expert, next_expert, valid_rows, n_active, act, exp_wd, None, tm, 1024)
            xs = moe_combine(slot0, slot1, xs, ys, route, row(final_norm_g), 256, final=(l == depth - 1))
    return xs[None]
```
